```python
import jax
import jax.numpy as jnp
from jax import lax
import numpy as np

D_MODEL = 2048
BATCH = 1
SEQ = 8192
DEPTH = 1
DEC_BATCH = 128
DEC_SEQ = 8
PAST_LEN = 2048
PAGE_SIZE = 128

HEAD_DIM = 128
N_ATT_HEADS = (D_MODEL // 2) // HEAD_DIM
N_KV_HEADS = 2
HEADS_PER_KV = N_ATT_HEADS // N_KV_HEADS
ATT_WIDTH = N_ATT_HEADS * HEAD_DIM
KV_WIDTH = N_KV_HEADS * HEAD_DIM
CONV_WIDTH = D_MODEL - ATT_WIDTH
CONV_K = 3
CMP_BLOCK = 32
CMP_STRIDE = 16
SEL_BLOCK = 64
N_SEL = 16
WINDOW = 512
Q_BLOCK = 128
D_FF = 5632
ROPE_THETA = 10000.0
N_BRANCH = 3
N_MOD = 9
EPS = 1e-6
NEG = -1e30
TINY = 1e-30

OFF_Q = 0
OFF_KC = OFF_Q + ATT_WIDTH
OFF_VC = OFF_KC + KV_WIDTH
OFF_KS = OFF_VC + KV_WIDTH
OFF_VS = OFF_KS + KV_WIDTH
OFF_KW = OFF_VS + KV_WIDTH
OFF_VW = OFF_KW + KV_WIDTH
OFF_G = OFF_VW + KV_WIDTH
OFF_CIN = OFF_G + N_ATT_HEADS * N_BRANCH
OFF_COUT = OFF_CIN + CONV_WIDTH
OFF_H = OFF_COUT + CONV_WIDTH
N_PROJ = OFF_H + CONV_WIDTH

kernel_name = 'hymba_nsa_shortconv_macaron_step'


def rmsnorm(x, g):
    xf = x.astype(jnp.float32)
    y = xf * lax.rsqrt(jnp.mean(xf * xf, axis=-1, keepdims=True) + EPS)
    return (y * g.astype(jnp.float32)).astype(x.dtype)


def rope(x, pos):
    half = HEAD_DIM // 2
    inv = ROPE_THETA ** (-jnp.arange(half, dtype=jnp.float32) * 2.0 / HEAD_DIM)
    ang = pos.astype(jnp.float32)[:, None] * inv[None, :]
    cos = jnp.cos(ang)[:, None, :]
    sin = jnp.sin(ang)[:, None, :]
    xf = x.astype(jnp.float32)
    x1, x2 = xf[..., :half], xf[..., half:]
    return jnp.concatenate([x1 * cos - x2 * sin, x2 * cos + x1 * sin], axis=-1).astype(x.dtype)


def masked_softmax(s, mask, axis):
    s = jnp.where(mask, s, NEG)
    m = jnp.max(s, axis=axis, keepdims=True)
    e = jnp.where(mask, jnp.exp(s - m), 0.0)
    return e / jnp.maximum(jnp.sum(e, axis=axis, keepdims=True), TINY)


def swiglu(h, w_gate, w_up, w_down):
    return (jax.nn.silu(h @ w_gate) * (h @ w_up)) @ w_down


def compress(k_raw, v_raw, w_ck, w_cv):
    n, length = k_raw.shape[0], k_raw.shape[1]
    r = CMP_BLOCK // CMP_STRIDE
    n_chunks = -(-length // CMP_STRIDE)
    n_cmp = n_chunks - r + 1

    def blocks(a):
        a = jnp.pad(a, ((0, 0), (0, n_chunks * CMP_STRIDE - length), (0, 0), (0, 0)))
        a = a.reshape(n, n_chunks, CMP_STRIDE, N_KV_HEADS, HEAD_DIM)
        return jnp.concatenate([a[:, j:j + n_cmp] for j in range(r)], axis=2)

    kc = jnp.einsum('ncjgd,jde->ncge', blocks(k_raw), w_ck)
    vc = jnp.einsum('ncjgd,jde->ncge', blocks(v_raw), w_cv)
    end = jnp.arange(n_cmp) * CMP_STRIDE + (CMP_BLOCK - 1)
    return rope(kc, end), vc, end


def selection_scores(p_grp, n_blocks):
    q, g, n_cmp = p_grp.shape
    r = CMP_BLOCK // CMP_STRIDE
    n_chunks = n_cmp + r - 1
    per = SEL_BLOCK // CMP_STRIDE
    pc = sum(jnp.pad(p_grp, ((0, 0), (0, 0), (j, r - 1 - j))) for j in range(r)) / r
    pc = jnp.pad(pc, ((0, 0), (0, 0), (0, n_blocks * per - n_chunks)))
    return pc.reshape(q, g, n_blocks, per).sum(-1)


def to_sel_blocks(a, n_blocks):
    a = jnp.pad(a, ((0, n_blocks * SEL_BLOCK - a.shape[0]), (0, 0), (0, 0)))
    return a.reshape(n_blocks, SEL_BLOCK, N_KV_HEADS, HEAD_DIM).transpose(2, 0, 1, 3)


def nsa_core(q, t, gates, kc, vc, c_end, ks, vs, kw, vw, w_pos):
    nq = q.shape[0]
    n_blocks = ks.shape[1]
    scale = HEAD_DIM ** -0.5
    qg = q.reshape(nq, N_KV_HEADS, HEADS_PER_KV, HEAD_DIM)
    tq = t[:, None]
    s_c = jnp.einsum('qgrd,ngd->qgrn', qg, kc, preferred_element_type=jnp.float32) * scale
    p_c = masked_softmax(s_c, (c_end[None, :] <= tq)[:, None, None, :], -1)
    o_c = jnp.einsum('qgrn,ngd->qgrd', p_c.astype(vc.dtype), vc)
    blk = jnp.arange(n_blocks)[None, :]
    cur = tq // SEL_BLOCK
    forced = (blk == 0) | (blk == cur) | (blk == cur - 1)
    valid = blk * SEL_BLOCK <= tq
    score = selection_scores(p_c.sum(axis=2), n_blocks)
    score = jnp.where(forced[:, None], jnp.inf, jnp.where(valid[:, None], score, -jnp.inf))
    _, idx = lax.top_k(score, min(N_SEL, n_blocks))
    g_ix = jnp.arange(N_KV_HEADS)[None, :, None]
    k_sel = ks[g_ix, idx]
    v_sel = vs[g_ix, idx]
    tok = idx[..., None] * SEL_BLOCK + jnp.arange(SEL_BLOCK)
    m_s = (tok <= t[:, None, None, None])[:, :, None]
    s_s = jnp.einsum('qgrd,qgnsd->qgrns', qg, k_sel, preferred_element_type=jnp.float32) * scale
    p_s = masked_softmax(s_s, m_s, (-2, -1))
    o_s = jnp.einsum('qgrns,qgnsd->qgrd', p_s.astype(v_sel.dtype), v_sel)
    wp = w_pos[None, :]
    m_w = ((wp <= tq) & (wp > tq - WINDOW) & (wp >= 0))[:, None, None, :]
    s_w = jnp.einsum('qgrd,kgd->qgrk', qg, kw, preferred_element_type=jnp.float32) * scale
    p_w = masked_softmax(s_w, m_w, -1)
    o_w = jnp.einsum('qgrk,kgd->qgrd', p_w.astype(vw.dtype), vw)
    branches = jnp.stack([o_c, o_s, o_w], axis=-1).reshape(nq, N_ATT_HEADS, HEAD_DIM, N_BRANCH)
    return jnp.einsum('qhdb,qhb->qhd', branches, gates)


def nsa_prompt(q, k_cmp, v_cmp, k_slc, v_slc, k_win, v_win, gates, w_ck, w_cv):
    n, s = q.shape[0], q.shape[1]
    kc, vc, c_end = compress(k_cmp, v_cmp, w_ck, w_cv)
    n_blocks = -(-s // SEL_BLOCK)
    ksb = jax.vmap(lambda a: to_sel_blocks(a, n_blocks))(k_slc)
    vsb = jax.vmap(lambda a: to_sel_blocks(a, n_blocks))(v_slc)
    band = ((0, 0), (WINDOW, 0), (0, 0), (0, 0))
    kwp, vwp = jnp.pad(k_win, band), jnp.pad(v_win, band)
    n_qb = s // Q_BLOCK

    def one(i):
        b = i // n_qb
        s0 = (i % n_qb) * Q_BLOCK
        take = lambda a, ln: lax.dynamic_slice(a, (b, s0) + (0,) * (a.ndim - 2), (1, ln) + a.shape[2:])[0]
        pick = lambda a: lax.dynamic_index_in_dim(a, b, 0, keepdims=False)
        t = s0 + jnp.arange(Q_BLOCK)
        w_pos = s0 - WINDOW + jnp.arange(WINDOW + Q_BLOCK)
        return nsa_core(take(q, Q_BLOCK), t, take(gates, Q_BLOCK), pick(kc), pick(vc), c_end,
                        pick(ksb), pick(vsb), take(kwp, WINDOW + Q_BLOCK), take(vwp, WINDOW + Q_BLOCK), w_pos)

    o = lax.map(one, jnp.arange(n * n_qb, dtype=jnp.int32))
    return o.reshape(n, s, N_ATT_HEADS, HEAD_DIM)


def nsa_sample(q, k_cmp, v_cmp, k_slc, v_slc, k_win, v_win, gates, cache_cmp, cache_slc, win_state, page_table, w_ck, w_cv):
    n_new = q.shape[1]
    past = page_table.shape[1] * PAGE_SIZE
    total = past + n_new
    wbuf = win_state.shape[1]
    n_blocks = -(-total // SEL_BLOCK)
    t = past + jnp.arange(n_new)
    w_pos = past - wbuf + jnp.arange(wbuf + n_new)

    def one(args):
        pages, q_i, kc_i, vc_i, ks_i, vs_i, kw_i, vw_i, g_i, ws_i = args
        cmp_past = cache_cmp[pages].reshape((past,) + cache_cmp.shape[2:])
        slc_past = cache_slc[pages].reshape((past,) + cache_slc.shape[2:])
        k_c = jnp.concatenate([cmp_past[:, 0], kc_i], axis=0)
        v_c = jnp.concatenate([cmp_past[:, 1], vc_i], axis=0)
        kc, vc, c_end = compress(k_c[None], v_c[None], w_ck, w_cv)
        ksb = to_sel_blocks(jnp.concatenate([slc_past[:, 0], ks_i], axis=0), n_blocks)
        vsb = to_sel_blocks(jnp.concatenate([slc_past[:, 1], vs_i], axis=0), n_blocks)
        kw = jnp.concatenate([ws_i[:, 0], kw_i], axis=0)
        vw = jnp.concatenate([ws_i[:, 1], vw_i], axis=0)
        return nsa_core(q_i, t, g_i, kc[0], vc[0], c_end, ksb, vsb, kw, vw, w_pos)

    return lax.map(one, (page_table, q, k_cmp, v_cmp, k_slc, v_slc, k_win, v_win, gates, win_state))


def short_conv(c_in, c_out, h, prev, conv_w, conv_b):
    u = jnp.concatenate([prev, c_in * h], axis=1)
    n_out = h.shape[1]
    y = sum(u[:, j:j + n_out] * conv_w[j] for j in range(CONV_K)) + conv_b
    return c_out * y, u[:, -(CONV_K - 1):]


def split_proj(p):
    n, s = p.shape[0], p.shape[1]
    heads = lambda a, h: a.reshape(n, s, h, HEAD_DIM)
    q = heads(p[..., OFF_Q:OFF_KC], N_ATT_HEADS)
    k_cmp = heads(p[..., OFF_KC:OFF_VC], N_KV_HEADS)
    v_cmp = heads(p[..., OFF_VC:OFF_KS], N_KV_HEADS)
    k_slc = heads(p[..., OFF_KS:OFF_VS], N_KV_HEADS)
    v_slc = heads(p[..., OFF_VS:OFF_KW], N_KV_HEADS)
    k_win = heads(p[..., OFF_KW:OFF_VW], N_KV_HEADS)
    v_win = heads(p[..., OFF_VW:OFF_G], N_KV_HEADS)
    gates = jax.nn.sigmoid(p[..., OFF_G:OFF_CIN].astype(jnp.float32)).astype(p.dtype)
    gates = gates.reshape(n, s, N_ATT_HEADS, N_BRANCH)
    c_in = p[..., OFF_CIN:OFF_COUT]
    c_out = p[..., OFF_COUT:OFF_H]
    h_conv = p[..., OFF_H:N_PROJ]
    return q, k_cmp, v_cmp, k_slc, v_slc, k_win, v_win, gates, c_in, c_out, h_conv


def layer_pre(x, cond, w_ada, b_ada, norm_ffn1, f_gate, f_up, f_down, norm_mix, w_in):
    mod = (jax.nn.silu(cond) @ w_ada + b_ada).reshape(cond.shape[0], N_MOD, D_MODEL)[:, :, None, :]
    h = rmsnorm(x, norm_ffn1) * (1 + mod[:, 1]) + mod[:, 0]
    x = x + 0.5 * (1 + mod[:, 2]) * swiglu(h, f_gate, f_up, f_down)
    h = rmsnorm(x, norm_mix) * (1 + mod[:, 4]) + mod[:, 3]
    return x, h @ w_in, mod


def layer_post(x, o_att, o_conv, mod, norm_att_out, norm_conv_out, w_out, norm_ffn2, f_gate, f_up, f_down):
    n, s = x.shape[0], x.shape[1]
    o = jnp.concatenate([rmsnorm(o_att.reshape(n, s, ATT_WIDTH), norm_att_out),
                         rmsnorm(o_conv, norm_conv_out)], axis=-1)
    x = x + (1 + mod[:, 5]) * (o @ w_out)
    h = rmsnorm(x, norm_ffn2) * (1 + mod[:, 7]) + mod[:, 6]
    return x + 0.5 * (1 + mod[:, 8]) * swiglu(h, f_gate, f_up, f_down)


def setup_inputs(seed: int = 0) -> dict:
    key = jax.random.key(seed)
    ks = jax.random.split(key, 32)
    f32 = jnp.float32
    nrm = lambda k, shape, sc: jax.random.normal(k, shape, f32) * sc
    gain = lambda k, shape: 1.0 + 0.05 * jax.random.normal(k, shape, f32)
    n_pages = PAST_LEN // PAGE_SIZE
    n_phys = (5 * DEC_BATCH * n_pages + 3) // 4
    wbuf = min(WINDOW, PAST_LEN)
    page_table = jax.random.permutation(ks[0], n_phys)[:DEC_BATCH * n_pages].reshape(DEC_BATCH, n_pages).astype(jnp.int32)
    dm = D_MODEL ** -0.5
    return {
        'x_prompt': nrm(ks[1], (BATCH, SEQ, D_MODEL), 1.0),
        'x_sample': nrm(ks[2], (DEC_BATCH, DEC_SEQ, D_MODEL), 1.0),
        'c_prompt': nrm(ks[3], (BATCH, D_MODEL), 1.0),
        'c_sample': nrm(ks[4], (DEC_BATCH, D_MODEL), 1.0),
        'cache_cmp_kv': nrm(ks[5], (DEPTH, n_phys, PAGE_SIZE, 2, N_KV_HEADS, HEAD_DIM), 1.0),
        'cache_slc_kv': nrm(ks[6], (DEPTH, n_phys, PAGE_SIZE, 2, N_KV_HEADS, HEAD_DIM), 1.0),
        'state_win_kv': nrm(ks[7], (DEPTH, DEC_BATCH, wbuf, 2, N_KV_HEADS, HEAD_DIM), 1.0),
        'state_conv': nrm(ks[8], (DEPTH, DEC_BATCH, CONV_K - 1, CONV_WIDTH), 1.0),
        'page_table': page_table,
        'w_ada': nrm(ks[9], (DEPTH, D_MODEL, N_MOD * D_MODEL), 0.1 * dm),
        'b_ada': nrm(ks[10], (DEPTH, N_MOD * D_MODEL), 0.01),
        'norm_ffn1': gain(ks[11], (DEPTH, D_MODEL)),
        'ffn1_gate': nrm(ks[12], (DEPTH, D_MODEL, D_FF), dm),
        'ffn1_up': nrm(ks[13], (DEPTH, D_MODEL, D_FF), dm),
        'ffn1_down': nrm(ks[14], (DEPTH, D_FF, D_MODEL), D_FF ** -0.5),
        'norm_mix': gain(ks[15], (DEPTH, D_MODEL)),
        'w_in': nrm(ks[16], (DEPTH, D_MODEL, N_PROJ), dm),
        'w_cmp_k': nrm(ks[17], (DEPTH, CMP_BLOCK, HEAD_DIM, HEAD_DIM), (CMP_BLOCK * HEAD_DIM) ** -0.5),
        'w_cmp_v': nrm(ks[18], (DEPTH, CMP_BLOCK, HEAD_DIM, HEAD_DIM), (CMP_BLOCK * HEAD_DIM) ** -0.5),
        'conv_w': nrm(ks[19], (DEPTH, CONV_K, CONV_WIDTH), CONV_K ** -0.5),
        'conv_b': nrm(ks[20], (DEPTH, CONV_WIDTH), 0.01),
        'norm_att_out': gain(ks[21], (DEPTH, ATT_WIDTH)),
        'norm_conv_out': gain(ks[22], (DEPTH, CONV_WIDTH)),
        'w_out': nrm(ks[23], (DEPTH, D_MODEL, D_MODEL), dm),
        'norm_ffn2': gain(ks[24], (DEPTH, D_MODEL)),
        'ffn2_gate': nrm(ks[25], (DEPTH, D_MODEL, D_FF), dm),
        'ffn2_up': nrm(ks[26], (DEPTH, D_MODEL, D_FF), dm),
        'ffn2_down': nrm(ks[27], (DEPTH, D_FF, D_MODEL), D_FF ** -0.5),
        'norm_final': gain(ks[28], (D_MODEL,)),
    }


def reference(x_prompt, x_sample, c_prompt, c_sample, cache_cmp_kv, cache_slc_kv, state_win_kv, state_conv,
              page_table, w_ada, b_ada, norm_ffn1, ffn1_gate, ffn1_up, ffn1_down, norm_mix, w_in,
              w_cmp_k, w_cmp_v, conv_w, conv_b, norm_att_out, norm_conv_out, w_out, norm_ffn2,
              ffn2_gate, ffn2_up, ffn2_down, norm_final):
    n_p, s_len = x_prompt.shape[0], x_prompt.shape[1]
    past = page_table.shape[1] * PAGE_SIZE
    pos_p = jnp.arange(s_len)
    pos_s = past + jnp.arange(x_sample.shape[1])
    keep_p = min(WINDOW, s_len)
    wbuf = state_win_kv.shape[2]
    xp, xs = x_prompt, x_sample
    cmp_p, slc_p, win_p, conv_p = [], [], [], []
    cmp_s, slc_s, win_s, conv_s = [], [], [], []
    for l in range(DEPTH):
        pre_w = (w_ada[l], b_ada[l], norm_ffn1[l], ffn1_gate[l], ffn1_up[l], ffn1_down[l], norm_mix[l], w_in[l])
        post_w = (norm_att_out[l], norm_conv_out[l], w_out[l], norm_ffn2[l], ffn2_gate[l], ffn2_up[l], ffn2_down[l])
        xp, pp, mod_p = layer_pre(xp, c_prompt, *pre_w)
        q, kc, vc, ks, vs, kw, vw, gates, cin, cout, hc = split_proj(pp)
        q, ks, kw = rope(q, pos_p), rope(ks, pos_p), rope(kw, pos_p)
        o_att = nsa_prompt(q, kc, vc, ks, vs, kw, vw, gates, w_cmp_k[l], w_cmp_v[l])
        prev0 = jnp.zeros((n_p, CONV_K - 1, CONV_WIDTH), hc.dtype)
        o_conv, conv_new = short_conv(cin, cout, hc, prev0, conv_w[l], conv_b[l])
        xp = layer_post(xp, o_att, o_conv, mod_p, *post_w)
        cmp_p.append(jnp.stack([kc, vc], axis=2))
        slc_p.append(jnp.stack([ks, vs], axis=2))
        win_p.append(jnp.stack([kw, vw], axis=2)[:, s_len - keep_p:])
        conv_p.append(conv_new)
        xs, ps, mod_s = layer_pre(xs, c_sample, *pre_w)
        q, kc, vc, ks, vs, kw, vw, gates, cin, cout, hc = split_proj(ps)
        q, ks, kw = rope(q, pos_s), rope(ks, pos_s), rope(kw, pos_s)
        o_att = nsa_sample(q, kc, vc, ks, vs, kw, vw, gates, cache_cmp_kv[l], cache_slc_kv[l],
                           state_win_kv[l], page_table, w_cmp_k[l], w_cmp_v[l])
        o_conv, conv_new = short_conv(cin, cout, hc, state_conv[l], conv_w[l], conv_b[l])
        xs = layer_post(xs, o_att, o_conv, mod_s, *post_w)
        cmp_s.append(jnp.stack([kc, vc], axis=2))
        slc_s.append(jnp.stack([ks, vs], axis=2))
        win_full = jnp.concatenate([state_win_kv[l], jnp.stack([kw, vw], axis=2)], axis=1)
        win_s.append(win_full[:, win_full.shape[1] - wbuf:])
        conv_s.append(conv_new)
    y_prompt = rmsnorm(xp, norm_final)
    y_sample = rmsnorm(xs, norm_final)
    return (y_prompt, y_sample, jnp.stack(cmp_p), jnp.stack(slc_p), jnp.stack(win_p), jnp.stack(conv_p),
            jnp.stack(cmp_s), jnp.stack(slc_s), jnp.stack(win_s), jnp.stack(conv_s))
```

```python
import functools

import jax
import jax.numpy as jnp
from jax import lax
from jax.experimental import pallas as pl
from jax.experimental.pallas import tpu as pltpu

F32 = jnp.float32
BF16 = jnp.bfloat16

D_MODEL = 2048
HEAD_DIM = 128
N_HEADS = 8
N_KV = 2
HEADS_PER_KV = N_HEADS // N_KV
ATT_W = N_HEADS * HEAD_DIM
KV_W = N_KV * HEAD_DIM
CONV_W = D_MODEL - ATT_W
CONV_K = 3
CMP_BLOCK = 32
CMP_STRIDE = 16
SEL_BLOCK = 64
N_SEL = 16
WINDOW = 512
Q_BLOCK = 128
N_BRANCH = 3
N_MOD = 9
ROPE_THETA = 10000.0
EPS = 1e-6
NEG = -1e30
TINY = 1e-30
SCALE = HEAD_DIM ** -0.5
SLAB_W = 2 * KV_W
GATE_PAD = 128
CHUNKS_PER_SEL = SEL_BLOCK // CMP_STRIDE

VMEM_LIMIT = 56 * 1024 * 1024


def _cparams(*sem):
    return pltpu.CompilerParams(dimension_semantics=sem, vmem_limit_bytes=VMEM_LIMIT)


def _dot(a, b):
    return jnp.dot(a, b, preferred_element_type=F32)


def _dot_nt(a, b):
    return lax.dot_general(a, b, (((1,), (1,)), ((), ())), preferred_element_type=F32)


def _rms(x):
    return x * lax.rsqrt(jnp.mean(x * x, axis=-1, keepdims=True) + EPS)


def _silu(x):
    return x * jax.nn.sigmoid(x)


def _rope(x, cos, sin_signed):
    return x * cos + pltpu.roll(x, HEAD_DIM // 2, axis=1) * sin_signed


def _masked_softmax(s, mask):
    s = jnp.where(mask, s, NEG)
    m = jnp.max(s, axis=-1, keepdims=True)
    e = jnp.where(mask, jnp.exp(s - m), 0.0)
    return e / jnp.maximum(jnp.sum(e, axis=-1, keepdims=True), TINY)


def _ada_kernel(c_ref, w_ref, b_ref, o_ref):
    a = _silu(c_ref[...]).astype(BF16)
    o_ref[...] = _dot(a, w_ref[...].astype(BF16)) + b_ref[...]


def _ada(c, w, b, tn=1024):
    m, n = c.shape[0], w.shape[1]
    return pl.pallas_call(
        _ada_kernel,
        grid=(n // tn,),
        in_specs=[pl.BlockSpec((m, D_MODEL), lambda j: (0, 0)),
                  pl.BlockSpec((D_MODEL, tn), lambda j: (0, j)),
                  pl.BlockSpec((1, tn), lambda j: (0, j))],
        out_specs=pl.BlockSpec((m, tn), lambda j: (0, j)),
        out_shape=jax.ShapeDtypeStruct((m, n), F32),
        compiler_params=_cparams("arbitrary"),
        name="ada",
    )(c, w, b.reshape(1, n))


def _ffn_kernel(x_ref, sh_ref, sc_ref, gt_ref, ng_ref, nf_ref, wg_ref, wu_ref, wd_ref, o_ref, h_scr, acc_scr,
                *, n_f, final_norm):
    j = pl.program_id(1)

    @pl.when(j == 0)
    def _():
        h = _rms(x_ref[...]) * ng_ref[...] * (1.0 + sc_ref[...]) + sh_ref[...]
        h_scr[...] = h.astype(BF16)
        acc_scr[...] = jnp.zeros_like(acc_scr)

    h = h_scr[...]
    a = (_silu(_dot(h, wg_ref[...])) * _dot(h, wu_ref[...])).astype(BF16)
    acc_scr[...] += _dot(a, wd_ref[...])

    @pl.when(j == n_f - 1)
    def _():
        out = x_ref[...] + 0.5 * (1.0 + gt_ref[...]) * acc_scr[...]
        if final_norm:
            out = _rms(out) * nf_ref[...]
        o_ref[...] = out


def _ffn(x, shift, scale, gate, norm_g, norm_final, wg, wu, wd, *, final_norm, tm=512, tf=512):
    t = x.shape[0]
    d_ff = wg.shape[1]
    n_f = d_ff // tf
    mrows = shift.shape[0]
    if mrows != 1:
        tm = tm // 2
    mod_spec = (pl.BlockSpec((1, D_MODEL), lambda i, j: (0, 0)) if mrows == 1
                else pl.BlockSpec((tm, D_MODEL), lambda i, j: (i, 0), pipeline_mode=pl.Buffered(1)))
    row_spec = pl.BlockSpec((tm, D_MODEL), lambda i, j: (i, 0))
    vec_spec = pl.BlockSpec((1, D_MODEL), lambda i, j: (0, 0))
    return pl.pallas_call(
        functools.partial(_ffn_kernel, n_f=n_f, final_norm=final_norm),
        grid=(t // tm, n_f),
        in_specs=[row_spec, mod_spec, mod_spec, mod_spec, vec_spec, vec_spec,
                  pl.BlockSpec((D_MODEL, tf), lambda i, j: (0, j)),
                  pl.BlockSpec((D_MODEL, tf), lambda i, j: (0, j)),
                  pl.BlockSpec((tf, D_MODEL), lambda i, j: (j, 0))],
        out_specs=row_spec,
        out_shape=jax.ShapeDtypeStruct((t, D_MODEL), F32),
        scratch_shapes=[pltpu.VMEM((tm, D_MODEL), BF16), pltpu.VMEM((tm, D_MODEL), F32)],
        compiler_params=_cparams("arbitrary", "arbitrary"),
        name="ffn",
    )(x, shift, scale, gate, norm_g, norm_final, wg, wu, wd)


PW_Q = 0
PW_KV = PW_Q + ATT_W
PW_CONV = PW_KV + 3 * SLAB_W
PW_GATE = PW_CONV + 3 * CONV_W
PW_TOTAL = PW_GATE + GATE_PAD


def _proj_kernel(x_ref, sh_ref, sc_ref, ng_ref, w_ref, cos_ref, sin_ref, p1_ref, p2_ref, cw_ref, cb_ref, nco_ref,
                 q_ref, cmp_ref, slc_ref, win_ref, ksb_ref, vsb_ref, kwb_ref, vwb_ref, gate_ref, ocn_ref, u_ref,
                 carry_scr, *, tm, seq_rows, carry):
    i = pl.program_id(0)
    h = (_rms(x_ref[...]) * ng_ref[...] * (1.0 + sc_ref[...]) + sh_ref[...]).astype(BF16)
    cos, sin = cos_ref[...], sin_ref[...]

    pq = _dot(h, w_ref[:, PW_Q:PW_KV])
    for hd in range(N_HEADS):
        blk = _rope(pq[:, hd * HEAD_DIM:(hd + 1) * HEAD_DIM], cos, sin) * SCALE
        q_ref[:, hd * HEAD_DIM:(hd + 1) * HEAD_DIM] = blk.astype(q_ref.dtype)

    pkv = _dot(h, w_ref[:, PW_KV:PW_CONV])
    cmp_ref[...] = pkv[:, 0:SLAB_W]
    for slab, (o_ref, kb_ref, vb_ref) in enumerate(((slc_ref, ksb_ref, vsb_ref), (win_ref, kwb_ref, vwb_ref)), 1):
        base = slab * SLAB_W
        for g in range(N_KV):
            k = _rope(pkv[:, base + g * HEAD_DIM:base + (g + 1) * HEAD_DIM], cos, sin)
            o_ref[:, g * HEAD_DIM:(g + 1) * HEAD_DIM] = k
            kb_ref[:, g * HEAD_DIM:(g + 1) * HEAD_DIM] = k.astype(BF16)
        v = pkv[:, base + KV_W:base + 2 * KV_W]
        o_ref[:, KV_W:2 * KV_W] = v
        vb_ref[...] = v.astype(BF16)

    gate_ref[...] = jax.nn.sigmoid(_dot(h, w_ref[:, PW_GATE:PW_TOTAL]))

    pc = _dot(h, w_ref[:, PW_CONV:PW_GATE])
    u = pc[:, 0:CONV_W] * pc[:, 2 * CONV_W:3 * CONV_W]
    c_out = pc[:, CONV_W:2 * CONV_W]
    row = lax.broadcasted_iota(jnp.int32, (tm, CONV_W), 0)
    if carry:
        @pl.when(i == 0)
        def _():
            carry_scr[...] = jnp.zeros_like(carry_scr)
        prev1 = carry_scr[7:8, :]
        prev2 = carry_scr[6:7, :]
        rs = row
    else:
        prev1 = p1_ref[...]
        prev2 = p2_ref[...]
        rs = row & (seq_rows - 1)
    um1 = jnp.where(rs >= 1, pltpu.roll(u, 1, axis=0), prev1)
    um2 = jnp.where(rs >= 2, pltpu.roll(u, 2, axis=0), jnp.where(rs == 1, prev1, prev2))
    y = um2 * cw_ref[0:1, :] + um1 * cw_ref[1:2, :] + u * cw_ref[2:3, :] + cb_ref[...]
    ocn_ref[...] = (_rms(c_out * y) * nco_ref[...]).astype(ocn_ref.dtype)
    if carry:
        carry_scr[...] = u[tm - 8:tm, :]
        u_ref[...] = u[tm - 8:tm, :]
    else:
        u_ref[...] = u


def _proj(x, shift, scale, norm_g, w, cos, sin, prev1, prev2, conv_w, conv_b, norm_co, *, carry, seq_rows, act_dtype,
          tm=256):
    t = x.shape[0]
    mrows = shift.shape[0]
    mod_spec = (pl.BlockSpec((1, D_MODEL), lambda i: (0, 0)) if mrows == 1
                else pl.BlockSpec((tm, D_MODEL), lambda i: (i, 0)))
    rows = lambda wdt: pl.BlockSpec((tm, wdt), lambda i: (i, 0))
    const = lambda r, wdt: pl.BlockSpec((r, wdt), lambda i: (0, 0))
    tab_spec = rows(HEAD_DIM) if carry else const(tm, HEAD_DIM)
    prev_spec = const(8, CONV_W) if carry else rows(CONV_W)
    u_rows = 8 if carry else t
    out_shape = [jax.ShapeDtypeStruct((t, ATT_W), act_dtype)]
    out_shape += [jax.ShapeDtypeStruct((t, SLAB_W), F32)] * 3
    out_shape += [jax.ShapeDtypeStruct((t, KV_W), BF16)] * 4
    out_shape += [jax.ShapeDtypeStruct((t, GATE_PAD), F32), jax.ShapeDtypeStruct((t, CONV_W), act_dtype),
                  jax.ShapeDtypeStruct((u_rows, CONV_W), F32)]
    out_specs = [rows(ATT_W)] + [rows(SLAB_W)] * 3 + [rows(KV_W)] * 4 + [rows(GATE_PAD), rows(CONV_W),
                                                                         const(8, CONV_W) if carry else rows(CONV_W)]
    return pl.pallas_call(
        functools.partial(_proj_kernel, tm=tm, seq_rows=seq_rows, carry=carry),
        grid=(t // tm,),
        in_specs=[rows(D_MODEL), mod_spec, mod_spec, const(1, D_MODEL),
                  pl.BlockSpec(memory_space=pltpu.VMEM),
                  tab_spec, tab_spec, prev_spec, prev_spec, const(CONV_K, CONV_W), const(1, CONV_W),
                  const(1, CONV_W)],
        out_specs=out_specs,
        out_shape=out_shape,
        scratch_shapes=[pltpu.VMEM((8, CONV_W), F32)],
        compiler_params=_cparams("arbitrary"),
        name="proj",
    )(x, shift, scale, norm_g, w, cos, sin, prev1, prev2, conv_w, conv_b, norm_co)


def _cmp_kernel(x_ref, w_ref, o_ref, *, n_j):
    for kv in range(2):
        for g in range(N_KV):
            off = kv * KV_W + g * HEAD_DIM
            xs = jnp.concatenate([x_ref[:, j * SLAB_W + off:j * SLAB_W + off + HEAD_DIM] for j in range(n_j)], axis=1)
            col = (kv * N_KV + g) * 2 * HEAD_DIM
            o_ref[:, col:col + 2 * HEAD_DIM] = _dot(xs.astype(BF16), w_ref[kv])


def _compress_products(x, w, tm):
    m, width = x.shape
    n_j = width // SLAB_W
    n_out = 2 * N_KV * 2 * HEAD_DIM
    tm = min(tm, m)
    return pl.pallas_call(
        functools.partial(_cmp_kernel, n_j=n_j),
        grid=(m // tm,),
        in_specs=[pl.BlockSpec((tm, width), lambda i: (i, 0)),
                  pl.BlockSpec((2, n_j * HEAD_DIM, 2 * HEAD_DIM), lambda i: (0, 0, 0))],
        out_specs=pl.BlockSpec((tm, n_out), lambda i: (i, 0)),
        out_shape=jax.ShapeDtypeStruct((m, n_out), F32),
        compiler_params=_cparams("arbitrary"),
        name="compress",
    )(x, w)


def _r_cols(kv, g):
    return (kv * N_KV + g) * 2 * HEAD_DIM


P_CK = 256
P_WKEYS = WINDOW + Q_BLOCK


def _pattn_kernel(q_ref, gate_ref, rp_ref, cosc_ref, sinc_ref, ks_ref, vs_ref, kw_ref, vw_ref, e3_ref, nao_ref,
                  o_ref, kc_scr, vc_scr, pt_scr, o_scr, *, n_cmp_pad, n_blocks):
    i = pl.program_id(0)
    s0 = i * Q_BLOCK
    nq = Q_BLOCK
    rows = HEADS_PER_KV * nq

    @pl.when(i == 0)
    def _():
        for g in range(N_KV):
            ck = _r_cols(0, g)
            kc = rp_ref[0:n_cmp_pad, ck:ck + HEAD_DIM] + rp_ref[pl.ds(1, n_cmp_pad), ck + HEAD_DIM:ck + 2 * HEAD_DIM]
            kc_scr[:, g * HEAD_DIM:(g + 1) * HEAD_DIM] = _rope(kc, cosc_ref[...], sinc_ref[...]).astype(BF16)
            cv = _r_cols(1, g)
            vc = rp_ref[0:n_cmp_pad, cv:cv + HEAD_DIM] + rp_ref[pl.ds(1, n_cmp_pad), cv + HEAD_DIM:cv + 2 * HEAD_DIM]
            vc_scr[:, g * HEAD_DIM:(g + 1) * HEAD_DIM] = vc.astype(BF16)
        pt_scr[...] = jnp.zeros_like(pt_scr)

    t_rows = s0 + (lax.broadcasted_iota(jnp.int32, (rows, 1), 0) & (nq - 1))
    t_col = s0 + lax.broadcasted_iota(jnp.int32, (nq, 1), 0)

    for g in range(N_KV):
        gs = slice(g * HEAD_DIM, (g + 1) * HEAD_DIM)
        qg = jnp.concatenate([q_ref[:, (g * HEADS_PER_KV + r) * HEAD_DIM:(g * HEADS_PER_KV + r + 1) * HEAD_DIM]
                              for r in range(HEADS_PER_KV)], axis=0)

        c_end = lax.broadcasted_iota(jnp.int32, (rows, n_cmp_pad), 1) * CMP_STRIDE + (CMP_BLOCK - 1)
        p_c = _masked_softmax(_dot_nt(qg, kc_scr[:, gs]), c_end <= t_rows)
        o_c = _dot(p_c.astype(BF16), vc_scr[:, gs])

        p_grp = p_c[0:nq] + p_c[nq:2 * nq] + p_c[2 * nq:3 * nq] + p_c[3 * nq:4 * nq]
        pt_scr[8:8 + n_cmp_pad, :] = p_grp.T
        st = lambda k: pt_scr[pl.ds(7 + k, n_blocks, stride=CHUNKS_PER_SEL), :]
        score = 0.5 * st(0) + st(1) + st(2) + st(3) + 0.5 * st(4)
        b_io = lax.broadcasted_iota(jnp.int32, (n_blocks, nq), 0)
        b_f = b_io.astype(F32)
        t_lane = s0 + lax.broadcasted_iota(jnp.int32, (n_blocks, nq), 1)
        cur = t_lane >> 6
        forced = (b_io == 0) | (b_io == cur) | (b_io == cur - 1)
        valid = b_io * SEL_BLOCK <= t_lane
        score = jnp.where(forced, jnp.inf, jnp.where(valid, score, -jnp.inf))

        def pick(_, c):
            work, sel = c
            m = jnp.max(work, axis=0, keepdims=True)
            idx = jnp.min(jnp.where(work == m, b_f, float(n_blocks)), axis=0, keepdims=True)
            hit = b_f == idx
            return jnp.where(hit, -jnp.inf, work), jnp.where(hit, 1.0, sel)

        _, sel_t = lax.fori_loop(0, min(N_SEL, n_blocks), pick, (score, jnp.zeros((n_blocks, nq), F32)))
        sel_q = sel_t.T.astype(BF16)

        def chunk(c, carry):
            m_i, l_i, acc = carry
            k0 = pl.multiple_of(c * P_CK, P_CK)
            s = _dot_nt(qg, ks_ref[pl.ds(k0, P_CK), gs])
            key = k0 + lax.broadcasted_iota(jnp.int32, (nq, P_CK), 1)
            ok = (_dot(sel_q, e3_ref[c]) > 0.5) & (key <= t_col)
            bias = jnp.where(ok, 0.0, NEG)
            s = s + jnp.concatenate([bias] * HEADS_PER_KV, axis=0)
            m_n = jnp.maximum(m_i, jnp.max(s, axis=-1, keepdims=True))
            alpha = jnp.exp(m_i - m_n)
            p = jnp.exp(s - m_n)
            l_n = alpha * l_i + jnp.sum(p, axis=-1, keepdims=True)
            acc = alpha * acc + _dot(p.astype(BF16), vs_ref[pl.ds(k0, P_CK), gs])
            return m_n, l_n, acc

        n_chunks = (s0 + nq + P_CK - 1) // P_CK
        _, l_s, acc_s = lax.fori_loop(0, n_chunks, chunk, (jnp.full((rows, 1), NEG, F32), jnp.zeros((rows, 1), F32),
                                                           jnp.zeros((rows, HEAD_DIM), F32)))
        o_s = acc_s / jnp.maximum(l_s, TINY)

        w0 = pl.multiple_of(s0, Q_BLOCK)
        j_io = lax.broadcasted_iota(jnp.int32, (rows, P_WKEYS), 1)
        ti = lax.broadcasted_iota(jnp.int32, (rows, P_WKEYS), 0) & (nq - 1)
        m_w = (j_io > ti) & (j_io <= ti + WINDOW) & (j_io >= WINDOW - s0)
        p_w = _masked_softmax(_dot_nt(qg, kw_ref[pl.ds(w0, P_WKEYS), gs]), m_w)
        o_w = _dot(p_w.astype(BF16), vw_ref[pl.ds(w0, P_WKEYS), gs])

        for r in range(HEADS_PER_KV):
            hd = g * HEADS_PER_KV + r
            rs = slice(r * nq, (r + 1) * nq)
            gc = gate_ref[:, hd * N_BRANCH + 0:hd * N_BRANCH + 1]
            gsl = gate_ref[:, hd * N_BRANCH + 1:hd * N_BRANCH + 2]
            gw = gate_ref[:, hd * N_BRANCH + 2:hd * N_BRANCH + 3]
            o_scr[:, hd * HEAD_DIM:(hd + 1) * HEAD_DIM] = gc * o_c[rs] + gsl * o_s[rs] + gw * o_w[rs]

    o_ref[...] = (_rms(o_scr[...]) * nao_ref[...]).astype(o_ref.dtype)


def _prompt_attention(q, gates, rp, cosc, sinc, ks, vs, kw_pad, vw_pad, e3, norm_ao):
    s_len = q.shape[0]
    n_cmp_pad = s_len // CMP_STRIDE
    n_blocks = s_len // SEL_BLOCK
    vmem = pl.BlockSpec(memory_space=pltpu.VMEM)
    rows = lambda wdt: pl.BlockSpec((Q_BLOCK, wdt), lambda i: (i, 0))
    return pl.pallas_call(
        functools.partial(_pattn_kernel, n_cmp_pad=n_cmp_pad, n_blocks=n_blocks),
        grid=(s_len // Q_BLOCK,),
        in_specs=[rows(ATT_W), rows(GATE_PAD), vmem, vmem, vmem, vmem, vmem, vmem, vmem, vmem,
                  pl.BlockSpec((1, ATT_W), lambda i: (0, 0))],
        out_specs=rows(ATT_W),
        out_shape=jax.ShapeDtypeStruct((s_len, ATT_W), BF16),
        scratch_shapes=[pltpu.VMEM((n_cmp_pad, KV_W), BF16), pltpu.VMEM((n_cmp_pad, KV_W), BF16),
                        pltpu.VMEM((n_cmp_pad + 16, Q_BLOCK), F32), pltpu.VMEM((Q_BLOCK, ATT_W), F32)],
        compiler_params=_cparams("arbitrary"),
        name="prompt_attn",
    )(q, gates, rp, cosc, sinc, ks, vs, kw_pad, vw_pad, e3, norm_ao)


def _sattn_kernel(pt_ref, *refs, n_pages, page, n_new, past, wbuf):
    del pt_ref
    r_pages = refs[0:n_pages]
    s_pages = refs[n_pages:2 * n_pages]
    (rnew_ref, q_ref, gate_ref, snew_ref, wst_ref, wnew_ref, cosc_ref, sinc_ref, e_ref, nao_ref,
     o_ref, wout_ref, r_scr, k_scr, v_scr, kw_scr, vw_scr, o_scr) = refs[2 * n_pages:]
    chunks_pp = page // CMP_STRIDE
    n_cmp = n_pages * chunks_pp
    rows = HEADS_PER_KV * n_new
    n_keys = k_scr.shape[0]
    n_wkeys = kw_scr.shape[0]

    for p in range(n_pages):
        r_scr[p * chunks_pp:(p + 1) * chunks_pp, :] = r_pages[p][...]
        k_scr[p * page:(p + 1) * page, :] = s_pages[p][0, :, 0:KV_W].astype(BF16)
        v_scr[p * page:(p + 1) * page, :] = s_pages[p][0, :, KV_W:SLAB_W].astype(BF16)
    r_scr[n_cmp:n_cmp + 8, :] = jnp.concatenate([rnew_ref[0], jnp.zeros((7, r_scr.shape[1]), F32)], axis=0)
    pad = n_keys - past
    k_scr[past:n_keys, :] = jnp.concatenate([snew_ref[:, 0:KV_W], jnp.zeros((pad - n_new, KV_W), F32)], axis=0).astype(BF16)
    v_scr[past:n_keys, :] = jnp.concatenate([snew_ref[:, KV_W:SLAB_W], jnp.zeros((pad - n_new, KV_W), F32)], axis=0).astype(BF16)
    kw_scr[0:wbuf, :] = wst_ref[0, :, 0:KV_W].astype(BF16)
    vw_scr[0:wbuf, :] = wst_ref[0, :, KV_W:SLAB_W].astype(BF16)
    wpad = n_wkeys - wbuf
    kw_scr[wbuf:n_wkeys, :] = jnp.concatenate([wnew_ref[:, 0:KV_W], jnp.zeros((wpad - n_new, KV_W), F32)], axis=0).astype(BF16)
    vw_scr[wbuf:n_wkeys, :] = jnp.concatenate([wnew_ref[:, KV_W:SLAB_W], jnp.zeros((wpad - n_new, KV_W), F32)], axis=0).astype(BF16)

    wout_ref[0, 0:wbuf - n_new, :] = wst_ref[0, n_new:wbuf, :]
    wout_ref[0, wbuf - n_new:wbuf, :] = wnew_ref[...]

    tok = lax.broadcasted_iota(jnp.int32, (rows, 1), 0) & (n_new - 1)
    t_rows = past + tok
    lane = lax.broadcasted_iota(jnp.int32, (n_new, HEAD_DIM), 1)
    n_pb = past // SEL_BLOCK

    for g in range(N_KV):
        gs = slice(g * HEAD_DIM, (g + 1) * HEAD_DIM)
        qg = jnp.concatenate([q_ref[:, (g * HEADS_PER_KV + r) * HEAD_DIM:(g * HEADS_PER_KV + r + 1) * HEAD_DIM]
                              for r in range(HEADS_PER_KV)], axis=0).astype(BF16)

        ck, cv = _r_cols(0, g), _r_cols(1, g)
        kc = r_scr[0:n_cmp, ck:ck + HEAD_DIM] + r_scr[pl.ds(1, n_cmp), ck + HEAD_DIM:ck + 2 * HEAD_DIM]
        kc = _rope(kc, cosc_ref[...], sinc_ref[...]).astype(BF16)
        vc = (r_scr[0:n_cmp, cv:cv + HEAD_DIM] + r_scr[pl.ds(1, n_cmp), cv + HEAD_DIM:cv + 2 * HEAD_DIM]).astype(BF16)

        c_end = lax.broadcasted_iota(jnp.int32, (rows, n_cmp), 1) * CMP_STRIDE + (CMP_BLOCK - 1)
        p_c = _masked_softmax(_dot_nt(qg, kc), c_end <= t_rows)
        o_c = _dot(p_c.astype(BF16), vc)

        p_grp = p_c[0:n_new]
        for r in range(1, HEADS_PER_KV):
            p_grp = p_grp + p_c[r * n_new:(r + 1) * n_new]
        pch = 0.5 * (p_grp + jnp.where(lane >= 1, pltpu.roll(p_grp, 1, axis=1), 0.0))
        score = pch
        for k in range(1, CHUNKS_PER_SEL):
            score = score + pltpu.roll(pch, HEAD_DIM - k, axis=1)
        blk = lane >> 2
        t_tok = past + lax.broadcasted_iota(jnp.int32, (n_new, HEAD_DIM), 0)
        cur = t_tok >> 6
        forced = (blk == 0) | (blk == cur) | (blk == cur - 1)
        score = jnp.where(forced, jnp.inf, score)
        ahead = jnp.zeros((n_new, HEAD_DIM), F32)
        for k in range(1, n_pb):
            other = pltpu.roll(score, CHUNKS_PER_SEL * k, axis=1)
            wins = (other > score) | ((other == score) & (blk >= k))
            ahead = ahead + jnp.where(wins, 1.0, 0.0)
        sel = jnp.where(((lane & (CHUNKS_PER_SEL - 1)) == 0) & (ahead < N_SEL - 1), 1.0, 0.0).astype(BF16)
        sel_keys = _dot(sel, e_ref[...])
        new_ok = lax.broadcasted_iota(jnp.int32, (n_new, pad), 1) <= lax.broadcasted_iota(jnp.int32, (n_new, pad), 0)
        bias = jnp.concatenate([jnp.where(sel_keys > 0.5, 0.0, NEG), jnp.where(new_ok, 0.0, NEG)], axis=1)
        bias = jnp.concatenate([bias] * HEADS_PER_KV, axis=0)
        s_s = _dot_nt(qg, k_scr[:, gs]) + bias
        e_s = jnp.exp(s_s - jnp.max(s_s, axis=-1, keepdims=True))
        p_s = e_s / jnp.maximum(jnp.sum(e_s, axis=-1, keepdims=True), TINY)
        o_s = _dot(p_s.astype(BF16), v_scr[:, gs])

        j_io = lax.broadcasted_iota(jnp.int32, (rows, n_wkeys), 1)
        m_w = (((j_io < wbuf) & (j_io > tok + (wbuf - WINDOW)) & (j_io >= wbuf - past))
               | ((j_io >= wbuf) & (j_io - wbuf <= tok)))
        p_w = _masked_softmax(_dot_nt(qg, kw_scr[:, gs]), m_w)
        o_w = _dot(p_w.astype(BF16), vw_scr[:, gs])

        for r in range(HEADS_PER_KV):
            hd = g * HEADS_PER_KV + r
            rs = slice(r * n_new, (r + 1) * n_new)
            gc = gate_ref[:, hd * N_BRANCH + 0:hd * N_BRANCH + 1]
            gsl = gate_ref[:, hd * N_BRANCH + 1:hd * N_BRANCH + 2]
            gw = gate_ref[:, hd * N_BRANCH + 2:hd * N_BRANCH + 3]
            o_scr[:, hd * HEAD_DIM:(hd + 1) * HEAD_DIM] = gc * o_c[rs] + gsl * o_s[rs] + gw * o_w[rs]

    o_ref[...] = _rms(o_scr[...]) * nao_ref[...]


def _sample_attention(page_table, r_all, r_new, q, gates, slc_cache, slc_new, win_state, win_new, cosc, sinc, e_mat,
                      norm_ao, *, n_new):
    n_seq, n_pages = page_table.shape
    page = slc_cache.shape[1]
    past = n_pages * page
    wbuf = win_state.shape[1]
    chunks_pp = page // CMP_STRIDE
    n_cmp = n_pages * chunks_pp
    assert n_cmp == HEAD_DIM and n_new == 8 and past % SEL_BLOCK == 0 and n_new <= SEL_BLOCK
    assert (past + n_new - 1) // SEL_BLOCK == past // SEL_BLOCK and wbuf == WINDOW and past >= WINDOW
    n_keys = past + HEAD_DIM
    n_wkeys = wbuf + HEAD_DIM
    r_w = r_all.shape[1]

    in_specs = [pl.BlockSpec((chunks_pp, r_w), functools.partial(lambda p, b, pt: (pt[b, p], 0), p))
                for p in range(n_pages)]
    in_specs += [pl.BlockSpec((1, page, SLAB_W), functools.partial(lambda p, b, pt: (pt[b, p], 0, 0), p))
                 for p in range(n_pages)]
    seq_rows = lambda wdt: pl.BlockSpec((n_new, wdt), lambda b, pt: (b, 0))
    const = lambda shape: pl.BlockSpec(shape, lambda b, pt: (0,) * len(shape))
    in_specs += [pl.BlockSpec((1, 1, r_w), lambda b, pt: (b, 0, 0)), seq_rows(ATT_W), seq_rows(GATE_PAD),
                 seq_rows(SLAB_W), pl.BlockSpec((1, wbuf, SLAB_W), lambda b, pt: (b, 0, 0)), seq_rows(SLAB_W),
                 const((n_cmp, HEAD_DIM)), const((n_cmp, HEAD_DIM)), const((HEAD_DIM, past)), const((1, ATT_W))]
    grid_spec = pltpu.PrefetchScalarGridSpec(
        num_scalar_prefetch=1,
        grid=(n_seq,),
        in_specs=in_specs,
        out_specs=[seq_rows(ATT_W), pl.BlockSpec((1, wbuf, SLAB_W), lambda b, pt: (b, 0, 0))],
        scratch_shapes=[pltpu.VMEM((n_cmp + 8, r_w), F32), pltpu.VMEM((n_keys, KV_W), BF16),
                        pltpu.VMEM((n_keys, KV_W), BF16), pltpu.VMEM((n_wkeys, KV_W), BF16),
                        pltpu.VMEM((n_wkeys, KV_W), BF16), pltpu.VMEM((n_new, ATT_W), F32)],
    )
    return pl.pallas_call(
        functools.partial(_sattn_kernel, n_pages=n_pages, page=page, n_new=n_new, past=past, wbuf=wbuf),
        grid_spec=grid_spec,
        out_shape=[jax.ShapeDtypeStruct((n_seq * n_new, ATT_W), F32),
                   jax.ShapeDtypeStruct((n_seq, wbuf, SLAB_W), F32)],
        compiler_params=_cparams("arbitrary"),
        name="sample_attn",
    )(page_table, *([r_all] * n_pages), *([slc_cache] * n_pages), r_new, q, gates, slc_new, win_state, win_new,
      cosc, sinc, e_mat, norm_ao)


def _outproj_kernel(x_ref, oa_ref, oc_ref, gt_ref, w_ref, o_ref):
    y = _dot(oa_ref[...].astype(BF16), w_ref[0:ATT_W, :]) + _dot(oc_ref[...].astype(BF16), w_ref[ATT_W:D_MODEL, :])
    o_ref[...] = x_ref[...] + (1.0 + gt_ref[...]) * y


def _outproj(x, oa, oc, gate, w, tm=512):
    t = x.shape[0]
    mrows = gate.shape[0]
    mod_spec = (pl.BlockSpec((1, D_MODEL), lambda i: (0, 0)) if mrows == 1
                else pl.BlockSpec((tm, D_MODEL), lambda i: (i, 0)))
    rows = lambda wdt: pl.BlockSpec((tm, wdt), lambda i: (i, 0))
    return pl.pallas_call(
        _outproj_kernel,
        grid=(t // tm,),
        in_specs=[rows(D_MODEL), rows(ATT_W), rows(CONV_W), mod_spec, pl.BlockSpec(memory_space=pltpu.VMEM)],
        out_specs=rows(D_MODEL),
        out_shape=jax.ShapeDtypeStruct((t, D_MODEL), F32),
        compiler_params=_cparams("arbitrary"),
        name="outproj",
    )(x, oa, oc, gate, w)


def _rope_tables(pos):
    half = HEAD_DIM // 2
    inv = ROPE_THETA ** (-jnp.arange(half, dtype=F32) * 2.0 / HEAD_DIM)
    ang = pos.astype(F32)[:, None] * inv[None, :]
    cos, sin = jnp.cos(ang), jnp.sin(ang)
    return jnp.concatenate([cos, cos], axis=1), jnp.concatenate([-sin, sin], axis=1)


def _pack_w_in(w_in):
    off_kv = ATT_W
    off_g = off_kv + 3 * SLAB_W
    off_c = off_g + N_HEADS * N_BRANCH
    gate_cols = jnp.pad(w_in[:, off_g:off_c], ((0, 0), (0, GATE_PAD - N_HEADS * N_BRANCH)))
    return jnp.concatenate([w_in[:, 0:off_g], w_in[:, off_c:off_c + 3 * CONV_W], gate_cols], axis=1).astype(BF16)


def _pack_w_cmp(w_ck, w_cv, n_j):
    def one(w):
        lo = w[0:n_j].reshape(n_j * HEAD_DIM, HEAD_DIM)
        hi = w[CMP_STRIDE:CMP_STRIDE + n_j].reshape(n_j * HEAD_DIM, HEAD_DIM)
        return jnp.concatenate([lo, hi], axis=1)
    return jnp.stack([one(w_ck), one(w_cv)]).astype(BF16)


def kernel(x_prompt, x_sample, c_prompt, c_sample, cache_cmp_kv, cache_slc_kv, state_win_kv, state_conv, page_table,
           w_ada, b_ada, norm_ffn1, ffn1_gate, ffn1_up, ffn1_down, norm_mix, w_in, w_cmp_k, w_cmp_v, conv_w, conv_b,
           norm_att_out, norm_conv_out, w_out, norm_ffn2, ffn2_gate, ffn2_up, ffn2_down, norm_final):
    n_p, s_len, _ = x_prompt.shape
    n_seq, n_new, _ = x_sample.shape
    depth = w_ada.shape[0]
    assert n_p == 1 and depth == 1
    n_pages = page_table.shape[1]
    page = cache_slc_kv.shape[2]
    n_phys = cache_slc_kv.shape[1]
    past = n_pages * page
    wbuf = state_win_kv.shape[2]
    keep_p = min(WINDOW, s_len)
    t_s = n_seq * n_new
    l = 0

    c_all = jnp.concatenate([c_sample, c_prompt, jnp.zeros((8 - n_p, D_MODEL), F32)], axis=0)
    mod = _ada(c_all, w_ada[l], b_ada[l])
    mod_p = [mod[n_seq:n_seq + 1, k * D_MODEL:(k + 1) * D_MODEL] for k in range(N_MOD)]
    mod_s = [jnp.repeat(mod[0:n_seq, k * D_MODEL:(k + 1) * D_MODEL], n_new, axis=0) for k in range(N_MOD)]

    row = lambda v: v.reshape(1, -1)
    f1 = (ffn1_gate[l].astype(BF16), ffn1_up[l].astype(BF16), ffn1_down[l].astype(BF16))
    f2 = (ffn2_gate[l].astype(BF16), ffn2_up[l].astype(BF16), ffn2_down[l].astype(BF16))
    w_proj = _pack_w_in(w_in[l])
    w_o = w_out[l].astype(BF16)
    nfin = row(norm_final)

    xp = x_prompt.reshape(s_len, D_MODEL)
    xs = x_sample.reshape(t_s, D_MODEL)

    xp = _ffn(xp, mod_p[0], mod_p[1], mod_p[2], row(norm_ffn1[l]), nfin, *f1, final_norm=False)
    xs = _ffn(xs, mod_s[0], mod_s[1], mod_s[2], row(norm_ffn1[l]), nfin, *f1, final_norm=False)

    cos_p, sin_p = _rope_tables(jnp.arange(s_len))
    cos_s, sin_s = _rope_tables(past + jnp.arange(n_new))
    tm_s = 256
    cos_s, sin_s = jnp.tile(cos_s, (tm_s // n_new, 1)), jnp.tile(sin_s, (tm_s // n_new, 1))
    zero8 = jnp.zeros((8, CONV_W), F32)
    conv_args = (conv_w[l], row(conv_b[l]), row(norm_conv_out[l]))
    (q_p, cmp_p, slc_p, win_p, ksb_p, vsb_p, kwb_p, vwb_p, gate_p, ocn_p, utail_p) = _proj(
        xp, mod_p[3], mod_p[4], row(norm_mix[l]), w_proj, cos_p, sin_p, zero8, zero8, *conv_args,
        carry=True, seq_rows=s_len, act_dtype=BF16)
    prev1 = jnp.repeat(state_conv[l][:, CONV_K - 2], n_new, axis=0)
    prev2 = jnp.repeat(state_conv[l][:, CONV_K - 3], n_new, axis=0)
    (q_s, cmp_s, slc_s, win_s, _, _, _, _, gate_s, ocn_s, u_s) = _proj(
        xs, mod_s[3], mod_s[4], row(norm_mix[l]), w_proj, cos_s, sin_s, prev1, prev2, *conv_args,
        carry=False, seq_rows=n_new, act_dtype=F32, tm=tm_s)

    n_j = CMP_STRIDE
    w_c = _pack_w_cmp(w_cmp_k[l], w_cmp_v[l], n_j)
    r_p = _compress_products(cmp_p.reshape(s_len // n_j, n_j * SLAB_W), w_c, tm=256)
    r_cache = _compress_products(cache_cmp_kv[l].reshape(n_phys * (page // n_j), n_j * SLAB_W), w_c, tm=256)
    r_new = _compress_products(cmp_s.reshape(n_seq, n_new * SLAB_W), _pack_w_cmp(w_cmp_k[l], w_cmp_v[l], n_new),
                               tm=n_seq)

    n_cmp_pad = s_len // CMP_STRIDE
    cosc, sinc = _rope_tables(jnp.arange(n_cmp_pad) * CMP_STRIDE + (CMP_BLOCK - 1))
    r_p = jnp.pad(r_p, ((0, 8), (0, 0)))
    band = ((WINDOW, 0), (0, 0))
    n_blocks = s_len // SEL_BLOCK
    key_blk = jnp.arange(s_len) // SEL_BLOCK
    e3 = (jnp.arange(n_blocks)[:, None] == key_blk[None, :]).astype(BF16)
    e3 = e3.reshape(n_blocks, s_len // P_CK, P_CK).transpose(1, 0, 2)
    oa_p = _prompt_attention(q_p, gate_p, r_p, cosc, sinc, ksb_p, vsb_p, jnp.pad(kwb_p, band), jnp.pad(vwb_p, band),
                             e3, row(norm_att_out[l]))

    n_cmp_s = past // CMP_STRIDE
    cosc_s, sinc_s = _rope_tables(jnp.arange(n_cmp_s) * CMP_STRIDE + (CMP_BLOCK - 1))
    e_s =(jnp.arange(HEAD_DIM)[:, None] == (jnp.arange(past) // SEL_BLOCK * CHUNKS_PER_SEL)[None, :]).astype(BF16)
    oa_s, win_new_state = _sample_attention(
        page_table, r_cache, r_new.reshape(n_seq, 1, -1), q_s, gate_s,
        cache_slc_kv[l].reshape(n_phys, page, SLAB_W), slc_s, state_win_kv[l].reshape(n_seq, wbuf, SLAB_W), win_s,
        cosc_s, sinc_s, e_s, row(norm_att_out[l]), n_new=n_new)

    xp = _outproj(xp, oa_p, ocn_p, mod_p[5], w_o)
    xs = _outproj(xs, oa_s, ocn_s, mod_s[5], w_o)
    yp = _ffn(xp, mod_p[6], mod_p[7], mod_p[8], row(norm_ffn2[l]), nfin, *f2, final_norm=True)
    ys = _ffn(xs, mod_s[6], mod_s[7], mod_s[8], row(norm_ffn2[l]), nfin, *f2, final_norm=True)

    kv6 = lambda a, n, s: a.reshape(1, n, s, 2, N_KV, HEAD_DIM)
    return (yp.reshape(n_p, s_len, D_MODEL), ys.reshape(n_seq, n_new, D_MODEL),
            kv6(cmp_p, n_p, s_len), kv6(slc_p, n_p, s_len), kv6(win_p[s_len - keep_p:], n_p, keep_p),
            utail_p[8 - (CONV_K - 1):].reshape(1, n_p, CONV_K - 1, CONV_W),
            kv6(cmp_s, n_seq, n_new), kv6(slc_s, n_seq, n_new), kv6(win_new_state, n_seq, wbuf),
            u_s.reshape(n_seq, n_new, CONV_W)[:, n_new - (CONV_K - 1):].reshape(1, n_seq, CONV_K - 1, CONV_W))
```

```python
import functools

import jax
import jax.numpy as jnp
from jax import lax
from jax.experimental import pallas as pl
from jax.experimental.pallas import tpu as pltpu

F32 = jnp.float32
BF16 = jnp.bfloat16

D_MODEL = 2048
HEAD_DIM = 128
N_HEADS = 8
N_KV = 2
HEADS_PER_KV = N_HEADS // N_KV
ATT_W = N_HEADS * HEAD_DIM
KV_W = N_KV * HEAD_DIM
CONV_W = D_MODEL - ATT_W
CONV_K = 3
CMP_BLOCK = 32
CMP_STRIDE = 16
SEL_BLOCK = 64
N_SEL = 16
WINDOW = 512
Q_BLOCK = 128
N_BRANCH = 3
N_MOD = 9
ROPE_THETA = 10000.0
EPS = 1e-6
NEG = -1e30
TINY = 1e-30
SCALE = HEAD_DIM ** -0.5
SLAB_W = 2 * KV_W
KV_ROWS = 2 * N_KV
GATE_PAD = 128
CHUNKS_PER_SEL = SEL_BLOCK // CMP_STRIDE

VMEM_LIMIT = 56 * 1024 * 1024


def _cparams(*sem):
    return pltpu.CompilerParams(dimension_semantics=sem, vmem_limit_bytes=VMEM_LIMIT)


def _dot(a, b):
    return jnp.dot(a, b, preferred_element_type=F32)


def _dot_nt(a, b):
    return lax.dot_general(a, b, (((1,), (1,)), ((), ())), preferred_element_type=F32)


def _rms(x):
    return x * lax.rsqrt(jnp.mean(x * x, axis=-1, keepdims=True) + EPS)


def _silu(x):
    return x * jax.nn.sigmoid(x)


def _rope(x, cos, sin_signed):
    return x * cos + pltpu.roll(x, HEAD_DIM // 2, axis=1) * sin_signed


def _masked_softmax(s, mask):
    s = jnp.where(mask, s, NEG)
    m = jnp.max(s, axis=-1, keepdims=True)
    e = jnp.where(mask, jnp.exp(s - m), 0.0)
    return e / jnp.maximum(jnp.sum(e, axis=-1, keepdims=True), TINY)


def _ada_kernel(c_ref, w_ref, b_ref, o_ref):
    a = _silu(c_ref[...]).astype(BF16)
    o_ref[...] = _dot(a, w_ref[...].astype(BF16)) + b_ref[...]


def _ada(c, w, b, tn=1024):
    m, n = c.shape[0], w.shape[1]
    return pl.pallas_call(
        _ada_kernel,
        grid=(n // tn,),
        in_specs=[pl.BlockSpec((m, D_MODEL), lambda j: (0, 0)),
                  pl.BlockSpec((D_MODEL, tn), lambda j: (0, j)),
                  pl.BlockSpec((1, tn), lambda j: (0, j))],
        out_specs=pl.BlockSpec((m, tn), lambda j: (0, j)),
        out_shape=jax.ShapeDtypeStruct((m, n), F32),
        compiler_params=_cparams("arbitrary"),
        name="ada",
    )(c, w, b.reshape(1, n))


def _ffn_kernel(x_ref, sh_ref, sc_ref, gt_ref, ng_ref, nf_ref, wg_ref, wu_ref, wd_ref, o_ref, h_scr, acc_scr,
                *, n_f, final_norm):
    j = pl.program_id(1)

    @pl.when(j == 0)
    def _():
        h = _rms(x_ref[...]) * ng_ref[...] * (1.0 + sc_ref[...]) + sh_ref[...]
        h_scr[...] = h.astype(BF16)
        acc_scr[...] = jnp.zeros_like(acc_scr)

    h = h_scr[...]
    a = (_silu(_dot(h, wg_ref[...])) * _dot(h, wu_ref[...])).astype(BF16)
    acc_scr[...] += _dot(a, wd_ref[...])

    @pl.when(j == n_f - 1)
    def _():
        out = x_ref[...] + 0.5 * (1.0 + gt_ref[...]) * acc_scr[...]
        if final_norm:
            out = _rms(out) * nf_ref[...]
        o_ref[...] = out


def _ffn(x, shift, scale, gate, norm_g, norm_final, wg, wu, wd, *, final_norm, tm=512, tf=512):
    t = x.shape[0]
    d_ff = wg.shape[1]
    n_f = d_ff // tf
    mrows = shift.shape[0]
    if mrows != 1:
        tm = tm // 2
    mod_spec = (pl.BlockSpec((1, D_MODEL), lambda i, j: (0, 0)) if mrows == 1
                else pl.BlockSpec((tm, D_MODEL), lambda i, j: (i, 0), pipeline_mode=pl.Buffered(1)))
    row_spec = pl.BlockSpec((tm, D_MODEL), lambda i, j: (i, 0))
    vec_spec = pl.BlockSpec((1, D_MODEL), lambda i, j: (0, 0))
    return pl.pallas_call(
        functools.partial(_ffn_kernel, n_f=n_f, final_norm=final_norm),
        grid=(t // tm, n_f),
        in_specs=[row_spec, mod_spec, mod_spec, mod_spec, vec_spec, vec_spec,
                  pl.BlockSpec((D_MODEL, tf), lambda i, j: (0, j)),
                  pl.BlockSpec((D_MODEL, tf), lambda i, j: (0, j)),
                  pl.BlockSpec((tf, D_MODEL), lambda i, j: (j, 0))],
        out_specs=row_spec,
        out_shape=jax.ShapeDtypeStruct((t, D_MODEL), F32),
        scratch_shapes=[pltpu.VMEM((tm, D_MODEL), BF16), pltpu.VMEM((tm, D_MODEL), F32)],
        compiler_params=_cparams("arbitrary", "arbitrary"),
        name="ffn",
    )(x, shift, scale, gate, norm_g, norm_final, wg, wu, wd)


PW_Q = 0
PW_KV = PW_Q + ATT_W
PW_CONV = PW_KV + 3 * SLAB_W
PW_GATE = PW_CONV + 3 * CONV_W
PW_TOTAL = PW_GATE + GATE_PAD


def _proj_kernel(x_ref, sh_ref, sc_ref, ng_ref, w_ref, cos_ref, sin_ref, p1_ref, p2_ref, cw_ref, cb_ref, nco_ref,
                 q_ref, cmp_ref, slc_ref, win_ref, ksb_ref, vsb_ref, kwb_ref, vwb_ref, gate_ref, ocn_ref, u_ref,
                 carry_scr, *, tm, seq_rows, carry):
    i = pl.program_id(0)
    h = (_rms(x_ref[...]) * ng_ref[...] * (1.0 + sc_ref[...]) + sh_ref[...]).astype(BF16)
    cos, sin = cos_ref[...], sin_ref[...]

    pq = _dot(h, w_ref[:, PW_Q:PW_KV])
    for hd in range(N_HEADS):
        blk = _rope(pq[:, hd * HEAD_DIM:(hd + 1) * HEAD_DIM], cos, sin) * SCALE
        q_ref[:, hd * HEAD_DIM:(hd + 1) * HEAD_DIM] = blk.astype(q_ref.dtype)

    pkv = _dot(h, w_ref[:, PW_KV:PW_CONV])
    for slab, (o_ref, kb_ref, vb_ref) in enumerate(((cmp_ref, None, None), (slc_ref, ksb_ref, vsb_ref),
                                                    (win_ref, kwb_ref, vwb_ref))):
        base = slab * SLAB_W
        for g in range(N_KV):
            gs = slice(g * HEAD_DIM, (g + 1) * HEAD_DIM)
            k = pkv[:, base + g * HEAD_DIM:base + (g + 1) * HEAD_DIM]
            v = pkv[:, base + KV_W + g * HEAD_DIM:base + KV_W + (g + 1) * HEAD_DIM]
            if kb_ref is not None:
                k = _rope(k, cos, sin)
                kb_ref[:, gs] = k.astype(BF16)
                vb_ref[:, gs] = v.astype(BF16)
            o_ref[pl.ds(g, tm, stride=KV_ROWS), :] = k
            o_ref[pl.ds(N_KV + g, tm, stride=KV_ROWS), :] = v

    gate_ref[...] = jax.nn.sigmoid(_dot(h, w_ref[:, PW_GATE:PW_TOTAL]))

    pc = _dot(h, w_ref[:, PW_CONV:PW_GATE])
    u = pc[:, 0:CONV_W] * pc[:, 2 * CONV_W:3 * CONV_W]
    c_out = pc[:, CONV_W:2 * CONV_W]
    row = lax.broadcasted_iota(jnp.int32, (tm, CONV_W), 0)
    if carry:
        @pl.when(i == 0)
        def _():
            carry_scr[...] = jnp.zeros_like(carry_scr)
        prev1 = carry_scr[7:8, :]
        prev2 = carry_scr[6:7, :]
        rs = row
    else:
        prev1 = p1_ref[...]
        prev2 = p2_ref[...]
        rs = row & (seq_rows - 1)
    um1 = jnp.where(rs >= 1, pltpu.roll(u, 1, axis=0), prev1)
    um2 = jnp.where(rs >= 2, pltpu.roll(u, 2, axis=0), jnp.where(rs == 1, prev1, prev2))
    y = um2 * cw_ref[0:1, :] + um1 * cw_ref[1:2, :] + u * cw_ref[2:3, :] + cb_ref[...]
    ocn_ref[...] = (_rms(c_out * y) * nco_ref[...]).astype(ocn_ref.dtype)
    if carry:
        carry_scr[...] = u[tm - 8:tm, :]
        u_ref[...] = u[tm - 8:tm, :]
    else:
        u_ref[...] = u


def _proj(x, shift, scale, norm_g, w, cos, sin, prev1, prev2, conv_w, conv_b, norm_co, *, carry, seq_rows, act_dtype,
          tm=256):
    t = x.shape[0]
    mrows = shift.shape[0]
    mod_spec = (pl.BlockSpec((1, D_MODEL), lambda i: (0, 0)) if mrows == 1
                else pl.BlockSpec((tm, D_MODEL), lambda i: (i, 0)))
    rows = lambda wdt: pl.BlockSpec((tm, wdt), lambda i: (i, 0))
    const = lambda r, wdt: pl.BlockSpec((r, wdt), lambda i: (0, 0))
    tab_spec = rows(HEAD_DIM) if carry else const(tm, HEAD_DIM)
    prev_spec = const(8, CONV_W) if carry else rows(CONV_W)
    u_rows = 8 if carry else t
    out_shape = [jax.ShapeDtypeStruct((t, ATT_W), act_dtype)]
    out_shape += [jax.ShapeDtypeStruct((t * KV_ROWS, HEAD_DIM), F32)] * 3
    out_shape += [jax.ShapeDtypeStruct((t, KV_W), BF16)] * 4
    out_shape += [jax.ShapeDtypeStruct((t, GATE_PAD), F32), jax.ShapeDtypeStruct((t, CONV_W), act_dtype),
                  jax.ShapeDtypeStruct((u_rows, CONV_W), F32)]
    lin_spec = pl.BlockSpec((tm * KV_ROWS, HEAD_DIM), lambda i: (i, 0))
    out_specs = [rows(ATT_W)] + [lin_spec] * 3 + [rows(KV_W)] * 4 + [rows(GATE_PAD), rows(CONV_W),
                                                                     const(8, CONV_W) if carry else rows(CONV_W)]
    return pl.pallas_call(
        functools.partial(_proj_kernel, tm=tm, seq_rows=seq_rows, carry=carry),
        grid=(t // tm,),
        in_specs=[rows(D_MODEL), mod_spec, mod_spec, const(1, D_MODEL),
                  pl.BlockSpec(memory_space=pltpu.VMEM),
                  tab_spec, tab_spec, prev_spec, prev_spec, const(CONV_K, CONV_W), const(1, CONV_W),
                  const(1, CONV_W)],
        out_specs=out_specs,
        out_shape=out_shape,
        scratch_shapes=[pltpu.VMEM((8, CONV_W), F32)],
        compiler_params=_cparams("arbitrary"),
        name="proj",
    )(x, shift, scale, norm_g, w, cos, sin, prev1, prev2, conv_w, conv_b, norm_co)


def _cmp_kernel(x_ref, w_ref, o_ref, *, n_j, tm):
    pitch = n_j * KV_ROWS
    for kv in range(2):
        for g in range(N_KV):
            xs = jnp.concatenate([x_ref[pl.ds(j * KV_ROWS + kv * N_KV + g, tm, stride=pitch), :] for j in range(n_j)],
                                 axis=1)
            col = (kv * N_KV + g) * 2 * HEAD_DIM
            o_ref[:, col:col + 2 * HEAD_DIM] = _dot(xs.astype(BF16), w_ref[kv])


def _compress_products(x, w, n_j, tm):
    pitch = n_j * KV_ROWS
    m = x.shape[0] // pitch
    n_out = 2 * N_KV * 2 * HEAD_DIM
    tm = min(tm, m)
    return pl.pallas_call(
        functools.partial(_cmp_kernel, n_j=n_j, tm=tm),
        grid=(m // tm,),
        in_specs=[pl.BlockSpec((tm * pitch, HEAD_DIM), lambda i: (i, 0)),
                  pl.BlockSpec((2, n_j * HEAD_DIM, 2 * HEAD_DIM), lambda i: (0, 0, 0))],
        out_specs=pl.BlockSpec((tm, n_out), lambda i: (i, 0)),
        out_shape=jax.ShapeDtypeStruct((m, n_out), F32),
        compiler_params=_cparams("arbitrary"),
        name="compress",
    )(x, w)


def _r_cols(kv, g):
    return (kv * N_KV + g) * 2 * HEAD_DIM


P_CK = 256
P_WKEYS = WINDOW + Q_BLOCK


def _pattn_kernel(q_ref, gate_ref, rp_ref, cosc_ref, sinc_ref, ks_ref, vs_ref, kw_ref, vw_ref, e3_ref, nao_ref,
                  o_ref, kc_scr, vc_scr, pt_scr, o_scr, *, n_cmp_pad, n_blocks):
    i = pl.program_id(0)
    s0 = i * Q_BLOCK
    nq = Q_BLOCK
    rows = HEADS_PER_KV * nq

    @pl.when(i == 0)
    def _():
        for g in range(N_KV):
            ck = _r_cols(0, g)
            kc = rp_ref[0:n_cmp_pad, ck:ck + HEAD_DIM] + rp_ref[pl.ds(1, n_cmp_pad), ck + HEAD_DIM:ck + 2 * HEAD_DIM]
            kc_scr[:, g * HEAD_DIM:(g + 1) * HEAD_DIM] = _rope(kc, cosc_ref[...], sinc_ref[...]).astype(BF16)
            cv = _r_cols(1, g)
            vc = rp_ref[0:n_cmp_pad, cv:cv + HEAD_DIM] + rp_ref[pl.ds(1, n_cmp_pad), cv + HEAD_DIM:cv + 2 * HEAD_DIM]
            vc_scr[:, g * HEAD_DIM:(g + 1) * HEAD_DIM] = vc.astype(BF16)
        pt_scr[...] = jnp.zeros_like(pt_scr)

    t_rows = s0 + (lax.broadcasted_iota(jnp.int32, (rows, 1), 0) & (nq - 1))
    t_col = s0 + lax.broadcasted_iota(jnp.int32, (nq, 1), 0)

    for g in range(N_KV):
        gs = slice(g * HEAD_DIM, (g + 1) * HEAD_DIM)
        qg = jnp.concatenate([q_ref[:, (g * HEADS_PER_KV + r) * HEAD_DIM:(g * HEADS_PER_KV + r + 1) * HEAD_DIM]
                              for r in range(HEADS_PER_KV)], axis=0)

        c_end = lax.broadcasted_iota(jnp.int32, (rows, n_cmp_pad), 1) * CMP_STRIDE + (CMP_BLOCK - 1)
        p_c = _masked_softmax(_dot_nt(qg, kc_scr[:, gs]), c_end <= t_rows)
        o_c = _dot(p_c.astype(BF16), vc_scr[:, gs])

        p_grp = p_c[0:nq] + p_c[nq:2 * nq] + p_c[2 * nq:3 * nq] + p_c[3 * nq:4 * nq]
        pt_scr[8:8 + n_cmp_pad, :] = p_grp.T
        st = lambda k: pt_scr[pl.ds(7 + k, n_blocks, stride=CHUNKS_PER_SEL), :]
        score = 0.5 * st(0) + st(1) + st(2) + st(3) + 0.5 * st(4)
        b_io = lax.broadcasted_iota(jnp.int32, (n_blocks, nq), 0)
        b_f = b_io.astype(F32)
        t_lane = s0 + lax.broadcasted_iota(jnp.int32, (n_blocks, nq), 1)
        cur = t_lane >> 6
        forced = (b_io == 0) | (b_io == cur) | (b_io == cur - 1)
        valid = b_io * SEL_BLOCK <= t_lane
        score = jnp.where(forced, jnp.inf, jnp.where(valid, score, -jnp.inf))

        def pick(_, c):
            work, sel = c
            m = jnp.max(work, axis=0, keepdims=True)
            idx = jnp.min(jnp.where(work == m, b_f, float(n_blocks)), axis=0, keepdims=True)
            hit = b_f == idx
            return jnp.where(hit, -jnp.inf, work), jnp.where(hit, 1.0, sel)

        _, sel_t = lax.fori_loop(0, min(N_SEL, n_blocks), pick, (score, jnp.zeros((n_blocks, nq), F32)))
        sel_q = sel_t.T.astype(BF16)

        def chunk(c, carry):
            m_i, l_i, acc = carry
            k0 = pl.multiple_of(c * P_CK, P_CK)
            s = _dot_nt(qg, ks_ref[pl.ds(k0, P_CK), gs])
            key = k0 + lax.broadcasted_iota(jnp.int32, (nq, P_CK), 1)
            ok = (_dot(sel_q, e3_ref[c]) > 0.5) & (key <= t_col)
            bias = jnp.where(ok, 0.0, NEG)
            s = s + jnp.concatenate([bias] * HEADS_PER_KV, axis=0)
            m_n = jnp.maximum(m_i, jnp.max(s, axis=-1, keepdims=True))
            alpha = jnp.exp(m_i - m_n)
            p = jnp.exp(s - m_n)
            l_n = alpha * l_i + jnp.sum(p, axis=-1, keepdims=True)
            acc = alpha * acc + _dot(p.astype(BF16), vs_ref[pl.ds(k0, P_CK), gs])
            return m_n, l_n, acc

        n_chunks = (s0 + nq + P_CK - 1) // P_CK
        _, l_s, acc_s = lax.fori_loop(0, n_chunks, chunk, (jnp.full((rows, 1), NEG, F32), jnp.zeros((rows, 1), F32),
                                                           jnp.zeros((rows, HEAD_DIM), F32)))
        o_s = acc_s / jnp.maximum(l_s, TINY)

        w0 = pl.multiple_of(s0, Q_BLOCK)
        j_io = lax.broadcasted_iota(jnp.int32, (rows, P_WKEYS), 1)
        ti = lax.broadcasted_iota(jnp.int32, (rows, P_WKEYS), 0) & (nq - 1)
        m_w = (j_io > ti) & (j_io <= ti + WINDOW) & (j_io >= WINDOW - s0)
        p_w = _masked_softmax(_dot_nt(qg, kw_ref[pl.ds(w0, P_WKEYS), gs]), m_w)
        o_w = _dot(p_w.astype(BF16), vw_ref[pl.ds(w0, P_WKEYS), gs])

        for r in range(HEADS_PER_KV):
            hd = g * HEADS_PER_KV + r
            rs = slice(r * nq, (r + 1) * nq)
            gc = gate_ref[:, hd * N_BRANCH + 0:hd * N_BRANCH + 1]
            gsl = gate_ref[:, hd * N_BRANCH + 1:hd * N_BRANCH + 2]
            gw = gate_ref[:, hd * N_BRANCH + 2:hd * N_BRANCH + 3]
            o_scr[:, hd * HEAD_DIM:(hd + 1) * HEAD_DIM] = gc * o_c[rs] + gsl * o_s[rs] + gw * o_w[rs]

    o_ref[...] = (_rms(o_scr[...]) * nao_ref[...]).astype(o_ref.dtype)


def _prompt_attention(q, gates, rp, cosc, sinc, ks, vs, kw_pad, vw_pad, e3, norm_ao):
    s_len = q.shape[0]
    n_cmp_pad = s_len // CMP_STRIDE
    n_blocks = s_len // SEL_BLOCK
    vmem = pl.BlockSpec(memory_space=pltpu.VMEM)
    rows = lambda wdt: pl.BlockSpec((Q_BLOCK, wdt), lambda i: (i, 0))
    return pl.pallas_call(
        functools.partial(_pattn_kernel, n_cmp_pad=n_cmp_pad, n_blocks=n_blocks),
        grid=(s_len // Q_BLOCK,),
        in_specs=[rows(ATT_W), rows(GATE_PAD), vmem, vmem, vmem, vmem, vmem, vmem, vmem, vmem,
                  pl.BlockSpec((1, ATT_W), lambda i: (0, 0))],
        out_specs=rows(ATT_W),
        out_shape=jax.ShapeDtypeStruct((s_len, ATT_W), BF16),
        scratch_shapes=[pltpu.VMEM((n_cmp_pad, KV_W), BF16), pltpu.VMEM((n_cmp_pad, KV_W), BF16),
                        pltpu.VMEM((n_cmp_pad + 16, Q_BLOCK), F32), pltpu.VMEM((Q_BLOCK, ATT_W), F32)],
        compiler_params=_cparams("arbitrary"),
        name="prompt_attn",
    )(q, gates, rp, cosc, sinc, ks, vs, kw_pad, vw_pad, e3, norm_ao)


def _sattn_kernel(pt_ref, *refs, n_pages, page, n_new, past, wbuf):
    del pt_ref
    r_pages = refs[0:n_pages]
    s_pages = refs[n_pages:2 * n_pages]
    (rnew_ref, q_ref, gate_ref, snew_ref, wst_ref, wnew_ref, cosc_ref, sinc_ref, e_ref, nao_ref,
     o_ref, wout_ref, r_scr, k_scr, v_scr, kw_scr, vw_scr, o_scr) = refs[2 * n_pages:]
    chunks_pp = page // CMP_STRIDE
    n_cmp = n_pages * chunks_pp
    rows = HEADS_PER_KV * n_new
    n_keys = k_scr.shape[0]
    n_wkeys = kw_scr.shape[0]

    for p in range(n_pages):
        r_scr[p * chunks_pp:(p + 1) * chunks_pp, :] = r_pages[p][...]
        for g in range(N_KV):
            gs = slice(g * HEAD_DIM, (g + 1) * HEAD_DIM)
            k_scr[p * page:(p + 1) * page, gs] = s_pages[p][pl.ds(g, page, stride=KV_ROWS), :].astype(BF16)
            v_scr[p * page:(p + 1) * page, gs] = s_pages[p][pl.ds(N_KV + g, page, stride=KV_ROWS), :].astype(BF16)
    r_scr[n_cmp:n_cmp + 8, :] = jnp.concatenate([rnew_ref[0], jnp.zeros((7, r_scr.shape[1]), F32)], axis=0)
    pad = n_keys - past
    wpad = n_wkeys - wbuf

    def with_zero_rows(new_rows, n_zero):
        return jnp.concatenate([new_rows, jnp.zeros((n_zero, HEAD_DIM), F32)], axis=0).astype(BF16)

    for g in range(N_KV):
        gs = slice(g * HEAD_DIM, (g + 1) * HEAD_DIM)
        k_scr[past:n_keys, gs] = with_zero_rows(snew_ref[pl.ds(g, n_new, stride=KV_ROWS), :], pad - n_new)
        v_scr[past:n_keys, gs] = with_zero_rows(snew_ref[pl.ds(N_KV + g, n_new, stride=KV_ROWS), :], pad - n_new)
        kw_scr[0:wbuf, gs] = wst_ref[pl.ds(g, wbuf, stride=KV_ROWS), :].astype(BF16)
        vw_scr[0:wbuf, gs] = wst_ref[pl.ds(N_KV + g, wbuf, stride=KV_ROWS), :].astype(BF16)
        kw_scr[wbuf:n_wkeys, gs] = with_zero_rows(wnew_ref[pl.ds(g, n_new, stride=KV_ROWS), :], wpad - n_new)
        vw_scr[wbuf:n_wkeys, gs] = with_zero_rows(wnew_ref[pl.ds(N_KV + g, n_new, stride=KV_ROWS), :], wpad - n_new)

    keep = (wbuf - n_new) * KV_ROWS
    wout_ref[0:keep, :] = wst_ref[n_new * KV_ROWS:wbuf * KV_ROWS, :]
    wout_ref[keep:wbuf * KV_ROWS, :] = wnew_ref[...]

    tok = lax.broadcasted_iota(jnp.int32, (rows, 1), 0) & (n_new - 1)
    t_rows = past + tok
    lane = lax.broadcasted_iota(jnp.int32, (n_new, HEAD_DIM), 1)
    n_pb = past // SEL_BLOCK

    for g in range(N_KV):
        gs = slice(g * HEAD_DIM, (g + 1) * HEAD_DIM)
        qg = jnp.concatenate([q_ref[:, (g * HEADS_PER_KV + r) * HEAD_DIM:(g * HEADS_PER_KV + r + 1) * HEAD_DIM]
                              for r in range(HEADS_PER_KV)], axis=0).astype(BF16)

        ck, cv = _r_cols(0, g), _r_cols(1, g)
        kc = r_scr[0:n_cmp, ck:ck + HEAD_DIM] + r_scr[pl.ds(1, n_cmp), ck + HEAD_DIM:ck + 2 * HEAD_DIM]
        kc = _rope(kc, cosc_ref[...], sinc_ref[...]).astype(BF16)
        vc = (r_scr[0:n_cmp, cv:cv + HEAD_DIM] + r_scr[pl.ds(1, n_cmp), cv + HEAD_DIM:cv + 2 * HEAD_DIM]).astype(BF16)

        c_end = lax.broadcasted_iota(jnp.int32, (rows, n_cmp), 1) * CMP_STRIDE + (CMP_BLOCK - 1)
        p_c = _masked_softmax(_dot_nt(qg, kc), c_end <= t_rows)
        o_c = _dot(p_c.astype(BF16), vc)

        p_grp = p_c[0:n_new]
        for r in range(1, HEADS_PER_KV):
            p_grp = p_grp + p_c[r * n_new:(r + 1) * n_new]
        pch = 0.5 * (p_grp + jnp.where(lane >= 1, pltpu.roll(p_grp, 1, axis=1), 0.0))
        score = pch
        for k in range(1, CHUNKS_PER_SEL):
            score = score + pltpu.roll(pch, HEAD_DIM - k, axis=1)
        blk = lane >> 2
        t_tok = past + lax.broadcasted_iota(jnp.int32, (n_new, HEAD_DIM), 0)
        cur = t_tok >> 6
        forced = (blk == 0) | (blk == cur) | (blk == cur - 1)
        score = jnp.where(forced, jnp.inf, score)
        ahead = jnp.zeros((n_new, HEAD_DIM), F32)
        for k in range(1, n_pb):
            other = pltpu.roll(score, CHUNKS_PER_SEL * k, axis=1)
            wins = (other > score) | ((other == score) & (blk >= k))
            ahead = ahead + jnp.where(wins, 1.0, 0.0)
        sel = jnp.where(((lane & (CHUNKS_PER_SEL - 1)) == 0) & (ahead < N_SEL - 1), 1.0, 0.0).astype(BF16)
        sel_keys = _dot(sel, e_ref[...])
        new_ok = lax.broadcasted_iota(jnp.int32, (n_new, pad), 1) <= lax.broadcasted_iota(jnp.int32, (n_new, pad), 0)
        bias = jnp.concatenate([jnp.where(sel_keys > 0.5, 0.0, NEG), jnp.where(new_ok, 0.0, NEG)], axis=1)
        bias = jnp.concatenate([bias] * HEADS_PER_KV, axis=0)
        s_s = _dot_nt(qg, k_scr[:, gs]) + bias
        e_s = jnp.exp(s_s - jnp.max(s_s, axis=-1, keepdims=True))
        p_s = e_s / jnp.maximum(jnp.sum(e_s, axis=-1, keepdims=True), TINY)
        o_s = _dot(p_s.astype(BF16), v_scr[:, gs])

        j_io = lax.broadcasted_iota(jnp.int32, (rows, n_wkeys), 1)
        m_w = (((j_io < wbuf) & (j_io > tok + (wbuf - WINDOW)) & (j_io >= wbuf - past))
               | ((j_io >= wbuf) & (j_io - wbuf <= tok)))
        p_w = _masked_softmax(_dot_nt(qg, kw_scr[:, gs]), m_w)
        o_w = _dot(p_w.astype(BF16), vw_scr[:, gs])

        for r in range(HEADS_PER_KV):
            hd = g * HEADS_PER_KV + r
            rs = slice(r * n_new, (r + 1) * n_new)
            gc = gate_ref[:, hd * N_BRANCH + 0:hd * N_BRANCH + 1]
            gsl = gate_ref[:, hd * N_BRANCH + 1:hd * N_BRANCH + 2]
            gw = gate_ref[:, hd * N_BRANCH + 2:hd * N_BRANCH + 3]
            o_scr[:, hd * HEAD_DIM:(hd + 1) * HEAD_DIM] = gc * o_c[rs] + gsl * o_s[rs] + gw * o_w[rs]

    o_ref[...] = _rms(o_scr[...]) * nao_ref[...]


def _sample_attention(page_table, r_all, r_new, q, gates, slc_cache, slc_new, win_state, win_new, cosc, sinc, e_mat,
                      norm_ao, *, n_new, page, wbuf):
    n_seq, n_pages = page_table.shape
    past = n_pages * page
    chunks_pp = page // CMP_STRIDE
    n_cmp = n_pages * chunks_pp
    assert n_cmp == HEAD_DIM and n_new == 8 and past % SEL_BLOCK == 0 and n_new <= SEL_BLOCK
    assert (past + n_new - 1) // SEL_BLOCK == past // SEL_BLOCK and wbuf == WINDOW and past >= WINDOW
    n_keys = past + HEAD_DIM
    n_wkeys = wbuf + HEAD_DIM
    r_w = r_all.shape[1]

    in_specs = [pl.BlockSpec((chunks_pp, r_w), functools.partial(lambda p, b, pt: (pt[b, p], 0), p))
                for p in range(n_pages)]
    in_specs += [pl.BlockSpec((page * KV_ROWS, HEAD_DIM), functools.partial(lambda p, b, pt: (pt[b, p], 0), p))
                 for p in range(n_pages)]
    seq_rows = lambda wdt: pl.BlockSpec((n_new, wdt), lambda b, pt: (b, 0))
    kv_rows = lambda n_tok: pl.BlockSpec((n_tok * KV_ROWS, HEAD_DIM), lambda b, pt: (b, 0))
    const = lambda shape: pl.BlockSpec(shape, lambda b, pt: (0,) * len(shape))
    in_specs += [pl.BlockSpec((1, 1, r_w), lambda b, pt: (b, 0, 0)), seq_rows(ATT_W), seq_rows(GATE_PAD),
                 kv_rows(n_new), kv_rows(wbuf), kv_rows(n_new),
                 const((n_cmp, HEAD_DIM)), const((n_cmp, HEAD_DIM)), const((HEAD_DIM, past)), const((1, ATT_W))]
    grid_spec = pltpu.PrefetchScalarGridSpec(
        num_scalar_prefetch=1,
        grid=(n_seq,),
        in_specs=in_specs,
        out_specs=[seq_rows(ATT_W), kv_rows(wbuf)],
        scratch_shapes=[pltpu.VMEM((n_cmp + 8, r_w), F32), pltpu.VMEM((n_keys, KV_W), BF16),
                        pltpu.VMEM((n_keys, KV_W), BF16), pltpu.VMEM((n_wkeys, KV_W), BF16),
                        pltpu.VMEM((n_wkeys, KV_W), BF16), pltpu.VMEM((n_new, ATT_W), F32)],
    )
    return pl.pallas_call(
        functools.partial(_sattn_kernel, n_pages=n_pages, page=page, n_new=n_new, past=past, wbuf=wbuf),
        grid_spec=grid_spec,
        out_shape=[jax.ShapeDtypeStruct((n_seq * n_new, ATT_W), F32),
                   jax.ShapeDtypeStruct((n_seq * wbuf * KV_ROWS, HEAD_DIM), F32)],
        compiler_params=_cparams("arbitrary"),
        name="sample_attn",
    )(page_table, *([r_all] * n_pages), *([slc_cache] * n_pages), r_new, q, gates, slc_new, win_state, win_new,
      cosc, sinc, e_mat, norm_ao)


def _outproj_kernel(x_ref, oa_ref, oc_ref, gt_ref, w_ref, o_ref):
    y = _dot(oa_ref[...].astype(BF16), w_ref[0:ATT_W, :]) + _dot(oc_ref[...].astype(BF16), w_ref[ATT_W:D_MODEL, :])
    o_ref[...] = x_ref[...] + (1.0 + gt_ref[...]) * y


def _outproj(x, oa, oc, gate, w, tm=512):
    t = x.shape[0]
    mrows = gate.shape[0]
    mod_spec = (pl.BlockSpec((1, D_MODEL), lambda i: (0, 0)) if mrows == 1
                else pl.BlockSpec((tm, D_MODEL), lambda i: (i, 0)))
    rows = lambda wdt: pl.BlockSpec((tm, wdt), lambda i: (i, 0))
    return pl.pallas_call(
        _outproj_kernel,
        grid=(t // tm,),
        in_specs=[rows(D_MODEL), rows(ATT_W), rows(CONV_W), mod_spec, pl.BlockSpec(memory_space=pltpu.VMEM)],
        out_specs=rows(D_MODEL),
        out_shape=jax.ShapeDtypeStruct((t, D_MODEL), F32),
        compiler_params=_cparams("arbitrary"),
        name="outproj",
    )(x, oa, oc, gate, w)


def _rope_tables(pos):
    half = HEAD_DIM // 2
    inv = ROPE_THETA ** (-jnp.arange(half, dtype=F32) * 2.0 / HEAD_DIM)
    ang = pos.astype(F32)[:, None] * inv[None, :]
    cos, sin = jnp.cos(ang), jnp.sin(ang)
    return jnp.concatenate([cos, cos], axis=1), jnp.concatenate([-sin, sin], axis=1)


def _pack_w_in(w_in):
    off_kv = ATT_W
    off_g = off_kv + 3 * SLAB_W
    off_c = off_g + N_HEADS * N_BRANCH
    gate_cols = jnp.pad(w_in[:, off_g:off_c], ((0, 0), (0, GATE_PAD - N_HEADS * N_BRANCH)))
    return jnp.concatenate([w_in[:, 0:off_g], w_in[:, off_c:off_c + 3 * CONV_W], gate_cols], axis=1).astype(BF16)


def _pack_w_cmp(w_ck, w_cv, n_j):
    def one(w):
        lo = w[0:n_j].reshape(n_j * HEAD_DIM, HEAD_DIM)
        hi = w[CMP_STRIDE:CMP_STRIDE + n_j].reshape(n_j * HEAD_DIM, HEAD_DIM)
        return jnp.concatenate([lo, hi], axis=1)
    return jnp.stack([one(w_ck), one(w_cv)]).astype(BF16)


def kernel(x_prompt, x_sample, c_prompt, c_sample, cache_cmp_kv, cache_slc_kv, state_win_kv, state_conv, page_table,
           w_ada, b_ada, norm_ffn1, ffn1_gate, ffn1_up, ffn1_down, norm_mix, w_in, w_cmp_k, w_cmp_v, conv_w, conv_b,
           norm_att_out, norm_conv_out, w_out, norm_ffn2, ffn2_gate, ffn2_up, ffn2_down, norm_final):
    n_p, s_len, _ = x_prompt.shape
    n_seq, n_new, _ = x_sample.shape
    depth = w_ada.shape[0]
    assert n_p == 1 and depth == 1
    n_pages = page_table.shape[1]
    page = cache_slc_kv.shape[2]
    n_phys = cache_slc_kv.shape[1]
    past = n_pages * page
    wbuf = state_win_kv.shape[2]
    keep_p = min(WINDOW, s_len)
    t_s = n_seq * n_new
    l = 0

    c_all = jnp.concatenate([c_sample, c_prompt, jnp.zeros((8 - n_p, D_MODEL), F32)], axis=0)
    mod = _ada(c_all, w_ada[l], b_ada[l])
    mod_p = [mod[n_seq:n_seq + 1, k * D_MODEL:(k + 1) * D_MODEL] for k in range(N_MOD)]
    mod_s = [jnp.repeat(mod[0:n_seq, k * D_MODEL:(k + 1) * D_MODEL], n_new, axis=0) for k in range(N_MOD)]

    row = lambda v: v.reshape(1, -1)
    f1 = (ffn1_gate[l].astype(BF16), ffn1_up[l].astype(BF16), ffn1_down[l].astype(BF16))
    f2 = (ffn2_gate[l].astype(BF16), ffn2_up[l].astype(BF16), ffn2_down[l].astype(BF16))
    w_proj = _pack_w_in(w_in[l])
    w_o = w_out[l].astype(BF16)
    nfin = row(norm_final)

    xp = x_prompt.reshape(s_len, D_MODEL)
    xs = x_sample.reshape(t_s, D_MODEL)

    xp = _ffn(xp, mod_p[0], mod_p[1], mod_p[2], row(norm_ffn1[l]), nfin, *f1, final_norm=False)
    xs = _ffn(xs, mod_s[0], mod_s[1], mod_s[2], row(norm_ffn1[l]), nfin, *f1, final_norm=False)

    cos_p, sin_p = _rope_tables(jnp.arange(s_len))
    cos_s, sin_s = _rope_tables(past + jnp.arange(n_new))
    tm_s = 256
    cos_s, sin_s = jnp.tile(cos_s, (tm_s // n_new, 1)), jnp.tile(sin_s, (tm_s // n_new, 1))
    zero8 = jnp.zeros((8, CONV_W), F32)
    conv_args = (conv_w[l], row(conv_b[l]), row(norm_conv_out[l]))
    (q_p, cmp_p, slc_p, win_p, ksb_p, vsb_p, kwb_p, vwb_p, gate_p, ocn_p, utail_p) = _proj(
        xp, mod_p[3], mod_p[4], row(norm_mix[l]), w_proj, cos_p, sin_p, zero8, zero8, *conv_args,
        carry=True, seq_rows=s_len, act_dtype=BF16)
    prev1 = jnp.repeat(state_conv[l][:, CONV_K - 2], n_new, axis=0)
    prev2 = jnp.repeat(state_conv[l][:, CONV_K - 3], n_new, axis=0)
    (q_s, cmp_s, slc_s, win_s, _, _, _, _, gate_s, ocn_s, u_s) = _proj(
        xs, mod_s[3], mod_s[4], row(norm_mix[l]), w_proj, cos_s, sin_s, prev1, prev2, *conv_args,
        carry=False, seq_rows=n_new, act_dtype=F32, tm=tm_s)

    n_j = CMP_STRIDE
    w_c = _pack_w_cmp(w_cmp_k[l], w_cmp_v[l], n_j)
    lin = lambda a: a.reshape(-1, HEAD_DIM)
    r_p = _compress_products(cmp_p, w_c, n_j, tm=256)
    r_cache = _compress_products(lin(cache_cmp_kv), w_c, n_j, tm=256)
    r_new = _compress_products(cmp_s, _pack_w_cmp(w_cmp_k[l], w_cmp_v[l], n_new), n_new, tm=n_seq)

    n_cmp_pad = s_len // CMP_STRIDE
    cosc, sinc = _rope_tables(jnp.arange(n_cmp_pad) * CMP_STRIDE + (CMP_BLOCK - 1))
    r_p = jnp.pad(r_p, ((0, 8), (0, 0)))
    band = ((WINDOW, 0), (0, 0))
    n_blocks = s_len // SEL_BLOCK
    key_blk = jnp.arange(s_len) // SEL_BLOCK
    e3 = (jnp.arange(n_blocks)[:, None] == key_blk[None, :]).astype(BF16)
    e3 = e3.reshape(n_blocks, s_len // P_CK, P_CK).transpose(1, 0, 2)
    oa_p = _prompt_attention(q_p, gate_p, r_p, cosc, sinc, ksb_p, vsb_p, jnp.pad(kwb_p, band), jnp.pad(vwb_p, band),
                             e3, row(norm_att_out[l]))

    n_cmp_s = past // CMP_STRIDE
    cosc_s, sinc_s = _rope_tables(jnp.arange(n_cmp_s) * CMP_STRIDE + (CMP_BLOCK - 1))
    e_s =(jnp.arange(HEAD_DIM)[:, None] == (jnp.arange(past) // SEL_BLOCK * CHUNKS_PER_SEL)[None, :]).astype(BF16)
    oa_s, win_new_state = _sample_attention(
        page_table, r_cache, r_new.reshape(n_seq, 1, -1), q_s, gate_s,
        lin(cache_slc_kv), slc_s, lin(state_win_kv), win_s,
        cosc_s, sinc_s, e_s, row(norm_att_out[l]), n_new=n_new, page=page, wbuf=wbuf)

    xp = _outproj(xp, oa_p, ocn_p, mod_p[5], w_o)
    xs = _outproj(xs, oa_s, ocn_s, mod_s[5], w_o)
    yp = _ffn(xp, mod_p[6], mod_p[7], mod_p[8], row(norm_ffn2[l]), nfin, *f2, final_norm=True)
    ys = _ffn(xs, mod_s[6], mod_s[7], mod_s[8], row(norm_ffn2[l]), nfin, *f2, final_norm=True)

    kv6 = lambda a, n, s: a.reshape(1, n, s, 2, N_KV, HEAD_DIM)
    return (yp.reshape(n_p, s_len, D_MODEL), ys.reshape(n_seq, n_new, D_MODEL),
            kv6(cmp_p, n_p, s_len), kv6(slc_p, n_p, s_len), kv6(win_p[(s_len - keep_p) * KV_ROWS:], n_p, keep_p),
            utail_p[8 - (CONV_K - 1):].reshape(1, n_p, CONV_K - 1, CONV_W),
            kv6(cmp_s, n_seq, n_new), kv6(slc_s, n_seq, n_new), kv6(win_new_state, n_seq, wbuf),
            u_s.reshape(n_seq, n_new, CONV_W)[:, n_new - (CONV_K - 1):].reshape(1, n_seq, CONV_K - 1, CONV_W))
```

```python
import functools

import jax
import jax.numpy as jnp
from jax import lax
from jax.experimental import pallas as pl
from jax.experimental.pallas import tpu as pltpu

F32 = jnp.float32
BF16 = jnp.bfloat16

D_MODEL = 2048
HEAD_DIM = 128
N_HEADS = 8
N_KV = 2
HEADS_PER_KV = N_HEADS // N_KV
ATT_W = N_HEADS * HEAD_DIM
KV_W = N_KV * HEAD_DIM
CONV_W = D_MODEL - ATT_W
CONV_K = 3
CMP_BLOCK = 32
CMP_STRIDE = 16
SEL_BLOCK = 64
N_SEL = 16
WINDOW = 512
Q_BLOCK = 128
N_BRANCH = 3
N_MOD = 9
ROPE_THETA = 10000.0
EPS = 1e-6
NEG = -1e30
TINY = 1e-30
SCALE = HEAD_DIM ** -0.5
LOG2E = 1.4426950408889634
SLAB_W = 2 * KV_W
KV_ROWS = 2 * N_KV
GATE_PAD = 128
CHUNKS_PER_SEL = SEL_BLOCK // CMP_STRIDE

VMEM_LIMIT = 56 * 1024 * 1024


def _cparams(*sem):
    return pltpu.CompilerParams(dimension_semantics=sem, vmem_limit_bytes=VMEM_LIMIT)


def _dot(a, b):
    return jnp.dot(a, b, preferred_element_type=F32)


def _dot_nt(a, b):
    return lax.dot_general(a, b, (((1,), (1,)), ((), ())), preferred_element_type=F32)


def _rms(x):
    return x * lax.rsqrt(jnp.mean(x * x, axis=-1, keepdims=True) + EPS)


def _silu(x):
    return x * jax.nn.sigmoid(x)


def _rope(x, cos, sin_signed):
    return x * cos + pltpu.roll(x, HEAD_DIM // 2, axis=1) * sin_signed


def _masked_softmax(s, mask):
    s = jnp.where(mask, s, NEG)
    m = jnp.max(s, axis=-1, keepdims=True)
    e = jnp.where(mask, jnp.exp2(s - m), 0.0)
    return e / jnp.maximum(jnp.sum(e, axis=-1, keepdims=True), TINY)


def _ada_kernel(c_ref, w_ref, b_ref, o_ref):
    a = _silu(c_ref[...]).astype(BF16)
    o_ref[...] = _dot(a, w_ref[...].astype(BF16)) + b_ref[...]


def _ada(c, w, b, tn=1024):
    m, n = c.shape[0], w.shape[1]
    return pl.pallas_call(
        _ada_kernel,
        grid=(n // tn,),
        in_specs=[pl.BlockSpec((m, D_MODEL), lambda j: (0, 0)),
                  pl.BlockSpec((D_MODEL, tn), lambda j: (0, j)),
                  pl.BlockSpec((1, tn), lambda j: (0, j))],
        out_specs=pl.BlockSpec((m, tn), lambda j: (0, j)),
        out_shape=jax.ShapeDtypeStruct((m, n), F32),
        compiler_params=_cparams("arbitrary"),
        name="ada",
    )(c, w, b.reshape(1, n))


def _ffn_kernel(x_ref, sh_ref, sc_ref, gt_ref, ng_ref, nf_ref, wg_ref, wu_ref, wd_ref, o_ref, h_scr, acc_scr,
                *, n_f, final_norm):
    j = pl.program_id(1)

    @pl.when(j == 0)
    def _():
        h = _rms(x_ref[...]) * ng_ref[...] * (1.0 + sc_ref[...]) + sh_ref[...]
        h_scr[...] = h.astype(BF16)
        acc_scr[...] = jnp.zeros_like(acc_scr)

    h = h_scr[...]
    a = (_silu(_dot(h, wg_ref[...])) * _dot(h, wu_ref[...])).astype(BF16)
    acc_scr[...] += _dot(a, wd_ref[...])

    @pl.when(j == n_f - 1)
    def _():
        out = x_ref[...] + 0.5 * (1.0 + gt_ref[...]) * acc_scr[...]
        if final_norm:
            out = _rms(out) * nf_ref[...]
        o_ref[...] = out


def _ffn(x, shift, scale, gate, norm_g, norm_final, wg, wu, wd, *, final_norm, tm=512, tf=512):
    t = x.shape[0]
    d_ff = wg.shape[1]
    n_f = d_ff // tf
    mrows = shift.shape[0]
    if mrows != 1:
        tm = tm // 2
    mod_spec = (pl.BlockSpec((1, D_MODEL), lambda i, j: (0, 0)) if mrows == 1
                else pl.BlockSpec((tm, D_MODEL), lambda i, j: (i, 0), pipeline_mode=pl.Buffered(1)))
    row_spec = pl.BlockSpec((tm, D_MODEL), lambda i, j: (i, 0))
    vec_spec = pl.BlockSpec((1, D_MODEL), lambda i, j: (0, 0))
    return pl.pallas_call(
        functools.partial(_ffn_kernel, n_f=n_f, final_norm=final_norm),
        grid=(t // tm, n_f),
        in_specs=[row_spec, mod_spec, mod_spec, mod_spec, vec_spec, vec_spec,
                  pl.BlockSpec((D_MODEL, tf), lambda i, j: (0, j)),
                  pl.BlockSpec((D_MODEL, tf), lambda i, j: (0, j)),
                  pl.BlockSpec((tf, D_MODEL), lambda i, j: (j, 0))],
        out_specs=row_spec,
        out_shape=jax.ShapeDtypeStruct((t, D_MODEL), F32),
        scratch_shapes=[pltpu.VMEM((tm, D_MODEL), BF16), pltpu.VMEM((tm, D_MODEL), F32)],
        compiler_params=_cparams("arbitrary", "arbitrary"),
        name="ffn",
    )(x, shift, scale, gate, norm_g, norm_final, wg, wu, wd)


PW_Q = 0
PW_KV = PW_Q + ATT_W
PW_CONV = PW_KV + 3 * SLAB_W
PW_GATE = PW_CONV + 3 * CONV_W
PW_TOTAL = PW_GATE + GATE_PAD


def _proj_kernel(x_ref, sh_ref, sc_ref, ng_ref, w_ref, cos_ref, sin_ref, p1_ref, p2_ref, cw_ref, cb_ref, nco_ref,
                 q_ref, cmp_ref, slc_ref, win_ref, ksb_ref, vsb_ref, kwb_ref, vwb_ref, gate_ref, ocn_ref, u_ref,
                 carry_scr, *, tm, seq_rows, carry):
    i = pl.program_id(0)
    h = (_rms(x_ref[...]) * ng_ref[...] * (1.0 + sc_ref[...]) + sh_ref[...]).astype(BF16)
    cos, sin = cos_ref[...], sin_ref[...]

    pq = _dot(h, w_ref[:, PW_Q:PW_KV])
    for hd in range(N_HEADS):
        blk = _rope(pq[:, hd * HEAD_DIM:(hd + 1) * HEAD_DIM], cos, sin) * (SCALE * LOG2E)
        q_ref[:, hd * HEAD_DIM:(hd + 1) * HEAD_DIM] = blk.astype(q_ref.dtype)

    pkv = _dot(h, w_ref[:, PW_KV:PW_CONV])
    for slab, (o_ref, kb_ref, vb_ref) in enumerate(((cmp_ref, None, None), (slc_ref, ksb_ref, vsb_ref),
                                                    (win_ref, kwb_ref, vwb_ref))):
        base = slab * SLAB_W
        for g in range(N_KV):
            gs = slice(g * HEAD_DIM, (g + 1) * HEAD_DIM)
            k = pkv[:, base + g * HEAD_DIM:base + (g + 1) * HEAD_DIM]
            v = pkv[:, base + KV_W + g * HEAD_DIM:base + KV_W + (g + 1) * HEAD_DIM]
            if kb_ref is not None:
                k = _rope(k, cos, sin)
                kb_ref[:, gs] = k.astype(BF16)
                vb_ref[:, gs] = v.astype(BF16)
            o_ref[pl.ds(g, tm, stride=KV_ROWS), :] = k
            o_ref[pl.ds(N_KV + g, tm, stride=KV_ROWS), :] = v

    gate_ref[...] = jax.nn.sigmoid(_dot(h, w_ref[:, PW_GATE:PW_TOTAL]))

    pc = _dot(h, w_ref[:, PW_CONV:PW_GATE])
    u = pc[:, 0:CONV_W] * pc[:, 2 * CONV_W:3 * CONV_W]
    c_out = pc[:, CONV_W:2 * CONV_W]
    row = lax.broadcasted_iota(jnp.int32, (tm, CONV_W), 0)
    if carry:
        @pl.when(i == 0)
        def _():
            carry_scr[...] = jnp.zeros_like(carry_scr)
        prev1 = carry_scr[7:8, :]
        prev2 = carry_scr[6:7, :]
        rs = row
    else:
        prev1 = p1_ref[...]
        prev2 = p2_ref[...]
        rs = row & (seq_rows - 1)
    um1 = jnp.where(rs >= 1, pltpu.roll(u, 1, axis=0), prev1)
    um2 = jnp.where(rs >= 2, pltpu.roll(u, 2, axis=0), jnp.where(rs == 1, prev1, prev2))
    y = um2 * cw_ref[0:1, :] + um1 * cw_ref[1:2, :] + u * cw_ref[2:3, :] + cb_ref[...]
    ocn_ref[...] = (_rms(c_out * y) * nco_ref[...]).astype(ocn_ref.dtype)
    if carry:
        carry_scr[...] = u[tm - 8:tm, :]
        u_ref[...] = u[tm - 8:tm, :]
    else:
        u_ref[...] = u


def _proj(x, shift, scale, norm_g, w, cos, sin, prev1, prev2, conv_w, conv_b, norm_co, *, carry, seq_rows, act_dtype,
          tm=256):
    t = x.shape[0]
    mrows = shift.shape[0]
    mod_spec = (pl.BlockSpec((1, D_MODEL), lambda i: (0, 0)) if mrows == 1
                else pl.BlockSpec((tm, D_MODEL), lambda i: (i, 0)))
    rows = lambda wdt: pl.BlockSpec((tm, wdt), lambda i: (i, 0))
    const = lambda r, wdt: pl.BlockSpec((r, wdt), lambda i: (0, 0))
    tab_spec = rows(HEAD_DIM) if carry else const(tm, HEAD_DIM)
    prev_spec = const(8, CONV_W) if carry else rows(CONV_W)
    u_rows = 8 if carry else t
    out_shape = [jax.ShapeDtypeStruct((t, ATT_W), act_dtype)]
    out_shape += [jax.ShapeDtypeStruct((t * KV_ROWS, HEAD_DIM), F32)] * 3
    out_shape += [jax.ShapeDtypeStruct((t, KV_W), BF16)] * 4
    out_shape += [jax.ShapeDtypeStruct((t, GATE_PAD), F32), jax.ShapeDtypeStruct((t, CONV_W), act_dtype),
                  jax.ShapeDtypeStruct((u_rows, CONV_W), F32)]
    lin_spec = pl.BlockSpec((tm * KV_ROWS, HEAD_DIM), lambda i: (i, 0))
    out_specs = [rows(ATT_W)] + [lin_spec] * 3 + [rows(KV_W)] * 4 + [rows(GATE_PAD), rows(CONV_W),
                                                                     const(8, CONV_W) if carry else rows(CONV_W)]
    return pl.pallas_call(
        functools.partial(_proj_kernel, tm=tm, seq_rows=seq_rows, carry=carry),
        grid=(t // tm,),
        in_specs=[rows(D_MODEL), mod_spec, mod_spec, const(1, D_MODEL),
                  pl.BlockSpec(memory_space=pltpu.VMEM),
                  tab_spec, tab_spec, prev_spec, prev_spec, const(CONV_K, CONV_W), const(1, CONV_W),
                  const(1, CONV_W)],
        out_specs=out_specs,
        out_shape=out_shape,
        scratch_shapes=[pltpu.VMEM((8, CONV_W), F32)],
        compiler_params=_cparams("arbitrary"),
        name="proj",
    )(x, shift, scale, norm_g, w, cos, sin, prev1, prev2, conv_w, conv_b, norm_co)


def _cmp_kernel(x_ref, w_ref, o_ref, *, n_j, tm):
    pitch = n_j * KV_ROWS
    for kv in range(2):
        for g in range(N_KV):
            xs = jnp.concatenate([x_ref[pl.ds(j * KV_ROWS + kv * N_KV + g, tm, stride=pitch), :] for j in range(n_j)],
                                 axis=1)
            col = (kv * N_KV + g) * 2 * HEAD_DIM
            o_ref[:, col:col + 2 * HEAD_DIM] = _dot(xs.astype(BF16), w_ref[kv])


def _compress_products(x, w, n_j, tm):
    pitch = n_j * KV_ROWS
    m = x.shape[0] // pitch
    n_out = 2 * N_KV * 2 * HEAD_DIM
    tm = min(tm, m)
    return pl.pallas_call(
        functools.partial(_cmp_kernel, n_j=n_j, tm=tm),
        grid=(m // tm,),
        in_specs=[pl.BlockSpec((tm * pitch, HEAD_DIM), lambda i: (i, 0)),
                  pl.BlockSpec((2, n_j * HEAD_DIM, 2 * HEAD_DIM), lambda i: (0, 0, 0))],
        out_specs=pl.BlockSpec((tm, n_out), lambda i: (i, 0)),
        out_shape=jax.ShapeDtypeStruct((m, n_out), F32),
        compiler_params=_cparams("arbitrary"),
        name="compress",
    )(x, w)


def _r_cols(kv, g):
    return (kv * N_KV + g) * 2 * HEAD_DIM


P_CK = 512
P_WKEYS = WINDOW + Q_BLOCK


def _pattn_kernel(q_ref, gate_ref, rp_ref, cosc_ref, sinc_ref, ks_ref, vs_ref, kw_ref, vw_ref, et_ref, nao_ref,
                  o_ref, kc_scr, vc_scr, pt_scr, o_scr, *, n_cmp_pad, n_blocks):
    i = pl.program_id(0)
    s0 = i * Q_BLOCK
    nq = Q_BLOCK
    rows = HEADS_PER_KV * nq

    @pl.when(i == 0)
    def _():
        for g in range(N_KV):
            ck = _r_cols(0, g)
            kc = rp_ref[0:n_cmp_pad, ck:ck + HEAD_DIM] + rp_ref[pl.ds(1, n_cmp_pad), ck + HEAD_DIM:ck + 2 * HEAD_DIM]
            kc_scr[:, g * HEAD_DIM:(g + 1) * HEAD_DIM] = _rope(kc, cosc_ref[...], sinc_ref[...]).astype(BF16)
            cv = _r_cols(1, g)
            vc = rp_ref[0:n_cmp_pad, cv:cv + HEAD_DIM] + rp_ref[pl.ds(1, n_cmp_pad), cv + HEAD_DIM:cv + 2 * HEAD_DIM]
            vc_scr[:, g * HEAD_DIM:(g + 1) * HEAD_DIM] = vc.astype(BF16)
        pt_scr[...] = jnp.zeros_like(pt_scr)

    t_rows = s0 + (lax.broadcasted_iota(jnp.int32, (rows, 1), 0) & (nq - 1))
    t_col = s0 + lax.broadcasted_iota(jnp.int32, (nq, 1), 0)
    lhs = []

    for g in range(N_KV):
        gs = slice(g * HEAD_DIM, (g + 1) * HEAD_DIM)
        qg = jnp.concatenate([q_ref[:, (g * HEADS_PER_KV + r) * HEAD_DIM:(g * HEADS_PER_KV + r + 1) * HEAD_DIM]
                              for r in range(HEADS_PER_KV)], axis=0)

        c_end = lax.broadcasted_iota(jnp.int32, (rows, n_cmp_pad), 1) * CMP_STRIDE + (CMP_BLOCK - 1)
        p_c = _masked_softmax(_dot_nt(qg, kc_scr[:, gs]), c_end <= t_rows)
        o_c = _dot(p_c.astype(BF16), vc_scr[:, gs])

        p_grp = p_c[0:nq] + p_c[nq:2 * nq] + p_c[2 * nq:3 * nq] + p_c[3 * nq:4 * nq]
        pt_scr[8:8 + n_cmp_pad, :] = p_grp.T
        st = lambda k: pt_scr[pl.ds(7 + k, n_blocks, stride=CHUNKS_PER_SEL), :]
        score = 0.5 * st(0) + st(1) + st(2) + st(3) + 0.5 * st(4)
        b_io = lax.broadcasted_iota(jnp.int32, (n_blocks, nq), 0)
        b_f = b_io.astype(F32)
        t_lane = s0 + lax.broadcasted_iota(jnp.int32, (n_blocks, nq), 1)
        cur = t_lane >> 6
        forced = (b_io == 0) | (b_io == cur) | (b_io == cur - 1)
        valid = b_io * SEL_BLOCK <= t_lane
        score = jnp.where(forced, jnp.inf, jnp.where(valid, score, -jnp.inf))

        def pick(_, c):
            work, sel = c
            m = jnp.max(work, axis=0, keepdims=True)
            idx = jnp.min(jnp.where(work == m, b_f, float(n_blocks)), axis=0, keepdims=True)
            hit = b_f == idx
            return jnp.where(hit, -jnp.inf, work), jnp.where(hit, 1.0, sel)

        _, sel_t = lax.fori_loop(0, min(N_SEL, n_blocks), pick, (score, jnp.zeros((n_blocks, nq), F32)))
        sel_bias = jnp.where(sel_t.T > 0.5, 0.0, NEG).astype(BF16)
        lhs.append(jnp.concatenate([qg, jnp.concatenate([sel_bias] * HEADS_PER_KV, axis=0)], axis=1))

        w0 = pl.multiple_of(s0, Q_BLOCK)
        j_io = lax.broadcasted_iota(jnp.int32, (rows, P_WKEYS), 1)
        ti = lax.broadcasted_iota(jnp.int32, (rows, P_WKEYS), 0) & (nq - 1)
        m_w = (j_io > ti) & (j_io <= ti + WINDOW) & (j_io >= WINDOW - s0)
        p_w = _masked_softmax(_dot_nt(qg, kw_ref[pl.ds(w0, P_WKEYS), gs]), m_w)
        o_w = _dot(p_w.astype(BF16), vw_ref[pl.ds(w0, P_WKEYS), gs])

        for r in range(HEADS_PER_KV):
            hd = g * HEADS_PER_KV + r
            rs = slice(r * nq, (r + 1) * nq)
            gc = gate_ref[:, hd * N_BRANCH + 0:hd * N_BRANCH + 1]
            gw = gate_ref[:, hd * N_BRANCH + 2:hd * N_BRANCH + 3]
            o_scr[:, hd * HEAD_DIM:(hd + 1) * HEAD_DIM] = gc * o_c[rs] + gw * o_w[rs]

    ones_half = jnp.ones((P_CK, HEAD_DIM), BF16)

    def chunk(c, carry, causal):
        k0 = pl.multiple_of(c * P_CK, P_CK)
        blk_hot = et_ref[pl.ds(k0, P_CK), :]
        if causal:
            key = k0 + lax.broadcasted_iota(jnp.int32, (nq, P_CK), 1)
            cb = jnp.where(key <= t_col, 0.0, NEG)
            cb = jnp.concatenate([cb] * HEADS_PER_KV, axis=0)
        out = []
        for g in range(N_KV):
            gs = slice(g * HEAD_DIM, (g + 1) * HEAD_DIM)
            m_i, acc = carry[g]
            s = _dot_nt(lhs[g], jnp.concatenate([ks_ref[pl.ds(k0, P_CK), gs], blk_hot], axis=1))
            if causal:
                s = s + cb
            m_n = jnp.maximum(m_i, jnp.max(s, axis=-1, keepdims=True))
            p = jnp.exp2(s - m_n).astype(BF16)
            v_aug = jnp.concatenate([vs_ref[pl.ds(k0, P_CK), gs], ones_half], axis=1)
            out.append((m_n, jnp.exp2(m_i - m_n) * acc + _dot(p, v_aug)))
        return tuple(out)

    n_chunks = (s0 + nq + P_CK - 1) // P_CK
    init = tuple((jnp.full((rows, 1), NEG, F32), jnp.zeros((rows, 2 * HEAD_DIM), F32)) for _ in range(N_KV))
    carry = lax.fori_loop(0, n_chunks - 1, lambda c, cr: chunk(c, cr, False), init)
    carry = chunk(n_chunks - 1, carry, True)
    for g in range(N_KV):
        acc = carry[g][1]
        o_s = acc[:, 0:HEAD_DIM] / jnp.maximum(acc[:, HEAD_DIM:2 * HEAD_DIM], TINY)
        for r in range(HEADS_PER_KV):
            hd = g * HEADS_PER_KV + r
            gsl = gate_ref[:, hd * N_BRANCH + 1:hd * N_BRANCH + 2]
            o_scr[:, hd * HEAD_DIM:(hd + 1) * HEAD_DIM] += gsl * o_s[r * nq:(r + 1) * nq]

    o_ref[...] = (_rms(o_scr[...]) * nao_ref[...]).astype(o_ref.dtype)


def _prompt_attention(q, gates, rp, cosc, sinc, ks, vs, kw_pad, vw_pad, e_t, norm_ao):
    s_len = q.shape[0]
    n_cmp_pad = s_len // CMP_STRIDE
    n_blocks = s_len // SEL_BLOCK
    vmem = pl.BlockSpec(memory_space=pltpu.VMEM)
    rows = lambda wdt: pl.BlockSpec((Q_BLOCK, wdt), lambda i: (i, 0))
    return pl.pallas_call(
        functools.partial(_pattn_kernel, n_cmp_pad=n_cmp_pad, n_blocks=n_blocks),
        grid=(s_len // Q_BLOCK,),
        in_specs=[rows(ATT_W), rows(GATE_PAD), vmem, vmem, vmem, vmem, vmem, vmem, vmem, vmem,
                  pl.BlockSpec((1, ATT_W), lambda i: (0, 0))],
        out_specs=rows(ATT_W),
        out_shape=jax.ShapeDtypeStruct((s_len, ATT_W), BF16),
        scratch_shapes=[pltpu.VMEM((n_cmp_pad, KV_W), BF16), pltpu.VMEM((n_cmp_pad, KV_W), BF16),
                        pltpu.VMEM((n_cmp_pad + 16, Q_BLOCK), F32), pltpu.VMEM((Q_BLOCK, ATT_W), F32)],
        compiler_params=_cparams("arbitrary"),
        name="prompt_attn",
    )(q, gates, rp, cosc, sinc, ks, vs, kw_pad, vw_pad, e_t, norm_ao)


def _sattn_kernel(pt_ref, *refs, n_pages, page, n_new, past, wbuf):
    del pt_ref
    r_pages = refs[0:n_pages]
    s_pages = refs[n_pages:2 * n_pages]
    (rnew_ref, q_ref, gate_ref, snew_ref, wst_ref, wnew_ref, cosc_ref, sinc_ref, e_ref, nao_ref,
     o_ref, wout_ref, r_scr, k_scr, v_scr, kw_scr, vw_scr, o_scr) = refs[2 * n_pages:]
    chunks_pp = page // CMP_STRIDE
    n_cmp = n_pages * chunks_pp
    rows = HEADS_PER_KV * n_new
    n_keys = k_scr.shape[0]
    n_wkeys = kw_scr.shape[0]

    for p in range(n_pages):
        r_scr[p * chunks_pp:(p + 1) * chunks_pp, :] = r_pages[p][...]
        for g in range(N_KV):
            gs = slice(g * HEAD_DIM, (g + 1) * HEAD_DIM)
            k_scr[p * page:(p + 1) * page, gs] = s_pages[p][pl.ds(g, page, stride=KV_ROWS), :].astype(BF16)
            v_scr[p * page:(p + 1) * page, gs] = s_pages[p][pl.ds(N_KV + g, page, stride=KV_ROWS), :].astype(BF16)
    r_scr[n_cmp:n_cmp + 8, :] = jnp.concatenate([rnew_ref[0], jnp.zeros((7, r_scr.shape[1]), F32)], axis=0)
    pad = n_keys - past
    wpad = n_wkeys - wbuf

    def with_zero_rows(new_rows, n_zero):
        return jnp.concatenate([new_rows, jnp.zeros((n_zero, HEAD_DIM), F32)], axis=0).astype(BF16)

    for g in range(N_KV):
        gs = slice(g * HEAD_DIM, (g + 1) * HEAD_DIM)
        k_scr[past:n_keys, gs] = with_zero_rows(snew_ref[pl.ds(g, n_new, stride=KV_ROWS), :], pad - n_new)
        v_scr[past:n_keys, gs] = with_zero_rows(snew_ref[pl.ds(N_KV + g, n_new, stride=KV_ROWS), :], pad - n_new)
        kw_scr[0:wbuf, gs] = wst_ref[pl.ds(g, wbuf, stride=KV_ROWS), :].astype(BF16)
        vw_scr[0:wbuf, gs] = wst_ref[pl.ds(N_KV + g, wbuf, stride=KV_ROWS), :].astype(BF16)
        kw_scr[wbuf:n_wkeys, gs] = with_zero_rows(wnew_ref[pl.ds(g, n_new, stride=KV_ROWS), :], wpad - n_new)
        vw_scr[wbuf:n_wkeys, gs] = with_zero_rows(wnew_ref[pl.ds(N_KV + g, n_new, stride=KV_ROWS), :], wpad - n_new)

    keep = (wbuf - n_new) * KV_ROWS
    wout_ref[0:keep, :] = wst_ref[n_new * KV_ROWS:wbuf * KV_ROWS, :]
    wout_ref[keep:wbuf * KV_ROWS, :] = wnew_ref[...]

    tok = lax.broadcasted_iota(jnp.int32, (rows, 1), 0) & (n_new - 1)
    t_rows = past + tok
    lane = lax.broadcasted_iota(jnp.int32, (n_new, HEAD_DIM), 1)
    n_pb = past // SEL_BLOCK

    for g in range(N_KV):
        gs = slice(g * HEAD_DIM, (g + 1) * HEAD_DIM)
        qg = jnp.concatenate([q_ref[:, (g * HEADS_PER_KV + r) * HEAD_DIM:(g * HEADS_PER_KV + r + 1) * HEAD_DIM]
                              for r in range(HEADS_PER_KV)], axis=0).astype(BF16)

        ck, cv = _r_cols(0, g), _r_cols(1, g)
        kc = r_scr[0:n_cmp, ck:ck + HEAD_DIM] + r_scr[pl.ds(1, n_cmp), ck + HEAD_DIM:ck + 2 * HEAD_DIM]
        kc = _rope(kc, cosc_ref[...], sinc_ref[...]).astype(BF16)
        vc = (r_scr[0:n_cmp, cv:cv + HEAD_DIM] + r_scr[pl.ds(1, n_cmp), cv + HEAD_DIM:cv + 2 * HEAD_DIM]).astype(BF16)

        c_end = lax.broadcasted_iota(jnp.int32, (rows, n_cmp), 1) * CMP_STRIDE + (CMP_BLOCK - 1)
        p_c = _masked_softmax(_dot_nt(qg, kc), c_end <= t_rows)
        o_c = _dot(p_c.astype(BF16), vc)

        p_grp = p_c[0:n_new]
        for r in range(1, HEADS_PER_KV):
            p_grp = p_grp + p_c[r * n_new:(r + 1) * n_new]
        pch = 0.5 * (p_grp + jnp.where(lane >= 1, pltpu.roll(p_grp, 1, axis=1), 0.0))
        score = pch
        for k in range(1, CHUNKS_PER_SEL):
            score = score + pltpu.roll(pch, HEAD_DIM - k, axis=1)
        blk = lane >> 2
        t_tok = past + lax.broadcasted_iota(jnp.int32, (n_new, HEAD_DIM), 0)
        cur = t_tok >> 6
        forced = (blk == 0) | (blk == cur) | (blk == cur - 1)
        score = jnp.where(forced, jnp.inf, score)
        ahead = jnp.zeros((n_new, HEAD_DIM), F32)
        for k in range(1, n_pb):
            other = pltpu.roll(score, CHUNKS_PER_SEL * k, axis=1)
            wins = (other > score) | ((other == score) & (blk >= k))
            ahead = ahead + jnp.where(wins, 1.0, 0.0)
        sel = jnp.where(((lane & (CHUNKS_PER_SEL - 1)) == 0) & (ahead < N_SEL - 1), 1.0, 0.0).astype(BF16)
        sel_keys = _dot(sel, e_ref[...])
        new_ok = lax.broadcasted_iota(jnp.int32, (n_new, pad), 1) <= lax.broadcasted_iota(jnp.int32, (n_new, pad), 0)
        bias = jnp.concatenate([jnp.where(sel_keys > 0.5, 0.0, NEG), jnp.where(new_ok, 0.0, NEG)], axis=1)
        bias = jnp.concatenate([bias] * HEADS_PER_KV, axis=0)
        s_s = _dot_nt(qg, k_scr[:, gs]) + bias
        e_s = jnp.exp2(s_s - jnp.max(s_s, axis=-1, keepdims=True))
        p_s = e_s / jnp.maximum(jnp.sum(e_s, axis=-1, keepdims=True), TINY)
        o_s = _dot(p_s.astype(BF16), v_scr[:, gs])

        j_io = lax.broadcasted_iota(jnp.int32, (rows, n_wkeys), 1)
        m_w = (((j_io < wbuf) & (j_io > tok + (wbuf - WINDOW)) & (j_io >= wbuf - past))
               | ((j_io >= wbuf) & (j_io - wbuf <= tok)))
        p_w = _masked_softmax(_dot_nt(qg, kw_scr[:, gs]), m_w)
        o_w = _dot(p_w.astype(BF16), vw_scr[:, gs])

        for r in range(HEADS_PER_KV):
            hd = g * HEADS_PER_KV + r
            rs = slice(r * n_new, (r + 1) * n_new)
            gc = gate_ref[:, hd * N_BRANCH + 0:hd * N_BRANCH + 1]
            gsl = gate_ref[:, hd * N_BRANCH + 1:hd * N_BRANCH + 2]
            gw = gate_ref[:, hd * N_BRANCH + 2:hd * N_BRANCH + 3]
            o_scr[:, hd * HEAD_DIM:(hd + 1) * HEAD_DIM] = gc * o_c[rs] + gsl * o_s[rs] + gw * o_w[rs]

    o_ref[...] = _rms(o_scr[...]) * nao_ref[...]


def _sample_attention(page_table, r_all, r_new, q, gates, slc_cache, slc_new, win_state, win_new, cosc, sinc, e_mat,
                      norm_ao, *, n_new, page, wbuf):
    n_seq, n_pages = page_table.shape
    past = n_pages * page
    chunks_pp = page // CMP_STRIDE
    n_cmp = n_pages * chunks_pp
    assert n_cmp == HEAD_DIM and n_new == 8 and past % SEL_BLOCK == 0 and n_new <= SEL_BLOCK
    assert (past + n_new - 1) // SEL_BLOCK == past // SEL_BLOCK and wbuf == WINDOW and past >= WINDOW
    n_keys = past + HEAD_DIM
    n_wkeys = wbuf + HEAD_DIM
    r_w = r_all.shape[1]

    in_specs = [pl.BlockSpec((chunks_pp, r_w), functools.partial(lambda p, b, pt: (pt[b, p], 0), p))
                for p in range(n_pages)]
    in_specs += [pl.BlockSpec((page * KV_ROWS, HEAD_DIM), functools.partial(lambda p, b, pt: (pt[b, p], 0), p))
                 for p in range(n_pages)]
    seq_rows = lambda wdt: pl.BlockSpec((n_new, wdt), lambda b, pt: (b, 0))
    kv_rows = lambda n_tok: pl.BlockSpec((n_tok * KV_ROWS, HEAD_DIM), lambda b, pt: (b, 0))
    const = lambda shape: pl.BlockSpec(shape, lambda b, pt: (0,) * len(shape))
    in_specs += [pl.BlockSpec((1, 1, r_w), lambda b, pt: (b, 0, 0)), seq_rows(ATT_W), seq_rows(GATE_PAD),
                 kv_rows(n_new), kv_rows(wbuf), kv_rows(n_new),
                 const((n_cmp, HEAD_DIM)), const((n_cmp, HEAD_DIM)), const((HEAD_DIM, past)), const((1, ATT_W))]
    grid_spec = pltpu.PrefetchScalarGridSpec(
        num_scalar_prefetch=1,
        grid=(n_seq,),
        in_specs=in_specs,
        out_specs=[seq_rows(ATT_W), kv_rows(wbuf)],
        scratch_shapes=[pltpu.VMEM((n_cmp + 8, r_w), F32), pltpu.VMEM((n_keys, KV_W), BF16),
                        pltpu.VMEM((n_keys, KV_W), BF16), pltpu.VMEM((n_wkeys, KV_W), BF16),
                        pltpu.VMEM((n_wkeys, KV_W), BF16), pltpu.VMEM((n_new, ATT_W), F32)],
    )
    return pl.pallas_call(
        functools.partial(_sattn_kernel, n_pages=n_pages, page=page, n_new=n_new, past=past, wbuf=wbuf),
        grid_spec=grid_spec,
        out_shape=[jax.ShapeDtypeStruct((n_seq * n_new, ATT_W), F32),
                   jax.ShapeDtypeStruct((n_seq * wbuf * KV_ROWS, HEAD_DIM), F32)],
        compiler_params=_cparams("arbitrary"),
        name="sample_attn",
    )(page_table, *([r_all] * n_pages), *([slc_cache] * n_pages), r_new, q, gates, slc_new, win_state, win_new,
      cosc, sinc, e_mat, norm_ao)


def _outproj_kernel(x_ref, oa_ref, oc_ref, gt_ref, w_ref, o_ref):
    y = _dot(oa_ref[...].astype(BF16), w_ref[0:ATT_W, :]) + _dot(oc_ref[...].astype(BF16), w_ref[ATT_W:D_MODEL, :])
    o_ref[...] = x_ref[...] + (1.0 + gt_ref[...]) * y


def _outproj(x, oa, oc, gate, w, tm=512):
    t = x.shape[0]
    mrows = gate.shape[0]
    mod_spec = (pl.BlockSpec((1, D_MODEL), lambda i: (0, 0)) if mrows == 1
                else pl.BlockSpec((tm, D_MODEL), lambda i: (i, 0)))
    rows = lambda wdt: pl.BlockSpec((tm, wdt), lambda i: (i, 0))
    return pl.pallas_call(
        _outproj_kernel,
        grid=(t // tm,),
        in_specs=[rows(D_MODEL), rows(ATT_W), rows(CONV_W), mod_spec, pl.BlockSpec(memory_space=pltpu.VMEM)],
        out_specs=rows(D_MODEL),
        out_shape=jax.ShapeDtypeStruct((t, D_MODEL), F32),
        compiler_params=_cparams("arbitrary"),
        name="outproj",
    )(x, oa, oc, gate, w)


def _rope_tables(pos):
    half = HEAD_DIM // 2
    inv = ROPE_THETA ** (-jnp.arange(half, dtype=F32) * 2.0 / HEAD_DIM)
    ang = pos.astype(F32)[:, None] * inv[None, :]
    cos, sin = jnp.cos(ang), jnp.sin(ang)
    return jnp.concatenate([cos, cos], axis=1), jnp.concatenate([-sin, sin], axis=1)


def _pack_w_in(w_in):
    off_kv = ATT_W
    off_g = off_kv + 3 * SLAB_W
    off_c = off_g + N_HEADS * N_BRANCH
    gate_cols = jnp.pad(w_in[:, off_g:off_c], ((0, 0), (0, GATE_PAD - N_HEADS * N_BRANCH)))
    return jnp.concatenate([w_in[:, 0:off_g], w_in[:, off_c:off_c + 3 * CONV_W], gate_cols], axis=1).astype(BF16)


def _pack_w_cmp(w_ck, w_cv, n_j):
    def one(w):
        lo = w[0:n_j].reshape(n_j * HEAD_DIM, HEAD_DIM)
        hi = w[CMP_STRIDE:CMP_STRIDE + n_j].reshape(n_j * HEAD_DIM, HEAD_DIM)
        return jnp.concatenate([lo, hi], axis=1)
    return jnp.stack([one(w_ck), one(w_cv)]).astype(BF16)


def kernel(x_prompt, x_sample, c_prompt, c_sample, cache_cmp_kv, cache_slc_kv, state_win_kv, state_conv, page_table,
           w_ada, b_ada, norm_ffn1, ffn1_gate, ffn1_up, ffn1_down, norm_mix, w_in, w_cmp_k, w_cmp_v, conv_w, conv_b,
           norm_att_out, norm_conv_out, w_out, norm_ffn2, ffn2_gate, ffn2_up, ffn2_down, norm_final):
    n_p, s_len, _ = x_prompt.shape
    n_seq, n_new, _ = x_sample.shape
    depth = w_ada.shape[0]
    assert n_p == 1 and depth == 1
    n_pages = page_table.shape[1]
    page = cache_slc_kv.shape[2]
    n_phys = cache_slc_kv.shape[1]
    past = n_pages * page
    wbuf = state_win_kv.shape[2]
    keep_p = min(WINDOW, s_len)
    t_s = n_seq * n_new
    l = 0

    c_all = jnp.concatenate([c_sample, c_prompt, jnp.zeros((8 - n_p, D_MODEL), F32)], axis=0)
    mod = _ada(c_all, w_ada[l], b_ada[l])
    mod_p = [mod[n_seq:n_seq + 1, k * D_MODEL:(k + 1) * D_MODEL] for k in range(N_MOD)]
    mod_s = [jnp.repeat(mod[0:n_seq, k * D_MODEL:(k + 1) * D_MODEL], n_new, axis=0) for k in range(N_MOD)]

    row = lambda v: v.reshape(1, -1)
    f1 = (ffn1_gate[l].astype(BF16), ffn1_up[l].astype(BF16), ffn1_down[l].astype(BF16))
    f2 = (ffn2_gate[l].astype(BF16), ffn2_up[l].astype(BF16), ffn2_down[l].astype(BF16))
    w_proj = _pack_w_in(w_in[l])
    w_o = w_out[l].astype(BF16)
    nfin = row(norm_final)

    xp = x_prompt.reshape(s_len, D_MODEL)
    xs = x_sample.reshape(t_s, D_MODEL)

    xp = _ffn(xp, mod_p[0], mod_p[1], mod_p[2], row(norm_ffn1[l]), nfin, *f1, final_norm=False)
    xs = _ffn(xs, mod_s[0], mod_s[1], mod_s[2], row(norm_ffn1[l]), nfin, *f1, final_norm=False)

    cos_p, sin_p = _rope_tables(jnp.arange(s_len))
    cos_s, sin_s = _rope_tables(past + jnp.arange(n_new))
    tm_s = 256
    cos_s, sin_s = jnp.tile(cos_s, (tm_s // n_new, 1)), jnp.tile(sin_s, (tm_s // n_new, 1))
    zero8 = jnp.zeros((8, CONV_W), F32)
    conv_args = (conv_w[l], row(conv_b[l]), row(norm_conv_out[l]))
    (q_p, cmp_p, slc_p, win_p, ksb_p, vsb_p, kwb_p, vwb_p, gate_p, ocn_p, utail_p) = _proj(
        xp, mod_p[3], mod_p[4], row(norm_mix[l]), w_proj, cos_p, sin_p, zero8, zero8, *conv_args,
        carry=True, seq_rows=s_len, act_dtype=BF16)
    prev1 = jnp.repeat(state_conv[l][:, CONV_K - 2], n_new, axis=0)
    prev2 = jnp.repeat(state_conv[l][:, CONV_K - 3], n_new, axis=0)
    (q_s, cmp_s, slc_s, win_s, _, _, _, _, gate_s, ocn_s, u_s) = _proj(
        xs, mod_s[3], mod_s[4], row(norm_mix[l]), w_proj, cos_s, sin_s, prev1, prev2, *conv_args,
        carry=False, seq_rows=n_new, act_dtype=F32, tm=tm_s)

    n_j = CMP_STRIDE
    w_c = _pack_w_cmp(w_cmp_k[l], w_cmp_v[l], n_j)
    lin = lambda a: a.reshape(-1, HEAD_DIM)
    r_p = _compress_products(cmp_p, w_c, n_j, tm=256)
    r_cache = _compress_products(lin(cache_cmp_kv), w_c, n_j, tm=256)
    r_new = _compress_products(cmp_s, _pack_w_cmp(w_cmp_k[l], w_cmp_v[l], n_new), n_new, tm=n_seq)

    n_cmp_pad = s_len // CMP_STRIDE
    cosc, sinc = _rope_tables(jnp.arange(n_cmp_pad) * CMP_STRIDE + (CMP_BLOCK - 1))
    r_p = jnp.pad(r_p, ((0, 8), (0, 0)))
    band = ((WINDOW, 0), (0, 0))
    n_blocks = s_len // SEL_BLOCK
    e_t = ((jnp.arange(s_len) // SEL_BLOCK)[:, None] == jnp.arange(n_blocks)[None, :]).astype(BF16)
    oa_p = _prompt_attention(q_p, gate_p, r_p, cosc, sinc, ksb_p, vsb_p, jnp.pad(kwb_p, band), jnp.pad(vwb_p, band),
                             e_t, row(norm_att_out[l]))

    n_cmp_s = past // CMP_STRIDE
    cosc_s, sinc_s = _rope_tables(jnp.arange(n_cmp_s) * CMP_STRIDE + (CMP_BLOCK - 1))
    e_s =(jnp.arange(HEAD_DIM)[:, None] == (jnp.arange(past) // SEL_BLOCK * CHUNKS_PER_SEL)[None, :]).astype(BF16)
    oa_s, win_new_state = _sample_attention(
        page_table, r_cache, r_new.reshape(n_seq, 1, -1), q_s, gate_s,
        lin(cache_slc_kv), slc_s, lin(state_win_kv), win_s,
        cosc_s, sinc_s, e_s, row(norm_att_out[l]), n_new=n_new, page=page, wbuf=wbuf)

    xp = _outproj(xp, oa_p, ocn_p, mod_p[5], w_o)
    xs = _outproj(xs, oa_s, ocn_s, mod_s[5], w_o)
    yp = _ffn(xp, mod_p[6], mod_p[7], mod_p[8], row(norm_ffn2[l]), nfin, *f2, final_norm=True)
    ys = _ffn(xs, mod_s[6], mod_s[7], mod_s[8], row(norm_ffn2[l]), nfin, *f2, final_norm=True)

    kv6 = lambda a, n, s: a.reshape(1, n, s, 2, N_KV, HEAD_DIM)
    return (yp.reshape(n_p, s_len, D_MODEL), ys.reshape(n_seq, n_new, D_MODEL),
            kv6(cmp_p, n_p, s_len), kv6(slc_p, n_p, s_len), kv6(win_p[(s_len - keep_p) * KV_ROWS:], n_p, keep_p),
            utail_p[8 - (CONV_K - 1):].reshape(1, n_p, CONV_K - 1, CONV_W),
            kv6(cmp_s, n_seq, n_new), kv6(slc_s, n_seq, n_new), kv6(win_new_state, n_seq, wbuf),
            u_s.reshape(n_seq, n_new, CONV_W)[:, n_new - (CONV_K - 1):].reshape(1, n_seq, CONV_K - 1, CONV_W))
```

```python
import functools

import jax
import jax.numpy as jnp
from jax import lax
from jax.experimental import pallas as pl
from jax.experimental.pallas import tpu as pltpu

F32 = jnp.float32
BF16 = jnp.bfloat16

D_MODEL = 2048
HEAD_DIM = 128
N_HEADS = 8
N_KV = 2
HEADS_PER_KV = N_HEADS // N_KV
ATT_W = N_HEADS * HEAD_DIM
KV_W = N_KV * HEAD_DIM
CONV_W = D_MODEL - ATT_W
CONV_K = 3
CMP_BLOCK = 32
CMP_STRIDE = 16
SEL_BLOCK = 64
N_SEL = 16
WINDOW = 512
Q_BLOCK = 128
N_BRANCH = 3
N_MOD = 9
ROPE_THETA = 10000.0
EPS = 1e-6
NEG = -1e30
TINY = 1e-30
SCALE = HEAD_DIM ** -0.5
LOG2E = 1.4426950408889634
SLAB_W = 2 * KV_W
KV_ROWS = 2 * N_KV
GATE_PAD = 128
CHUNKS_PER_SEL = SEL_BLOCK // CMP_STRIDE

VMEM_LIMIT = 56 * 1024 * 1024


VMEM_LIMIT_FFN = 60 * 1024 * 1024


def _cparams(*sem, vmem=VMEM_LIMIT):
    return pltpu.CompilerParams(dimension_semantics=sem, vmem_limit_bytes=vmem)


def _dot(a, b):
    return jnp.dot(a, b, preferred_element_type=F32)


def _dot_nt(a, b):
    return lax.dot_general(a, b, (((1,), (1,)), ((), ())), preferred_element_type=F32)


def _rms(x):
    return x * lax.rsqrt(jnp.mean(x * x, axis=-1, keepdims=True) + EPS)


def _silu(x):
    return x * jax.nn.sigmoid(x)


def _rope(x, cos, sin_signed):
    return x * cos + pltpu.roll(x, HEAD_DIM // 2, axis=1) * sin_signed


def _masked_softmax(s, mask):
    s = jnp.where(mask, s, NEG)
    m = jnp.max(s, axis=-1, keepdims=True)
    e = jnp.where(mask, jnp.exp2(s - m), 0.0)
    return e / jnp.maximum(jnp.sum(e, axis=-1, keepdims=True), TINY)


def _ada_kernel(c_ref, w_ref, b_ref, o_ref):
    a = _silu(c_ref[...]).astype(BF16)
    o_ref[...] = _dot(a, w_ref[...].astype(BF16)) + b_ref[...]


def _ada(c, w, b, tn=1024):
    m, n = c.shape[0], w.shape[1]
    return pl.pallas_call(
        _ada_kernel,
        grid=(n // tn,),
        in_specs=[pl.BlockSpec((m, D_MODEL), lambda j: (0, 0)),
                  pl.BlockSpec((D_MODEL, tn), lambda j: (0, j)),
                  pl.BlockSpec((1, tn), lambda j: (0, j))],
        out_specs=pl.BlockSpec((m, tn), lambda j: (0, j)),
        out_shape=jax.ShapeDtypeStruct((m, n), F32),
        compiler_params=_cparams("arbitrary"),
        name="ada",
    )(c, w, b.reshape(1, n))


def _ffn_kernel(x_ref, sh_ref, sc_ref, gt_ref, ng_ref, nf_ref, wg_ref, wu_ref, wd_ref, o_ref, h_scr, *, n_f, final_norm):
    j = pl.program_id(1)

    @pl.when(j == 0)
    def _():
        h = _rms(x_ref[...]) * ng_ref[...] * (1.0 + sc_ref[...]) + sh_ref[...]
        h_scr[...] = h.astype(BF16)
        o_ref[...] = jnp.zeros_like(o_ref)

    h = h_scr[...]
    a = (_silu(_dot(h, wg_ref[...])) * _dot(h, wu_ref[...])).astype(BF16)
    o_ref[...] += _dot(a, wd_ref[...])

    @pl.when(j == n_f - 1)
    def _():
        out = x_ref[...] + 0.5 * (1.0 + gt_ref[...]) * o_ref[...]
        if final_norm:
            out = _rms(out) * nf_ref[...]
        o_ref[...] = out


def _ffn(x, shift, scale, gate, norm_g, norm_final, wg, wu, wd, *, final_norm, tm=1024, tf=256):
    t = x.shape[0]
    d_ff = wg.shape[1]
    n_f = d_ff // tf
    mrows = shift.shape[0]
    if mrows != 1:
        tm = tm // 4
    mod_spec = (pl.BlockSpec((1, D_MODEL), lambda i, j: (0, 0)) if mrows == 1
                else pl.BlockSpec((tm, D_MODEL), lambda i, j: (i, 0), pipeline_mode=pl.Buffered(1)))
    row_spec = pl.BlockSpec((tm, D_MODEL), lambda i, j: (i, 0))
    vec_spec = pl.BlockSpec((1, D_MODEL), lambda i, j: (0, 0))
    return pl.pallas_call(
        functools.partial(_ffn_kernel, n_f=n_f, final_norm=final_norm),
        grid=(t // tm, n_f),
        in_specs=[row_spec, mod_spec, mod_spec, mod_spec, vec_spec, vec_spec,
                  pl.BlockSpec((D_MODEL, tf), lambda i, j: (0, j)),
                  pl.BlockSpec((D_MODEL, tf), lambda i, j: (0, j)),
                  pl.BlockSpec((tf, D_MODEL), lambda i, j: (j, 0))],
        out_specs=row_spec,
        out_shape=jax.ShapeDtypeStruct((t, D_MODEL), F32),
        scratch_shapes=[pltpu.VMEM((tm, D_MODEL), BF16)],
        compiler_params=_cparams("arbitrary", "arbitrary", vmem=VMEM_LIMIT_FFN),
        name="ffn",
    )(x, shift, scale, gate, norm_g, norm_final, wg, wu, wd)


PW_Q = 0
PW_KV = PW_Q + ATT_W
PW_CONV = PW_KV + 3 * SLAB_W
PW_GATE = PW_CONV + 3 * CONV_W
PW_TOTAL = PW_GATE + GATE_PAD


def _proj_kernel(x_ref, sh_ref, sc_ref, ng_ref, w_ref, cos_ref, sin_ref, p1_ref, p2_ref, cw_ref, cb_ref, nco_ref,
                 q_ref, cmp_ref, slc_ref, win_ref, ksb_ref, vsb_ref, kwb_ref, vwb_ref, gate_ref, ocn_ref, u_ref,
                 carry_scr, *, tm, seq_rows, carry):
    i = pl.program_id(0)
    h = (_rms(x_ref[...]) * ng_ref[...] * (1.0 + sc_ref[...]) + sh_ref[...]).astype(BF16)
    cos, sin = cos_ref[...], sin_ref[...]

    pq = _dot(h, w_ref[:, PW_Q:PW_KV])
    for hd in range(N_HEADS):
        blk = _rope(pq[:, hd * HEAD_DIM:(hd + 1) * HEAD_DIM], cos, sin) * (SCALE * LOG2E)
        q_ref[:, hd * HEAD_DIM:(hd + 1) * HEAD_DIM] = blk.astype(q_ref.dtype)

    pkv = _dot(h, w_ref[:, PW_KV:PW_CONV])
    for slab, (o_ref, kb_ref, vb_ref) in enumerate(((cmp_ref, None, None), (slc_ref, ksb_ref, vsb_ref),
                                                    (win_ref, kwb_ref, vwb_ref))):
        base = slab * SLAB_W
        for g in range(N_KV):
            gs = slice(g * HEAD_DIM, (g + 1) * HEAD_DIM)
            k = pkv[:, base + g * HEAD_DIM:base + (g + 1) * HEAD_DIM]
            v = pkv[:, base + KV_W + g * HEAD_DIM:base + KV_W + (g + 1) * HEAD_DIM]
            if kb_ref is not None:
                k = _rope(k, cos, sin)
                kb_ref[:, gs] = k.astype(BF16)
                vb_ref[:, gs] = v.astype(BF16)
            o_ref[pl.ds(g, tm, stride=KV_ROWS), :] = k
            o_ref[pl.ds(N_KV + g, tm, stride=KV_ROWS), :] = v

    gate_ref[...] = jax.nn.sigmoid(_dot(h, w_ref[:, PW_GATE:PW_TOTAL]))

    pc = _dot(h, w_ref[:, PW_CONV:PW_GATE])
    u = pc[:, 0:CONV_W] * pc[:, 2 * CONV_W:3 * CONV_W]
    c_out = pc[:, CONV_W:2 * CONV_W]
    row = lax.broadcasted_iota(jnp.int32, (tm, CONV_W), 0)
    if carry:
        @pl.when(i == 0)
        def _():
            carry_scr[...] = jnp.zeros_like(carry_scr)
        prev1 = carry_scr[7:8, :]
        prev2 = carry_scr[6:7, :]
        rs = row
    else:
        prev1 = p1_ref[...]
        prev2 = p2_ref[...]
        rs = row & (seq_rows - 1)
    um1 = jnp.where(rs >= 1, pltpu.roll(u, 1, axis=0), prev1)
    um2 = jnp.where(rs >= 2, pltpu.roll(u, 2, axis=0), jnp.where(rs == 1, prev1, prev2))
    y = um2 * cw_ref[0:1, :] + um1 * cw_ref[1:2, :] + u * cw_ref[2:3, :] + cb_ref[...]
    ocn_ref[...] = (_rms(c_out * y) * nco_ref[...]).astype(ocn_ref.dtype)
    if carry:
        carry_scr[...] = u[tm - 8:tm, :]
        u_ref[...] = u[tm - 8:tm, :]
    else:
        u_ref[...] = u


def _proj(x, shift, scale, norm_g, w, cos, sin, prev1, prev2, conv_w, conv_b, norm_co, *, carry, seq_rows, act_dtype,
          tm=256):
    t = x.shape[0]
    mrows = shift.shape[0]
    mod_spec = (pl.BlockSpec((1, D_MODEL), lambda i: (0, 0)) if mrows == 1
                else pl.BlockSpec((tm, D_MODEL), lambda i: (i, 0)))
    rows = lambda wdt: pl.BlockSpec((tm, wdt), lambda i: (i, 0))
    const = lambda r, wdt: pl.BlockSpec((r, wdt), lambda i: (0, 0))
    tab_spec = rows(HEAD_DIM) if carry else const(tm, HEAD_DIM)
    prev_spec = const(8, CONV_W) if carry else rows(CONV_W)
    u_rows = 8 if carry else t
    out_shape = [jax.ShapeDtypeStruct((t, ATT_W), act_dtype)]
    out_shape += [jax.ShapeDtypeStruct((t * KV_ROWS, HEAD_DIM), F32)] * 3
    out_shape += [jax.ShapeDtypeStruct((t, KV_W), BF16)] * 4
    out_shape += [jax.ShapeDtypeStruct((t, GATE_PAD), F32), jax.ShapeDtypeStruct((t, CONV_W), act_dtype),
                  jax.ShapeDtypeStruct((u_rows, CONV_W), F32)]
    lin_spec = pl.BlockSpec((tm * KV_ROWS, HEAD_DIM), lambda i: (i, 0))
    out_specs = [rows(ATT_W)] + [lin_spec] * 3 + [rows(KV_W)] * 4 + [rows(GATE_PAD), rows(CONV_W),
                                                                     const(8, CONV_W) if carry else rows(CONV_W)]
    return pl.pallas_call(
        functools.partial(_proj_kernel, tm=tm, seq_rows=seq_rows, carry=carry),
        grid=(t // tm,),
        in_specs=[rows(D_MODEL), mod_spec, mod_spec, const(1, D_MODEL),
                  pl.BlockSpec(memory_space=pltpu.VMEM),
                  tab_spec, tab_spec, prev_spec, prev_spec, const(CONV_K, CONV_W), const(1, CONV_W),
                  const(1, CONV_W)],
        out_specs=out_specs,
        out_shape=out_shape,
        scratch_shapes=[pltpu.VMEM((8, CONV_W), F32)],
        compiler_params=_cparams("arbitrary"),
        name="proj",
    )(x, shift, scale, norm_g, w, cos, sin, prev1, prev2, conv_w, conv_b, norm_co)


def _cmp_kernel(x_ref, w_ref, o_ref, *, n_j, tm):
    pitch = n_j * KV_ROWS
    for kv in range(2):
        for g in range(N_KV):
            xs = jnp.concatenate([x_ref[pl.ds(j * KV_ROWS + kv * N_KV + g, tm, stride=pitch), :] for j in range(n_j)],
                                 axis=1)
            col = (kv * N_KV + g) * 2 * HEAD_DIM
            o_ref[:, col:col + 2 * HEAD_DIM] = _dot(xs.astype(BF16), w_ref[kv])


def _compress_products(x, w, n_j, tm):
    pitch = n_j * KV_ROWS
    m = x.shape[0] // pitch
    n_out = 2 * N_KV * 2 * HEAD_DIM
    tm = min(tm, m)
    return pl.pallas_call(
        functools.partial(_cmp_kernel, n_j=n_j, tm=tm),
        grid=(m // tm,),
        in_specs=[pl.BlockSpec((tm * pitch, HEAD_DIM), lambda i: (i, 0)),
                  pl.BlockSpec((2, n_j * HEAD_DIM, 2 * HEAD_DIM), lambda i: (0, 0, 0))],
        out_specs=pl.BlockSpec((tm, n_out), lambda i: (i, 0)),
        out_shape=jax.ShapeDtypeStruct((m, n_out), F32),
        compiler_params=_cparams("arbitrary"),
        name="compress",
    )(x, w)


def _r_cols(kv, g):
    return (kv * N_KV + g) * 2 * HEAD_DIM


P_CK = 512
P_WKEYS = WINDOW + Q_BLOCK


def _pattn_kernel(q_ref, gate_ref, rp_ref, cosc_ref, sinc_ref, ks_ref, vs_ref, kw_ref, vw_ref, et_ref, nao_ref,
                  o_ref, kc_scr, vc_scr, pt_scr, o_scr, *, n_cmp_pad, n_blocks):
    i = pl.program_id(0)
    s0 = i * Q_BLOCK
    nq = Q_BLOCK
    rows = HEADS_PER_KV * nq

    @pl.when(i == 0)
    def _():
        for g in range(N_KV):
            ck = _r_cols(0, g)
            kc = rp_ref[0:n_cmp_pad, ck:ck + HEAD_DIM] + rp_ref[pl.ds(1, n_cmp_pad), ck + HEAD_DIM:ck + 2 * HEAD_DIM]
            kc_scr[:, g * HEAD_DIM:(g + 1) * HEAD_DIM] = _rope(kc, cosc_ref[...], sinc_ref[...]).astype(BF16)
            cv = _r_cols(1, g)
            vc = rp_ref[0:n_cmp_pad, cv:cv + HEAD_DIM] + rp_ref[pl.ds(1, n_cmp_pad), cv + HEAD_DIM:cv + 2 * HEAD_DIM]
            vc_scr[:, g * HEAD_DIM:(g + 1) * HEAD_DIM] = vc.astype(BF16)
        pt_scr[...] = jnp.zeros_like(pt_scr)

    t_rows = s0 + (lax.broadcasted_iota(jnp.int32, (rows, 1), 0) & (nq - 1))
    t_col = s0 + lax.broadcasted_iota(jnp.int32, (nq, 1), 0)
    lhs = []

    for g in range(N_KV):
        gs = slice(g * HEAD_DIM, (g + 1) * HEAD_DIM)
        qg = jnp.concatenate([q_ref[:, (g * HEADS_PER_KV + r) * HEAD_DIM:(g * HEADS_PER_KV + r + 1) * HEAD_DIM]
                              for r in range(HEADS_PER_KV)], axis=0)

        c_end = lax.broadcasted_iota(jnp.int32, (rows, n_cmp_pad), 1) * CMP_STRIDE + (CMP_BLOCK - 1)
        p_c = _masked_softmax(_dot_nt(qg, kc_scr[:, gs]), c_end <= t_rows)
        o_c = _dot(p_c.astype(BF16), vc_scr[:, gs])

        p_grp = p_c[0:nq] + p_c[nq:2 * nq] + p_c[2 * nq:3 * nq] + p_c[3 * nq:4 * nq]
        pt_scr[8:8 + n_cmp_pad, :] = p_grp.T
        st = lambda k: pt_scr[pl.ds(7 + k, n_blocks, stride=CHUNKS_PER_SEL), :]
        score = 0.5 * st(0) + st(1) + st(2) + st(3) + 0.5 * st(4)
        b_io = lax.broadcasted_iota(jnp.int32, (n_blocks, nq), 0)
        b_f = b_io.astype(F32)
        t_lane = s0 + lax.broadcasted_iota(jnp.int32, (n_blocks, nq), 1)
        cur = t_lane >> 6
        forced = (b_io == 0) | (b_io == cur) | (b_io == cur - 1)
        valid = b_io * SEL_BLOCK <= t_lane
        score = jnp.where(forced, jnp.inf, jnp.where(valid, score, -jnp.inf))

        def pick(_, c):
            work, sel = c
            m = jnp.max(work, axis=0, keepdims=True)
            idx = jnp.min(jnp.where(work == m, b_f, float(n_blocks)), axis=0, keepdims=True)
            hit = b_f == idx
            return jnp.where(hit, -jnp.inf, work), jnp.where(hit, 1.0, sel)

        _, sel_t = lax.fori_loop(0, min(N_SEL, n_blocks), pick, (score, jnp.zeros((n_blocks, nq), F32)))
        sel_bias = jnp.where(sel_t.T > 0.5, 0.0, NEG).astype(BF16)
        lhs.append(jnp.concatenate([qg, jnp.concatenate([sel_bias] * HEADS_PER_KV, axis=0)], axis=1))

        w0 = pl.multiple_of(s0, Q_BLOCK)
        j_io = lax.broadcasted_iota(jnp.int32, (rows, P_WKEYS), 1)
        ti = lax.broadcasted_iota(jnp.int32, (rows, P_WKEYS), 0) & (nq - 1)
        m_w = (j_io > ti) & (j_io <= ti + WINDOW) & (j_io >= WINDOW - s0)
        p_w = _masked_softmax(_dot_nt(qg, kw_ref[pl.ds(w0, P_WKEYS), gs]), m_w)
        o_w = _dot(p_w.astype(BF16), vw_ref[pl.ds(w0, P_WKEYS), gs])

        for r in range(HEADS_PER_KV):
            hd = g * HEADS_PER_KV + r
            rs = slice(r * nq, (r + 1) * nq)
            gc = gate_ref[:, hd * N_BRANCH + 0:hd * N_BRANCH + 1]
            gw = gate_ref[:, hd * N_BRANCH + 2:hd * N_BRANCH + 3]
            o_scr[:, hd * HEAD_DIM:(hd + 1) * HEAD_DIM] = gc * o_c[rs] + gw * o_w[rs]

    ones_half = jnp.ones((P_CK, HEAD_DIM), BF16)

    def chunk(c, carry, causal):
        k0 = pl.multiple_of(c * P_CK, P_CK)
        blk_hot = et_ref[pl.ds(k0, P_CK), :]
        if causal:
            key = k0 + lax.broadcasted_iota(jnp.int32, (nq, P_CK), 1)
            cb = jnp.where(key <= t_col, 0.0, NEG)
            cb = jnp.concatenate([cb] * HEADS_PER_KV, axis=0)
        out = []
        for g in range(N_KV):
            gs = slice(g * HEAD_DIM, (g + 1) * HEAD_DIM)
            m_i, acc = carry[g]
            s = _dot_nt(lhs[g], jnp.concatenate([ks_ref[pl.ds(k0, P_CK), gs], blk_hot], axis=1))
            if causal:
                s = s + cb
            m_n = jnp.maximum(m_i, jnp.max(s, axis=-1, keepdims=True))
            p = jnp.exp2(s - m_n).astype(BF16)
            v_aug = jnp.concatenate([vs_ref[pl.ds(k0, P_CK), gs], ones_half], axis=1)
            out.append((m_n, jnp.exp2(m_i - m_n) * acc + _dot(p, v_aug)))
        return tuple(out)

    n_chunks = (s0 + nq + P_CK - 1) // P_CK
    init = tuple((jnp.full((rows, 1), NEG, F32), jnp.zeros((rows, 2 * HEAD_DIM), F32)) for _ in range(N_KV))
    carry = lax.fori_loop(0, n_chunks - 1, lambda c, cr: chunk(c, cr, False), init)
    carry = chunk(n_chunks - 1, carry, True)
    for g in range(N_KV):
        acc = carry[g][1]
        o_s = acc[:, 0:HEAD_DIM] / jnp.maximum(acc[:, HEAD_DIM:2 * HEAD_DIM], TINY)
        for r in range(HEADS_PER_KV):
            hd = g * HEADS_PER_KV + r
            gsl = gate_ref[:, hd * N_BRANCH + 1:hd * N_BRANCH + 2]
            o_scr[:, hd * HEAD_DIM:(hd + 1) * HEAD_DIM] += gsl * o_s[r * nq:(r + 1) * nq]

    o_ref[...] = (_rms(o_scr[...]) * nao_ref[...]).astype(o_ref.dtype)


def _prompt_attention(q, gates, rp, cosc, sinc, ks, vs, kw_pad, vw_pad, e_t, norm_ao):
    s_len = q.shape[0]
    n_cmp_pad = s_len // CMP_STRIDE
    n_blocks = s_len // SEL_BLOCK
    vmem = pl.BlockSpec(memory_space=pltpu.VMEM)
    rows = lambda wdt: pl.BlockSpec((Q_BLOCK, wdt), lambda i: (i, 0))
    return pl.pallas_call(
        functools.partial(_pattn_kernel, n_cmp_pad=n_cmp_pad, n_blocks=n_blocks),
        grid=(s_len // Q_BLOCK,),
        in_specs=[rows(ATT_W), rows(GATE_PAD), vmem, vmem, vmem, vmem, vmem, vmem, vmem, vmem,
                  pl.BlockSpec((1, ATT_W), lambda i: (0, 0))],
        out_specs=rows(ATT_W),
        out_shape=jax.ShapeDtypeStruct((s_len, ATT_W), BF16),
        scratch_shapes=[pltpu.VMEM((n_cmp_pad, KV_W), BF16), pltpu.VMEM((n_cmp_pad, KV_W), BF16),
                        pltpu.VMEM((n_cmp_pad + 16, Q_BLOCK), F32), pltpu.VMEM((Q_BLOCK, ATT_W), F32)],
        compiler_params=_cparams("arbitrary"),
        name="prompt_attn",
    )(q, gates, rp, cosc, sinc, ks, vs, kw_pad, vw_pad, e_t, norm_ao)


def _sattn_kernel(pt_ref, *refs, n_seqs, n_pages, page, n_new, past, wbuf):
    del pt_ref
    n_pg = n_seqs * n_pages
    (rnew_ref, q_ref, gate_ref, snew_ref, wst_ref, wnew_ref, cosc_ref, sinc_ref, e_ref, nao_ref,
     o_ref, wout_ref, r_all_scr, k_all_scr, v_all_scr, kw_all_scr, vw_all_scr, o_scr) = refs[2 * n_pg:]
    for sq in range(n_seqs):
        _sattn_one(refs[sq * n_pages:(sq + 1) * n_pages], refs[n_pg + sq * n_pages:n_pg + (sq + 1) * n_pages],
                   rnew_ref.at[sq], q_ref, gate_ref, snew_ref, wst_ref, wnew_ref, cosc_ref, sinc_ref, e_ref, nao_ref,
                   o_ref, wout_ref, r_all_scr.at[sq], k_all_scr.at[sq], v_all_scr.at[sq], kw_all_scr.at[sq],
                   vw_all_scr.at[sq], o_scr, sq=sq, n_pages=n_pages, page=page, n_new=n_new, past=past, wbuf=wbuf)


def _sattn_one(r_pages, s_pages, rnew_ref, q_ref, gate_ref, snew_ref, wst_ref, wnew_ref, cosc_ref, sinc_ref, e_ref,
               nao_ref, o_ref, wout_ref, r_scr, k_scr, v_scr, kw_scr, vw_scr, o_scr, *, sq, n_pages, page, n_new, past,
               wbuf):
    chunks_pp = page // CMP_STRIDE
    n_cmp = n_pages * chunks_pp
    rows = HEADS_PER_KV * n_new
    n_keys = k_scr.shape[0]
    n_wkeys = kw_scr.shape[0]
    q0 = sq * n_new
    qs = slice(q0, q0 + n_new)
    n0 = sq * n_new * KV_ROWS
    w0 = sq * wbuf * KV_ROWS

    for p in range(n_pages):
        r_scr[p * chunks_pp:(p + 1) * chunks_pp, :] = r_pages[p][...]
        for g in range(N_KV):
            gs = slice(g * HEAD_DIM, (g + 1) * HEAD_DIM)
            k_scr[p * page:(p + 1) * page, gs] = s_pages[p][pl.ds(g, page, stride=KV_ROWS), :].astype(BF16)
            v_scr[p * page:(p + 1) * page, gs] = s_pages[p][pl.ds(N_KV + g, page, stride=KV_ROWS), :].astype(BF16)
    r_scr[n_cmp:n_cmp + 8, :] = jnp.concatenate([rnew_ref[...], jnp.zeros((7, r_scr.shape[1]), F32)], axis=0)
    pad = n_keys - past
    wpad = n_wkeys - wbuf

    def with_zero_rows(new_rows, n_zero):
        return jnp.concatenate([new_rows, jnp.zeros((n_zero, HEAD_DIM), F32)], axis=0).astype(BF16)

    for g in range(N_KV):
        gs = slice(g * HEAD_DIM, (g + 1) * HEAD_DIM)
        k_scr[past:n_keys, gs] = with_zero_rows(snew_ref[pl.ds(n0 + g, n_new, stride=KV_ROWS), :], pad - n_new)
        v_scr[past:n_keys, gs] = with_zero_rows(snew_ref[pl.ds(n0 + N_KV + g, n_new, stride=KV_ROWS), :], pad - n_new)
        kw_scr[0:wbuf, gs] = wst_ref[pl.ds(w0 + g, wbuf, stride=KV_ROWS), :].astype(BF16)
        vw_scr[0:wbuf, gs] = wst_ref[pl.ds(w0 + N_KV + g, wbuf, stride=KV_ROWS), :].astype(BF16)
        kw_scr[wbuf:n_wkeys, gs] = with_zero_rows(wnew_ref[pl.ds(n0 + g, n_new, stride=KV_ROWS), :], wpad - n_new)
        vw_scr[wbuf:n_wkeys, gs] = with_zero_rows(wnew_ref[pl.ds(n0 + N_KV + g, n_new, stride=KV_ROWS), :],
                                                  wpad - n_new)

    keep = (wbuf - n_new) * KV_ROWS
    wout_ref[w0:w0 + keep, :] = wst_ref[w0 + n_new * KV_ROWS:w0 + wbuf * KV_ROWS, :]
    wout_ref[w0 + keep:w0 + wbuf * KV_ROWS, :] = wnew_ref[n0:n0 + n_new * KV_ROWS, :]

    tok = lax.broadcasted_iota(jnp.int32, (rows, 1), 0) & (n_new - 1)
    t_rows = past + tok
    lane = lax.broadcasted_iota(jnp.int32, (n_new, HEAD_DIM), 1)
    n_pb = past // SEL_BLOCK

    for g in range(N_KV):
        gs = slice(g * HEAD_DIM, (g + 1) * HEAD_DIM)
        qg = jnp.concatenate([q_ref[qs, (g * HEADS_PER_KV + r) * HEAD_DIM:(g * HEADS_PER_KV + r + 1) * HEAD_DIM]
                              for r in range(HEADS_PER_KV)], axis=0).astype(BF16)

        ck, cv = _r_cols(0, g), _r_cols(1, g)
        kc = r_scr[0:n_cmp, ck:ck + HEAD_DIM] + r_scr[pl.ds(1, n_cmp), ck + HEAD_DIM:ck + 2 * HEAD_DIM]
        kc = _rope(kc, cosc_ref[...], sinc_ref[...]).astype(BF16)
        vc = (r_scr[0:n_cmp, cv:cv + HEAD_DIM] + r_scr[pl.ds(1, n_cmp), cv + HEAD_DIM:cv + 2 * HEAD_DIM]).astype(BF16)

        c_end = lax.broadcasted_iota(jnp.int32, (rows, n_cmp), 1) * CMP_STRIDE + (CMP_BLOCK - 1)
        p_c = _masked_softmax(_dot_nt(qg, kc), c_end <= t_rows)
        o_c = _dot(p_c.astype(BF16), vc)

        p_grp = p_c[0:n_new]
        for r in range(1, HEADS_PER_KV):
            p_grp = p_grp + p_c[r * n_new:(r + 1) * n_new]
        pch = 0.5 * (p_grp + jnp.where(lane >= 1, pltpu.roll(p_grp, 1, axis=1), 0.0))
        score = pch
        for k in range(1, CHUNKS_PER_SEL):
            score = score + pltpu.roll(pch, HEAD_DIM - k, axis=1)
        blk = lane >> 2
        t_tok = past + lax.broadcasted_iota(jnp.int32, (n_new, HEAD_DIM), 0)
        cur = t_tok >> 6
        forced = (blk == 0) | (blk == cur) | (blk == cur - 1)
        score = jnp.where(forced, jnp.inf, score)
        ahead = jnp.zeros((n_new, HEAD_DIM), F32)
        for k in range(1, n_pb):
            other = pltpu.roll(score, CHUNKS_PER_SEL * k, axis=1)
            wins = (other > score) | ((other == score) & (blk >= k))
            ahead = ahead + jnp.where(wins, 1.0, 0.0)
        sel = jnp.where(((lane & (CHUNKS_PER_SEL - 1)) == 0) & (ahead < N_SEL - 1), 1.0, 0.0).astype(BF16)
        sel_keys = _dot(sel, e_ref[...])
        new_ok = lax.broadcasted_iota(jnp.int32, (n_new, pad), 1) <= lax.broadcasted_iota(jnp.int32, (n_new, pad), 0)
        bias = jnp.concatenate([jnp.where(sel_keys > 0.5, 0.0, NEG), jnp.where(new_ok, 0.0, NEG)], axis=1)
        bias = jnp.concatenate([bias] * HEADS_PER_KV, axis=0)
        s_s = _dot_nt(qg, k_scr[:, gs]) + bias
        e_s = jnp.exp2(s_s - jnp.max(s_s, axis=-1, keepdims=True))
        p_s = e_s / jnp.maximum(jnp.sum(e_s, axis=-1, keepdims=True), TINY)
        o_s = _dot(p_s.astype(BF16), v_scr[:, gs])

        j_io = lax.broadcasted_iota(jnp.int32, (rows, n_wkeys), 1)
        m_w = (((j_io < wbuf) & (j_io > tok + (wbuf - WINDOW)) & (j_io >= wbuf - past))
               | ((j_io >= wbuf) & (j_io - wbuf <= tok)))
        p_w = _masked_softmax(_dot_nt(qg, kw_scr[:, gs]), m_w)
        o_w = _dot(p_w.astype(BF16), vw_scr[:, gs])

        for r in range(HEADS_PER_KV):
            hd = g * HEADS_PER_KV + r
            rs = slice(r * n_new, (r + 1) * n_new)
            gc = gate_ref[qs, hd * N_BRANCH + 0:hd * N_BRANCH + 1]
            gsl = gate_ref[qs, hd * N_BRANCH + 1:hd * N_BRANCH + 2]
            gw = gate_ref[qs, hd * N_BRANCH + 2:hd * N_BRANCH + 3]
            o_scr[qs, hd * HEAD_DIM:(hd + 1) * HEAD_DIM] = gc * o_c[rs] + gsl * o_s[rs] + gw * o_w[rs]

    o_ref[qs, :] = _rms(o_scr[qs, :]) * nao_ref[...]


S_SEQS = 2


def _sample_attention(page_table, r_all, r_new, q, gates, slc_cache, slc_new, win_state, win_new, cosc, sinc, e_mat,
                      norm_ao, *, n_new, page, wbuf):
    n_seq, n_pages = page_table.shape
    past = n_pages * page
    chunks_pp = page // CMP_STRIDE
    n_cmp = n_pages * chunks_pp
    assert n_cmp == HEAD_DIM and n_new == 8 and past % SEL_BLOCK == 0 and n_new <= SEL_BLOCK
    assert (past + n_new - 1) // SEL_BLOCK == past // SEL_BLOCK and wbuf == WINDOW and past >= WINDOW
    n_keys = past + HEAD_DIM
    n_wkeys = wbuf + HEAD_DIM
    r_w = r_all.shape[1]

    ns = S_SEQS
    assert n_seq % ns == 0
    page_map = lambda sq, p: (lambda b, pt: (pt[b * ns + sq, p], 0))
    in_specs = [pl.BlockSpec((chunks_pp, r_w), page_map(sq, p)) for sq in range(ns) for p in range(n_pages)]
    in_specs += [pl.BlockSpec((page * KV_ROWS, HEAD_DIM), page_map(sq, p)) for sq in range(ns) for p in range(n_pages)]
    seq_rows = lambda wdt: pl.BlockSpec((ns * n_new, wdt), lambda b, pt: (b, 0))
    kv_rows = lambda n_tok: pl.BlockSpec((ns * n_tok * KV_ROWS, HEAD_DIM), lambda b, pt: (b, 0))
    const = lambda shape: pl.BlockSpec(shape, lambda b, pt: (0,) * len(shape))
    in_specs += [pl.BlockSpec((ns, 1, r_w), lambda b, pt: (b, 0, 0)), seq_rows(ATT_W), seq_rows(GATE_PAD),
                 kv_rows(n_new), kv_rows(wbuf), kv_rows(n_new),
                 const((n_cmp, HEAD_DIM)), const((n_cmp, HEAD_DIM)), const((HEAD_DIM, past)), const((1, ATT_W))]
    grid_spec = pltpu.PrefetchScalarGridSpec(
        num_scalar_prefetch=1,
        grid=(n_seq // ns,),
        in_specs=in_specs,
        out_specs=[seq_rows(ATT_W), kv_rows(wbuf)],
        scratch_shapes=[pltpu.VMEM((ns, n_cmp + 8, r_w), F32), pltpu.VMEM((ns, n_keys, KV_W), BF16),
                        pltpu.VMEM((ns, n_keys, KV_W), BF16), pltpu.VMEM((ns, n_wkeys, KV_W), BF16),
                        pltpu.VMEM((ns, n_wkeys, KV_W), BF16), pltpu.VMEM((ns * n_new, ATT_W), F32)],
    )
    return pl.pallas_call(
        functools.partial(_sattn_kernel, n_seqs=ns, n_pages=n_pages, page=page, n_new=n_new, past=past, wbuf=wbuf),
        grid_spec=grid_spec,
        out_shape=[jax.ShapeDtypeStruct((n_seq * n_new, ATT_W), F32),
                   jax.ShapeDtypeStruct((n_seq * wbuf * KV_ROWS, HEAD_DIM), F32)],
        compiler_params=_cparams("arbitrary"),
        name="sample_attn",
    )(page_table, *([r_all] * (ns * n_pages)), *([slc_cache] * (ns * n_pages)), r_new, q, gates, slc_new, win_state,
      win_new, cosc, sinc, e_mat, norm_ao)


def _outproj_kernel(x_ref, oa_ref, oc_ref, gt_ref, w_ref, o_ref):
    y = _dot(oa_ref[...].astype(BF16), w_ref[0:ATT_W, :]) + _dot(oc_ref[...].astype(BF16), w_ref[ATT_W:D_MODEL, :])
    o_ref[...] = x_ref[...] + (1.0 + gt_ref[...]) * y


def _outproj(x, oa, oc, gate, w, tm=512):
    t = x.shape[0]
    mrows = gate.shape[0]
    mod_spec = (pl.BlockSpec((1, D_MODEL), lambda i: (0, 0)) if mrows == 1
                else pl.BlockSpec((tm, D_MODEL), lambda i: (i, 0)))
    rows = lambda wdt: pl.BlockSpec((tm, wdt), lambda i: (i, 0))
    return pl.pallas_call(
        _outproj_kernel,
        grid=(t // tm,),
        in_specs=[rows(D_MODEL), rows(ATT_W), rows(CONV_W), mod_spec, pl.BlockSpec(memory_space=pltpu.VMEM)],
        out_specs=rows(D_MODEL),
        out_shape=jax.ShapeDtypeStruct((t, D_MODEL), F32),
        compiler_params=_cparams("arbitrary"),
        name="outproj",
    )(x, oa, oc, gate, w)


def _rope_tables(pos):
    half = HEAD_DIM // 2
    inv = ROPE_THETA ** (-jnp.arange(half, dtype=F32) * 2.0 / HEAD_DIM)
    ang = pos.astype(F32)[:, None] * inv[None, :]
    cos, sin = jnp.cos(ang), jnp.sin(ang)
    return jnp.concatenate([cos, cos], axis=1), jnp.concatenate([-sin, sin], axis=1)


def _pack_w_in(w_in):
    off_kv = ATT_W
    off_g = off_kv + 3 * SLAB_W
    off_c = off_g + N_HEADS * N_BRANCH
    gate_cols = jnp.pad(w_in[:, off_g:off_c], ((0, 0), (0, GATE_PAD - N_HEADS * N_BRANCH)))
    return jnp.concatenate([w_in[:, 0:off_g], w_in[:, off_c:off_c + 3 * CONV_W], gate_cols], axis=1).astype(BF16)


def _pack_w_cmp(w_ck, w_cv, n_j):
    def one(w):
        lo = w[0:n_j].reshape(n_j * HEAD_DIM, HEAD_DIM)
        hi = w[CMP_STRIDE:CMP_STRIDE + n_j].reshape(n_j * HEAD_DIM, HEAD_DIM)
        return jnp.concatenate([lo, hi], axis=1)
    return jnp.stack([one(w_ck), one(w_cv)]).astype(BF16)


def kernel(x_prompt, x_sample, c_prompt, c_sample, cache_cmp_kv, cache_slc_kv, state_win_kv, state_conv, page_table,
           w_ada, b_ada, norm_ffn1, ffn1_gate, ffn1_up, ffn1_down, norm_mix, w_in, w_cmp_k, w_cmp_v, conv_w, conv_b,
           norm_att_out, norm_conv_out, w_out, norm_ffn2, ffn2_gate, ffn2_up, ffn2_down, norm_final):
    n_p, s_len, _ = x_prompt.shape
    n_seq, n_new, _ = x_sample.shape
    depth = w_ada.shape[0]
    assert n_p == 1 and depth == 1
    n_pages = page_table.shape[1]
    page = cache_slc_kv.shape[2]
    n_phys = cache_slc_kv.shape[1]
    past = n_pages * page
    wbuf = state_win_kv.shape[2]
    keep_p = min(WINDOW, s_len)
    t_s = n_seq * n_new
    l = 0

    c_all = jnp.concatenate([c_sample, c_prompt, jnp.zeros((8 - n_p, D_MODEL), F32)], axis=0)
    mod = _ada(c_all, w_ada[l], b_ada[l])
    mod_p = [mod[n_seq:n_seq + 1, k * D_MODEL:(k + 1) * D_MODEL] for k in range(N_MOD)]
    mod_s = [jnp.repeat(mod[0:n_seq, k * D_MODEL:(k + 1) * D_MODEL], n_new, axis=0) for k in range(N_MOD)]

    row = lambda v: v.reshape(1, -1)
    f1 = (ffn1_gate[l].astype(BF16), ffn1_up[l].astype(BF16), ffn1_down[l].astype(BF16))
    f2 = (ffn2_gate[l].astype(BF16), ffn2_up[l].astype(BF16), ffn2_down[l].astype(BF16))
    w_proj = _pack_w_in(w_in[l])
    w_o = w_out[l].astype(BF16)
    nfin = row(norm_final)

    xp = x_prompt.reshape(s_len, D_MODEL)
    xs = x_sample.reshape(t_s, D_MODEL)

    xp = _ffn(xp, mod_p[0], mod_p[1], mod_p[2], row(norm_ffn1[l]), nfin, *f1, final_norm=False)
    xs = _ffn(xs, mod_s[0], mod_s[1], mod_s[2], row(norm_ffn1[l]), nfin, *f1, final_norm=False)

    cos_p, sin_p = _rope_tables(jnp.arange(s_len))
    cos_s, sin_s = _rope_tables(past + jnp.arange(n_new))
    tm_s = 256
    cos_s, sin_s = jnp.tile(cos_s, (tm_s // n_new, 1)), jnp.tile(sin_s, (tm_s // n_new, 1))
    zero8 = jnp.zeros((8, CONV_W), F32)
    conv_args = (conv_w[l], row(conv_b[l]), row(norm_conv_out[l]))
    (q_p, cmp_p, slc_p, win_p, ksb_p, vsb_p, kwb_p, vwb_p, gate_p, ocn_p, utail_p) = _proj(
        xp, mod_p[3], mod_p[4], row(norm_mix[l]), w_proj, cos_p, sin_p, zero8, zero8, *conv_args,
        carry=True, seq_rows=s_len, act_dtype=BF16)
    prev1 = jnp.repeat(state_conv[l][:, CONV_K - 2], n_new, axis=0)
    prev2 = jnp.repeat(state_conv[l][:, CONV_K - 3], n_new, axis=0)
    (q_s, cmp_s, slc_s, win_s, _, _, _, _, gate_s, ocn_s, u_s) = _proj(
        xs, mod_s[3], mod_s[4], row(norm_mix[l]), w_proj, cos_s, sin_s, prev1, prev2, *conv_args,
        carry=False, seq_rows=n_new, act_dtype=F32, tm=tm_s)

    n_j = CMP_STRIDE
    w_c = _pack_w_cmp(w_cmp_k[l], w_cmp_v[l], n_j)
    lin = lambda a: a.reshape(-1, HEAD_DIM)
    r_p = _compress_products(cmp_p, w_c, n_j, tm=256)
    r_cache = _compress_products(lin(cache_cmp_kv), w_c, n_j, tm=256)
    r_new = _compress_products(cmp_s, _pack_w_cmp(w_cmp_k[l], w_cmp_v[l], n_new), n_new, tm=n_seq)

    n_cmp_pad = s_len // CMP_STRIDE
    cosc, sinc = _rope_tables(jnp.arange(n_cmp_pad) * CMP_STRIDE + (CMP_BLOCK - 1))
    r_p = jnp.pad(r_p, ((0, 8), (0, 0)))
    band = ((WINDOW, 0), (0, 0))
    n_blocks = s_len // SEL_BLOCK
    e_t = ((jnp.arange(s_len) // SEL_BLOCK)[:, None] == jnp.arange(n_blocks)[None, :]).astype(BF16)
    oa_p = _prompt_attention(q_p, gate_p, r_p, cosc, sinc, ksb_p, vsb_p, jnp.pad(kwb_p, band), jnp.pad(vwb_p, band),
                             e_t, row(norm_att_out[l]))

    n_cmp_s = past // CMP_STRIDE
    cosc_s, sinc_s = _rope_tables(jnp.arange(n_cmp_s) * CMP_STRIDE + (CMP_BLOCK - 1))
    e_s =(jnp.arange(HEAD_DIM)[:, None] == (jnp.arange(past) // SEL_BLOCK * CHUNKS_PER_SEL)[None, :]).astype(BF16)
    oa_s, win_new_state = _sample_attention(
        page_table, r_cache, r_new.reshape(n_seq, 1, -1), q_s, gate_s,
        lin(cache_slc_kv), slc_s, lin(state_win_kv), win_s,
        cosc_s, sinc_s, e_s, row(norm_att_out[l]), n_new=n_new, page=page, wbuf=wbuf)

    xp = _outproj(xp, oa_p, ocn_p, mod_p[5], w_o)
    xs = _outproj(xs, oa_s, ocn_s, mod_s[5], w_o)
    yp = _ffn(xp, mod_p[6], mod_p[7], mod_p[8], row(norm_ffn2[l]), nfin, *f2, final_norm=True)
    ys = _ffn(xs, mod_s[6], mod_s[7], mod_s[8], row(norm_ffn2[l]), nfin, *f2, final_norm=True)

    kv6 = lambda a, n, s: a.reshape(1, n, s, 2, N_KV, HEAD_DIM)
    return (yp.reshape(n_p, s_len, D_MODEL), ys.reshape(n_seq, n_new, D_MODEL),
            kv6(cmp_p, n_p, s_len), kv6(slc_p, n_p, s_len), kv6(win_p[(s_len - keep_p) * KV_ROWS:], n_p, keep_p),
            utail_p[8 - (CONV_K - 1):].reshape(1, n_p, CONV_K - 1, CONV_W),
            kv6(cmp_s, n_seq, n_new), kv6(slc_s, n_seq, n_new), kv6(win_new_state, n_seq, wbuf),
            u_s.reshape(n_seq, n_new, CONV_W)[:, n_new - (CONV_K - 1):].reshape(1, n_seq, CONV_K - 1, CONV_W))
```

```python
import functools

import jax
import jax.numpy as jnp
from jax import lax
from jax.experimental import pallas as pl
from jax.experimental.pallas import tpu as pltpu

F32 = jnp.float32
BF16 = jnp.bfloat16

D_MODEL = 2048
HEAD_DIM = 128
N_HEADS = 8
N_KV = 2
HEADS_PER_KV = N_HEADS // N_KV
ATT_W = N_HEADS * HEAD_DIM
KV_W = N_KV * HEAD_DIM
CONV_W = D_MODEL - ATT_W
CONV_K = 3
CMP_BLOCK = 32
CMP_STRIDE = 16
SEL_BLOCK = 64
N_SEL = 16
WINDOW = 512
Q_BLOCK = 128
N_BRANCH = 3
N_MOD = 9
ROPE_THETA = 10000.0
EPS = 1e-6
NEG = -1e30
TINY = 1e-30
SCALE = HEAD_DIM ** -0.5
LOG2E = 1.4426950408889634
SLAB_W = 2 * KV_W
KV_ROWS = 2 * N_KV
GATE_PAD = 128
CHUNKS_PER_SEL = SEL_BLOCK // CMP_STRIDE

VMEM_LIMIT = 56 * 1024 * 1024


def _cparams(*sem):
    return pltpu.CompilerParams(dimension_semantics=sem, vmem_limit_bytes=VMEM_LIMIT)


def _dot(a, b):
    return jnp.dot(a, b, preferred_element_type=F32)


def _dot_nt(a, b):
    return lax.dot_general(a, b, (((1,), (1,)), ((), ())), preferred_element_type=F32)


def _rms(x):
    return x * lax.rsqrt(jnp.mean(x * x, axis=-1, keepdims=True) + EPS)


def _silu(x):
    return x * jax.nn.sigmoid(x)


def _rope(x, cos, sin_signed):
    return x * cos + pltpu.roll(x, HEAD_DIM // 2, axis=1) * sin_signed


def _masked_softmax(s, mask):
    s = jnp.where(mask, s, NEG)
    m = jnp.max(s, axis=-1, keepdims=True)
    e = jnp.where(mask, jnp.exp2(s - m), 0.0)
    return e / jnp.maximum(jnp.sum(e, axis=-1, keepdims=True), TINY)


def _ada_kernel(c_ref, w_ref, b_ref, o_ref):
    a = _silu(c_ref[...]).astype(BF16)
    o_ref[...] = _dot(a, w_ref[...].astype(BF16)) + b_ref[...]


def _ada(c, w, b, tn=1024):
    m, n = c.shape[0], w.shape[1]
    return pl.pallas_call(
        _ada_kernel,
        grid=(n // tn,),
        in_specs=[pl.BlockSpec((m, D_MODEL), lambda j: (0, 0)),
                  pl.BlockSpec((D_MODEL, tn), lambda j: (0, j)),
                  pl.BlockSpec((1, tn), lambda j: (0, j))],
        out_specs=pl.BlockSpec((m, tn), lambda j: (0, j)),
        out_shape=jax.ShapeDtypeStruct((m, n), F32),
        compiler_params=_cparams("arbitrary"),
        name="ada",
    )(c, w, b.reshape(1, n))


def _ffn_kernel(x_ref, sh_ref, sc_ref, gt_ref, ng_ref, nf_ref, wg_ref, wu_ref, wd_ref, o_ref, h_scr, *, n_f, final_norm):
    j = pl.program_id(1)

    @pl.when(j == 0)
    def _():
        h = _rms(x_ref[...]) * ng_ref[...] * (1.0 + sc_ref[...]) + sh_ref[...]
        h_scr[...] = h.astype(BF16)
        o_ref[...] = jnp.zeros_like(o_ref)

    h = h_scr[...]
    a = (_silu(_dot(h, wg_ref[...])) * _dot(h, wu_ref[...])).astype(BF16)
    o_ref[...] += _dot(a, wd_ref[...])

    @pl.when(j == n_f - 1)
    def _():
        out = x_ref[...] + 0.5 * (1.0 + gt_ref[...]) * o_ref[...]
        if final_norm:
            out = _rms(out) * nf_ref[...]
        o_ref[...] = out


def _ffn(x, shift, scale, gate, norm_g, norm_final, wg, wu, wd, *, final_norm, tm=512, tf=512):
    t = x.shape[0]
    d_ff = wg.shape[1]
    n_f = d_ff // tf
    mrows = shift.shape[0]
    if mrows != 1:
        tm = tm // 2
    mod_spec = (pl.BlockSpec((1, D_MODEL), lambda i, j: (0, 0)) if mrows == 1
                else pl.BlockSpec((tm, D_MODEL), lambda i, j: (i, 0), pipeline_mode=pl.Buffered(1)))
    row_spec = pl.BlockSpec((tm, D_MODEL), lambda i, j: (i, 0))
    vec_spec = pl.BlockSpec((1, D_MODEL), lambda i, j: (0, 0))
    return pl.pallas_call(
        functools.partial(_ffn_kernel, n_f=n_f, final_norm=final_norm),
        grid=(t // tm, n_f),
        in_specs=[row_spec, mod_spec, mod_spec, mod_spec, vec_spec, vec_spec,
                  pl.BlockSpec((D_MODEL, tf), lambda i, j: (0, j)),
                  pl.BlockSpec((D_MODEL, tf), lambda i, j: (0, j)),
                  pl.BlockSpec((tf, D_MODEL), lambda i, j: (j, 0))],
        out_specs=row_spec,
        out_shape=jax.ShapeDtypeStruct((t, D_MODEL), F32),
        scratch_shapes=[pltpu.VMEM((tm, D_MODEL), BF16)],
        compiler_params=_cparams("arbitrary", "arbitrary"),
        name="ffn",
    )(x, shift, scale, gate, norm_g, norm_final, wg, wu, wd)


PW_Q = 0
PW_KV = PW_Q + ATT_W
PW_CONV = PW_KV + 3 * SLAB_W
PW_GATE = PW_CONV + 3 * CONV_W
PW_TOTAL = PW_GATE + GATE_PAD


def _proj_kernel(x_ref, sh_ref, sc_ref, ng_ref, w_ref, cos_ref, sin_ref, p1_ref, p2_ref, cw_ref, cb_ref, nco_ref,
                 q_ref, cmp_ref, slc_ref, win_ref, ksb_ref, vst_ref, kwb_ref, vwb_ref, gate_ref, ocn_ref, u_ref, qt_ref,
                 carry_scr, *, tm, seq_rows, carry):
    i = pl.program_id(0)
    h = (_rms(x_ref[...]) * ng_ref[...] * (1.0 + sc_ref[...]) + sh_ref[...]).astype(BF16)
    cos, sin = cos_ref[...], sin_ref[...]

    pq = _dot(h, w_ref[:, PW_Q:PW_KV])
    for hd in range(N_HEADS):
        hs = slice(hd * HEAD_DIM, (hd + 1) * HEAD_DIM)
        blk = _rope(pq[:, hs], cos, sin) * (SCALE * LOG2E)
        q_ref[:, hs] = blk.astype(q_ref.dtype)
        qt_ref[hs, :] = blk.T.astype(BF16)

    pkv = _dot(h, w_ref[:, PW_KV:PW_CONV])
    for slab, (o_ref, kb_ref, vb_ref) in enumerate(((cmp_ref, None, None), (slc_ref, ksb_ref, None),
                                                    (win_ref, kwb_ref, vwb_ref))):
        base = slab * SLAB_W
        for g in range(N_KV):
            gs = slice(g * HEAD_DIM, (g + 1) * HEAD_DIM)
            k = pkv[:, base + g * HEAD_DIM:base + (g + 1) * HEAD_DIM]
            v = pkv[:, base + KV_W + g * HEAD_DIM:base + KV_W + (g + 1) * HEAD_DIM]
            if kb_ref is not None:
                k = _rope(k, cos, sin)
                kb_ref[:, gs] = k.astype(BF16)
                if vb_ref is not None:
                    vb_ref[:, gs] = v.astype(BF16)
                else:
                    vst_ref[0, gs, :] = v.T.astype(BF16)
            o_ref[pl.ds(g, tm, stride=KV_ROWS), :] = k
            o_ref[pl.ds(N_KV + g, tm, stride=KV_ROWS), :] = v

    gate_ref[...] = jax.nn.sigmoid(_dot(h, w_ref[:, PW_GATE:PW_TOTAL]))

    pc = _dot(h, w_ref[:, PW_CONV:PW_GATE])
    u = pc[:, 0:CONV_W] * pc[:, 2 * CONV_W:3 * CONV_W]
    c_out = pc[:, CONV_W:2 * CONV_W]
    row = lax.broadcasted_iota(jnp.int32, (tm, CONV_W), 0)
    if carry:
        @pl.when(i == 0)
        def _():
            carry_scr[...] = jnp.zeros_like(carry_scr)
        prev1 = carry_scr[7:8, :]
        prev2 = carry_scr[6:7, :]
        rs = row
    else:
        prev1 = p1_ref[...]
        prev2 = p2_ref[...]
        rs = row & (seq_rows - 1)
    um1 = jnp.where(rs >= 1, pltpu.roll(u, 1, axis=0), prev1)
    um2 = jnp.where(rs >= 2, pltpu.roll(u, 2, axis=0), jnp.where(rs == 1, prev1, prev2))
    y = um2 * cw_ref[0:1, :] + um1 * cw_ref[1:2, :] + u * cw_ref[2:3, :] + cb_ref[...]
    ocn_ref[...] = (_rms(c_out * y) * nco_ref[...]).astype(ocn_ref.dtype)
    if carry:
        carry_scr[...] = u[tm - 8:tm, :]
        u_ref[...] = u[tm - 8:tm, :]
    else:
        u_ref[...] = u


def _proj(x, shift, scale, norm_g, w, cos, sin, prev1, prev2, conv_w, conv_b, norm_co, *, carry, seq_rows, act_dtype,
          tm=256):
    t = x.shape[0]
    mrows = shift.shape[0]
    mod_spec = (pl.BlockSpec((1, D_MODEL), lambda i: (0, 0)) if mrows == 1
                else pl.BlockSpec((tm, D_MODEL), lambda i: (i, 0)))
    rows = lambda wdt: pl.BlockSpec((tm, wdt), lambda i: (i, 0))
    const = lambda r, wdt: pl.BlockSpec((r, wdt), lambda i: (0, 0))
    tab_spec = rows(HEAD_DIM) if carry else const(tm, HEAD_DIM)
    prev_spec = const(8, CONV_W) if carry else rows(CONV_W)
    u_rows = 8 if carry else t
    out_shape = [jax.ShapeDtypeStruct((t, ATT_W), act_dtype)]
    out_shape += [jax.ShapeDtypeStruct((t * KV_ROWS, HEAD_DIM), F32)] * 3
    per_ck = P_CK // tm
    kvb = jax.ShapeDtypeStruct((t, KV_W), BF16)
    out_shape += [kvb, jax.ShapeDtypeStruct((t // P_CK, KV_W, P_CK), BF16), kvb, kvb]
    out_shape += [jax.ShapeDtypeStruct((t, GATE_PAD), F32), jax.ShapeDtypeStruct((t, CONV_W), act_dtype),
                  jax.ShapeDtypeStruct((u_rows, CONV_W), F32), jax.ShapeDtypeStruct((ATT_W, t), BF16)]
    lin_spec = pl.BlockSpec((tm * KV_ROWS, HEAD_DIM), lambda i: (i, 0))
    vst_spec = pl.BlockSpec((1, KV_W, tm), lambda i: (i // per_ck, 0, i % per_ck))
    out_specs = [rows(ATT_W)] + [lin_spec] * 3 + [rows(KV_W), vst_spec, rows(KV_W), rows(KV_W)]
    out_specs += [rows(GATE_PAD), rows(CONV_W), const(8, CONV_W) if carry else rows(CONV_W),
                  pl.BlockSpec((ATT_W, tm), lambda i: (0, i))]
    return pl.pallas_call(
        functools.partial(_proj_kernel, tm=tm, seq_rows=seq_rows, carry=carry),
        grid=(t // tm,),
        in_specs=[rows(D_MODEL), mod_spec, mod_spec, const(1, D_MODEL),
                  pl.BlockSpec(memory_space=pltpu.VMEM),
                  tab_spec, tab_spec, prev_spec, prev_spec, const(CONV_K, CONV_W), const(1, CONV_W),
                  const(1, CONV_W)],
        out_specs=out_specs,
        out_shape=out_shape,
        scratch_shapes=[pltpu.VMEM((8, CONV_W), F32)],
        compiler_params=_cparams("arbitrary"),
        name="proj",
    )(x, shift, scale, norm_g, w, cos, sin, prev1, prev2, conv_w, conv_b, norm_co)


def _cmp_kernel(x_ref, w_ref, o_ref, *, n_j, tm):
    pitch = n_j * KV_ROWS
    for kv in range(2):
        for g in range(N_KV):
            xs = jnp.concatenate([x_ref[pl.ds(j * KV_ROWS + kv * N_KV + g, tm, stride=pitch), :] for j in range(n_j)],
                                 axis=1)
            col = (kv * N_KV + g) * 2 * HEAD_DIM
            o_ref[:, col:col + 2 * HEAD_DIM] = _dot(xs.astype(BF16), w_ref[kv])


def _compress_products(x, w, n_j, tm):
    pitch = n_j * KV_ROWS
    m = x.shape[0] // pitch
    n_out = 2 * N_KV * 2 * HEAD_DIM
    tm = min(tm, m)
    return pl.pallas_call(
        functools.partial(_cmp_kernel, n_j=n_j, tm=tm),
        grid=(m // tm,),
        in_specs=[pl.BlockSpec((tm * pitch, HEAD_DIM), lambda i: (i, 0)),
                  pl.BlockSpec((2, n_j * HEAD_DIM, 2 * HEAD_DIM), lambda i: (0, 0, 0))],
        out_specs=pl.BlockSpec((tm, n_out), lambda i: (i, 0)),
        out_shape=jax.ShapeDtypeStruct((m, n_out), F32),
        compiler_params=_cparams("arbitrary"),
        name="compress",
    )(x, w)


def _r_cols(kv, g):
    return (kv * N_KV + g) * 2 * HEAD_DIM


P_CK = 512
P_WKEYS = WINDOW + Q_BLOCK


def _pattn_kernel(q_ref, qt_ref, gate_ref, rp_ref, cosc_ref, sinc_ref, ks_ref, vst_ref, kw_ref, vw_ref, et_ref,
                  nao_ref, o_ref, kc_scr, vc_scr, pt_scr, o_scr, s_scr, *, n_cmp_pad, n_blocks):
    i = pl.program_id(0)
    s0 = i * Q_BLOCK
    nq = Q_BLOCK
    rows = HEADS_PER_KV * nq

    @pl.when(i == 0)
    def _():
        for g in range(N_KV):
            ck = _r_cols(0, g)
            kc = rp_ref[0:n_cmp_pad, ck:ck + HEAD_DIM] + rp_ref[pl.ds(1, n_cmp_pad), ck + HEAD_DIM:ck + 2 * HEAD_DIM]
            kc_scr[:, g * HEAD_DIM:(g + 1) * HEAD_DIM] = _rope(kc, cosc_ref[...], sinc_ref[...]).astype(BF16)
            cv = _r_cols(1, g)
            vc = rp_ref[0:n_cmp_pad, cv:cv + HEAD_DIM] + rp_ref[pl.ds(1, n_cmp_pad), cv + HEAD_DIM:cv + 2 * HEAD_DIM]
            vc_scr[:, g * HEAD_DIM:(g + 1) * HEAD_DIM] = vc.astype(BF16)
        pt_scr[...] = jnp.zeros_like(pt_scr)

    t_rows = s0 + (lax.broadcasted_iota(jnp.int32, (rows, 1), 0) & (nq - 1))
    scores = []

    for g in range(N_KV):
        gs = slice(g * HEAD_DIM, (g + 1) * HEAD_DIM)
        qg = jnp.concatenate([q_ref[:, (g * HEADS_PER_KV + r) * HEAD_DIM:(g * HEADS_PER_KV + r + 1) * HEAD_DIM]
                              for r in range(HEADS_PER_KV)], axis=0)

        c_end = lax.broadcasted_iota(jnp.int32, (rows, n_cmp_pad), 1) * CMP_STRIDE + (CMP_BLOCK - 1)
        p_c = _masked_softmax(_dot_nt(qg, kc_scr[:, gs]), c_end <= t_rows)
        o_c = _dot(p_c.astype(BF16), vc_scr[:, gs])

        p_grp = p_c[0:nq] + p_c[nq:2 * nq] + p_c[2 * nq:3 * nq] + p_c[3 * nq:4 * nq]
        pt_scr[8:8 + n_cmp_pad, :] = p_grp.T
        st = lambda k: pt_scr[pl.ds(7 + k, n_blocks, stride=CHUNKS_PER_SEL), :]
        score = 0.5 * st(0) + st(1) + st(2) + st(3) + 0.5 * st(4)
        b_io = lax.broadcasted_iota(jnp.int32, (n_blocks, nq), 0)
        t_lane = s0 + lax.broadcasted_iota(jnp.int32, (n_blocks, nq), 1)
        cur = t_lane >> 6
        forced = (b_io == 0) | (b_io == cur) | (b_io == cur - 1)
        valid = b_io * SEL_BLOCK <= t_lane
        scores.append(jnp.where(forced, jnp.inf, jnp.where(valid, score, -jnp.inf)))

        w0 = pl.multiple_of(s0, Q_BLOCK)
        j_io = lax.broadcasted_iota(jnp.int32, (rows, P_WKEYS), 1)
        ti = lax.broadcasted_iota(jnp.int32, (rows, P_WKEYS), 0) & (nq - 1)
        m_w = (j_io > ti) & (j_io <= ti + WINDOW) & (j_io >= WINDOW - s0)
        p_w = _masked_softmax(_dot_nt(qg, kw_ref[pl.ds(w0, P_WKEYS), gs]), m_w)
        o_w = _dot(p_w.astype(BF16), vw_ref[pl.ds(w0, P_WKEYS), gs])

        for r in range(HEADS_PER_KV):
            hd = g * HEADS_PER_KV + r
            rs = slice(r * nq, (r + 1) * nq)
            gc = gate_ref[:, hd * N_BRANCH + 0:hd * N_BRANCH + 1]
            gw = gate_ref[:, hd * N_BRANCH + 2:hd * N_BRANCH + 3]
            o_scr[:, hd * HEAD_DIM:(hd + 1) * HEAD_DIM] = gc * o_c[rs] + gw * o_w[rs]

    b_f = lax.broadcasted_iota(jnp.int32, (n_blocks, nq), 0).astype(F32)

    def pick(_, c):
        out = []
        for work, sel in c:
            m = jnp.max(work, axis=0, keepdims=True)
            idx = jnp.min(jnp.where(work == m, b_f, float(n_blocks)), axis=0, keepdims=True)
            hit = b_f == idx
            out.append((jnp.where(hit, -jnp.inf, work), jnp.where(hit, 1.0, sel)))
        return tuple(out)

    picked = lax.fori_loop(0, min(N_SEL, n_blocks), pick,
                           tuple((sc, jnp.zeros((n_blocks, nq), F32)) for sc in scores))

    rhs = []
    for g in range(N_KV):
        sel_bias = jnp.where(picked[g][1] > 0.5, 0.0, NEG).astype(BF16)
        q_t = jnp.concatenate([qt_ref[(g * HEADS_PER_KV + r) * HEAD_DIM:(g * HEADS_PER_KV + r + 1) * HEAD_DIM, :]
                               for r in range(HEADS_PER_KV)], axis=1)
        rhs.append(jnp.concatenate([q_t, jnp.concatenate([sel_bias] * HEADS_PER_KV, axis=1)], axis=0))

    def qk_scores(c, slot):
        k0 = pl.multiple_of(c * P_CK, P_CK)
        blk_hot = et_ref[pl.ds(k0, P_CK), :]
        for g in range(N_KV):
            gs = slice(g * HEAD_DIM, (g + 1) * HEAD_DIM)
            s_scr[slot, g] = _dot(jnp.concatenate([ks_ref[pl.ds(k0, P_CK), gs], blk_hot], axis=1), rhs[g])

    def softmax_pv(c, slot, carry, causal):
        if causal:
            key = c * P_CK + lax.broadcasted_iota(jnp.int32, (P_CK, rows), 0)
            t_q = s0 + (lax.broadcasted_iota(jnp.int32, (P_CK, rows), 1) & (nq - 1))
            cb = jnp.where(key <= t_q, 0.0, NEG)
        out = []
        for g in range(N_KV):
            gs = slice(g * HEAD_DIM, (g + 1) * HEAD_DIM)
            m_i, l_i, acc = carry[g]
            s = s_scr[slot, g]
            if causal:
                s = s + cb
            m_n = jnp.maximum(m_i, jnp.max(s, axis=0, keepdims=True))
            p = jnp.exp2(s - m_n)
            alpha = jnp.exp2(m_i - m_n)
            l_n = alpha * l_i + jnp.sum(p, axis=0, keepdims=True)
            out.append((m_n, l_n, alpha * acc + _dot(vst_ref[c, gs, :], p.astype(BF16))))
        return tuple(out)

    def pair(pr, carry):
        c = 2 * pr
        qk_scores(c + 1, 1)
        carry = softmax_pv(c, 0, carry, False)
        qk_scores(c + 2, 0)
        return softmax_pv(c + 1, 1, carry, False)

    def odd_tail(c, carry):
        qk_scores(c + 1, 1)
        return softmax_pv(c, 0, carry, False)

    n_chunks = (s0 + nq + P_CK - 1) // P_CK
    n_pairs = (n_chunks - 1) // 2
    init = tuple((jnp.full((1, rows), NEG, F32), jnp.zeros((1, rows), F32), jnp.zeros((HEAD_DIM, rows), F32))
                 for _ in range(N_KV))
    qk_scores(0, 0)
    carry = lax.fori_loop(0, n_pairs, pair, init)
    last = n_chunks - 1
    carry = lax.cond(last > 2 * n_pairs, lambda cr: odd_tail(last - 1, cr), lambda cr: cr, carry)
    carry = lax.cond(last > 2 * n_pairs, lambda cr: softmax_pv(last, 1, cr, True),
                     lambda cr: softmax_pv(last, 0, cr, True), carry)
    for g in range(N_KV):
        _, l_s, acc = carry[g]
        o_st = acc / jnp.maximum(l_s, TINY)
        for r in range(HEADS_PER_KV):
            hd = g * HEADS_PER_KV + r
            gsl = gate_ref[:, hd * N_BRANCH + 1:hd * N_BRANCH + 2]
            o_scr[:, hd * HEAD_DIM:(hd + 1) * HEAD_DIM] += gsl * o_st[:, r * nq:(r + 1) * nq].T

    o_ref[...] = (_rms(o_scr[...]) * nao_ref[...]).astype(o_ref.dtype)


def _prompt_attention(q, q_t, gates, rp, cosc, sinc, ks, vs_t, kw_pad, vw_pad, e_t, norm_ao):
    s_len = q.shape[0]
    n_cmp_pad = s_len // CMP_STRIDE
    n_blocks = s_len // SEL_BLOCK
    vmem = pl.BlockSpec(memory_space=pltpu.VMEM)
    rows = lambda wdt: pl.BlockSpec((Q_BLOCK, wdt), lambda i: (i, 0))
    return pl.pallas_call(
        functools.partial(_pattn_kernel, n_cmp_pad=n_cmp_pad, n_blocks=n_blocks),
        grid=(s_len // Q_BLOCK,),
        in_specs=[rows(ATT_W), pl.BlockSpec((ATT_W, Q_BLOCK), lambda i: (0, i)), rows(GATE_PAD),
                  vmem, vmem, vmem, vmem, vmem, vmem, vmem, vmem, pl.BlockSpec((1, ATT_W), lambda i: (0, 0))],
        out_specs=rows(ATT_W),
        out_shape=jax.ShapeDtypeStruct((s_len, ATT_W), BF16),
        scratch_shapes=[pltpu.VMEM((n_cmp_pad, KV_W), BF16), pltpu.VMEM((n_cmp_pad, KV_W), BF16),
                        pltpu.VMEM((n_cmp_pad + 16, Q_BLOCK), F32), pltpu.VMEM((Q_BLOCK, ATT_W), F32),
                        pltpu.VMEM((2, N_KV, P_CK, HEADS_PER_KV * Q_BLOCK), F32)],
        compiler_params=_cparams("arbitrary"),
        name="prompt_attn",
    )(q, q_t, gates, rp, cosc, sinc, ks, vs_t, kw_pad, vw_pad, e_t, norm_ao)


def _sattn_kernel(pt_ref, *refs, n_seqs, n_pages, page, n_new, past, wbuf):
    del pt_ref
    n_pg = n_seqs * n_pages
    (rnew_ref, q_ref, gate_ref, snew_ref, wst_ref, wnew_ref, cosc_ref, sinc_ref, e_ref, nao_ref,
     o_ref, wout_ref, r_all_scr, k_all_scr, v_all_scr, kw_all_scr, vw_all_scr, o_scr) = refs[2 * n_pg:]
    for sq in range(n_seqs):
        _sattn_one(refs[sq * n_pages:(sq + 1) * n_pages], refs[n_pg + sq * n_pages:n_pg + (sq + 1) * n_pages],
                   rnew_ref.at[sq], q_ref, gate_ref, snew_ref, wst_ref, wnew_ref, cosc_ref, sinc_ref, e_ref, nao_ref,
                   o_ref, wout_ref, r_all_scr.at[sq], k_all_scr.at[sq], v_all_scr.at[sq], kw_all_scr.at[sq],
                   vw_all_scr.at[sq], o_scr, sq=sq, n_pages=n_pages, page=page, n_new=n_new, past=past, wbuf=wbuf)


def _sattn_one(r_pages, s_pages, rnew_ref, q_ref, gate_ref, snew_ref, wst_ref, wnew_ref, cosc_ref, sinc_ref, e_ref,
               nao_ref, o_ref, wout_ref, r_scr, k_scr, v_scr, kw_scr, vw_scr, o_scr, *, sq, n_pages, page, n_new, past,
               wbuf):
    chunks_pp = page // CMP_STRIDE
    n_cmp = n_pages * chunks_pp
    rows = HEADS_PER_KV * n_new
    n_keys = k_scr.shape[0]
    n_wkeys = kw_scr.shape[0]
    q0 = sq * n_new
    qs = slice(q0, q0 + n_new)
    n0 = sq * n_new * KV_ROWS
    w0 = sq * wbuf * KV_ROWS

    for p in range(n_pages):
        r_scr[p * chunks_pp:(p + 1) * chunks_pp, :] = r_pages[p][...]
        for g in range(N_KV):
            gs = slice(g * HEAD_DIM, (g + 1) * HEAD_DIM)
            k_scr[p * page:(p + 1) * page, gs] = s_pages[p][pl.ds(g, page, stride=KV_ROWS), :].astype(BF16)
            v_scr[p * page:(p + 1) * page, gs] = s_pages[p][pl.ds(N_KV + g, page, stride=KV_ROWS), :].astype(BF16)
    r_scr[n_cmp:n_cmp + 8, :] = jnp.concatenate([rnew_ref[...], jnp.zeros((7, r_scr.shape[1]), F32)], axis=0)
    pad = n_keys - past
    wpad = n_wkeys - wbuf

    def with_zero_rows(new_rows, n_zero):
        return jnp.concatenate([new_rows, jnp.zeros((n_zero, HEAD_DIM), F32)], axis=0).astype(BF16)

    for g in range(N_KV):
        gs = slice(g * HEAD_DIM, (g + 1) * HEAD_DIM)
        k_scr[past:n_keys, gs] = with_zero_rows(snew_ref[pl.ds(n0 + g, n_new, stride=KV_ROWS), :], pad - n_new)
        v_scr[past:n_keys, gs] = with_zero_rows(snew_ref[pl.ds(n0 + N_KV + g, n_new, stride=KV_ROWS), :], pad - n_new)
        kw_scr[0:wbuf, gs] = wst_ref[pl.ds(w0 + g, wbuf, stride=KV_ROWS), :].astype(BF16)
        vw_scr[0:wbuf, gs] = wst_ref[pl.ds(w0 + N_KV + g, wbuf, stride=KV_ROWS), :].astype(BF16)
        kw_scr[wbuf:n_wkeys, gs] = with_zero_rows(wnew_ref[pl.ds(n0 + g, n_new, stride=KV_ROWS), :], wpad - n_new)
        vw_scr[wbuf:n_wkeys, gs] = with_zero_rows(wnew_ref[pl.ds(n0 + N_KV + g, n_new, stride=KV_ROWS), :],
                                                  wpad - n_new)

    keep = (wbuf - n_new) * KV_ROWS
    wout_ref[w0:w0 + keep, :] = wst_ref[w0 + n_new * KV_ROWS:w0 + wbuf * KV_ROWS, :]
    wout_ref[w0 + keep:w0 + wbuf * KV_ROWS, :] = wnew_ref[n0:n0 + n_new * KV_ROWS, :]

    tok = lax.broadcasted_iota(jnp.int32, (rows, 1), 0) & (n_new - 1)
    t_rows = past + tok
    lane = lax.broadcasted_iota(jnp.int32, (n_new, HEAD_DIM), 1)
    n_pb = past // SEL_BLOCK

    for g in range(N_KV):
        gs = slice(g * HEAD_DIM, (g + 1) * HEAD_DIM)
        qg = jnp.concatenate([q_ref[qs, (g * HEADS_PER_KV + r) * HEAD_DIM:(g * HEADS_PER_KV + r + 1) * HEAD_DIM]
                              for r in range(HEADS_PER_KV)], axis=0).astype(BF16)

        ck, cv = _r_cols(0, g), _r_cols(1, g)
        kc = r_scr[0:n_cmp, ck:ck + HEAD_DIM] + r_scr[pl.ds(1, n_cmp), ck + HEAD_DIM:ck + 2 * HEAD_DIM]
        kc = _rope(kc, cosc_ref[...], sinc_ref[...]).astype(BF16)
        vc = (r_scr[0:n_cmp, cv:cv + HEAD_DIM] + r_scr[pl.ds(1, n_cmp), cv + HEAD_DIM:cv + 2 * HEAD_DIM]).astype(BF16)

        c_end = lax.broadcasted_iota(jnp.int32, (rows, n_cmp), 1) * CMP_STRIDE + (CMP_BLOCK - 1)
        p_c = _masked_softmax(_dot_nt(qg, kc), c_end <= t_rows)
        o_c = _dot(p_c.astype(BF16), vc)

        p_grp = p_c[0:n_new]
        for r in range(1, HEADS_PER_KV):
            p_grp = p_grp + p_c[r * n_new:(r + 1) * n_new]
        pch = 0.5 * (p_grp + jnp.where(lane >= 1, pltpu.roll(p_grp, 1, axis=1), 0.0))
        score = pch
        for k in range(1, CHUNKS_PER_SEL):
            score = score + pltpu.roll(pch, HEAD_DIM - k, axis=1)
        blk = lane >> 2
        t_tok = past + lax.broadcasted_iota(jnp.int32, (n_new, HEAD_DIM), 0)
        cur = t_tok >> 6
        forced = (blk == 0) | (blk == cur) | (blk == cur - 1)
        score = jnp.where(forced, jnp.inf, score)
        ahead = jnp.zeros((n_new, HEAD_DIM), F32)
        for k in range(1, n_pb):
            other = pltpu.roll(score, CHUNKS_PER_SEL * k, axis=1)
            wins = (other > score) | ((other == score) & (blk >= k))
            ahead = ahead + jnp.where(wins, 1.0, 0.0)
        sel = jnp.where(((lane & (CHUNKS_PER_SEL - 1)) == 0) & (ahead < N_SEL - 1), 1.0, 0.0).astype(BF16)
        sel_keys = _dot(sel, e_ref[...])
        new_ok = lax.broadcasted_iota(jnp.int32, (n_new, pad), 1) <= lax.broadcasted_iota(jnp.int32, (n_new, pad), 0)
        bias = jnp.concatenate([jnp.where(sel_keys > 0.5, 0.0, NEG), jnp.where(new_ok, 0.0, NEG)], axis=1)
        bias = jnp.concatenate([bias] * HEADS_PER_KV, axis=0)
        s_s = _dot_nt(qg, k_scr[:, gs]) + bias
        e_s = jnp.exp2(s_s - jnp.max(s_s, axis=-1, keepdims=True))
        p_s = e_s / jnp.maximum(jnp.sum(e_s, axis=-1, keepdims=True), TINY)
        o_s = _dot(p_s.astype(BF16), v_scr[:, gs])

        j_io = lax.broadcasted_iota(jnp.int32, (rows, n_wkeys), 1)
        m_w = (((j_io < wbuf) & (j_io > tok + (wbuf - WINDOW)) & (j_io >= wbuf - past))
               | ((j_io >= wbuf) & (j_io - wbuf <= tok)))
        p_w = _masked_softmax(_dot_nt(qg, kw_scr[:, gs]), m_w)
        o_w = _dot(p_w.astype(BF16), vw_scr[:, gs])

        for r in range(HEADS_PER_KV):
            hd = g * HEADS_PER_KV + r
            rs = slice(r * n_new, (r + 1) * n_new)
            gc = gate_ref[qs, hd * N_BRANCH + 0:hd * N_BRANCH + 1]
            gsl = gate_ref[qs, hd * N_BRANCH + 1:hd * N_BRANCH + 2]
            gw = gate_ref[qs, hd * N_BRANCH + 2:hd * N_BRANCH + 3]
            o_scr[qs, hd * HEAD_DIM:(hd + 1) * HEAD_DIM] = gc * o_c[rs] + gsl * o_s[rs] + gw * o_w[rs]

    o_ref[qs, :] = _rms(o_scr[qs, :]) * nao_ref[...]


S_SEQS = 2


def _sample_attention(page_table, r_all, r_new, q, gates, slc_cache, slc_new, win_state, win_new, cosc, sinc, e_mat,
                      norm_ao, *, n_new, page, wbuf):
    n_seq, n_pages = page_table.shape
    past = n_pages * page
    chunks_pp = page // CMP_STRIDE
    n_cmp = n_pages * chunks_pp
    assert n_cmp == HEAD_DIM and n_new == 8 and past % SEL_BLOCK == 0 and n_new <= SEL_BLOCK
    assert (past + n_new - 1) // SEL_BLOCK == past // SEL_BLOCK and wbuf == WINDOW and past >= WINDOW
    n_keys = past + HEAD_DIM
    n_wkeys = wbuf + HEAD_DIM
    r_w = r_all.shape[1]

    ns = S_SEQS
    assert n_seq % ns == 0
    page_map = lambda sq, p: (lambda b, pt: (pt[b * ns + sq, p], 0))
    in_specs = [pl.BlockSpec((chunks_pp, r_w), page_map(sq, p)) for sq in range(ns) for p in range(n_pages)]
    in_specs += [pl.BlockSpec((page * KV_ROWS, HEAD_DIM), page_map(sq, p)) for sq in range(ns) for p in range(n_pages)]
    seq_rows = lambda wdt: pl.BlockSpec((ns * n_new, wdt), lambda b, pt: (b, 0))
    kv_rows = lambda n_tok: pl.BlockSpec((ns * n_tok * KV_ROWS, HEAD_DIM), lambda b, pt: (b, 0))
    const = lambda shape: pl.BlockSpec(shape, lambda b, pt: (0,) * len(shape))
    in_specs += [pl.BlockSpec((ns, 1, r_w), lambda b, pt: (b, 0, 0)), seq_rows(ATT_W), seq_rows(GATE_PAD),
                 kv_rows(n_new), kv_rows(wbuf), kv_rows(n_new),
                 const((n_cmp, HEAD_DIM)), const((n_cmp, HEAD_DIM)), const((HEAD_DIM, past)), const((1, ATT_W))]
    grid_spec = pltpu.PrefetchScalarGridSpec(
        num_scalar_prefetch=1,
        grid=(n_seq // ns,),
        in_specs=in_specs,
        out_specs=[seq_rows(ATT_W), kv_rows(wbuf)],
        scratch_shapes=[pltpu.VMEM((ns, n_cmp + 8, r_w), F32), pltpu.VMEM((ns, n_keys, KV_W), BF16),
                        pltpu.VMEM((ns, n_keys, KV_W), BF16), pltpu.VMEM((ns, n_wkeys, KV_W), BF16),
                        pltpu.VMEM((ns, n_wkeys, KV_W), BF16), pltpu.VMEM((ns * n_new, ATT_W), F32)],
    )
    return pl.pallas_call(
        functools.partial(_sattn_kernel, n_seqs=ns, n_pages=n_pages, page=page, n_new=n_new, past=past, wbuf=wbuf),
        grid_spec=grid_spec,
        out_shape=[jax.ShapeDtypeStruct((n_seq * n_new, ATT_W), F32),
                   jax.ShapeDtypeStruct((n_seq * wbuf * KV_ROWS, HEAD_DIM), F32)],
        compiler_params=_cparams("arbitrary"),
        name="sample_attn",
    )(page_table, *([r_all] * (ns * n_pages)), *([slc_cache] * (ns * n_pages)), r_new, q, gates, slc_new, win_state,
      win_new, cosc, sinc, e_mat, norm_ao)


def _outproj_kernel(x_ref, oa_ref, oc_ref, gt_ref, w_ref, o_ref):
    y = _dot(oa_ref[...].astype(BF16), w_ref[0:ATT_W, :]) + _dot(oc_ref[...].astype(BF16), w_ref[ATT_W:D_MODEL, :])
    o_ref[...] = x_ref[...] + (1.0 + gt_ref[...]) * y


def _outproj(x, oa, oc, gate, w, tm=512):
    t = x.shape[0]
    mrows = gate.shape[0]
    mod_spec = (pl.BlockSpec((1, D_MODEL), lambda i: (0, 0)) if mrows == 1
                else pl.BlockSpec((tm, D_MODEL), lambda i: (i, 0)))
    rows = lambda wdt: pl.BlockSpec((tm, wdt), lambda i: (i, 0))
    return pl.pallas_call(
        _outproj_kernel,
        grid=(t // tm,),
        in_specs=[rows(D_MODEL), rows(ATT_W), rows(CONV_W), mod_spec, pl.BlockSpec(memory_space=pltpu.VMEM)],
        out_specs=rows(D_MODEL),
        out_shape=jax.ShapeDtypeStruct((t, D_MODEL), F32),
        compiler_params=_cparams("arbitrary"),
        name="outproj",
    )(x, oa, oc, gate, w)


def _rope_tables(pos):
    half = HEAD_DIM // 2
    inv = ROPE_THETA ** (-jnp.arange(half, dtype=F32) * 2.0 / HEAD_DIM)
    ang = pos.astype(F32)[:, None] * inv[None, :]
    cos, sin = jnp.cos(ang), jnp.sin(ang)
    return jnp.concatenate([cos, cos], axis=1), jnp.concatenate([-sin, sin], axis=1)


def _pack_w_in(w_in):
    off_kv = ATT_W
    off_g = off_kv + 3 * SLAB_W
    off_c = off_g + N_HEADS * N_BRANCH
    gate_cols = jnp.pad(w_in[:, off_g:off_c], ((0, 0), (0, GATE_PAD - N_HEADS * N_BRANCH)))
    return jnp.concatenate([w_in[:, 0:off_g], w_in[:, off_c:off_c + 3 * CONV_W], gate_cols], axis=1).astype(BF16)


def _pack_w_cmp(w_ck, w_cv, n_j):
    def one(w):
        lo = w[0:n_j].reshape(n_j * HEAD_DIM, HEAD_DIM)
        hi = w[CMP_STRIDE:CMP_STRIDE + n_j].reshape(n_j * HEAD_DIM, HEAD_DIM)
        return jnp.concatenate([lo, hi], axis=1)
    return jnp.stack([one(w_ck), one(w_cv)]).astype(BF16)


def kernel(x_prompt, x_sample, c_prompt, c_sample, cache_cmp_kv, cache_slc_kv, state_win_kv, state_conv, page_table,
           w_ada, b_ada, norm_ffn1, ffn1_gate, ffn1_up, ffn1_down, norm_mix, w_in, w_cmp_k, w_cmp_v, conv_w, conv_b,
           norm_att_out, norm_conv_out, w_out, norm_ffn2, ffn2_gate, ffn2_up, ffn2_down, norm_final):
    n_p, s_len, _ = x_prompt.shape
    n_seq, n_new, _ = x_sample.shape
    depth = w_ada.shape[0]
    assert n_p == 1 and depth == 1
    n_pages = page_table.shape[1]
    page = cache_slc_kv.shape[2]
    n_phys = cache_slc_kv.shape[1]
    past = n_pages * page
    wbuf = state_win_kv.shape[2]
    keep_p = min(WINDOW, s_len)
    t_s = n_seq * n_new
    l = 0

    c_all = jnp.concatenate([c_sample, c_prompt, jnp.zeros((8 - n_p, D_MODEL), F32)], axis=0)
    mod = _ada(c_all, w_ada[l], b_ada[l])
    mod_p = [mod[n_seq:n_seq + 1, k * D_MODEL:(k + 1) * D_MODEL] for k in range(N_MOD)]
    mod_s = [jnp.repeat(mod[0:n_seq, k * D_MODEL:(k + 1) * D_MODEL], n_new, axis=0) for k in range(N_MOD)]

    row = lambda v: v.reshape(1, -1)
    f1 = (ffn1_gate[l].astype(BF16), ffn1_up[l].astype(BF16), ffn1_down[l].astype(BF16))
    f2 = (ffn2_gate[l].astype(BF16), ffn2_up[l].astype(BF16), ffn2_down[l].astype(BF16))
    w_proj = _pack_w_in(w_in[l])
    w_o = w_out[l].astype(BF16)
    nfin = row(norm_final)

    xp = x_prompt.reshape(s_len, D_MODEL)
    xs = x_sample.reshape(t_s, D_MODEL)

    xp = _ffn(xp, mod_p[0], mod_p[1], mod_p[2], row(norm_ffn1[l]), nfin, *f1, final_norm=False)
    xs = _ffn(xs, mod_s[0], mod_s[1], mod_s[2], row(norm_ffn1[l]), nfin, *f1, final_norm=False)

    cos_p, sin_p = _rope_tables(jnp.arange(s_len))
    cos_s, sin_s = _rope_tables(past + jnp.arange(n_new))
    tm_s = 256
    cos_s, sin_s = jnp.tile(cos_s, (tm_s // n_new, 1)), jnp.tile(sin_s, (tm_s // n_new, 1))
    zero8 = jnp.zeros((8, CONV_W), F32)
    conv_args = (conv_w[l], row(conv_b[l]), row(norm_conv_out[l]))
    (q_p, cmp_p, slc_p, win_p, ksb_p, vst_p, kwb_p, vwb_p, gate_p, ocn_p, utail_p, qt_p) = _proj(
        xp, mod_p[3], mod_p[4], row(norm_mix[l]), w_proj, cos_p, sin_p, zero8, zero8, *conv_args,
        carry=True, seq_rows=s_len, act_dtype=BF16)
    prev1 = jnp.repeat(state_conv[l][:, CONV_K - 2], n_new, axis=0)
    prev2 = jnp.repeat(state_conv[l][:, CONV_K - 3], n_new, axis=0)
    (q_s, cmp_s, slc_s, win_s, _, _, _, _, gate_s, ocn_s, u_s, _) = _proj(
        xs, mod_s[3], mod_s[4], row(norm_mix[l]), w_proj, cos_s, sin_s, prev1, prev2, *conv_args,
        carry=False, seq_rows=n_new, act_dtype=F32, tm=tm_s)

    n_j = CMP_STRIDE
    w_c = _pack_w_cmp(w_cmp_k[l], w_cmp_v[l], n_j)
    lin = lambda a: a.reshape(-1, HEAD_DIM)
    r_p = _compress_products(cmp_p, w_c, n_j, tm=256)
    r_cache = _compress_products(lin(cache_cmp_kv), w_c, n_j, tm=256)
    r_new = _compress_products(cmp_s, _pack_w_cmp(w_cmp_k[l], w_cmp_v[l], n_new), n_new, tm=n_seq)

    n_cmp_pad = s_len // CMP_STRIDE
    cosc, sinc = _rope_tables(jnp.arange(n_cmp_pad) * CMP_STRIDE + (CMP_BLOCK - 1))
    r_p = jnp.pad(r_p, ((0, 8), (0, 0)))
    band = ((WINDOW, 0), (0, 0))
    n_blocks = s_len // SEL_BLOCK
    e_t = ((jnp.arange(s_len) // SEL_BLOCK)[:, None] == jnp.arange(n_blocks)[None, :]).astype(BF16)
    oa_p = _prompt_attention(q_p, qt_p, gate_p, r_p, cosc, sinc, ksb_p, vst_p, jnp.pad(kwb_p, band),
                             jnp.pad(vwb_p, band), e_t, row(norm_att_out[l]))

    n_cmp_s = past // CMP_STRIDE
    cosc_s, sinc_s = _rope_tables(jnp.arange(n_cmp_s) * CMP_STRIDE + (CMP_BLOCK - 1))
    e_s =(jnp.arange(HEAD_DIM)[:, None] == (jnp.arange(past) // SEL_BLOCK * CHUNKS_PER_SEL)[None, :]).astype(BF16)
    oa_s, win_new_state = _sample_attention(
        page_table, r_cache, r_new.reshape(n_seq, 1, -1), q_s, gate_s,
        lin(cache_slc_kv), slc_s, lin(state_win_kv), win_s,
        cosc_s, sinc_s, e_s, row(norm_att_out[l]), n_new=n_new, page=page, wbuf=wbuf)

    xp = _outproj(xp, oa_p, ocn_p, mod_p[5], w_o)
    xs = _outproj(xs, oa_s, ocn_s, mod_s[5], w_o)
    yp = _ffn(xp, mod_p[6], mod_p[7], mod_p[8], row(norm_ffn2[l]), nfin, *f2, final_norm=True)
    ys = _ffn(xs, mod_s[6], mod_s[7], mod_s[8], row(norm_ffn2[l]), nfin, *f2, final_norm=True)

    kv6 = lambda a, n, s: a.reshape(1, n, s, 2, N_KV, HEAD_DIM)
    return (yp.reshape(n_p, s_len, D_MODEL), ys.reshape(n_seq, n_new, D_MODEL),
            kv6(cmp_p, n_p, s_len), kv6(slc_p, n_p, s_len), kv6(win_p[(s_len - keep_p) * KV_ROWS:], n_p, keep_p),
            utail_p[8 - (CONV_K - 1):].reshape(1, n_p, CONV_K - 1, CONV_W),
            kv6(cmp_s, n_seq, n_new), kv6(slc_s, n_seq, n_new), kv6(win_new_state, n_seq, wbuf),
            u_s.reshape(n_seq, n_new, CONV_W)[:, n_new - (CONV_K - 1):].reshape(1, n_seq, CONV_K - 1, CONV_W))
```

```python
import functools

import jax
import jax.numpy as jnp
from jax import lax
from jax.experimental import pallas as pl
from jax.experimental.pallas import tpu as pltpu

F32 = jnp.float32
BF16 = jnp.bfloat16

D_MODEL = 2048
HEAD_DIM = 128
N_HEADS = 8
N_KV = 2
HEADS_PER_KV = N_HEADS // N_KV
ATT_W = N_HEADS * HEAD_DIM
KV_W = N_KV * HEAD_DIM
CONV_W = D_MODEL - ATT_W
CONV_K = 3
CMP_BLOCK = 32
CMP_STRIDE = 16
SEL_BLOCK = 64
N_SEL = 16
WINDOW = 512
Q_BLOCK = 128
N_BRANCH = 3
N_MOD = 9
ROPE_THETA = 10000.0
EPS = 1e-6
NEG = -1e30
TINY = 1e-30
SCALE = HEAD_DIM ** -0.5
LOG2E = 1.4426950408889634
SLAB_W = 2 * KV_W
KV_ROWS = 2 * N_KV
GATE_PAD = 128
CHUNKS_PER_SEL = SEL_BLOCK // CMP_STRIDE

VMEM_LIMIT = 56 * 1024 * 1024


def _cparams(*sem):
    return pltpu.CompilerParams(dimension_semantics=sem, vmem_limit_bytes=VMEM_LIMIT)


def _dot(a, b):
    return jnp.dot(a, b, preferred_element_type=F32)


def _dot_nt(a, b):
    return lax.dot_general(a, b, (((1,), (1,)), ((), ())), preferred_element_type=F32)


def _rms(x):
    return x * lax.rsqrt(jnp.mean(x * x, axis=-1, keepdims=True) + EPS)


def _silu(x):
    return x * jax.nn.sigmoid(x)


def _rope(x, cos, sin_signed):
    return x * cos + pltpu.roll(x, HEAD_DIM // 2, axis=1) * sin_signed


def _masked_softmax(s, mask):
    s = jnp.where(mask, s, NEG)
    m = jnp.max(s, axis=-1, keepdims=True)
    e = jnp.where(mask, jnp.exp2(s - m), 0.0)
    return e / jnp.maximum(jnp.sum(e, axis=-1, keepdims=True), TINY)


def _ada_kernel(c_ref, w_ref, b_ref, o_ref):
    a = _silu(c_ref[...]).astype(BF16)
    o_ref[...] = _dot(a, w_ref[...].astype(BF16)) + b_ref[...]


def _ada(c, w, b, tn=1024):
    m, n = c.shape[0], w.shape[1]
    return pl.pallas_call(
        _ada_kernel,
        grid=(n // tn,),
        in_specs=[pl.BlockSpec((m, D_MODEL), lambda j: (0, 0)),
                  pl.BlockSpec((D_MODEL, tn), lambda j: (0, j)),
                  pl.BlockSpec((1, tn), lambda j: (0, j))],
        out_specs=pl.BlockSpec((m, tn), lambda j: (0, j)),
        out_shape=jax.ShapeDtypeStruct((m, n), F32),
        compiler_params=_cparams("arbitrary"),
        name="ada",
    )(c, w, b.reshape(1, n))


def _ffn_kernel(x_ref, sh_ref, sc_ref, gt_ref, ng_ref, nf_ref, wg_ref, wu_ref, wd_ref, o_ref, h_scr, *, n_f, final_norm):
    j = pl.program_id(1)

    @pl.when(j == 0)
    def _():
        h = _rms(x_ref[...]) * ng_ref[...] * (1.0 + sc_ref[...]) + sh_ref[...]
        h_scr[...] = h.astype(BF16)
        o_ref[...] = jnp.zeros_like(o_ref)

    h = h_scr[...]
    a = (_silu(_dot(h, wg_ref[...])) * _dot(h, wu_ref[...])).astype(BF16)
    o_ref[...] += _dot(a, wd_ref[...])

    @pl.when(j == n_f - 1)
    def _():
        out = x_ref[...] + 0.5 * (1.0 + gt_ref[...]) * o_ref[...]
        if final_norm:
            out = _rms(out) * nf_ref[...]
        o_ref[...] = out


def _ffn(x, shift, scale, gate, norm_g, norm_final, wg, wu, wd, *, final_norm, tm=512, tf=512):
    t = x.shape[0]
    d_ff = wg.shape[1]
    n_f = d_ff // tf
    mrows = shift.shape[0]
    if mrows != 1:
        tm = tm // 2
    mod_spec = (pl.BlockSpec((1, D_MODEL), lambda i, j: (0, 0)) if mrows == 1
                else pl.BlockSpec((tm, D_MODEL), lambda i, j: (i, 0), pipeline_mode=pl.Buffered(1)))
    row_spec = pl.BlockSpec((tm, D_MODEL), lambda i, j: (i, 0))
    vec_spec = pl.BlockSpec((1, D_MODEL), lambda i, j: (0, 0))
    return pl.pallas_call(
        functools.partial(_ffn_kernel, n_f=n_f, final_norm=final_norm),
        grid=(t // tm, n_f),
        in_specs=[row_spec, mod_spec, mod_spec, mod_spec, vec_spec, vec_spec,
                  pl.BlockSpec((D_MODEL, tf), lambda i, j: (0, j)),
                  pl.BlockSpec((D_MODEL, tf), lambda i, j: (0, j)),
                  pl.BlockSpec((tf, D_MODEL), lambda i, j: (j, 0))],
        out_specs=row_spec,
        out_shape=jax.ShapeDtypeStruct((t, D_MODEL), F32),
        scratch_shapes=[pltpu.VMEM((tm, D_MODEL), BF16)],
        compiler_params=_cparams("arbitrary", "arbitrary"),
        name="ffn",
    )(x, shift, scale, gate, norm_g, norm_final, wg, wu, wd)


PW_Q = 0
PW_KV = PW_Q + ATT_W
PW_CONV = PW_KV + 3 * SLAB_W
PW_GATE = PW_CONV + 3 * CONV_W
PW_TOTAL = PW_GATE + GATE_PAD


def _proj_kernel(x_ref, sh_ref, sc_ref, ng_ref, w_ref, cos_ref, sin_ref, p1_ref, p2_ref, cw_ref, cb_ref, nco_ref,
                 q_ref, cmp_ref, slc_ref, win_ref, ksb_ref, vst_ref, kwb_ref, vwb_ref, gate_ref, ocn_ref, u_ref, qt_ref,
                 carry_scr, *, tm, seq_rows, carry):
    i = pl.program_id(0)
    h = (_rms(x_ref[...]) * ng_ref[...] * (1.0 + sc_ref[...]) + sh_ref[...]).astype(BF16)
    cos, sin = cos_ref[...], sin_ref[...]

    pq = _dot(h, w_ref[:, PW_Q:PW_KV])
    for hd in range(N_HEADS):
        hs = slice(hd * HEAD_DIM, (hd + 1) * HEAD_DIM)
        blk = _rope(pq[:, hs], cos, sin) * (SCALE * LOG2E)
        q_ref[:, hs] = blk.astype(q_ref.dtype)
        qt_ref[hs, :] = blk.T.astype(BF16)

    pkv = _dot(h, w_ref[:, PW_KV:PW_CONV])
    for slab, (o_ref, kb_ref, vb_ref) in enumerate(((cmp_ref, None, None), (slc_ref, ksb_ref, None),
                                                    (win_ref, kwb_ref, vwb_ref))):
        base = slab * SLAB_W
        for g in range(N_KV):
            gs = slice(g * HEAD_DIM, (g + 1) * HEAD_DIM)
            k = pkv[:, base + g * HEAD_DIM:base + (g + 1) * HEAD_DIM]
            v = pkv[:, base + KV_W + g * HEAD_DIM:base + KV_W + (g + 1) * HEAD_DIM]
            if kb_ref is not None:
                k = _rope(k, cos, sin)
                kb_ref[:, gs] = k.astype(BF16)
                if vb_ref is not None:
                    vb_ref[:, gs] = v.astype(BF16)
                else:
                    vst_ref[0, gs, :] = v.T.astype(BF16)
            o_ref[pl.ds(g, tm, stride=KV_ROWS), :] = k
            o_ref[pl.ds(N_KV + g, tm, stride=KV_ROWS), :] = v

    gate_ref[...] = jax.nn.sigmoid(_dot(h, w_ref[:, PW_GATE:PW_TOTAL]))

    pc = _dot(h, w_ref[:, PW_CONV:PW_GATE])
    u = pc[:, 0:CONV_W] * pc[:, 2 * CONV_W:3 * CONV_W]
    c_out = pc[:, CONV_W:2 * CONV_W]
    row = lax.broadcasted_iota(jnp.int32, (tm, CONV_W), 0)
    if carry:
        @pl.when(i == 0)
        def _():
            carry_scr[...] = jnp.zeros_like(carry_scr)
        prev1 = carry_scr[7:8, :]
        prev2 = carry_scr[6:7, :]
        rs = row
    else:
        prev1 = p1_ref[...]
        prev2 = p2_ref[...]
        rs = row & (seq_rows - 1)
    um1 = jnp.where(rs >= 1, pltpu.roll(u, 1, axis=0), prev1)
    um2 = jnp.where(rs >= 2, pltpu.roll(u, 2, axis=0), jnp.where(rs == 1, prev1, prev2))
    y = um2 * cw_ref[0:1, :] + um1 * cw_ref[1:2, :] + u * cw_ref[2:3, :] + cb_ref[...]
    ocn_ref[...] = (_rms(c_out * y) * nco_ref[...]).astype(ocn_ref.dtype)
    if carry:
        carry_scr[...] = u[tm - 8:tm, :]
        u_ref[...] = u[tm - 8:tm, :]
    else:
        u_ref[...] = u


def _proj(x, shift, scale, norm_g, w, cos, sin, prev1, prev2, conv_w, conv_b, norm_co, *, carry, seq_rows, act_dtype,
          tm=256):
    t = x.shape[0]
    mrows = shift.shape[0]
    mod_spec = (pl.BlockSpec((1, D_MODEL), lambda i: (0, 0)) if mrows == 1
                else pl.BlockSpec((tm, D_MODEL), lambda i: (i, 0)))
    rows = lambda wdt: pl.BlockSpec((tm, wdt), lambda i: (i, 0))
    const = lambda r, wdt: pl.BlockSpec((r, wdt), lambda i: (0, 0))
    tab_spec = rows(HEAD_DIM) if carry else const(tm, HEAD_DIM)
    prev_spec = const(8, CONV_W) if carry else rows(CONV_W)
    u_rows = 8 if carry else t
    out_shape = [jax.ShapeDtypeStruct((t, ATT_W), act_dtype)]
    out_shape += [jax.ShapeDtypeStruct((t * KV_ROWS, HEAD_DIM), F32)] * 3
    per_ck = P_CK // tm
    kvb = jax.ShapeDtypeStruct((t, KV_W), BF16)
    out_shape += [kvb, jax.ShapeDtypeStruct((t // P_CK, KV_W, P_CK), BF16), kvb, kvb]
    out_shape += [jax.ShapeDtypeStruct((t, GATE_PAD), F32), jax.ShapeDtypeStruct((t, CONV_W), act_dtype),
                  jax.ShapeDtypeStruct((u_rows, CONV_W), F32), jax.ShapeDtypeStruct((ATT_W, t), BF16)]
    lin_spec = pl.BlockSpec((tm * KV_ROWS, HEAD_DIM), lambda i: (i, 0))
    vst_spec = pl.BlockSpec((1, KV_W, tm), lambda i: (i // per_ck, 0, i % per_ck))
    out_specs = [rows(ATT_W)] + [lin_spec] * 3 + [rows(KV_W), vst_spec, rows(KV_W), rows(KV_W)]
    out_specs += [rows(GATE_PAD), rows(CONV_W), const(8, CONV_W) if carry else rows(CONV_W),
                  pl.BlockSpec((ATT_W, tm), lambda i: (0, i))]
    return pl.pallas_call(
        functools.partial(_proj_kernel, tm=tm, seq_rows=seq_rows, carry=carry),
        grid=(t // tm,),
        in_specs=[rows(D_MODEL), mod_spec, mod_spec, const(1, D_MODEL),
                  pl.BlockSpec(memory_space=pltpu.VMEM),
                  tab_spec, tab_spec, prev_spec, prev_spec, const(CONV_K, CONV_W), const(1, CONV_W),
                  const(1, CONV_W)],
        out_specs=out_specs,
        out_shape=out_shape,
        scratch_shapes=[pltpu.VMEM((8, CONV_W), F32)],
        compiler_params=_cparams("arbitrary"),
        name="proj",
    )(x, shift, scale, norm_g, w, cos, sin, prev1, prev2, conv_w, conv_b, norm_co)


def _cmp_kernel(x_ref, w_ref, o_ref, *, n_j, tm):
    pitch = n_j * KV_ROWS
    for kv in range(2):
        for g in range(N_KV):
            xs = jnp.concatenate([x_ref[pl.ds(j * KV_ROWS + kv * N_KV + g, tm, stride=pitch), :] for j in range(n_j)],
                                 axis=1)
            col = (kv * N_KV + g) * 2 * HEAD_DIM
            o_ref[:, col:col + 2 * HEAD_DIM] = _dot(xs.astype(BF16), w_ref[kv])


def _compress_products(x, w, n_j, tm):
    pitch = n_j * KV_ROWS
    m = x.shape[0] // pitch
    n_out = 2 * N_KV * 2 * HEAD_DIM
    tm = min(tm, m)
    return pl.pallas_call(
        functools.partial(_cmp_kernel, n_j=n_j, tm=tm),
        grid=(m // tm,),
        in_specs=[pl.BlockSpec((tm * pitch, HEAD_DIM), lambda i: (i, 0)),
                  pl.BlockSpec((2, n_j * HEAD_DIM, 2 * HEAD_DIM), lambda i: (0, 0, 0))],
        out_specs=pl.BlockSpec((tm, n_out), lambda i: (i, 0)),
        out_shape=jax.ShapeDtypeStruct((m, n_out), F32),
        compiler_params=_cparams("arbitrary"),
        name="compress",
    )(x, w)


def _r_cols(kv, g):
    return (kv * N_KV + g) * 2 * HEAD_DIM


P_CK = 512
P_WKEYS = WINDOW + Q_BLOCK


def _pattn_kernel(q_ref, qt_ref, gate_ref, rp_ref, cosc_ref, sinc_ref, ks_ref, vst_ref, kw_ref, vw_ref, et_ref,
                  nao_ref, o_ref, kc_scr, vc_scr, pt_scr, o_scr, s_scr, *, n_cmp_pad, n_blocks):
    i = pl.program_id(0)
    s0 = i * Q_BLOCK
    nq = Q_BLOCK
    rows = HEADS_PER_KV * nq

    @pl.when(i == 0)
    def _():
        for g in range(N_KV):
            ck = _r_cols(0, g)
            kc = rp_ref[0:n_cmp_pad, ck:ck + HEAD_DIM] + rp_ref[pl.ds(1, n_cmp_pad), ck + HEAD_DIM:ck + 2 * HEAD_DIM]
            kc_scr[:, g * HEAD_DIM:(g + 1) * HEAD_DIM] = _rope(kc, cosc_ref[...], sinc_ref[...]).astype(BF16)
            cv = _r_cols(1, g)
            vc = rp_ref[0:n_cmp_pad, cv:cv + HEAD_DIM] + rp_ref[pl.ds(1, n_cmp_pad), cv + HEAD_DIM:cv + 2 * HEAD_DIM]
            vc_scr[:, g * HEAD_DIM:(g + 1) * HEAD_DIM] = vc.astype(BF16)
        pt_scr[...] = jnp.zeros_like(pt_scr)

    t_rows = s0 + (lax.broadcasted_iota(jnp.int32, (rows, 1), 0) & (nq - 1))
    scores = []

    for g in range(N_KV):
        gs = slice(g * HEAD_DIM, (g + 1) * HEAD_DIM)
        qg = jnp.concatenate([q_ref[:, (g * HEADS_PER_KV + r) * HEAD_DIM:(g * HEADS_PER_KV + r + 1) * HEAD_DIM]
                              for r in range(HEADS_PER_KV)], axis=0)

        c_end = lax.broadcasted_iota(jnp.int32, (rows, n_cmp_pad), 1) * CMP_STRIDE + (CMP_BLOCK - 1)
        p_c = _masked_softmax(_dot_nt(qg, kc_scr[:, gs]), c_end <= t_rows)
        o_c = _dot(p_c.astype(BF16), vc_scr[:, gs])

        p_grp = p_c[0:nq] + p_c[nq:2 * nq] + p_c[2 * nq:3 * nq] + p_c[3 * nq:4 * nq]
        pt_scr[8:8 + n_cmp_pad, :] = p_grp.T
        st = lambda k: pt_scr[pl.ds(7 + k, n_blocks, stride=CHUNKS_PER_SEL), :]
        score = 0.5 * st(0) + st(1) + st(2) + st(3) + 0.5 * st(4)
        b_io = lax.broadcasted_iota(jnp.int32, (n_blocks, nq), 0)
        t_lane = s0 + lax.broadcasted_iota(jnp.int32, (n_blocks, nq), 1)
        cur = t_lane >> 6
        forced = (b_io == 0) | (b_io == cur) | (b_io == cur - 1)
        valid = b_io * SEL_BLOCK <= t_lane
        scores.append(jnp.where(forced, jnp.inf, jnp.where(valid, score, -jnp.inf)))

        w0 = pl.multiple_of(s0, Q_BLOCK)
        j_io = lax.broadcasted_iota(jnp.int32, (rows, P_WKEYS), 1)
        ti = lax.broadcasted_iota(jnp.int32, (rows, P_WKEYS), 0) & (nq - 1)
        m_w = (j_io > ti) & (j_io <= ti + WINDOW) & (j_io >= WINDOW - s0)
        p_w = _masked_softmax(_dot_nt(qg, kw_ref[pl.ds(w0, P_WKEYS), gs]), m_w)
        o_w = _dot(p_w.astype(BF16), vw_ref[pl.ds(w0, P_WKEYS), gs])

        for r in range(HEADS_PER_KV):
            hd = g * HEADS_PER_KV + r
            rs = slice(r * nq, (r + 1) * nq)
            gc = gate_ref[:, hd * N_BRANCH + 0:hd * N_BRANCH + 1]
            gw = gate_ref[:, hd * N_BRANCH + 2:hd * N_BRANCH + 3]
            o_scr[:, hd * HEAD_DIM:(hd + 1) * HEAD_DIM] = gc * o_c[rs] + gw * o_w[rs]

    b_f = lax.broadcasted_iota(jnp.int32, (n_blocks, nq), 0).astype(F32)

    def pick(_, c):
        out = []
        for work, sel in c:
            m = jnp.max(work, axis=0, keepdims=True)
            idx = jnp.min(jnp.where(work == m, b_f, float(n_blocks)), axis=0, keepdims=True)
            hit = b_f == idx
            out.append((jnp.where(hit, -jnp.inf, work), jnp.where(hit, 1.0, sel)))
        return tuple(out)

    picked = lax.fori_loop(0, min(N_SEL, n_blocks), pick,
                           tuple((sc, jnp.zeros((n_blocks, nq), F32)) for sc in scores))

    rhs = []
    for g in range(N_KV):
        sel_bias = jnp.where(picked[g][1] > 0.5, 0.0, NEG).astype(BF16)
        q_t = jnp.concatenate([qt_ref[(g * HEADS_PER_KV + r) * HEAD_DIM:(g * HEADS_PER_KV + r + 1) * HEAD_DIM, :]
                               for r in range(HEADS_PER_KV)], axis=1)
        rhs.append(jnp.concatenate([q_t, jnp.concatenate([sel_bias] * HEADS_PER_KV, axis=1)], axis=0))

    def qk_scores(c, slot):
        k0 = pl.multiple_of(c * P_CK, P_CK)
        blk_hot = et_ref[pl.ds(k0, P_CK), :]
        for g in range(N_KV):
            gs = slice(g * HEAD_DIM, (g + 1) * HEAD_DIM)
            s_scr[slot, g] = _dot(jnp.concatenate([ks_ref[pl.ds(k0, P_CK), gs], blk_hot], axis=1), rhs[g])

    def softmax_pv(c, slot, carry, causal):
        if causal:
            key = c * P_CK + lax.broadcasted_iota(jnp.int32, (P_CK, rows), 0)
            t_q = s0 + (lax.broadcasted_iota(jnp.int32, (P_CK, rows), 1) & (nq - 1))
            cb = jnp.where(key <= t_q, 0.0, NEG)
        out = []
        for g in range(N_KV):
            gs = slice(g * HEAD_DIM, (g + 1) * HEAD_DIM)
            m_i, l_i, acc = carry[g]
            s = s_scr[slot, g]
            if causal:
                s = s + cb
            m_n = jnp.maximum(m_i, jnp.max(s, axis=0, keepdims=True))
            p = jnp.exp2(s - m_n)
            alpha = jnp.exp2(m_i - m_n)
            l_n = alpha * l_i + jnp.sum(p, axis=0, keepdims=True)
            out.append((m_n, l_n, alpha * acc + _dot(vst_ref[c, gs, :], p.astype(BF16))))
        return tuple(out)

    def pair(pr, carry):
        c = 2 * pr
        qk_scores(c + 1, 1)
        carry = softmax_pv(c, 0, carry, False)
        qk_scores(c + 2, 0)
        return softmax_pv(c + 1, 1, carry, False)

    def odd_tail(c, carry):
        qk_scores(c + 1, 1)
        return softmax_pv(c, 0, carry, False)

    n_chunks = (s0 + nq + P_CK - 1) // P_CK
    n_pairs = (n_chunks - 1) // 2
    init = tuple((jnp.full((1, rows), NEG, F32), jnp.zeros((1, rows), F32), jnp.zeros((HEAD_DIM, rows), F32))
                 for _ in range(N_KV))
    qk_scores(0, 0)
    carry = lax.fori_loop(0, n_pairs, pair, init)
    last = n_chunks - 1
    carry = lax.cond(last > 2 * n_pairs, lambda cr: odd_tail(last - 1, cr), lambda cr: cr, carry)
    carry = lax.cond(last > 2 * n_pairs, lambda cr: softmax_pv(last, 1, cr, True),
                     lambda cr: softmax_pv(last, 0, cr, True), carry)
    for g in range(N_KV):
        _, l_s, acc = carry[g]
        o_st = acc / jnp.maximum(l_s, TINY)
        for r in range(HEADS_PER_KV):
            hd = g * HEADS_PER_KV + r
            gsl = gate_ref[:, hd * N_BRANCH + 1:hd * N_BRANCH + 2]
            o_scr[:, hd * HEAD_DIM:(hd + 1) * HEAD_DIM] += gsl * o_st[:, r * nq:(r + 1) * nq].T

    o_ref[...] = (_rms(o_scr[...]) * nao_ref[...]).astype(o_ref.dtype)


def _prompt_attention(q, q_t, gates, rp, cosc, sinc, ks, vs_t, kw_pad, vw_pad, e_t, norm_ao):
    s_len = q.shape[0]
    n_cmp_pad = s_len // CMP_STRIDE
    n_blocks = s_len // SEL_BLOCK
    vmem = pl.BlockSpec(memory_space=pltpu.VMEM)
    rows = lambda wdt: pl.BlockSpec((Q_BLOCK, wdt), lambda i: (i, 0))
    return pl.pallas_call(
        functools.partial(_pattn_kernel, n_cmp_pad=n_cmp_pad, n_blocks=n_blocks),
        grid=(s_len // Q_BLOCK,),
        in_specs=[rows(ATT_W), pl.BlockSpec((ATT_W, Q_BLOCK), lambda i: (0, i)), rows(GATE_PAD),
                  vmem, vmem, vmem, vmem, vmem, vmem, vmem, vmem, pl.BlockSpec((1, ATT_W), lambda i: (0, 0))],
        out_specs=rows(ATT_W),
        out_shape=jax.ShapeDtypeStruct((s_len, ATT_W), BF16),
        scratch_shapes=[pltpu.VMEM((n_cmp_pad, KV_W), BF16), pltpu.VMEM((n_cmp_pad, KV_W), BF16),
                        pltpu.VMEM((n_cmp_pad + 16, Q_BLOCK), F32), pltpu.VMEM((Q_BLOCK, ATT_W), F32),
                        pltpu.VMEM((2, N_KV, P_CK, HEADS_PER_KV * Q_BLOCK), F32)],
        compiler_params=_cparams("arbitrary"),
        name="prompt_attn",
    )(q, q_t, gates, rp, cosc, sinc, ks, vs_t, kw_pad, vw_pad, e_t, norm_ao)


def _sattn_kernel(pt_ref, *refs, n_seqs, n_pages, page, n_new, past, wbuf):
    del pt_ref
    n_pg = n_seqs * n_pages
    r_pages, s_pages = refs[0:n_pg], refs[n_pg:2 * n_pg]
    (rnew_ref, q_ref, gate_ref, snew_ref, wst_ref, wnew_ref, cosc_ref, sinc_ref, e_ref, nao_ref,
     o_ref, wout_ref, r_scr, k_scr, v_scr, kw_scr, vw_scr, o_scr) = refs[2 * n_pg:]
    chunks_pp = page // CMP_STRIDE
    n_cmp = n_pages * chunks_pp
    rows = HEADS_PER_KV * n_new
    n_keys = k_scr.shape[1]
    n_wkeys = kw_scr.shape[1]
    pad = n_keys - past
    wpad = n_wkeys - wbuf
    batch = [(sq, g) for sq in range(n_seqs) for g in range(N_KV)]
    cat = lambda parts: jnp.concatenate(parts, axis=0)

    def with_zero_rows(new_rows, n_zero):
        return cat([new_rows, jnp.zeros((n_zero, HEAD_DIM), F32)]).astype(BF16)

    for sq in range(n_seqs):
        n0 = sq * n_new * KV_ROWS
        w0 = sq * wbuf * KV_ROWS
        for p in range(n_pages):
            r_scr[sq, p * chunks_pp:(p + 1) * chunks_pp, :] = r_pages[sq * n_pages + p][...]
            pg = s_pages[sq * n_pages + p]
            for g in range(N_KV):
                b = sq * N_KV + g
                k_scr[b, p * page:(p + 1) * page, :] = pg[pl.ds(g, page, stride=KV_ROWS), :].astype(BF16)
                v_scr[b, p * page:(p + 1) * page, :] = pg[pl.ds(N_KV + g, page, stride=KV_ROWS), :].astype(BF16)
        r_scr[sq, n_cmp:n_cmp + 8, :] = cat([rnew_ref[sq], jnp.zeros((7, r_scr.shape[2]), F32)])
        for g in range(N_KV):
            b = sq * N_KV + g
            k_scr[b, past:n_keys, :] = with_zero_rows(snew_ref[pl.ds(n0 + g, n_new, stride=KV_ROWS), :], pad - n_new)
            v_scr[b, past:n_keys, :] = with_zero_rows(snew_ref[pl.ds(n0 + N_KV + g, n_new, stride=KV_ROWS), :],
                                                      pad - n_new)
            kw_scr[b, 0:wbuf, :] = wst_ref[pl.ds(w0 + g, wbuf, stride=KV_ROWS), :].astype(BF16)
            vw_scr[b, 0:wbuf, :] = wst_ref[pl.ds(w0 + N_KV + g, wbuf, stride=KV_ROWS), :].astype(BF16)
            kw_scr[b, wbuf:n_wkeys, :] = with_zero_rows(wnew_ref[pl.ds(n0 + g, n_new, stride=KV_ROWS), :],
                                                        wpad - n_new)
            vw_scr[b, wbuf:n_wkeys, :] = with_zero_rows(wnew_ref[pl.ds(n0 + N_KV + g, n_new, stride=KV_ROWS), :],
                                                        wpad - n_new)
        keep = (wbuf - n_new) * KV_ROWS
        wout_ref[w0:w0 + keep, :] = wst_ref[w0 + n_new * KV_ROWS:w0 + wbuf * KV_ROWS, :]
        wout_ref[w0 + keep:w0 + wbuf * KV_ROWS, :] = wnew_ref[n0:n0 + n_new * KV_ROWS, :]

    n_b = len(batch)
    all_rows = n_b * rows
    tok = lax.broadcasted_iota(jnp.int32, (all_rows, 1), 0) & (n_new - 1)
    t_rows = past + tok

    qg, kc, vc = [], [], []
    for sq, g in batch:
        qs = slice(sq * n_new, (sq + 1) * n_new)
        qg.append(cat([q_ref[qs, (g * HEADS_PER_KV + r) * HEAD_DIM:(g * HEADS_PER_KV + r + 1) * HEAD_DIM]
                       for r in range(HEADS_PER_KV)]).astype(BF16))
        ck, cv = _r_cols(0, g), _r_cols(1, g)
        kc.append(r_scr[sq, 0:n_cmp, ck:ck + HEAD_DIM] + r_scr[sq, pl.ds(1, n_cmp), ck + HEAD_DIM:ck + 2 * HEAD_DIM])
        vc.append((r_scr[sq, 0:n_cmp, cv:cv + HEAD_DIM]
                   + r_scr[sq, pl.ds(1, n_cmp), cv + HEAD_DIM:cv + 2 * HEAD_DIM]).astype(BF16))
    cos_all, sin_all = cat([cosc_ref[...]] * n_b), cat([sinc_ref[...]] * n_b)
    kc_all = _rope(cat(kc), cos_all, sin_all).astype(BF16)
    rb = lambda b: slice(b * rows, (b + 1) * rows)

    c_end = lax.broadcasted_iota(jnp.int32, (all_rows, n_cmp), 1) * CMP_STRIDE + (CMP_BLOCK - 1)
    s_c = cat([_dot_nt(qg[b], kc_all[b * n_cmp:(b + 1) * n_cmp]) for b in range(n_b)])
    p_c = _masked_softmax(s_c, c_end <= t_rows)
    p_cb = p_c.astype(BF16)
    o_c = [_dot(p_cb[rb(b)], vc[b]) for b in range(n_b)]

    sel_rows = n_b * n_new
    p_grp = cat([sum(p_c[b * rows + r * n_new:b * rows + (r + 1) * n_new] for r in range(HEADS_PER_KV))
                 for b in range(n_b)])
    lane = lax.broadcasted_iota(jnp.int32, (sel_rows, HEAD_DIM), 1)
    pch = 0.5 * (p_grp + jnp.where(lane >= 1, pltpu.roll(p_grp, 1, axis=1), 0.0))
    score = pch
    for k in range(1, CHUNKS_PER_SEL):
        score = score + pltpu.roll(pch, HEAD_DIM - k, axis=1)
    blk = lane >> 2
    t_tok = past + (lax.broadcasted_iota(jnp.int32, (sel_rows, HEAD_DIM), 0) & (n_new - 1))
    cur = t_tok >> 6
    forced = (blk == 0) | (blk == cur) | (blk == cur - 1)
    score = jnp.where(forced, jnp.inf, score)
    n_pb = past // SEL_BLOCK
    ahead = jnp.zeros((sel_rows, HEAD_DIM), F32)
    for k in range(1, n_pb):
        other = pltpu.roll(score, CHUNKS_PER_SEL * k, axis=1)
        wins = (other > score) | ((other == score) & (blk >= k))
        ahead = ahead + jnp.where(wins, 1.0, 0.0)
    sel = jnp.where(((lane & (CHUNKS_PER_SEL - 1)) == 0) & (ahead < N_SEL - 1), 1.0, 0.0).astype(BF16)
    sel_keys = _dot(sel, e_ref[...])
    new_ok = (lax.broadcasted_iota(jnp.int32, (sel_rows, pad), 1)
              <= (lax.broadcasted_iota(jnp.int32, (sel_rows, pad), 0) & (n_new - 1)))
    bias = jnp.concatenate([jnp.where(sel_keys > 0.5, 0.0, NEG), jnp.where(new_ok, 0.0, NEG)], axis=1)
    bias = cat([bias[b * n_new:(b + 1) * n_new] for b in range(n_b) for _ in range(HEADS_PER_KV)])

    s_s = cat([_dot_nt(qg[b], k_scr[b]) for b in range(n_b)]) + bias
    e_s = jnp.exp2(s_s - jnp.max(s_s, axis=-1, keepdims=True))
    p_s = (e_s / jnp.maximum(jnp.sum(e_s, axis=-1, keepdims=True), TINY)).astype(BF16)
    o_s = [_dot(p_s[rb(b)], v_scr[b]) for b in range(n_b)]

    j_io = lax.broadcasted_iota(jnp.int32, (all_rows, n_wkeys), 1)
    m_w = (((j_io < wbuf) & (j_io > tok + (wbuf - WINDOW)) & (j_io >= wbuf - past))
           | ((j_io >= wbuf) & (j_io - wbuf <= tok)))
    p_w = _masked_softmax(cat([_dot_nt(qg[b], kw_scr[b]) for b in range(n_b)]), m_w).astype(BF16)
    o_w = [_dot(p_w[rb(b)], vw_scr[b]) for b in range(n_b)]

    for b, (sq, g) in enumerate(batch):
        qs = slice(sq * n_new, (sq + 1) * n_new)
        for r in range(HEADS_PER_KV):
            hd = g * HEADS_PER_KV + r
            rs = slice(r * n_new, (r + 1) * n_new)
            gc = gate_ref[qs, hd * N_BRANCH + 0:hd * N_BRANCH + 1]
            gsl = gate_ref[qs, hd * N_BRANCH + 1:hd * N_BRANCH + 2]
            gw = gate_ref[qs, hd * N_BRANCH + 2:hd * N_BRANCH + 3]
            o_scr[qs, hd * HEAD_DIM:(hd + 1) * HEAD_DIM] = gc * o_c[b][rs] + gsl * o_s[b][rs] + gw * o_w[b][rs]

    o_ref[...] = _rms(o_scr[...]) * nao_ref[...]


S_SEQS = 2


def _sample_attention(page_table, r_all, r_new, q, gates, slc_cache, slc_new, win_state, win_new, cosc, sinc, e_mat,
                      norm_ao, *, n_new, page, wbuf):
    n_seq, n_pages = page_table.shape
    past = n_pages * page
    chunks_pp = page // CMP_STRIDE
    n_cmp = n_pages * chunks_pp
    assert n_cmp == HEAD_DIM and n_new == 8 and past % SEL_BLOCK == 0 and n_new <= SEL_BLOCK
    assert (past + n_new - 1) // SEL_BLOCK == past // SEL_BLOCK and wbuf == WINDOW and past >= WINDOW
    n_keys = past + HEAD_DIM
    n_wkeys = wbuf + HEAD_DIM
    r_w = r_all.shape[1]

    ns = S_SEQS
    assert n_seq % ns == 0
    page_map = lambda sq, p: (lambda b, pt: (pt[b * ns + sq, p], 0))
    in_specs = [pl.BlockSpec((chunks_pp, r_w), page_map(sq, p)) for sq in range(ns) for p in range(n_pages)]
    in_specs += [pl.BlockSpec((page * KV_ROWS, HEAD_DIM), page_map(sq, p)) for sq in range(ns) for p in range(n_pages)]
    seq_rows = lambda wdt: pl.BlockSpec((ns * n_new, wdt), lambda b, pt: (b, 0))
    kv_rows = lambda n_tok: pl.BlockSpec((ns * n_tok * KV_ROWS, HEAD_DIM), lambda b, pt: (b, 0))
    const = lambda shape: pl.BlockSpec(shape, lambda b, pt: (0,) * len(shape))
    in_specs += [pl.BlockSpec((ns, 1, r_w), lambda b, pt: (b, 0, 0)), seq_rows(ATT_W), seq_rows(GATE_PAD),
                 kv_rows(n_new), kv_rows(wbuf), kv_rows(n_new),
                 const((n_cmp, HEAD_DIM)), const((n_cmp, HEAD_DIM)), const((HEAD_DIM, past)), const((1, ATT_W))]
    grid_spec = pltpu.PrefetchScalarGridSpec(
        num_scalar_prefetch=1,
        grid=(n_seq // ns,),
        in_specs=in_specs,
        out_specs=[seq_rows(ATT_W), kv_rows(wbuf)],
        scratch_shapes=[pltpu.VMEM((ns, n_cmp + 8, r_w), F32), pltpu.VMEM((ns * N_KV, n_keys, HEAD_DIM), BF16),
                        pltpu.VMEM((ns * N_KV, n_keys, HEAD_DIM), BF16), pltpu.VMEM((ns * N_KV, n_wkeys, HEAD_DIM), BF16),
                        pltpu.VMEM((ns * N_KV, n_wkeys, HEAD_DIM), BF16), pltpu.VMEM((ns * n_new, ATT_W), F32)],
    )
    return pl.pallas_call(
        functools.partial(_sattn_kernel, n_seqs=ns, n_pages=n_pages, page=page, n_new=n_new, past=past, wbuf=wbuf),
        grid_spec=grid_spec,
        out_shape=[jax.ShapeDtypeStruct((n_seq * n_new, ATT_W), F32),
                   jax.ShapeDtypeStruct((n_seq * wbuf * KV_ROWS, HEAD_DIM), F32)],
        compiler_params=_cparams("arbitrary"),
        name="sample_attn",
    )(page_table, *([r_all] * (ns * n_pages)), *([slc_cache] * (ns * n_pages)), r_new, q, gates, slc_new, win_state,
      win_new, cosc, sinc, e_mat, norm_ao)


def _outproj_kernel(x_ref, oa_ref, oc_ref, gt_ref, w_ref, o_ref):
    y = _dot(oa_ref[...].astype(BF16), w_ref[0:ATT_W, :]) + _dot(oc_ref[...].astype(BF16), w_ref[ATT_W:D_MODEL, :])
    o_ref[...] = x_ref[...] + (1.0 + gt_ref[...]) * y


def _outproj(x, oa, oc, gate, w, tm=512):
    t = x.shape[0]
    mrows = gate.shape[0]
    mod_spec = (pl.BlockSpec((1, D_MODEL), lambda i: (0, 0)) if mrows == 1
                else pl.BlockSpec((tm, D_MODEL), lambda i: (i, 0)))
    rows = lambda wdt: pl.BlockSpec((tm, wdt), lambda i: (i, 0))
    return pl.pallas_call(
        _outproj_kernel,
        grid=(t // tm,),
        in_specs=[rows(D_MODEL), rows(ATT_W), rows(CONV_W), mod_spec, pl.BlockSpec(memory_space=pltpu.VMEM)],
        out_specs=rows(D_MODEL),
        out_shape=jax.ShapeDtypeStruct((t, D_MODEL), F32),
        compiler_params=_cparams("arbitrary"),
        name="outproj",
    )(x, oa, oc, gate, w)


def _rope_tables(pos):
    half = HEAD_DIM // 2
    inv = ROPE_THETA ** (-jnp.arange(half, dtype=F32) * 2.0 / HEAD_DIM)
    ang = pos.astype(F32)[:, None] * inv[None, :]
    cos, sin = jnp.cos(ang), jnp.sin(ang)
    return jnp.concatenate([cos, cos], axis=1), jnp.concatenate([-sin, sin], axis=1)


def _pack_w_in(w_in):
    off_kv = ATT_W
    off_g = off_kv + 3 * SLAB_W
    off_c = off_g + N_HEADS * N_BRANCH
    gate_cols = jnp.pad(w_in[:, off_g:off_c], ((0, 0), (0, GATE_PAD - N_HEADS * N_BRANCH)))
    return jnp.concatenate([w_in[:, 0:off_g], w_in[:, off_c:off_c + 3 * CONV_W], gate_cols], axis=1).astype(BF16)


def _pack_w_cmp(w_ck, w_cv, n_j):
    def one(w):
        lo = w[0:n_j].reshape(n_j * HEAD_DIM, HEAD_DIM)
        hi = w[CMP_STRIDE:CMP_STRIDE + n_j].reshape(n_j * HEAD_DIM, HEAD_DIM)
        return jnp.concatenate([lo, hi], axis=1)
    return jnp.stack([one(w_ck), one(w_cv)]).astype(BF16)


def kernel(x_prompt, x_sample, c_prompt, c_sample, cache_cmp_kv, cache_slc_kv, state_win_kv, state_conv, page_table,
           w_ada, b_ada, norm_ffn1, ffn1_gate, ffn1_up, ffn1_down, norm_mix, w_in, w_cmp_k, w_cmp_v, conv_w, conv_b,
           norm_att_out, norm_conv_out, w_out, norm_ffn2, ffn2_gate, ffn2_up, ffn2_down, norm_final):
    n_p, s_len, _ = x_prompt.shape
    n_seq, n_new, _ = x_sample.shape
    depth = w_ada.shape[0]
    assert n_p == 1 and depth == 1
    n_pages = page_table.shape[1]
    page = cache_slc_kv.shape[2]
    n_phys = cache_slc_kv.shape[1]
    past = n_pages * page
    wbuf = state_win_kv.shape[2]
    keep_p = min(WINDOW, s_len)
    t_s = n_seq * n_new
    l = 0

    c_all = jnp.concatenate([c_sample, c_prompt, jnp.zeros((8 - n_p, D_MODEL), F32)], axis=0)
    mod = _ada(c_all, w_ada[l], b_ada[l])
    mod_p = [mod[n_seq:n_seq + 1, k * D_MODEL:(k + 1) * D_MODEL] for k in range(N_MOD)]
    mod_s = [jnp.repeat(mod[0:n_seq, k * D_MODEL:(k + 1) * D_MODEL], n_new, axis=0) for k in range(N_MOD)]

    row = lambda v: v.reshape(1, -1)
    f1 = (ffn1_gate[l].astype(BF16), ffn1_up[l].astype(BF16), ffn1_down[l].astype(BF16))
    f2 = (ffn2_gate[l].astype(BF16), ffn2_up[l].astype(BF16), ffn2_down[l].astype(BF16))
    w_proj = _pack_w_in(w_in[l])
    w_o = w_out[l].astype(BF16)
    nfin = row(norm_final)

    xp = x_prompt.reshape(s_len, D_MODEL)
    xs = x_sample.reshape(t_s, D_MODEL)

    xp = _ffn(xp, mod_p[0], mod_p[1], mod_p[2], row(norm_ffn1[l]), nfin, *f1, final_norm=False)
    xs = _ffn(xs, mod_s[0], mod_s[1], mod_s[2], row(norm_ffn1[l]), nfin, *f1, final_norm=False)

    cos_p, sin_p = _rope_tables(jnp.arange(s_len))
    cos_s, sin_s = _rope_tables(past + jnp.arange(n_new))
    tm_s = 256
    cos_s, sin_s = jnp.tile(cos_s, (tm_s // n_new, 1)), jnp.tile(sin_s, (tm_s // n_new, 1))
    zero8 = jnp.zeros((8, CONV_W), F32)
    conv_args = (conv_w[l], row(conv_b[l]), row(norm_conv_out[l]))
    (q_p, cmp_p, slc_p, win_p, ksb_p, vst_p, kwb_p, vwb_p, gate_p, ocn_p, utail_p, qt_p) = _proj(
        xp, mod_p[3], mod_p[4], row(norm_mix[l]), w_proj, cos_p, sin_p, zero8, zero8, *conv_args,
        carry=True, seq_rows=s_len, act_dtype=BF16)
    prev1 = jnp.repeat(state_conv[l][:, CONV_K - 2], n_new, axis=0)
    prev2 = jnp.repeat(state_conv[l][:, CONV_K - 3], n_new, axis=0)
    (q_s, cmp_s, slc_s, win_s, _, _, _, _, gate_s, ocn_s, u_s, _) = _proj(
        xs, mod_s[3], mod_s[4], row(norm_mix[l]), w_proj, cos_s, sin_s, prev1, prev2, *conv_args,
        carry=False, seq_rows=n_new, act_dtype=F32, tm=tm_s)

    n_j = CMP_STRIDE
    w_c = _pack_w_cmp(w_cmp_k[l], w_cmp_v[l], n_j)
    lin = lambda a: a.reshape(-1, HEAD_DIM)
    r_p = _compress_products(cmp_p, w_c, n_j, tm=256)
    r_cache = _compress_products(lin(cache_cmp_kv), w_c, n_j, tm=256)
    r_new = _compress_products(cmp_s, _pack_w_cmp(w_cmp_k[l], w_cmp_v[l], n_new), n_new, tm=n_seq)

    n_cmp_pad = s_len // CMP_STRIDE
    cosc, sinc = _rope_tables(jnp.arange(n_cmp_pad) * CMP_STRIDE + (CMP_BLOCK - 1))
    r_p = jnp.pad(r_p, ((0, 8), (0, 0)))
    band = ((WINDOW, 0), (0, 0))
    n_blocks = s_len // SEL_BLOCK
    e_t = ((jnp.arange(s_len) // SEL_BLOCK)[:, None] == jnp.arange(n_blocks)[None, :]).astype(BF16)
    oa_p = _prompt_attention(q_p, qt_p, gate_p, r_p, cosc, sinc, ksb_p, vst_p, jnp.pad(kwb_p, band),
                             jnp.pad(vwb_p, band), e_t, row(norm_att_out[l]))

    n_cmp_s = past // CMP_STRIDE
    cosc_s, sinc_s = _rope_tables(jnp.arange(n_cmp_s) * CMP_STRIDE + (CMP_BLOCK - 1))
    e_s =(jnp.arange(HEAD_DIM)[:, None] == (jnp.arange(past) // SEL_BLOCK * CHUNKS_PER_SEL)[None, :]).astype(BF16)
    oa_s, win_new_state = _sample_attention(
        page_table, r_cache, r_new.reshape(n_seq, 1, -1), q_s, gate_s,
        lin(cache_slc_kv), slc_s, lin(state_win_kv), win_s,
        cosc_s, sinc_s, e_s, row(norm_att_out[l]), n_new=n_new, page=page, wbuf=wbuf)

    xp = _outproj(xp, oa_p, ocn_p, mod_p[5], w_o)
    xs = _outproj(xs, oa_s, ocn_s, mod_s[5], w_o)
    yp = _ffn(xp, mod_p[6], mod_p[7], mod_p[8], row(norm_ffn2[l]), nfin, *f2, final_norm=True)
    ys = _ffn(xs, mod_s[6], mod_s[7], mod_s[8], row(norm_ffn2[l]), nfin, *f2, final_norm=True)

    kv6 = lambda a, n, s: a.reshape(1, n, s, 2, N_KV, HEAD_DIM)
    return (yp.reshape(n_p, s_len, D_MODEL), ys.reshape(n_seq, n_new, D_MODEL),
            kv6(cmp_p, n_p, s_len), kv6(slc_p, n_p, s_len), kv6(win_p[(s_len - keep_p) * KV_ROWS:], n_p, keep_p),
            utail_p[8 - (CONV_K - 1):].reshape(1, n_p, CONV_K - 1, CONV_W),
            kv6(cmp_s, n_seq, n_new), kv6(slc_s, n_seq, n_new), kv6(win_new_state, n_seq, wbuf),
            u_s.reshape(n_seq, n_new, CONV_W)[:, n_new - (CONV_K - 1):].reshape(1, n_seq, CONV_K - 1, CONV_W))
```

```python
import functools

import jax
import jax.numpy as jnp
from jax import lax
from jax.experimental import pallas as pl
from jax.experimental.pallas import tpu as pltpu

F32 = jnp.float32
BF16 = jnp.bfloat16

D_MODEL = 2048
HEAD_DIM = 128
N_HEADS = 8
N_KV = 2
HEADS_PER_KV = N_HEADS // N_KV
ATT_W = N_HEADS * HEAD_DIM
KV_W = N_KV * HEAD_DIM
CONV_W = D_MODEL - ATT_W
CONV_K = 3
CMP_BLOCK = 32
CMP_STRIDE = 16
SEL_BLOCK = 64
N_SEL = 16
WINDOW = 512
Q_BLOCK = 128
N_BRANCH = 3
N_MOD = 9
ROPE_THETA = 10000.0
EPS = 1e-6
NEG = -1e30
TINY = 1e-30
SCALE = HEAD_DIM ** -0.5
LOG2E = 1.4426950408889634
SLAB_W = 2 * KV_W
KV_ROWS = 2 * N_KV
GATE_PAD = 128
SUBLANES = 8
CHUNKS_PER_SEL = SEL_BLOCK // CMP_STRIDE

VMEM_LIMIT = 56 * 1024 * 1024


def _cparams(*sem):
    return pltpu.CompilerParams(dimension_semantics=sem, vmem_limit_bytes=VMEM_LIMIT)


def _dot(a, b):
    return jnp.dot(a, b, preferred_element_type=F32)


def _dot_nt(a, b):
    return lax.dot_general(a, b, (((1,), (1,)), ((), ())), preferred_element_type=F32)


def _rms(x):
    return x * lax.rsqrt(jnp.mean(x * x, axis=-1, keepdims=True) + EPS)


def _silu(x):
    return x * jax.nn.sigmoid(x)


def _rope(x, cos, sin_signed):
    return x * cos + pltpu.roll(x, HEAD_DIM // 2, axis=1) * sin_signed


def _masked_softmax(s, mask):
    s = jnp.where(mask, s, NEG)
    m = jnp.max(s, axis=-1, keepdims=True)
    e = jnp.where(mask, jnp.exp2(s - m), 0.0)
    return e / jnp.maximum(jnp.sum(e, axis=-1, keepdims=True), TINY)


def _ada_kernel(c_ref, w_ref, b_ref, o_ref):
    a = _silu(c_ref[...]).astype(BF16)
    o_ref[...] = _dot(a, w_ref[...].astype(BF16)) + b_ref[...]


def _ada(c, w, b, tn=1024):
    m, n = c.shape[0], w.shape[1]
    return pl.pallas_call(
        _ada_kernel,
        grid=(n // tn,),
        in_specs=[pl.BlockSpec((m, D_MODEL), lambda j: (0, 0)),
                  pl.BlockSpec((D_MODEL, tn), lambda j: (0, j)),
                  pl.BlockSpec((1, tn), lambda j: (0, j))],
        out_specs=pl.BlockSpec((m, tn), lambda j: (0, j)),
        out_shape=jax.ShapeDtypeStruct((m, n), F32),
        compiler_params=_cparams("arbitrary"),
        name="ada",
    )(c, w, b.reshape(1, n))


def _ffn_kernel(x_ref, sh_ref, sc_ref, gt_ref, ng_ref, nf_ref, wg_ref, wu_ref, wd_ref, o_ref, h_scr, *, n_f, final_norm,
                rep):
    j = pl.program_id(1)
    mod = lambda ref: ref[...] if rep == 1 else jnp.concatenate([ref[...]] * rep, axis=0)

    @pl.when(j == 0)
    def _():
        h = _rms(x_ref[...]) * ng_ref[...] * (1.0 + mod(sc_ref)) + mod(sh_ref)
        h_scr[...] = h.astype(BF16)
        o_ref[...] = jnp.zeros_like(o_ref)

    h = h_scr[...]
    a = (_silu(_dot(h, wg_ref[...])) * _dot(h, wu_ref[...])).astype(BF16)
    o_ref[...] += _dot(a, wd_ref[...])

    @pl.when(j == n_f - 1)
    def _():
        out = x_ref[...] + 0.5 * (1.0 + mod(gt_ref)) * o_ref[...]
        if final_norm:
            out = _rms(out) * nf_ref[...]
        o_ref[...] = out


def _ffn(x, shift, scale, gate, norm_g, norm_final, wg, wu, wd, *, final_norm, tm=512, tf=512):
    t = x.shape[0]
    d_ff = wg.shape[1]
    n_f = d_ff // tf
    mrows = shift.shape[0]
    rep = 1 if mrows == 1 else tm // mrows
    assert mrows == 1 or (tm % mrows == 0 and t % tm == 0)
    mod_spec = pl.BlockSpec((mrows, D_MODEL), lambda i, j: (0, 0))
    row_spec = pl.BlockSpec((tm, D_MODEL), lambda i, j: (i, 0))
    vec_spec = pl.BlockSpec((1, D_MODEL), lambda i, j: (0, 0))
    return pl.pallas_call(
        functools.partial(_ffn_kernel, n_f=n_f, final_norm=final_norm, rep=rep),
        grid=(t // tm, n_f),
        in_specs=[row_spec, mod_spec, mod_spec, mod_spec, vec_spec, vec_spec,
                  pl.BlockSpec((D_MODEL, tf), lambda i, j: (0, j)),
                  pl.BlockSpec((D_MODEL, tf), lambda i, j: (0, j)),
                  pl.BlockSpec((tf, D_MODEL), lambda i, j: (j, 0))],
        out_specs=row_spec,
        out_shape=jax.ShapeDtypeStruct((t, D_MODEL), F32),
        scratch_shapes=[pltpu.VMEM((tm, D_MODEL), BF16)],
        compiler_params=_cparams("arbitrary", "arbitrary"),
        name="ffn",
    )(x, shift, scale, gate, norm_g, norm_final, wg, wu, wd)


PW_Q = 0
PW_KV = PW_Q + ATT_W
PW_CONV = PW_KV + 3 * SLAB_W
PW_GATE = PW_CONV + 3 * CONV_W
PW_TOTAL = PW_GATE + GATE_PAD


def _proj_kernel(x_ref, sh_ref, sc_ref, ng_ref, w_ref, cos_ref, sin_ref, p1_ref, p2_ref, cw_ref, cb_ref, nco_ref,
                 q_ref, cmp_ref, slc_ref, win_ref, ksb_ref, vst_ref, kwb_ref, vwb_ref, gate_ref, ocn_ref, u_ref, qt_ref,
                 carry_scr, *, tm, seq_rows, carry):
    i = pl.program_id(0)
    h = (_rms(x_ref[...]) * ng_ref[...] * (1.0 + sc_ref[...]) + sh_ref[...]).astype(BF16)
    cos, sin = cos_ref[...], sin_ref[...]

    pq = _dot(h, w_ref[:, PW_Q:PW_KV])
    for hd in range(N_HEADS):
        hs = slice(hd * HEAD_DIM, (hd + 1) * HEAD_DIM)
        blk = _rope(pq[:, hs], cos, sin) * (SCALE * LOG2E)
        q_ref[:, hs] = blk.astype(q_ref.dtype)
        qt_ref[hs, :] = blk.T.astype(BF16)

    pkv = _dot(h, w_ref[:, PW_KV:PW_CONV])
    for slab, (o_ref, kb_ref, vb_ref) in enumerate(((cmp_ref, None, None), (slc_ref, ksb_ref, None),
                                                    (win_ref, kwb_ref, vwb_ref))):
        base = slab * SLAB_W
        for g in range(N_KV):
            gs = slice(g * HEAD_DIM, (g + 1) * HEAD_DIM)
            k = pkv[:, base + g * HEAD_DIM:base + (g + 1) * HEAD_DIM]
            v = pkv[:, base + KV_W + g * HEAD_DIM:base + KV_W + (g + 1) * HEAD_DIM]
            if kb_ref is not None:
                k = _rope(k, cos, sin)
                kb_ref[:, gs] = k.astype(BF16)
                if vb_ref is not None:
                    vb_ref[:, gs] = v.astype(BF16)
                else:
                    vst_ref[0, gs, :] = v.T.astype(BF16)
            o_ref[pl.ds(g, tm, stride=KV_ROWS), :] = k
            o_ref[pl.ds(N_KV + g, tm, stride=KV_ROWS), :] = v

    gate_ref[...] = jax.nn.sigmoid(_dot(h, w_ref[:, PW_GATE:PW_TOTAL]))

    pc = _dot(h, w_ref[:, PW_CONV:PW_GATE])
    u = pc[:, 0:CONV_W] * pc[:, 2 * CONV_W:3 * CONV_W]
    c_out = pc[:, CONV_W:2 * CONV_W]
    row = lax.broadcasted_iota(jnp.int32, (tm, CONV_W), 0)
    if carry:
        @pl.when(i == 0)
        def _():
            carry_scr[...] = jnp.zeros_like(carry_scr)
        prev1 = carry_scr[7:8, :]
        prev2 = carry_scr[6:7, :]
        rs = row
    else:
        prev1 = p1_ref[...]
        prev2 = p2_ref[...]
        rs = row & (seq_rows - 1)
    um1 = jnp.where(rs >= 1, pltpu.roll(u, 1, axis=0), prev1)
    um2 = jnp.where(rs >= 2, pltpu.roll(u, 2, axis=0), jnp.where(rs == 1, prev1, prev2))
    y = um2 * cw_ref[0:1, :] + um1 * cw_ref[1:2, :] + u * cw_ref[2:3, :] + cb_ref[...]
    ocn_ref[...] = (_rms(c_out * y) * nco_ref[...]).astype(ocn_ref.dtype)
    if carry:
        carry_scr[...] = u[tm - 8:tm, :]
        u_ref[...] = u[tm - 8:tm, :]
    else:
        u_ref[...] = u


def _proj(x, shift, scale, norm_g, w, cos, sin, prev1, prev2, conv_w, conv_b, norm_co, *, carry, seq_rows, act_dtype,
          tm=256):
    t = x.shape[0]
    mrows = shift.shape[0]
    mod_spec = (pl.BlockSpec((1, D_MODEL), lambda i: (0, 0)) if mrows == 1
                else pl.BlockSpec((tm, D_MODEL), lambda i: (i, 0)))
    rows = lambda wdt: pl.BlockSpec((tm, wdt), lambda i: (i, 0))
    const = lambda r, wdt: pl.BlockSpec((r, wdt), lambda i: (0, 0))
    tab_spec = rows(HEAD_DIM) if carry else const(tm, HEAD_DIM)
    prev_spec = const(8, CONV_W) if carry else rows(CONV_W)
    u_rows = 8 if carry else t
    out_shape = [jax.ShapeDtypeStruct((t, ATT_W), act_dtype)]
    out_shape += [jax.ShapeDtypeStruct((t * KV_ROWS, HEAD_DIM), F32)] * 3
    per_ck = P_CK // tm
    kvb = jax.ShapeDtypeStruct((t, KV_W), BF16)
    out_shape += [kvb, jax.ShapeDtypeStruct((t // P_CK, KV_W, P_CK), BF16), kvb, kvb]
    out_shape += [jax.ShapeDtypeStruct((t, GATE_PAD), F32), jax.ShapeDtypeStruct((t, CONV_W), act_dtype),
                  jax.ShapeDtypeStruct((u_rows, CONV_W), F32), jax.ShapeDtypeStruct((ATT_W, t), BF16)]
    lin_spec = pl.BlockSpec((tm * KV_ROWS, HEAD_DIM), lambda i: (i, 0))
    vst_spec = pl.BlockSpec((1, KV_W, tm), lambda i: (i // per_ck, 0, i % per_ck))
    out_specs = [rows(ATT_W)] + [lin_spec] * 3 + [rows(KV_W), vst_spec, rows(KV_W), rows(KV_W)]
    out_specs += [rows(GATE_PAD), rows(CONV_W), const(8, CONV_W) if carry else rows(CONV_W),
                  pl.BlockSpec((ATT_W, tm), lambda i: (0, i))]
    return pl.pallas_call(
        functools.partial(_proj_kernel, tm=tm, seq_rows=seq_rows, carry=carry),
        grid=(t // tm,),
        in_specs=[rows(D_MODEL), mod_spec, mod_spec, const(1, D_MODEL),
                  pl.BlockSpec(memory_space=pltpu.VMEM),
                  tab_spec, tab_spec, prev_spec, prev_spec, const(CONV_K, CONV_W), const(1, CONV_W),
                  const(1, CONV_W)],
        out_specs=out_specs,
        out_shape=out_shape,
        scratch_shapes=[pltpu.VMEM((8, CONV_W), F32)],
        compiler_params=_cparams("arbitrary"),
        name="proj",
    )(x, shift, scale, norm_g, w, cos, sin, prev1, prev2, conv_w, conv_b, norm_co)


def _cmp_kernel(x_ref, w_ref, o_ref, pad_scr, *, n_j, tm):
    pitch = n_j * KV_ROWS
    ppitch = pitch + SUBLANES

    def repitch(c, _):
        src = pl.multiple_of(c * pitch, SUBLANES)
        dst = pl.multiple_of(c * ppitch, SUBLANES)
        pad_scr[pl.ds(dst, pitch), :] = x_ref[pl.ds(src, pitch), :]
        return 0

    lax.fori_loop(0, tm, repitch, 0, unroll=8)
    flat = pad_scr
    for kv in range(2):
        for g in range(N_KV):
            xs = jnp.concatenate([flat[pl.ds(j * KV_ROWS + kv * N_KV + g, tm, stride=ppitch), :] for j in range(n_j)],
                                 axis=1)
            col = (kv * N_KV + g) * 2 * HEAD_DIM
            o_ref[:, col:col + 2 * HEAD_DIM] = _dot(xs.astype(BF16), w_ref[kv])


def _compress_products(x, w, n_j, tm):
    pitch = n_j * KV_ROWS
    m = x.shape[0] // pitch
    n_out = 2 * N_KV * 2 * HEAD_DIM
    tm = min(tm, m)
    return pl.pallas_call(
        functools.partial(_cmp_kernel, n_j=n_j, tm=tm),
        grid=(m // tm,),
        in_specs=[pl.BlockSpec((tm * pitch, HEAD_DIM), lambda i: (i, 0)),
                  pl.BlockSpec((2, n_j * HEAD_DIM, 2 * HEAD_DIM), lambda i: (0, 0, 0))],
        out_specs=pl.BlockSpec((tm, n_out), lambda i: (i, 0)),
        out_shape=jax.ShapeDtypeStruct((m, n_out), F32),
        scratch_shapes=[pltpu.VMEM((tm * (pitch + SUBLANES), HEAD_DIM), F32)],
        compiler_params=_cparams("arbitrary"),
        name="compress",
    )(x, w)


def _r_cols(kv, g):
    return (kv * N_KV + g) * 2 * HEAD_DIM


P_CK = 512
P_WKEYS = WINDOW + Q_BLOCK


def _pattn_kernel(q_ref, qt_ref, gate_ref, rp_ref, cosc_ref, sinc_ref, ks_ref, vst_ref, kw_ref, vw_ref, et_ref,
                  nao_ref, o_ref, kc_scr, vc_scr, pt_scr, o_scr, s_scr, *, n_cmp_pad, n_blocks):
    i = pl.program_id(0)
    s0 = i * Q_BLOCK
    nq = Q_BLOCK
    rows = HEADS_PER_KV * nq

    @pl.when(i == 0)
    def _():
        for g in range(N_KV):
            ck = _r_cols(0, g)
            kc = rp_ref[0:n_cmp_pad, ck:ck + HEAD_DIM] + rp_ref[pl.ds(1, n_cmp_pad), ck + HEAD_DIM:ck + 2 * HEAD_DIM]
            kc_scr[:, g * HEAD_DIM:(g + 1) * HEAD_DIM] = _rope(kc, cosc_ref[...], sinc_ref[...]).astype(BF16)
            cv = _r_cols(1, g)
            vc = rp_ref[0:n_cmp_pad, cv:cv + HEAD_DIM] + rp_ref[pl.ds(1, n_cmp_pad), cv + HEAD_DIM:cv + 2 * HEAD_DIM]
            vc_scr[:, g * HEAD_DIM:(g + 1) * HEAD_DIM] = vc.astype(BF16)
        pt_scr[...] = jnp.zeros_like(pt_scr)

    t_rows = s0 + (lax.broadcasted_iota(jnp.int32, (rows, 1), 0) & (nq - 1))
    scores = []

    for g in range(N_KV):
        gs = slice(g * HEAD_DIM, (g + 1) * HEAD_DIM)
        qg = jnp.concatenate([q_ref[:, (g * HEADS_PER_KV + r) * HEAD_DIM:(g * HEADS_PER_KV + r + 1) * HEAD_DIM]
                              for r in range(HEADS_PER_KV)], axis=0)

        c_end = lax.broadcasted_iota(jnp.int32, (rows, n_cmp_pad), 1) * CMP_STRIDE + (CMP_BLOCK - 1)
        p_c = _masked_softmax(_dot_nt(qg, kc_scr[:, gs]), c_end <= t_rows)
        o_c = _dot(p_c.astype(BF16), vc_scr[:, gs])

        p_grp = p_c[0:nq] + p_c[nq:2 * nq] + p_c[2 * nq:3 * nq] + p_c[3 * nq:4 * nq]
        pt_scr[8:8 + n_cmp_pad, :] = p_grp.T
        st = lambda k: pt_scr[pl.ds(7 + k, n_blocks, stride=CHUNKS_PER_SEL), :]
        score = 0.5 * st(0) + st(1) + st(2) + st(3) + 0.5 * st(4)
        b_io = lax.broadcasted_iota(jnp.int32, (n_blocks, nq), 0)
        t_lane = s0 + lax.broadcasted_iota(jnp.int32, (n_blocks, nq), 1)
        cur = t_lane >> 6
        forced = (b_io == 0) | (b_io == cur) | (b_io == cur - 1)
        valid = b_io * SEL_BLOCK <= t_lane
        scores.append(jnp.where(forced, jnp.inf, jnp.where(valid, score, -jnp.inf)))

        w0 = pl.multiple_of(s0, Q_BLOCK)
        j_io = lax.broadcasted_iota(jnp.int32, (rows, P_WKEYS), 1)
        ti = lax.broadcasted_iota(jnp.int32, (rows, P_WKEYS), 0) & (nq - 1)
        m_w = (j_io > ti) & (j_io <= ti + WINDOW) & (j_io >= WINDOW - s0)
        p_w = _masked_softmax(_dot_nt(qg, kw_ref[pl.ds(w0, P_WKEYS), gs]), m_w)
        o_w = _dot(p_w.astype(BF16), vw_ref[pl.ds(w0, P_WKEYS), gs])

        for r in range(HEADS_PER_KV):
            hd = g * HEADS_PER_KV + r
            rs = slice(r * nq, (r + 1) * nq)
            gc = gate_ref[:, hd * N_BRANCH + 0:hd * N_BRANCH + 1]
            gw = gate_ref[:, hd * N_BRANCH + 2:hd * N_BRANCH + 3]
            o_scr[:, hd * HEAD_DIM:(hd + 1) * HEAD_DIM] = gc * o_c[rs] + gw * o_w[rs]

    b_f = lax.broadcasted_iota(jnp.int32, (n_blocks, nq), 0).astype(F32)

    def pick(_, c):
        out = []
        for work, sel in c:
            m = jnp.max(work, axis=0, keepdims=True)
            idx = jnp.min(jnp.where(work == m, b_f, float(n_blocks)), axis=0, keepdims=True)
            hit = b_f == idx
            out.append((jnp.where(hit, -jnp.inf, work), jnp.where(hit, 1.0, sel)))
        return tuple(out)

    picked = lax.fori_loop(0, min(N_SEL, n_blocks), pick,
                           tuple((sc, jnp.zeros((n_blocks, nq), F32)) for sc in scores))

    rhs = []
    for g in range(N_KV):
        sel_bias = jnp.where(picked[g][1] > 0.5, 0.0, NEG).astype(BF16)
        q_t = jnp.concatenate([qt_ref[(g * HEADS_PER_KV + r) * HEAD_DIM:(g * HEADS_PER_KV + r + 1) * HEAD_DIM, :]
                               for r in range(HEADS_PER_KV)], axis=1)
        rhs.append(jnp.concatenate([q_t, jnp.concatenate([sel_bias] * HEADS_PER_KV, axis=1)], axis=0))

    def qk_scores(c, slot):
        k0 = pl.multiple_of(c * P_CK, P_CK)
        blk_hot = et_ref[pl.ds(k0, P_CK), :]
        for g in range(N_KV):
            gs = slice(g * HEAD_DIM, (g + 1) * HEAD_DIM)
            s_scr[slot, g] = _dot(jnp.concatenate([ks_ref[pl.ds(k0, P_CK), gs], blk_hot], axis=1), rhs[g])

    def softmax_pv(c, slot, carry, causal):
        if causal:
            key = c * P_CK + lax.broadcasted_iota(jnp.int32, (P_CK, rows), 0)
            t_q = s0 + (lax.broadcasted_iota(jnp.int32, (P_CK, rows), 1) & (nq - 1))
            cb = jnp.where(key <= t_q, 0.0, NEG)
        out = []
        for g in range(N_KV):
            gs = slice(g * HEAD_DIM, (g + 1) * HEAD_DIM)
            m_i, l_i, acc = carry[g]
            s = s_scr[slot, g]
            if causal:
                s = s + cb
            m_n = jnp.maximum(m_i, jnp.max(s, axis=0, keepdims=True))
            p = jnp.exp2(s - m_n)
            alpha = jnp.exp2(m_i - m_n)
            l_n = alpha * l_i + jnp.sum(p, axis=0, keepdims=True)
            out.append((m_n, l_n, alpha * acc + _dot(vst_ref[c, gs, :], p.astype(BF16))))
        return tuple(out)

    def pair(pr, carry):
        c = 2 * pr
        qk_scores(c + 1, 1)
        carry = softmax_pv(c, 0, carry, False)
        qk_scores(c + 2, 0)
        return softmax_pv(c + 1, 1, carry, False)

    def odd_tail(c, carry):
        qk_scores(c + 1, 1)
        return softmax_pv(c, 0, carry, False)

    n_chunks = (s0 + nq + P_CK - 1) // P_CK
    n_pairs = (n_chunks - 1) // 2
    init = tuple((jnp.full((1, rows), NEG, F32), jnp.zeros((1, rows), F32), jnp.zeros((HEAD_DIM, rows), F32))
                 for _ in range(N_KV))
    qk_scores(0, 0)
    carry = lax.fori_loop(0, n_pairs, pair, init)
    last = n_chunks - 1
    carry = lax.cond(last > 2 * n_pairs, lambda cr: odd_tail(last - 1, cr), lambda cr: cr, carry)
    carry = lax.cond(last > 2 * n_pairs, lambda cr: softmax_pv(last, 1, cr, True),
                     lambda cr: softmax_pv(last, 0, cr, True), carry)
    for g in range(N_KV):
        _, l_s, acc = carry[g]
        o_st = acc / jnp.maximum(l_s, TINY)
        for r in range(HEADS_PER_KV):
            hd = g * HEADS_PER_KV + r
            gsl = gate_ref[:, hd * N_BRANCH + 1:hd * N_BRANCH + 2]
            o_scr[:, hd * HEAD_DIM:(hd + 1) * HEAD_DIM] += gsl * o_st[:, r * nq:(r + 1) * nq].T

    o_ref[...] = (_rms(o_scr[...]) * nao_ref[...]).astype(o_ref.dtype)


def _prompt_attention(q, q_t, gates, rp, cosc, sinc, ks, vs_t, kw_pad, vw_pad, e_t, norm_ao):
    s_len = q.shape[0]
    n_cmp_pad = s_len // CMP_STRIDE
    n_blocks = s_len // SEL_BLOCK
    vmem = pl.BlockSpec(memory_space=pltpu.VMEM)
    rows = lambda wdt: pl.BlockSpec((Q_BLOCK, wdt), lambda i: (i, 0))
    return pl.pallas_call(
        functools.partial(_pattn_kernel, n_cmp_pad=n_cmp_pad, n_blocks=n_blocks),
        grid=(s_len // Q_BLOCK,),
        in_specs=[rows(ATT_W), pl.BlockSpec((ATT_W, Q_BLOCK), lambda i: (0, i)), rows(GATE_PAD),
                  vmem, vmem, vmem, vmem, vmem, vmem, vmem, vmem, pl.BlockSpec((1, ATT_W), lambda i: (0, 0))],
        out_specs=rows(ATT_W),
        out_shape=jax.ShapeDtypeStruct((s_len, ATT_W), BF16),
        scratch_shapes=[pltpu.VMEM((n_cmp_pad, KV_W), BF16), pltpu.VMEM((n_cmp_pad, KV_W), BF16),
                        pltpu.VMEM((n_cmp_pad + 16, Q_BLOCK), F32), pltpu.VMEM((Q_BLOCK, ATT_W), F32),
                        pltpu.VMEM((2, N_KV, P_CK, HEADS_PER_KV * Q_BLOCK), F32)],
        compiler_params=_cparams("arbitrary"),
        name="prompt_attn",
    )(q, q_t, gates, rp, cosc, sinc, ks, vs_t, kw_pad, vw_pad, e_t, norm_ao)


def _sattn_kernel(pt_ref, *refs, n_seqs, n_pages, page, n_new, past, wbuf):
    del pt_ref
    n_pg = n_seqs * n_pages
    r_pages, s_pages = refs[0:n_pg], refs[n_pg:2 * n_pg]
    (rnew_ref, q_ref, gate_ref, snew_ref, wst_ref, wnew_ref, cosc_ref, sinc_ref, e_ref, nao_ref,
     o_ref, wout_ref, r_scr, k_scr, v_scr, kw_scr, vw_scr, o_scr) = refs[2 * n_pg:]
    chunks_pp = page // CMP_STRIDE
    n_cmp = n_pages * chunks_pp
    rows = HEADS_PER_KV * n_new
    n_keys = k_scr.shape[1]
    n_wkeys = kw_scr.shape[1]
    pad = n_keys - past
    wpad = n_wkeys - wbuf
    batch = [(sq, g) for sq in range(n_seqs) for g in range(N_KV)]
    cat = lambda parts: jnp.concatenate(parts, axis=0)

    def with_zero_rows(new_rows, n_zero):
        return cat([new_rows, jnp.zeros((n_zero, HEAD_DIM), F32)]).astype(BF16)

    for sq in range(n_seqs):
        n0 = sq * n_new * KV_ROWS
        w0 = sq * wbuf * KV_ROWS
        for p in range(n_pages):
            r_scr[sq, p * chunks_pp:(p + 1) * chunks_pp, :] = r_pages[sq * n_pages + p][...]
            pg = s_pages[sq * n_pages + p]
            for g in range(N_KV):
                b = sq * N_KV + g
                k_scr[b, p * page:(p + 1) * page, :] = pg[pl.ds(g, page, stride=KV_ROWS), :].astype(BF16)
                v_scr[b, p * page:(p + 1) * page, :] = pg[pl.ds(N_KV + g, page, stride=KV_ROWS), :].astype(BF16)
        r_scr[sq, n_cmp:n_cmp + 8, :] = cat([rnew_ref[sq], jnp.zeros((7, r_scr.shape[2]), F32)])
        for g in range(N_KV):
            b = sq * N_KV + g
            k_scr[b, past:n_keys, :] = with_zero_rows(snew_ref[pl.ds(n0 + g, n_new, stride=KV_ROWS), :], pad - n_new)
            v_scr[b, past:n_keys, :] = with_zero_rows(snew_ref[pl.ds(n0 + N_KV + g, n_new, stride=KV_ROWS), :],
                                                      pad - n_new)
            kw_scr[b, 0:wbuf, :] = wst_ref[pl.ds(w0 + g, wbuf, stride=KV_ROWS), :].astype(BF16)
            vw_scr[b, 0:wbuf, :] = wst_ref[pl.ds(w0 + N_KV + g, wbuf, stride=KV_ROWS), :].astype(BF16)
            kw_scr[b, wbuf:n_wkeys, :] = with_zero_rows(wnew_ref[pl.ds(n0 + g, n_new, stride=KV_ROWS), :],
                                                        wpad - n_new)
            vw_scr[b, wbuf:n_wkeys, :] = with_zero_rows(wnew_ref[pl.ds(n0 + N_KV + g, n_new, stride=KV_ROWS), :],
                                                        wpad - n_new)
        keep = (wbuf - n_new) * KV_ROWS
        wout_ref[w0:w0 + keep, :] = wst_ref[w0 + n_new * KV_ROWS:w0 + wbuf * KV_ROWS, :]
        wout_ref[w0 + keep:w0 + wbuf * KV_ROWS, :] = wnew_ref[n0:n0 + n_new * KV_ROWS, :]

    n_b = len(batch)
    all_rows = n_b * rows
    tok = lax.broadcasted_iota(jnp.int32, (all_rows, 1), 0) & (n_new - 1)
    t_rows = past + tok

    qg, kc, vc = [], [], []
    for sq, g in batch:
        qs = slice(sq * n_new, (sq + 1) * n_new)
        qg.append(cat([q_ref[qs, (g * HEADS_PER_KV + r) * HEAD_DIM:(g * HEADS_PER_KV + r + 1) * HEAD_DIM]
                       for r in range(HEADS_PER_KV)]).astype(BF16))
        ck, cv = _r_cols(0, g), _r_cols(1, g)
        kc.append(r_scr[sq, 0:n_cmp, ck:ck + HEAD_DIM] + r_scr[sq, pl.ds(1, n_cmp), ck + HEAD_DIM:ck + 2 * HEAD_DIM])
        vc.append((r_scr[sq, 0:n_cmp, cv:cv + HEAD_DIM]
                   + r_scr[sq, pl.ds(1, n_cmp), cv + HEAD_DIM:cv + 2 * HEAD_DIM]).astype(BF16))
    cos_all, sin_all = cat([cosc_ref[...]] * n_b), cat([sinc_ref[...]] * n_b)
    kc_all = _rope(cat(kc), cos_all, sin_all).astype(BF16)
    rb = lambda b: slice(b * rows, (b + 1) * rows)

    c_end = lax.broadcasted_iota(jnp.int32, (all_rows, n_cmp), 1) * CMP_STRIDE + (CMP_BLOCK - 1)
    s_c = cat([_dot_nt(qg[b], kc_all[b * n_cmp:(b + 1) * n_cmp]) for b in range(n_b)])
    p_c = _masked_softmax(s_c, c_end <= t_rows)
    p_cb = p_c.astype(BF16)
    o_c = [_dot(p_cb[rb(b)], vc[b]) for b in range(n_b)]

    sel_rows = n_b * n_new
    p_grp = cat([sum(p_c[b * rows + r * n_new:b * rows + (r + 1) * n_new] for r in range(HEADS_PER_KV))
                 for b in range(n_b)])
    lane = lax.broadcasted_iota(jnp.int32, (sel_rows, HEAD_DIM), 1)
    pch = 0.5 * (p_grp + jnp.where(lane >= 1, pltpu.roll(p_grp, 1, axis=1), 0.0))
    score = pch
    for k in range(1, CHUNKS_PER_SEL):
        score = score + pltpu.roll(pch, HEAD_DIM - k, axis=1)
    blk = lane >> 2
    t_tok = past + (lax.broadcasted_iota(jnp.int32, (sel_rows, HEAD_DIM), 0) & (n_new - 1))
    cur = t_tok >> 6
    forced = (blk == 0) | (blk == cur) | (blk == cur - 1)
    score = jnp.where(forced, jnp.inf, score)
    n_pb = past // SEL_BLOCK
    ahead = jnp.zeros((sel_rows, HEAD_DIM), F32)
    for k in range(1, n_pb):
        other = pltpu.roll(score, CHUNKS_PER_SEL * k, axis=1)
        wins = (other > score) | ((other == score) & (blk >= k))
        ahead = ahead + jnp.where(wins, 1.0, 0.0)
    sel = jnp.where(((lane & (CHUNKS_PER_SEL - 1)) == 0) & (ahead < N_SEL - 1), 1.0, 0.0).astype(BF16)
    sel_keys = _dot(sel, e_ref[...])
    new_ok = (lax.broadcasted_iota(jnp.int32, (sel_rows, pad), 1)
              <= (lax.broadcasted_iota(jnp.int32, (sel_rows, pad), 0) & (n_new - 1)))
    bias = jnp.concatenate([jnp.where(sel_keys > 0.5, 0.0, NEG), jnp.where(new_ok, 0.0, NEG)], axis=1)
    bias = cat([bias[b * n_new:(b + 1) * n_new] for b in range(n_b) for _ in range(HEADS_PER_KV)])

    s_s = cat([_dot_nt(qg[b], k_scr[b]) for b in range(n_b)]) + bias
    e_s = jnp.exp2(s_s - jnp.max(s_s, axis=-1, keepdims=True))
    p_s = (e_s / jnp.maximum(jnp.sum(e_s, axis=-1, keepdims=True), TINY)).astype(BF16)
    o_s = [_dot(p_s[rb(b)], v_scr[b]) for b in range(n_b)]

    j_io = lax.broadcasted_iota(jnp.int32, (all_rows, n_wkeys), 1)
    m_w = (((j_io < wbuf) & (j_io > tok + (wbuf - WINDOW)) & (j_io >= wbuf - past))
           | ((j_io >= wbuf) & (j_io - wbuf <= tok)))
    p_w = _masked_softmax(cat([_dot_nt(qg[b], kw_scr[b]) for b in range(n_b)]), m_w).astype(BF16)
    o_w = [_dot(p_w[rb(b)], vw_scr[b]) for b in range(n_b)]

    for b, (sq, g) in enumerate(batch):
        qs = slice(sq * n_new, (sq + 1) * n_new)
        for r in range(HEADS_PER_KV):
            hd = g * HEADS_PER_KV + r
            rs = slice(r * n_new, (r + 1) * n_new)
            gc = gate_ref[qs, hd * N_BRANCH + 0:hd * N_BRANCH + 1]
            gsl = gate_ref[qs, hd * N_BRANCH + 1:hd * N_BRANCH + 2]
            gw = gate_ref[qs, hd * N_BRANCH + 2:hd * N_BRANCH + 3]
            o_scr[qs, hd * HEAD_DIM:(hd + 1) * HEAD_DIM] = gc * o_c[b][rs] + gsl * o_s[b][rs] + gw * o_w[b][rs]

    o_ref[...] = _rms(o_scr[...]) * nao_ref[...]


S_SEQS = 2


def _sample_attention(page_table, r_all, r_new, q, gates, slc_cache, slc_new, win_state, win_new, cosc, sinc, e_mat,
                      norm_ao, *, n_new, page, wbuf):
    n_seq, n_pages = page_table.shape
    past = n_pages * page
    chunks_pp = page // CMP_STRIDE
    n_cmp = n_pages * chunks_pp
    assert n_cmp == HEAD_DIM and n_new == 8 and past % SEL_BLOCK == 0 and n_new <= SEL_BLOCK
    assert (past + n_new - 1) // SEL_BLOCK == past // SEL_BLOCK and wbuf == WINDOW and past >= WINDOW
    n_keys = past + HEAD_DIM
    n_wkeys = wbuf + HEAD_DIM
    r_w = r_all.shape[1]

    ns = S_SEQS
    assert n_seq % ns == 0
    page_map = lambda sq, p: (lambda b, pt: (pt[b * ns + sq, p], 0))
    in_specs = [pl.BlockSpec((chunks_pp, r_w), page_map(sq, p)) for sq in range(ns) for p in range(n_pages)]
    in_specs += [pl.BlockSpec((page * KV_ROWS, HEAD_DIM), page_map(sq, p)) for sq in range(ns) for p in range(n_pages)]
    seq_rows = lambda wdt: pl.BlockSpec((ns * n_new, wdt), lambda b, pt: (b, 0))
    kv_rows = lambda n_tok: pl.BlockSpec((ns * n_tok * KV_ROWS, HEAD_DIM), lambda b, pt: (b, 0))
    const = lambda shape: pl.BlockSpec(shape, lambda b, pt: (0,) * len(shape))
    in_specs += [pl.BlockSpec((ns, 1, r_w), lambda b, pt: (b, 0, 0)), seq_rows(ATT_W), seq_rows(GATE_PAD),
                 kv_rows(n_new), kv_rows(wbuf), kv_rows(n_new),
                 const((n_cmp, HEAD_DIM)), const((n_cmp, HEAD_DIM)), const((HEAD_DIM, past)), const((1, ATT_W))]
    grid_spec = pltpu.PrefetchScalarGridSpec(
        num_scalar_prefetch=1,
        grid=(n_seq // ns,),
        in_specs=in_specs,
        out_specs=[seq_rows(ATT_W), kv_rows(wbuf)],
        scratch_shapes=[pltpu.VMEM((ns, n_cmp + 8, r_w), F32), pltpu.VMEM((ns * N_KV, n_keys, HEAD_DIM), BF16),
                        pltpu.VMEM((ns * N_KV, n_keys, HEAD_DIM), BF16), pltpu.VMEM((ns * N_KV, n_wkeys, HEAD_DIM), BF16),
                        pltpu.VMEM((ns * N_KV, n_wkeys, HEAD_DIM), BF16), pltpu.VMEM((ns * n_new, ATT_W), F32)],
    )
    return pl.pallas_call(
        functools.partial(_sattn_kernel, n_seqs=ns, n_pages=n_pages, page=page, n_new=n_new, past=past, wbuf=wbuf),
        grid_spec=grid_spec,
        out_shape=[jax.ShapeDtypeStruct((n_seq * n_new, ATT_W), F32),
                   jax.ShapeDtypeStruct((n_seq * wbuf * KV_ROWS, HEAD_DIM), F32)],
        compiler_params=_cparams("arbitrary"),
        name="sample_attn",
    )(page_table, *([r_all] * (ns * n_pages)), *([slc_cache] * (ns * n_pages)), r_new, q, gates, slc_new, win_state,
      win_new, cosc, sinc, e_mat, norm_ao)


def _outproj_kernel(x_ref, oa_ref, oc_ref, gt_ref, w_ref, o_ref):
    y = _dot(oa_ref[...].astype(BF16), w_ref[0:ATT_W, :]) + _dot(oc_ref[...].astype(BF16), w_ref[ATT_W:D_MODEL, :])
    o_ref[...] = x_ref[...] + (1.0 + gt_ref[...]) * y


def _outproj(x, oa, oc, gate, w, tm=512):
    t = x.shape[0]
    mrows = gate.shape[0]
    mod_spec = (pl.BlockSpec((1, D_MODEL), lambda i: (0, 0)) if mrows == 1
                else pl.BlockSpec((tm, D_MODEL), lambda i: (i, 0)))
    rows = lambda wdt: pl.BlockSpec((tm, wdt), lambda i: (i, 0))
    return pl.pallas_call(
        _outproj_kernel,
        grid=(t // tm,),
        in_specs=[rows(D_MODEL), rows(ATT_W), rows(CONV_W), mod_spec, pl.BlockSpec(memory_space=pltpu.VMEM)],
        out_specs=rows(D_MODEL),
        out_shape=jax.ShapeDtypeStruct((t, D_MODEL), F32),
        compiler_params=_cparams("arbitrary"),
        name="outproj",
    )(x, oa, oc, gate, w)


def _rope_tables(pos):
    half = HEAD_DIM // 2
    inv = ROPE_THETA ** (-jnp.arange(half, dtype=F32) * 2.0 / HEAD_DIM)
    ang = pos.astype(F32)[:, None] * inv[None, :]
    cos, sin = jnp.cos(ang), jnp.sin(ang)
    return jnp.concatenate([cos, cos], axis=1), jnp.concatenate([-sin, sin], axis=1)


def _pack_w_in(w_in):
    off_kv = ATT_W
    off_g = off_kv + 3 * SLAB_W
    off_c = off_g + N_HEADS * N_BRANCH
    gate_cols = jnp.pad(w_in[:, off_g:off_c], ((0, 0), (0, GATE_PAD - N_HEADS * N_BRANCH)))
    return jnp.concatenate([w_in[:, 0:off_g], w_in[:, off_c:off_c + 3 * CONV_W], gate_cols], axis=1).astype(BF16)


def _pack_w_cmp(w_ck, w_cv, n_j):
    def one(w):
        lo = w[0:n_j].reshape(n_j * HEAD_DIM, HEAD_DIM)
        hi = w[CMP_STRIDE:CMP_STRIDE + n_j].reshape(n_j * HEAD_DIM, HEAD_DIM)
        return jnp.concatenate([lo, hi], axis=1)
    return jnp.stack([one(w_ck), one(w_cv)]).astype(BF16)


def kernel(x_prompt, x_sample, c_prompt, c_sample, cache_cmp_kv, cache_slc_kv, state_win_kv, state_conv, page_table,
           w_ada, b_ada, norm_ffn1, ffn1_gate, ffn1_up, ffn1_down, norm_mix, w_in, w_cmp_k, w_cmp_v, conv_w, conv_b,
           norm_att_out, norm_conv_out, w_out, norm_ffn2, ffn2_gate, ffn2_up, ffn2_down, norm_final):
    n_p, s_len, _ = x_prompt.shape
    n_seq, n_new, _ = x_sample.shape
    depth = w_ada.shape[0]
    assert n_p == 1 and depth == 1
    n_pages = page_table.shape[1]
    page = cache_slc_kv.shape[2]
    n_phys = cache_slc_kv.shape[1]
    past = n_pages * page
    wbuf = state_win_kv.shape[2]
    keep_p = min(WINDOW, s_len)
    t_s = n_seq * n_new
    l = 0

    c_all = jnp.concatenate([c_sample, c_prompt, jnp.zeros((8 - n_p, D_MODEL), F32)], axis=0)
    mod = _ada(c_all, w_ada[l], b_ada[l])
    mod_p = [mod[n_seq:n_seq + 1, k * D_MODEL:(k + 1) * D_MODEL] for k in range(N_MOD)]
    mod_q = [mod[0:n_seq, k * D_MODEL:(k + 1) * D_MODEL] for k in range(N_MOD)]
    mod_s = {k: jnp.repeat(mod_q[k], n_new, axis=0) for k in (3, 4, 5)}
    tok_major = lambda a: a.reshape(n_seq, n_new, D_MODEL).transpose(1, 0, 2).reshape(t_s, D_MODEL)
    seq_major = lambda a: a.reshape(n_new, n_seq, D_MODEL).transpose(1, 0, 2).reshape(t_s, D_MODEL)

    row = lambda v: v.reshape(1, -1)
    f1 = (ffn1_gate[l].astype(BF16), ffn1_up[l].astype(BF16), ffn1_down[l].astype(BF16))
    f2 = (ffn2_gate[l].astype(BF16), ffn2_up[l].astype(BF16), ffn2_down[l].astype(BF16))
    w_proj = _pack_w_in(w_in[l])
    w_o = w_out[l].astype(BF16)
    nfin = row(norm_final)

    xp = x_prompt.reshape(s_len, D_MODEL)
    xs = x_sample.reshape(t_s, D_MODEL)

    xp = _ffn(xp, mod_p[0], mod_p[1], mod_p[2], row(norm_ffn1[l]), nfin, *f1, final_norm=False)
    xs = seq_major(_ffn(tok_major(xs), mod_q[0], mod_q[1], mod_q[2], row(norm_ffn1[l]), nfin, *f1, final_norm=False))

    cos_p, sin_p = _rope_tables(jnp.arange(s_len))
    cos_s, sin_s = _rope_tables(past + jnp.arange(n_new))
    tm_s = 256
    cos_s, sin_s = jnp.tile(cos_s, (tm_s // n_new, 1)), jnp.tile(sin_s, (tm_s // n_new, 1))
    zero8 = jnp.zeros((8, CONV_W), F32)
    conv_args = (conv_w[l], row(conv_b[l]), row(norm_conv_out[l]))
    (q_p, cmp_p, slc_p, win_p, ksb_p, vst_p, kwb_p, vwb_p, gate_p, ocn_p, utail_p, qt_p) = _proj(
        xp, mod_p[3], mod_p[4], row(norm_mix[l]), w_proj, cos_p, sin_p, zero8, zero8, *conv_args,
        carry=True, seq_rows=s_len, act_dtype=BF16)
    prev1 = jnp.repeat(state_conv[l][:, CONV_K - 2], n_new, axis=0)
    prev2 = jnp.repeat(state_conv[l][:, CONV_K - 3], n_new, axis=0)
    (q_s, cmp_s, slc_s, win_s, _, _, _, _, gate_s, ocn_s, u_s, _) = _proj(
        xs, mod_s[3], mod_s[4], row(norm_mix[l]), w_proj, cos_s, sin_s, prev1, prev2, *conv_args,
        carry=False, seq_rows=n_new, act_dtype=F32, tm=tm_s)

    n_j = CMP_STRIDE
    w_c = _pack_w_cmp(w_cmp_k[l], w_cmp_v[l], n_j)
    lin = lambda a: a.reshape(-1, HEAD_DIM)
    r_p = _compress_products(cmp_p, w_c, n_j, tm=256)
    r_cache = _compress_products(lin(cache_cmp_kv), w_c, n_j, tm=256)
    r_new = _compress_products(cmp_s, _pack_w_cmp(w_cmp_k[l], w_cmp_v[l], n_new), n_new, tm=n_seq)

    n_cmp_pad = s_len // CMP_STRIDE
    cosc, sinc = _rope_tables(jnp.arange(n_cmp_pad) * CMP_STRIDE + (CMP_BLOCK - 1))
    r_p = jnp.pad(r_p, ((0, 8), (0, 0)))
    band = ((WINDOW, 0), (0, 0))
    n_blocks = s_len // SEL_BLOCK
    e_t = ((jnp.arange(s_len) // SEL_BLOCK)[:, None] == jnp.arange(n_blocks)[None, :]).astype(BF16)
    oa_p = _prompt_attention(q_p, qt_p, gate_p, r_p, cosc, sinc, ksb_p, vst_p, jnp.pad(kwb_p, band),
                             jnp.pad(vwb_p, band), e_t, row(norm_att_out[l]))

    n_cmp_s = past // CMP_STRIDE
    cosc_s, sinc_s = _rope_tables(jnp.arange(n_cmp_s) * CMP_STRIDE + (CMP_BLOCK - 1))
    e_s =(jnp.arange(HEAD_DIM)[:, None] == (jnp.arange(past) // SEL_BLOCK * CHUNKS_PER_SEL)[None, :]).astype(BF16)
    oa_s, win_new_state = _sample_attention(
        page_table, r_cache, r_new.reshape(n_seq, 1, -1), q_s, gate_s,
        lin(cache_slc_kv), slc_s, lin(state_win_kv), win_s,
        cosc_s, sinc_s, e_s, row(norm_att_out[l]), n_new=n_new, page=page, wbuf=wbuf)

    xp = _outproj(xp, oa_p, ocn_p, mod_p[5], w_o)
    xs = _outproj(xs, oa_s, ocn_s, mod_s[5], w_o)
    yp = _ffn(xp, mod_p[6], mod_p[7], mod_p[8], row(norm_ffn2[l]), nfin, *f2, final_norm=True)
    ys = seq_major(_ffn(tok_major(xs), mod_q[6], mod_q[7], mod_q[8], row(norm_ffn2[l]), nfin, *f2, final_norm=True))

    kv6 = lambda a, n, s: a.reshape(1, n, s, 2, N_KV, HEAD_DIM)
    return (yp.reshape(n_p, s_len, D_MODEL), ys.reshape(n_seq, n_new, D_MODEL),
            kv6(cmp_p, n_p, s_len), kv6(slc_p, n_p, s_len), kv6(win_p[(s_len - keep_p) * KV_ROWS:], n_p, keep_p),
            utail_p[8 - (CONV_K - 1):].reshape(1, n_p, CONV_K - 1, CONV_W),
            kv6(cmp_s, n_seq, n_new), kv6(slc_s, n_seq, n_new), kv6(win_new_state, n_seq, wbuf),
            u_s.reshape(n_seq, n_new, CONV_W)[:, n_new - (CONV_K - 1):].reshape(1, n_seq, CONV_K - 1, CONV_W))
```

```python
import functools

import jax
import jax.numpy as jnp
from jax import lax
from jax.experimental import pallas as pl
from jax.experimental.pallas import tpu as pltpu

F32 = jnp.float32
BF16 = jnp.bfloat16

D_MODEL = 2048
HEAD_DIM = 128
N_HEADS = 8
N_KV = 2
HEADS_PER_KV = N_HEADS // N_KV
ATT_W = N_HEADS * HEAD_DIM
KV_W = N_KV * HEAD_DIM
CONV_W = D_MODEL - ATT_W
CONV_K = 3
CMP_BLOCK = 32
CMP_STRIDE = 16
SEL_BLOCK = 64
N_SEL = 16
WINDOW = 512
Q_BLOCK = 128
N_BRANCH = 3
N_MOD = 9
ROPE_THETA = 10000.0
EPS = 1e-6
NEG = -1e30
TINY = 1e-30
SCALE = HEAD_DIM ** -0.5
LOG2E = 1.4426950408889634
SLAB_W = 2 * KV_W
KV_ROWS = 2 * N_KV
GATE_PAD = 128
SUBLANES = 8
CHUNKS_PER_SEL = SEL_BLOCK // CMP_STRIDE

VMEM_LIMIT = 56 * 1024 * 1024


def _cparams(*sem):
    return pltpu.CompilerParams(dimension_semantics=sem, vmem_limit_bytes=VMEM_LIMIT)


def _dot(a, b):
    return jnp.dot(a, b, preferred_element_type=F32)


def _dot_nt(a, b):
    return lax.dot_general(a, b, (((1,), (1,)), ((), ())), preferred_element_type=F32)


def _rms(x):
    return x * lax.rsqrt(jnp.mean(x * x, axis=-1, keepdims=True) + EPS)


def _silu(x):
    return x * jax.nn.sigmoid(x)


def _rope(x, cos, sin_signed):
    return x * cos + pltpu.roll(x, HEAD_DIM // 2, axis=1) * sin_signed


def _masked_softmax(s, mask):
    s = jnp.where(mask, s, NEG)
    m = jnp.max(s, axis=-1, keepdims=True)
    e = jnp.where(mask, jnp.exp2(s - m), 0.0)
    return e / jnp.maximum(jnp.sum(e, axis=-1, keepdims=True), TINY)


def _ada_kernel(c_ref, w_ref, b_ref, o_ref):
    a = _silu(c_ref[...]).astype(BF16)
    o_ref[...] = _dot(a, w_ref[...].astype(BF16)) + b_ref[...]


def _ada(c, w, b, tn=1024):
    m, n = c.shape[0], w.shape[1]
    return pl.pallas_call(
        _ada_kernel,
        grid=(n // tn,),
        in_specs=[pl.BlockSpec((m, D_MODEL), lambda j: (0, 0)),
                  pl.BlockSpec((D_MODEL, tn), lambda j: (0, j)),
                  pl.BlockSpec((1, tn), lambda j: (0, j))],
        out_specs=pl.BlockSpec((m, tn), lambda j: (0, j)),
        out_shape=jax.ShapeDtypeStruct((m, n), F32),
        compiler_params=_cparams("arbitrary"),
        name="ada",
    )(c, w, b.reshape(1, n))


def _ffn_kernel(x_ref, sh_ref, sc_ref, gt_ref, ng_ref, nf_ref, wg_ref, wu_ref, wd_ref, o_ref, h_scr, *, n_f, final_norm,
                rep):
    j = pl.program_id(1)
    mod = lambda ref: ref[...] if rep == 1 else jnp.concatenate([ref[...]] * rep, axis=0)

    @pl.when(j == 0)
    def _():
        h = _rms(x_ref[...]) * ng_ref[...] * (1.0 + mod(sc_ref)) + mod(sh_ref)
        h_scr[...] = h.astype(BF16)
        o_ref[...] = jnp.zeros_like(o_ref)

    h = h_scr[...]
    a = (_silu(_dot(h, wg_ref[...])) * _dot(h, wu_ref[...])).astype(BF16)
    o_ref[...] += _dot(a, wd_ref[...])

    @pl.when(j == n_f - 1)
    def _():
        out = x_ref[...] + 0.5 * (1.0 + mod(gt_ref)) * o_ref[...]
        if final_norm:
            out = _rms(out) * nf_ref[...]
        o_ref[...] = out


def _ffn(x, shift, scale, gate, norm_g, norm_final, wg, wu, wd, *, final_norm, tm=512, tf=512):
    t = x.shape[0]
    d_ff = wg.shape[1]
    n_f = d_ff // tf
    mrows = shift.shape[0]
    rep = 1 if mrows == 1 else tm // mrows
    assert mrows == 1 or (tm % mrows == 0 and t % tm == 0)
    mod_spec = pl.BlockSpec((mrows, D_MODEL), lambda i, j: (0, 0))
    row_spec = pl.BlockSpec((tm, D_MODEL), lambda i, j: (i, 0))
    vec_spec = pl.BlockSpec((1, D_MODEL), lambda i, j: (0, 0))
    return pl.pallas_call(
        functools.partial(_ffn_kernel, n_f=n_f, final_norm=final_norm, rep=rep),
        grid=(t // tm, n_f),
        in_specs=[row_spec, mod_spec, mod_spec, mod_spec, vec_spec, vec_spec,
                  pl.BlockSpec((D_MODEL, tf), lambda i, j: (0, j)),
                  pl.BlockSpec((D_MODEL, tf), lambda i, j: (0, j)),
                  pl.BlockSpec((tf, D_MODEL), lambda i, j: (j, 0))],
        out_specs=row_spec,
        out_shape=jax.ShapeDtypeStruct((t, D_MODEL), F32),
        scratch_shapes=[pltpu.VMEM((tm, D_MODEL), BF16)],
        compiler_params=_cparams("arbitrary", "arbitrary"),
        name="ffn",
    )(x, shift, scale, gate, norm_g, norm_final, wg, wu, wd)


PW_Q = 0
PW_KV = PW_Q + ATT_W
PW_CONV = PW_KV + 3 * SLAB_W
PW_GATE = PW_CONV + 3 * CONV_W
PW_TOTAL = PW_GATE + GATE_PAD


def _proj_kernel(x_ref, sh_ref, sc_ref, ng_ref, w_ref, cos_ref, sin_ref, p1_ref, p2_ref, cw_ref, cb_ref, nco_ref,
                 q_ref, cmp_ref, slc_ref, win_ref, ksb_ref, vst_ref, kwb_ref, vwt_ref, gate_ref, ocn_ref, u_ref, qt_ref,
                 gatet_ref, carry_scr, *, tm, seq_rows, carry):
    i = pl.program_id(0)
    h = (_rms(x_ref[...]) * ng_ref[...] * (1.0 + sc_ref[...]) + sh_ref[...]).astype(BF16)
    cos, sin = cos_ref[...], sin_ref[...]

    pq = _dot(h, w_ref[:, PW_Q:PW_KV])
    for hd in range(N_HEADS):
        hs = slice(hd * HEAD_DIM, (hd + 1) * HEAD_DIM)
        blk = _rope(pq[:, hs], cos, sin) * (SCALE * LOG2E)
        q_ref[:, hs] = blk.astype(q_ref.dtype)
        qt_ref[hs, :] = blk.T.astype(BF16)

    pkv = _dot(h, w_ref[:, PW_KV:PW_CONV])
    for slab, (o_ref, kb_ref) in enumerate(((cmp_ref, None), (slc_ref, ksb_ref), (win_ref, kwb_ref))):
        base = slab * SLAB_W
        for g in range(N_KV):
            gs = slice(g * HEAD_DIM, (g + 1) * HEAD_DIM)
            k = pkv[:, base + g * HEAD_DIM:base + (g + 1) * HEAD_DIM]
            v = pkv[:, base + KV_W + g * HEAD_DIM:base + KV_W + (g + 1) * HEAD_DIM]
            if kb_ref is not None:
                k = _rope(k, cos, sin)
                kb_ref[:, gs] = k.astype(BF16)
                v_t = v.T.astype(BF16)
                if slab == 1:
                    vst_ref[0, gs, :] = v_t
                else:
                    for blk in range(tm // Q_BLOCK):
                        vwt_ref[blk, gs, :] = v_t[:, blk * Q_BLOCK:(blk + 1) * Q_BLOCK]
            o_ref[pl.ds(g, tm, stride=KV_ROWS), :] = k
            o_ref[pl.ds(N_KV + g, tm, stride=KV_ROWS), :] = v

    gate = jax.nn.sigmoid(_dot(h, w_ref[:, PW_GATE:PW_TOTAL]))
    gate_ref[...] = gate
    gatet_ref[...] = gate.T

    pc = _dot(h, w_ref[:, PW_CONV:PW_GATE])
    u = pc[:, 0:CONV_W] * pc[:, 2 * CONV_W:3 * CONV_W]
    c_out = pc[:, CONV_W:2 * CONV_W]
    row = lax.broadcasted_iota(jnp.int32, (tm, CONV_W), 0)
    if carry:
        @pl.when(i == 0)
        def _():
            carry_scr[...] = jnp.zeros_like(carry_scr)
        prev1 = carry_scr[7:8, :]
        prev2 = carry_scr[6:7, :]
        rs = row
    else:
        prev1 = p1_ref[...]
        prev2 = p2_ref[...]
        rs = row & (seq_rows - 1)
    um1 = jnp.where(rs >= 1, pltpu.roll(u, 1, axis=0), prev1)
    um2 = jnp.where(rs >= 2, pltpu.roll(u, 2, axis=0), jnp.where(rs == 1, prev1, prev2))
    y = um2 * cw_ref[0:1, :] + um1 * cw_ref[1:2, :] + u * cw_ref[2:3, :] + cb_ref[...]
    ocn_ref[...] = (_rms(c_out * y) * nco_ref[...]).astype(ocn_ref.dtype)
    if carry:
        carry_scr[...] = u[tm - 8:tm, :]
        u_ref[...] = u[tm - 8:tm, :]
    else:
        u_ref[...] = u


def _proj(x, shift, scale, norm_g, w, cos, sin, prev1, prev2, conv_w, conv_b, norm_co, *, carry, seq_rows, act_dtype,
          tm=256):
    t = x.shape[0]
    mrows = shift.shape[0]
    mod_spec = (pl.BlockSpec((1, D_MODEL), lambda i: (0, 0)) if mrows == 1
                else pl.BlockSpec((tm, D_MODEL), lambda i: (i, 0)))
    rows = lambda wdt: pl.BlockSpec((tm, wdt), lambda i: (i, 0))
    const = lambda r, wdt: pl.BlockSpec((r, wdt), lambda i: (0, 0))
    tab_spec = rows(HEAD_DIM) if carry else const(tm, HEAD_DIM)
    prev_spec = const(8, CONV_W) if carry else rows(CONV_W)
    u_rows = 8 if carry else t
    out_shape = [jax.ShapeDtypeStruct((t, ATT_W), act_dtype)]
    out_shape += [jax.ShapeDtypeStruct((t * KV_ROWS, HEAD_DIM), F32)] * 3
    per_ck = P_CK // tm
    kvb = jax.ShapeDtypeStruct((t, KV_W), BF16)
    out_shape += [kvb, jax.ShapeDtypeStruct((t // P_CK, KV_W, P_CK), BF16), kvb,
                  jax.ShapeDtypeStruct((t // Q_BLOCK, KV_W, Q_BLOCK), BF16)]
    out_shape += [jax.ShapeDtypeStruct((t, GATE_PAD), F32), jax.ShapeDtypeStruct((t, CONV_W), act_dtype),
                  jax.ShapeDtypeStruct((u_rows, CONV_W), F32), jax.ShapeDtypeStruct((ATT_W, t), BF16),
                  jax.ShapeDtypeStruct((GATE_PAD, t), F32)]
    lin_spec = pl.BlockSpec((tm * KV_ROWS, HEAD_DIM), lambda i: (i, 0))
    vst_spec = pl.BlockSpec((1, KV_W, tm), lambda i: (i // per_ck, 0, i % per_ck))
    vwt_spec = pl.BlockSpec((tm // Q_BLOCK, KV_W, Q_BLOCK), lambda i: (i, 0, 0))
    out_specs = [rows(ATT_W)] + [lin_spec] * 3 + [rows(KV_W), vst_spec, rows(KV_W), vwt_spec]
    out_specs += [rows(GATE_PAD), rows(CONV_W), const(8, CONV_W) if carry else rows(CONV_W),
                  pl.BlockSpec((ATT_W, tm), lambda i: (0, i)), pl.BlockSpec((GATE_PAD, tm), lambda i: (0, i))]
    return pl.pallas_call(
        functools.partial(_proj_kernel, tm=tm, seq_rows=seq_rows, carry=carry),
        grid=(t // tm,),
        in_specs=[rows(D_MODEL), mod_spec, mod_spec, const(1, D_MODEL),
                  pl.BlockSpec(memory_space=pltpu.VMEM),
                  tab_spec, tab_spec, prev_spec, prev_spec, const(CONV_K, CONV_W), const(1, CONV_W),
                  const(1, CONV_W)],
        out_specs=out_specs,
        out_shape=out_shape,
        scratch_shapes=[pltpu.VMEM((8, CONV_W), F32)],
        compiler_params=_cparams("arbitrary"),
        name="proj",
    )(x, shift, scale, norm_g, w, cos, sin, prev1, prev2, conv_w, conv_b, norm_co)


def _cmp_kernel(x_ref, w_ref, o_ref, pad_scr, *, n_j, tm):
    pitch = n_j * KV_ROWS
    ppitch = pitch + SUBLANES

    def repitch(c, _):
        src = pl.multiple_of(c * pitch, SUBLANES)
        dst = pl.multiple_of(c * ppitch, SUBLANES)
        pad_scr[pl.ds(dst, pitch), :] = x_ref[pl.ds(src, pitch), :]
        return 0

    lax.fori_loop(0, tm, repitch, 0, unroll=8)
    flat = pad_scr
    for kv in range(2):
        for g in range(N_KV):
            xs = jnp.concatenate([flat[pl.ds(j * KV_ROWS + kv * N_KV + g, tm, stride=ppitch), :] for j in range(n_j)],
                                 axis=1)
            col = (kv * N_KV + g) * 2 * HEAD_DIM
            o_ref[:, col:col + 2 * HEAD_DIM] = _dot(xs.astype(BF16), w_ref[kv])


def _compress_products(x, w, n_j, tm):
    pitch = n_j * KV_ROWS
    m = x.shape[0] // pitch
    n_out = 2 * N_KV * 2 * HEAD_DIM
    tm = min(tm, m)
    return pl.pallas_call(
        functools.partial(_cmp_kernel, n_j=n_j, tm=tm),
        grid=(m // tm,),
        in_specs=[pl.BlockSpec((tm * pitch, HEAD_DIM), lambda i: (i, 0)),
                  pl.BlockSpec((2, n_j * HEAD_DIM, 2 * HEAD_DIM), lambda i: (0, 0, 0))],
        out_specs=pl.BlockSpec((tm, n_out), lambda i: (i, 0)),
        out_shape=jax.ShapeDtypeStruct((m, n_out), F32),
        scratch_shapes=[pltpu.VMEM((tm * (pitch + SUBLANES), HEAD_DIM), F32)],
        compiler_params=_cparams("arbitrary"),
        name="compress",
    )(x, w)


def _r_cols(kv, g):
    return (kv * N_KV + g) * 2 * HEAD_DIM


P_CK = 512
P_WKEYS = WINDOW + Q_BLOCK


def _pattn_t_kernel(qt_ref, gatet_ref, rp_ref, cosc_ref, sinc_ref, ks_ref, vst_ref, kw_ref, vwt_ref, et_ref, band_ref,
                    nao_ref, o_ref, kc_scr, vct_scr, pt_scr, ot_scr, s_scr, *, n_cmp_pad, n_blocks):
    i = pl.program_id(0)
    s0 = i * Q_BLOCK
    nq = Q_BLOCK
    rows = HEADS_PER_KV * nq
    lanes4 = lambda a: jnp.concatenate([a] * HEADS_PER_KV, axis=1)

    @pl.when(i == 0)
    def _():
        for g in range(N_KV):
            gs = slice(g * HEAD_DIM, (g + 1) * HEAD_DIM)
            ck, cv = _r_cols(0, g), _r_cols(1, g)
            kc = rp_ref[0:n_cmp_pad, ck:ck + HEAD_DIM] + rp_ref[pl.ds(1, n_cmp_pad), ck + HEAD_DIM:ck + 2 * HEAD_DIM]
            kc_scr[:, gs] = _rope(kc, cosc_ref[...], sinc_ref[...]).astype(BF16)
            vc = rp_ref[0:n_cmp_pad, cv:cv + HEAD_DIM] + rp_ref[pl.ds(1, n_cmp_pad), cv + HEAD_DIM:cv + 2 * HEAD_DIM]
            vct_scr[gs, :] = vc.T.astype(BF16)
        pt_scr[...] = jnp.zeros_like(pt_scr)

    t_q = s0 + lax.broadcasted_iota(jnp.int32, (n_cmp_pad, nq), 1)
    c_end = lax.broadcasted_iota(jnp.int32, (n_cmp_pad, nq), 0) * CMP_STRIDE + (CMP_BLOCK - 1)
    bias_c = lanes4(jnp.where(c_end <= t_q, 0.0, NEG))
    j_io = lax.broadcasted_iota(jnp.int32, (P_WKEYS, nq), 0)
    bias_w = lanes4(jnp.where(j_io >= WINDOW - s0, band_ref[...], NEG))

    scores, rhs = [], []
    for g in range(N_KV):
        gs = slice(g * HEAD_DIM, (g + 1) * HEAD_DIM)
        q_t = jnp.concatenate([qt_ref[(g * HEADS_PER_KV + r) * HEAD_DIM:(g * HEADS_PER_KV + r + 1) * HEAD_DIM, :]
                               for r in range(HEADS_PER_KV)], axis=1)
        rhs.append(q_t)

        s = _dot(kc_scr[:, gs], q_t) + bias_c
        m = jnp.max(s, axis=0, keepdims=True)
        e = jnp.exp2(s - m)
        inv = jnp.where(m > 0.5 * NEG, 1.0 / jnp.maximum(jnp.sum(e, axis=0, keepdims=True), TINY), 0.0)
        o_c = _dot(vct_scr[gs, :], e.astype(BF16)) * inv

        p = e * inv
        pt_scr[8:8 + n_cmp_pad, :] = sum(p[:, r * nq:(r + 1) * nq] for r in range(HEADS_PER_KV))
        st = lambda k: pt_scr[pl.ds(7 + k, n_blocks, stride=CHUNKS_PER_SEL), :]
        score = 0.5 * st(0) + st(1) + st(2) + st(3) + 0.5 * st(4)
        b_io = lax.broadcasted_iota(jnp.int32, (n_blocks, nq), 0)
        t_lane = s0 + lax.broadcasted_iota(jnp.int32, (n_blocks, nq), 1)
        cur = t_lane >> 6
        forced = (b_io == 0) | (b_io == cur) | (b_io == cur - 1)
        valid = b_io * SEL_BLOCK <= t_lane
        scores.append(jnp.where(forced, jnp.inf, jnp.where(valid, score, -jnp.inf)))

        blocks = [jnp.maximum(i + k - WINDOW // nq, 0) for k in range(P_WKEYS // nq)]
        k_w = jnp.concatenate([kw_ref[pl.ds(pl.multiple_of(b * nq, nq), nq), gs] for b in blocks], axis=0)
        v_wt = jnp.concatenate([vwt_ref[b, gs, :] for b in blocks], axis=1)
        s = _dot(k_w, q_t) + bias_w
        e = jnp.exp2(s - jnp.max(s, axis=0, keepdims=True))
        o_w = _dot(v_wt, e.astype(BF16)) / jnp.maximum(jnp.sum(e, axis=0, keepdims=True), TINY)

        for r in range(HEADS_PER_KV):
            hd = g * HEADS_PER_KV + r
            ls = slice(r * nq, (r + 1) * nq)
            gc = gatet_ref[hd * N_BRANCH + 0:hd * N_BRANCH + 1, :]
            gw = gatet_ref[hd * N_BRANCH + 2:hd * N_BRANCH + 3, :]
            ot_scr[hd * HEAD_DIM:(hd + 1) * HEAD_DIM, :] = gc * o_c[:, ls] + gw * o_w[:, ls]

    b_f = lax.broadcasted_iota(jnp.int32, (n_blocks, nq), 0).astype(F32)

    def pick(_, c):
        out = []
        for work, sel in c:
            m = jnp.max(work, axis=0, keepdims=True)
            idx = jnp.min(jnp.where(work == m, b_f, float(n_blocks)), axis=0, keepdims=True)
            hit = b_f == idx
            out.append((jnp.where(hit, -jnp.inf, work), jnp.where(hit, 1.0, sel)))
        return tuple(out)

    picked = lax.fori_loop(0, min(N_SEL, n_blocks), pick,
                           tuple((sc, jnp.zeros((n_blocks, nq), F32)) for sc in scores))

    for g in range(N_KV):
        sel_bias = jnp.where(picked[g][1] > 0.5, 0.0, NEG).astype(BF16)
        rhs[g] = jnp.concatenate([rhs[g], lanes4(sel_bias)], axis=0)

    def qk_scores(c, slot):
        k0 = pl.multiple_of(c * P_CK, P_CK)
        blk_hot = et_ref[pl.ds(k0, P_CK), :]
        for g in range(N_KV):
            gs = slice(g * HEAD_DIM, (g + 1) * HEAD_DIM)
            s_scr[slot, g] = _dot(jnp.concatenate([ks_ref[pl.ds(k0, P_CK), gs], blk_hot], axis=1), rhs[g])

    def softmax_pv(c, slot, carry, causal):
        if causal:
            key = c * P_CK + lax.broadcasted_iota(jnp.int32, (P_CK, rows), 0)
            t_k = s0 + (lax.broadcasted_iota(jnp.int32, (P_CK, rows), 1) & (nq - 1))
            cb = jnp.where(key <= t_k, 0.0, NEG)
        out = []
        for g in range(N_KV):
            gs = slice(g * HEAD_DIM, (g + 1) * HEAD_DIM)
            m_i, l_i, acc = carry[g]
            s = s_scr[slot, g]
            if causal:
                s = s + cb
            m_n = jnp.maximum(m_i, jnp.max(s, axis=0, keepdims=True))
            p = jnp.exp2(s - m_n)
            alpha = jnp.exp2(m_i - m_n)
            l_n = alpha * l_i + jnp.sum(p, axis=0, keepdims=True)
            out.append((m_n, l_n, alpha * acc + _dot(vst_ref[c, gs, :], p.astype(BF16))))
        return tuple(out)

    def pair(pr, carry):
        c = 2 * pr
        qk_scores(c + 1, 1)
        carry = softmax_pv(c, 0, carry, False)
        qk_scores(c + 2, 0)
        return softmax_pv(c + 1, 1, carry, False)

    def odd_tail(c, carry):
        qk_scores(c + 1, 1)
        return softmax_pv(c, 0, carry, False)

    n_chunks = (s0 + nq + P_CK - 1) // P_CK
    n_pairs = (n_chunks - 1) // 2
    init = tuple((jnp.full((1, rows), NEG, F32), jnp.zeros((1, rows), F32), jnp.zeros((HEAD_DIM, rows), F32))
                 for _ in range(N_KV))
    qk_scores(0, 0)
    carry = lax.fori_loop(0, n_pairs, pair, init)
    last = n_chunks - 1
    carry = lax.cond(last > 2 * n_pairs, lambda cr: odd_tail(last - 1, cr), lambda cr: cr, carry)
    carry = lax.cond(last > 2 * n_pairs, lambda cr: softmax_pv(last, 1, cr, True),
                     lambda cr: softmax_pv(last, 0, cr, True), carry)
    for g in range(N_KV):
        _, l_s, acc = carry[g]
        o_st = acc / jnp.maximum(l_s, TINY)
        for r in range(HEADS_PER_KV):
            hd = g * HEADS_PER_KV + r
            gsl = gatet_ref[hd * N_BRANCH + 1:hd * N_BRANCH + 2, :]
            ot_scr[hd * HEAD_DIM:(hd + 1) * HEAD_DIM, :] += gsl * o_st[:, r * nq:(r + 1) * nq]

    o_t = ot_scr[...]
    o_t = o_t * lax.rsqrt(jnp.mean(o_t * o_t, axis=0, keepdims=True) + EPS)
    o_ref[...] = (o_t.T * nao_ref[...]).astype(o_ref.dtype)


def _prompt_attention_t(q_t, gates_t, rp, cosc, sinc, ks, vs_t, kw, vw_t, e_t, band, norm_ao):
    s_len = q_t.shape[1]
    n_cmp_pad = s_len // CMP_STRIDE
    n_blocks = s_len // SEL_BLOCK
    vmem = pl.BlockSpec(memory_space=pltpu.VMEM)
    cols = lambda h: pl.BlockSpec((h, Q_BLOCK), lambda i: (0, i))
    return pl.pallas_call(
        functools.partial(_pattn_t_kernel, n_cmp_pad=n_cmp_pad, n_blocks=n_blocks),
        grid=(s_len // Q_BLOCK,),
        in_specs=[cols(ATT_W), cols(GATE_PAD), vmem, vmem, vmem, vmem, vmem, vmem, vmem, vmem, vmem,
                  pl.BlockSpec((1, ATT_W), lambda i: (0, 0))],
        out_specs=pl.BlockSpec((Q_BLOCK, ATT_W), lambda i: (i, 0)),
        out_shape=jax.ShapeDtypeStruct((s_len, ATT_W), BF16),
        scratch_shapes=[pltpu.VMEM((n_cmp_pad, KV_W), BF16), pltpu.VMEM((KV_W, n_cmp_pad), BF16),
                        pltpu.VMEM((n_cmp_pad + 16, Q_BLOCK), F32), pltpu.VMEM((ATT_W, Q_BLOCK), F32),
                        pltpu.VMEM((2, N_KV, P_CK, HEADS_PER_KV * Q_BLOCK), F32)],
        compiler_params=_cparams("arbitrary"),
        name="prompt_attn",
    )(q_t, gates_t, rp, cosc, sinc, ks, vs_t, kw, vw_t, e_t, band, norm_ao)


def _sattn_kernel(pt_ref, *refs, n_seqs, n_pages, page, n_new, past, wbuf):
    del pt_ref
    n_pg = n_seqs * n_pages
    r_pages, s_pages = refs[0:n_pg], refs[n_pg:2 * n_pg]
    (rnew_ref, q_ref, gate_ref, snew_ref, wst_ref, wnew_ref, cosc_ref, sinc_ref, e_ref, nao_ref,
     o_ref, wout_ref, r_scr, k_scr, v_scr, kw_scr, vw_scr, o_scr) = refs[2 * n_pg:]
    chunks_pp = page // CMP_STRIDE
    n_cmp = n_pages * chunks_pp
    rows = HEADS_PER_KV * n_new
    n_keys = k_scr.shape[1]
    n_wkeys = kw_scr.shape[1]
    pad = n_keys - past
    wpad = n_wkeys - wbuf
    batch = [(sq, g) for sq in range(n_seqs) for g in range(N_KV)]
    cat = lambda parts: jnp.concatenate(parts, axis=0)

    def with_zero_rows(new_rows, n_zero):
        return cat([new_rows, jnp.zeros((n_zero, HEAD_DIM), F32)]).astype(BF16)

    for sq in range(n_seqs):
        n0 = sq * n_new * KV_ROWS
        w0 = sq * wbuf * KV_ROWS
        for p in range(n_pages):
            r_scr[sq, p * chunks_pp:(p + 1) * chunks_pp, :] = r_pages[sq * n_pages + p][...]
            pg = s_pages[sq * n_pages + p]
            for g in range(N_KV):
                b = sq * N_KV + g
                k_scr[b, p * page:(p + 1) * page, :] = pg[pl.ds(g, page, stride=KV_ROWS), :].astype(BF16)
                v_scr[b, p * page:(p + 1) * page, :] = pg[pl.ds(N_KV + g, page, stride=KV_ROWS), :].astype(BF16)
        r_scr[sq, n_cmp:n_cmp + 8, :] = cat([rnew_ref[sq], jnp.zeros((7, r_scr.shape[2]), F32)])
        for g in range(N_KV):
            b = sq * N_KV + g
            k_scr[b, past:n_keys, :] = with_zero_rows(snew_ref[pl.ds(n0 + g, n_new, stride=KV_ROWS), :], pad - n_new)
            v_scr[b, past:n_keys, :] = with_zero_rows(snew_ref[pl.ds(n0 + N_KV + g, n_new, stride=KV_ROWS), :],
                                                      pad - n_new)
            kw_scr[b, 0:wbuf, :] = wst_ref[pl.ds(w0 + g, wbuf, stride=KV_ROWS), :].astype(BF16)
            vw_scr[b, 0:wbuf, :] = wst_ref[pl.ds(w0 + N_KV + g, wbuf, stride=KV_ROWS), :].astype(BF16)
            kw_scr[b, wbuf:n_wkeys, :] = with_zero_rows(wnew_ref[pl.ds(n0 + g, n_new, stride=KV_ROWS), :],
                                                        wpad - n_new)
            vw_scr[b, wbuf:n_wkeys, :] = with_zero_rows(wnew_ref[pl.ds(n0 + N_KV + g, n_new, stride=KV_ROWS), :],
                                                        wpad - n_new)
        keep = (wbuf - n_new) * KV_ROWS
        wout_ref[w0:w0 + keep, :] = wst_ref[w0 + n_new * KV_ROWS:w0 + wbuf * KV_ROWS, :]
        wout_ref[w0 + keep:w0 + wbuf * KV_ROWS, :] = wnew_ref[n0:n0 + n_new * KV_ROWS, :]

    n_b = len(batch)
    all_rows = n_b * rows
    tok = lax.broadcasted_iota(jnp.int32, (all_rows, 1), 0) & (n_new - 1)
    t_rows = past + tok

    qg, kc, vc = [], [], []
    for sq, g in batch:
        qs = slice(sq * n_new, (sq + 1) * n_new)
        qg.append(cat([q_ref[qs, (g * HEADS_PER_KV + r) * HEAD_DIM:(g * HEADS_PER_KV + r + 1) * HEAD_DIM]
                       for r in range(HEADS_PER_KV)]).astype(BF16))
        ck, cv = _r_cols(0, g), _r_cols(1, g)
        kc.append(r_scr[sq, 0:n_cmp, ck:ck + HEAD_DIM] + r_scr[sq, pl.ds(1, n_cmp), ck + HEAD_DIM:ck + 2 * HEAD_DIM])
        vc.append((r_scr[sq, 0:n_cmp, cv:cv + HEAD_DIM]
                   + r_scr[sq, pl.ds(1, n_cmp), cv + HEAD_DIM:cv + 2 * HEAD_DIM]).astype(BF16))
    cos_all, sin_all = cat([cosc_ref[...]] * n_b), cat([sinc_ref[...]] * n_b)
    kc_all = _rope(cat(kc), cos_all, sin_all).astype(BF16)
    rb = lambda b: slice(b * rows, (b + 1) * rows)

    c_end = lax.broadcasted_iota(jnp.int32, (all_rows, n_cmp), 1) * CMP_STRIDE + (CMP_BLOCK - 1)
    s_c = cat([_dot_nt(qg[b], kc_all[b * n_cmp:(b + 1) * n_cmp]) for b in range(n_b)])
    p_c = _masked_softmax(s_c, c_end <= t_rows)
    p_cb = p_c.astype(BF16)
    o_c = [_dot(p_cb[rb(b)], vc[b]) for b in range(n_b)]

    sel_rows = n_b * n_new
    p_grp = cat([sum(p_c[b * rows + r * n_new:b * rows + (r + 1) * n_new] for r in range(HEADS_PER_KV))
                 for b in range(n_b)])
    lane = lax.broadcasted_iota(jnp.int32, (sel_rows, HEAD_DIM), 1)
    pch = 0.5 * (p_grp + jnp.where(lane >= 1, pltpu.roll(p_grp, 1, axis=1), 0.0))
    score = pch
    for k in range(1, CHUNKS_PER_SEL):
        score = score + pltpu.roll(pch, HEAD_DIM - k, axis=1)
    blk = lane >> 2
    t_tok = past + (lax.broadcasted_iota(jnp.int32, (sel_rows, HEAD_DIM), 0) & (n_new - 1))
    cur = t_tok >> 6
    forced = (blk == 0) | (blk == cur) | (blk == cur - 1)
    score = jnp.where(forced, jnp.inf, score)
    n_pb = past // SEL_BLOCK
    ahead = jnp.zeros((sel_rows, HEAD_DIM), F32)
    for k in range(1, n_pb):
        other = pltpu.roll(score, CHUNKS_PER_SEL * k, axis=1)
        wins = (other > score) | ((other == score) & (blk >= k))
        ahead = ahead + jnp.where(wins, 1.0, 0.0)
    sel = jnp.where(((lane & (CHUNKS_PER_SEL - 1)) == 0) & (ahead < N_SEL - 1), 1.0, 0.0).astype(BF16)
    sel_keys = _dot(sel, e_ref[...])
    new_ok = (lax.broadcasted_iota(jnp.int32, (sel_rows, pad), 1)
              <= (lax.broadcasted_iota(jnp.int32, (sel_rows, pad), 0) & (n_new - 1)))
    bias = jnp.concatenate([jnp.where(sel_keys > 0.5, 0.0, NEG), jnp.where(new_ok, 0.0, NEG)], axis=1)
    bias = cat([bias[b * n_new:(b + 1) * n_new] for b in range(n_b) for _ in range(HEADS_PER_KV)])

    s_s = cat([_dot_nt(qg[b], k_scr[b]) for b in range(n_b)]) + bias
    e_s = jnp.exp2(s_s - jnp.max(s_s, axis=-1, keepdims=True))
    p_s = (e_s / jnp.maximum(jnp.sum(e_s, axis=-1, keepdims=True), TINY)).astype(BF16)
    o_s = [_dot(p_s[rb(b)], v_scr[b]) for b in range(n_b)]

    j_io = lax.broadcasted_iota(jnp.int32, (all_rows, n_wkeys), 1)
    m_w = (((j_io < wbuf) & (j_io > tok + (wbuf - WINDOW)) & (j_io >= wbuf - past))
           | ((j_io >= wbuf) & (j_io - wbuf <= tok)))
    p_w = _masked_softmax(cat([_dot_nt(qg[b], kw_scr[b]) for b in range(n_b)]), m_w).astype(BF16)
    o_w = [_dot(p_w[rb(b)], vw_scr[b]) for b in range(n_b)]

    for b, (sq, g) in enumerate(batch):
        qs = slice(sq * n_new, (sq + 1) * n_new)
        for r in range(HEADS_PER_KV):
            hd = g * HEADS_PER_KV + r
            rs = slice(r * n_new, (r + 1) * n_new)
            gc = gate_ref[qs, hd * N_BRANCH + 0:hd * N_BRANCH + 1]
            gsl = gate_ref[qs, hd * N_BRANCH + 1:hd * N_BRANCH + 2]
            gw = gate_ref[qs, hd * N_BRANCH + 2:hd * N_BRANCH + 3]
            o_scr[qs, hd * HEAD_DIM:(hd + 1) * HEAD_DIM] = gc * o_c[b][rs] + gsl * o_s[b][rs] + gw * o_w[b][rs]

    o_ref[...] = _rms(o_scr[...]) * nao_ref[...]


S_SEQS = 2


def _sample_attention(page_table, r_all, r_new, q, gates, slc_cache, slc_new, win_state, win_new, cosc, sinc, e_mat,
                      norm_ao, *, n_new, page, wbuf):
    n_seq, n_pages = page_table.shape
    past = n_pages * page
    chunks_pp = page // CMP_STRIDE
    n_cmp = n_pages * chunks_pp
    assert n_cmp == HEAD_DIM and n_new == 8 and past % SEL_BLOCK == 0 and n_new <= SEL_BLOCK
    assert (past + n_new - 1) // SEL_BLOCK == past // SEL_BLOCK and wbuf == WINDOW and past >= WINDOW
    n_keys = past + HEAD_DIM
    n_wkeys = wbuf + HEAD_DIM
    r_w = r_all.shape[1]

    ns = S_SEQS
    assert n_seq % ns == 0
    page_map = lambda sq, p: (lambda b, pt: (pt[b * ns + sq, p], 0))
    in_specs = [pl.BlockSpec((chunks_pp, r_w), page_map(sq, p)) for sq in range(ns) for p in range(n_pages)]
    in_specs += [pl.BlockSpec((page * KV_ROWS, HEAD_DIM), page_map(sq, p)) for sq in range(ns) for p in range(n_pages)]
    seq_rows = lambda wdt: pl.BlockSpec((ns * n_new, wdt), lambda b, pt: (b, 0))
    kv_rows = lambda n_tok: pl.BlockSpec((ns * n_tok * KV_ROWS, HEAD_DIM), lambda b, pt: (b, 0))
    const = lambda shape: pl.BlockSpec(shape, lambda b, pt: (0,) * len(shape))
    in_specs += [pl.BlockSpec((ns, 1, r_w), lambda b, pt: (b, 0, 0)), seq_rows(ATT_W), seq_rows(GATE_PAD),
                 kv_rows(n_new), kv_rows(wbuf), kv_rows(n_new),
                 const((n_cmp, HEAD_DIM)), const((n_cmp, HEAD_DIM)), const((HEAD_DIM, past)), const((1, ATT_W))]
    grid_spec = pltpu.PrefetchScalarGridSpec(
        num_scalar_prefetch=1,
        grid=(n_seq // ns,),
        in_specs=in_specs,
        out_specs=[seq_rows(ATT_W), kv_rows(wbuf)],
        scratch_shapes=[pltpu.VMEM((ns, n_cmp + 8, r_w), F32), pltpu.VMEM((ns * N_KV, n_keys, HEAD_DIM), BF16),
                        pltpu.VMEM((ns * N_KV, n_keys, HEAD_DIM), BF16), pltpu.VMEM((ns * N_KV, n_wkeys, HEAD_DIM), BF16),
                        pltpu.VMEM((ns * N_KV, n_wkeys, HEAD_DIM), BF16), pltpu.VMEM((ns * n_new, ATT_W), F32)],
    )
    return pl.pallas_call(
        functools.partial(_sattn_kernel, n_seqs=ns, n_pages=n_pages, page=page, n_new=n_new, past=past, wbuf=wbuf),
        grid_spec=grid_spec,
        out_shape=[jax.ShapeDtypeStruct((n_seq * n_new, ATT_W), F32),
                   jax.ShapeDtypeStruct((n_seq * wbuf * KV_ROWS, HEAD_DIM), F32)],
        compiler_params=_cparams("arbitrary"),
        name="sample_attn",
    )(page_table, *([r_all] * (ns * n_pages)), *([slc_cache] * (ns * n_pages)), r_new, q, gates, slc_new, win_state,
      win_new, cosc, sinc, e_mat, norm_ao)


def _outproj_kernel(x_ref, oa_ref, oc_ref, gt_ref, w_ref, o_ref):
    y = _dot(oa_ref[...].astype(BF16), w_ref[0:ATT_W, :]) + _dot(oc_ref[...].astype(BF16), w_ref[ATT_W:D_MODEL, :])
    o_ref[...] = x_ref[...] + (1.0 + gt_ref[...]) * y


def _outproj(x, oa, oc, gate, w, tm=512):
    t = x.shape[0]
    mrows = gate.shape[0]
    mod_spec = (pl.BlockSpec((1, D_MODEL), lambda i: (0, 0)) if mrows == 1
                else pl.BlockSpec((tm, D_MODEL), lambda i: (i, 0)))
    rows = lambda wdt: pl.BlockSpec((tm, wdt), lambda i: (i, 0))
    return pl.pallas_call(
        _outproj_kernel,
        grid=(t // tm,),
        in_specs=[rows(D_MODEL), rows(ATT_W), rows(CONV_W), mod_spec, pl.BlockSpec(memory_space=pltpu.VMEM)],
        out_specs=rows(D_MODEL),
        out_shape=jax.ShapeDtypeStruct((t, D_MODEL), F32),
        compiler_params=_cparams("arbitrary"),
        name="outproj",
    )(x, oa, oc, gate, w)


def _rope_tables(pos):
    half = HEAD_DIM // 2
    inv = ROPE_THETA ** (-jnp.arange(half, dtype=F32) * 2.0 / HEAD_DIM)
    ang = pos.astype(F32)[:, None] * inv[None, :]
    cos, sin = jnp.cos(ang), jnp.sin(ang)
    return jnp.concatenate([cos, cos], axis=1), jnp.concatenate([-sin, sin], axis=1)


def _pack_w_in(w_in):
    off_kv = ATT_W
    off_g = off_kv + 3 * SLAB_W
    off_c = off_g + N_HEADS * N_BRANCH
    gate_cols = jnp.pad(w_in[:, off_g:off_c], ((0, 0), (0, GATE_PAD - N_HEADS * N_BRANCH)))
    return jnp.concatenate([w_in[:, 0:off_g], w_in[:, off_c:off_c + 3 * CONV_W], gate_cols], axis=1).astype(BF16)


def _pack_w_cmp(w_ck, w_cv, n_j):
    def one(w):
        lo = w[0:n_j].reshape(n_j * HEAD_DIM, HEAD_DIM)
        hi = w[CMP_STRIDE:CMP_STRIDE + n_j].reshape(n_j * HEAD_DIM, HEAD_DIM)
        return jnp.concatenate([lo, hi], axis=1)
    return jnp.stack([one(w_ck), one(w_cv)]).astype(BF16)


def kernel(x_prompt, x_sample, c_prompt, c_sample, cache_cmp_kv, cache_slc_kv, state_win_kv, state_conv, page_table,
           w_ada, b_ada, norm_ffn1, ffn1_gate, ffn1_up, ffn1_down, norm_mix, w_in, w_cmp_k, w_cmp_v, conv_w, conv_b,
           norm_att_out, norm_conv_out, w_out, norm_ffn2, ffn2_gate, ffn2_up, ffn2_down, norm_final):
    n_p, s_len, _ = x_prompt.shape
    n_seq, n_new, _ = x_sample.shape
    depth = w_ada.shape[0]
    assert n_p == 1 and depth == 1
    n_pages = page_table.shape[1]
    page = cache_slc_kv.shape[2]
    n_phys = cache_slc_kv.shape[1]
    past = n_pages * page
    wbuf = state_win_kv.shape[2]
    keep_p = min(WINDOW, s_len)
    t_s = n_seq * n_new
    l = 0

    c_all = jnp.concatenate([c_sample, c_prompt, jnp.zeros((8 - n_p, D_MODEL), F32)], axis=0)
    mod = _ada(c_all, w_ada[l], b_ada[l])
    mod_p = [mod[n_seq:n_seq + 1, k * D_MODEL:(k + 1) * D_MODEL] for k in range(N_MOD)]
    mod_q = [mod[0:n_seq, k * D_MODEL:(k + 1) * D_MODEL] for k in range(N_MOD)]
    mod_s = {k: jnp.repeat(mod_q[k], n_new, axis=0) for k in (3, 4, 5)}
    tok_major = lambda a: a.reshape(n_seq, n_new, D_MODEL).transpose(1, 0, 2).reshape(t_s, D_MODEL)
    seq_major = lambda a: a.reshape(n_new, n_seq, D_MODEL).transpose(1, 0, 2).reshape(t_s, D_MODEL)

    row = lambda v: v.reshape(1, -1)
    f1 = (ffn1_gate[l].astype(BF16), ffn1_up[l].astype(BF16), ffn1_down[l].astype(BF16))
    f2 = (ffn2_gate[l].astype(BF16), ffn2_up[l].astype(BF16), ffn2_down[l].astype(BF16))
    w_proj = _pack_w_in(w_in[l])
    w_o = w_out[l].astype(BF16)
    nfin = row(norm_final)

    xp = x_prompt.reshape(s_len, D_MODEL)
    xs = x_sample.reshape(t_s, D_MODEL)

    xp = _ffn(xp, mod_p[0], mod_p[1], mod_p[2], row(norm_ffn1[l]), nfin, *f1, final_norm=False)
    xs = seq_major(_ffn(tok_major(xs), mod_q[0], mod_q[1], mod_q[2], row(norm_ffn1[l]), nfin, *f1, final_norm=False))

    cos_p, sin_p = _rope_tables(jnp.arange(s_len))
    cos_s, sin_s = _rope_tables(past + jnp.arange(n_new))
    tm_s = 256
    cos_s, sin_s = jnp.tile(cos_s, (tm_s // n_new, 1)), jnp.tile(sin_s, (tm_s // n_new, 1))
    zero8 = jnp.zeros((8, CONV_W), F32)
    conv_args = (conv_w[l], row(conv_b[l]), row(norm_conv_out[l]))
    (_, cmp_p, slc_p, win_p, ksb_p, vst_p, kwb_p, vwt_p, _, ocn_p, utail_p, qt_p, gatet_p) = _proj(
        xp, mod_p[3], mod_p[4], row(norm_mix[l]), w_proj, cos_p, sin_p, zero8, zero8, *conv_args,
        carry=True, seq_rows=s_len, act_dtype=BF16)
    prev1 = jnp.repeat(state_conv[l][:, CONV_K - 2], n_new, axis=0)
    prev2 = jnp.repeat(state_conv[l][:, CONV_K - 3], n_new, axis=0)
    (q_s, cmp_s, slc_s, win_s, _, _, _, _, gate_s, ocn_s, u_s, _, _) = _proj(
        xs, mod_s[3], mod_s[4], row(norm_mix[l]), w_proj, cos_s, sin_s, prev1, prev2, *conv_args,
        carry=False, seq_rows=n_new, act_dtype=F32, tm=tm_s)

    n_j = CMP_STRIDE
    w_c = _pack_w_cmp(w_cmp_k[l], w_cmp_v[l], n_j)
    lin = lambda a: a.reshape(-1, HEAD_DIM)
    r_p = _compress_products(cmp_p, w_c, n_j, tm=256)
    r_cache = _compress_products(lin(cache_cmp_kv), w_c, n_j, tm=256)
    r_new = _compress_products(cmp_s, _pack_w_cmp(w_cmp_k[l], w_cmp_v[l], n_new), n_new, tm=n_seq)

    n_cmp_pad = s_len // CMP_STRIDE
    cosc, sinc = _rope_tables(jnp.arange(n_cmp_pad) * CMP_STRIDE + (CMP_BLOCK - 1))
    r_p = jnp.pad(r_p, ((0, 8), (0, 0)))
    n_blocks = s_len // SEL_BLOCK
    e_t = ((jnp.arange(s_len) // SEL_BLOCK)[:, None] == jnp.arange(n_blocks)[None, :]).astype(BF16)
    j_w, ti_w = jnp.arange(P_WKEYS)[:, None], jnp.arange(Q_BLOCK)[None, :]
    band = jnp.where((j_w > ti_w) & (j_w <= ti_w + WINDOW), 0.0, NEG).astype(F32)
    oa_p = _prompt_attention_t(qt_p, gatet_p, r_p, cosc, sinc, ksb_p, vst_p, kwb_p, vwt_p, e_t, band,
                               row(norm_att_out[l]))

    n_cmp_s = past // CMP_STRIDE
    cosc_s, sinc_s = _rope_tables(jnp.arange(n_cmp_s) * CMP_STRIDE + (CMP_BLOCK - 1))
    e_s =(jnp.arange(HEAD_DIM)[:, None] == (jnp.arange(past) // SEL_BLOCK * CHUNKS_PER_SEL)[None, :]).astype(BF16)
    oa_s, win_new_state = _sample_attention(
        page_table, r_cache, r_new.reshape(n_seq, 1, -1), q_s, gate_s,
        lin(cache_slc_kv), slc_s, lin(state_win_kv), win_s,
        cosc_s, sinc_s, e_s, row(norm_att_out[l]), n_new=n_new, page=page, wbuf=wbuf)

    xp = _outproj(xp, oa_p, ocn_p, mod_p[5], w_o)
    xs = _outproj(xs, oa_s, ocn_s, mod_s[5], w_o)
    yp = _ffn(xp, mod_p[6], mod_p[7], mod_p[8], row(norm_ffn2[l]), nfin, *f2, final_norm=True)
    ys = seq_major(_ffn(tok_major(xs), mod_q[6], mod_q[7], mod_q[8], row(norm_ffn2[l]), nfin, *f2, final_norm=True))

    kv6 = lambda a, n, s: a.reshape(1, n, s, 2, N_KV, HEAD_DIM)
    return (yp.reshape(n_p, s_len, D_MODEL), ys.reshape(n_seq, n_new, D_MODEL),
            kv6(cmp_p, n_p, s_len), kv6(slc_p, n_p, s_len), kv6(win_p[(s_len - keep_p) * KV_ROWS:], n_p, keep_p),
            utail_p[8 - (CONV_K - 1):].reshape(1, n_p, CONV_K - 1, CONV_W),
            kv6(cmp_s, n_seq, n_new), kv6(slc_s, n_seq, n_new), kv6(win_new_state, n_seq, wbuf),
            u_s.reshape(n_seq, n_new, CONV_W)[:, n_new - (CONV_K - 1):].reshape(1, n_seq, CONV_K - 1, CONV_W))
```

```python
import functools

import jax
import jax.numpy as jnp
from jax import lax
from jax.experimental import pallas as pl
from jax.experimental.pallas import tpu as pltpu

F32 = jnp.float32
BF16 = jnp.bfloat16

D_MODEL = 2048
HEAD_DIM = 128
N_HEADS = 8
N_KV = 2
HEADS_PER_KV = N_HEADS // N_KV
ATT_W = N_HEADS * HEAD_DIM
KV_W = N_KV * HEAD_DIM
CONV_W = D_MODEL - ATT_W
CONV_K = 3
CMP_BLOCK = 32
CMP_STRIDE = 16
SEL_BLOCK = 64
N_SEL = 16
WINDOW = 512
Q_BLOCK = 128
N_BRANCH = 3
N_MOD = 9
ROPE_THETA = 10000.0
EPS = 1e-6
NEG = -1e30
TINY = 1e-30
SCALE = HEAD_DIM ** -0.5
LOG2E = 1.4426950408889634
SLAB_W = 2 * KV_W
KV_ROWS = 2 * N_KV
GATE_PAD = 128
SUBLANES = 8
CHUNKS_PER_SEL = SEL_BLOCK // CMP_STRIDE

VMEM_LIMIT = 56 * 1024 * 1024


def _cparams(*sem):
    return pltpu.CompilerParams(dimension_semantics=sem, vmem_limit_bytes=VMEM_LIMIT)


def _dot(a, b):
    return jnp.dot(a, b, preferred_element_type=F32)


def _dot_nt(a, b):
    return lax.dot_general(a, b, (((1,), (1,)), ((), ())), preferred_element_type=F32)


def _rms(x):
    return x * lax.rsqrt(jnp.mean(x * x, axis=-1, keepdims=True) + EPS)


def _silu(x):
    return x * jax.nn.sigmoid(x)


def _rope(x, cos, sin_signed):
    return x * cos + pltpu.roll(x, HEAD_DIM // 2, axis=1) * sin_signed


def _masked_softmax(s, mask):
    s = jnp.where(mask, s, NEG)
    m = jnp.max(s, axis=-1, keepdims=True)
    e = jnp.where(mask, jnp.exp2(s - m), 0.0)
    return e / jnp.maximum(jnp.sum(e, axis=-1, keepdims=True), TINY)


def _mod_rows(ref, rep):
    m = ref[...]
    return m if rep == 1 else jnp.repeat(m, rep, axis=0)


def _mod_spec(mrows, t, tm, n_grid_axes):
    rep = 1 if mrows == 1 else t // mrows
    shape = (1, D_MODEL) if mrows == 1 else (tm // rep, D_MODEL)
    first = (lambda i: 0) if mrows == 1 else (lambda i: i)
    index = (lambda i: (first(i), 0)) if n_grid_axes == 1 else (lambda i, j: (first(i), 0))
    return pl.BlockSpec(shape, index), rep


def _ada_kernel(c_ref, w_ref, b_ref, o_ref):
    a = _silu(c_ref[...]).astype(BF16)
    o_ref[...] = _dot(a, w_ref[...].astype(BF16)) + b_ref[...]


def _ada(c, w, b, tn=1024):
    m, n = c.shape[0], w.shape[1]
    return pl.pallas_call(
        _ada_kernel,
        grid=(n // tn,),
        in_specs=[pl.BlockSpec((m, D_MODEL), lambda j: (0, 0)),
                  pl.BlockSpec((D_MODEL, tn), lambda j: (0, j)),
                  pl.BlockSpec((1, tn), lambda j: (0, j))],
        out_specs=pl.BlockSpec((m, tn), lambda j: (0, j)),
        out_shape=jax.ShapeDtypeStruct((m, n), F32),
        compiler_params=_cparams("arbitrary"),
        name="ada",
    )(c, w, b.reshape(1, n))


def _ffn_kernel(x_ref, sh_ref, sc_ref, gt_ref, ng_ref, nf_ref, wg_ref, wu_ref, wd_ref, o_ref, h_scr, *, n_f, final_norm,
                rep):
    j = pl.program_id(1)
    mod = lambda ref: _mod_rows(ref, rep)

    @pl.when(j == 0)
    def _():
        h = _rms(x_ref[...]) * ng_ref[...] * (1.0 + mod(sc_ref)) + mod(sh_ref)
        h_scr[...] = h.astype(BF16)
        o_ref[...] = jnp.zeros_like(o_ref)

    h = h_scr[...]
    a = (_silu(_dot(h, wg_ref[...])) * _dot(h, wu_ref[...])).astype(BF16)
    o_ref[...] += _dot(a, wd_ref[...])

    @pl.when(j == n_f - 1)
    def _():
        out = x_ref[...] + 0.5 * (1.0 + mod(gt_ref)) * o_ref[...]
        if final_norm:
            out = _rms(out) * nf_ref[...]
        o_ref[...] = out


def _ffn(x, shift, scale, gate, norm_g, norm_final, wg, wu, wd, *, final_norm, tm=512, tf=512):
    t = x.shape[0]
    d_ff = wg.shape[1]
    n_f = d_ff // tf
    mod_spec, rep = _mod_spec(shift.shape[0], t, tm, 2)
    row_spec = pl.BlockSpec((tm, D_MODEL), lambda i, j: (i, 0))
    vec_spec = pl.BlockSpec((1, D_MODEL), lambda i, j: (0, 0))
    return pl.pallas_call(
        functools.partial(_ffn_kernel, n_f=n_f, final_norm=final_norm, rep=rep),
        grid=(t // tm, n_f),
        in_specs=[row_spec, mod_spec, mod_spec, mod_spec, vec_spec, vec_spec,
                  pl.BlockSpec((D_MODEL, tf), lambda i, j: (0, j)),
                  pl.BlockSpec((D_MODEL, tf), lambda i, j: (0, j)),
                  pl.BlockSpec((tf, D_MODEL), lambda i, j: (j, 0))],
        out_specs=row_spec,
        out_shape=jax.ShapeDtypeStruct((t, D_MODEL), F32),
        scratch_shapes=[pltpu.VMEM((tm, D_MODEL), BF16)],
        compiler_params=_cparams("arbitrary", "arbitrary"),
        name="ffn",
    )(x, shift, scale, gate, norm_g, norm_final, wg, wu, wd)


PW_Q = 0
PW_KV = PW_Q + ATT_W
PW_CONV = PW_KV + 3 * SLAB_W
PW_GATE = PW_CONV + 3 * CONV_W
PW_TOTAL = PW_GATE + GATE_PAD


def _proj_kernel(x_ref, sh_ref, sc_ref, ng_ref, w_ref, cos_ref, sin_ref, p1_ref, p2_ref, cw_ref, cb_ref, nco_ref,
                 q_ref, cmp_ref, slc_ref, win_ref, ksb_ref, vst_ref, kwb_ref, vwt_ref, gate_ref, ocn_ref, u_ref, qt_ref,
                 gatet_ref, carry_scr, *, tm, seq_rows, carry, rep):
    i = pl.program_id(0)
    h = (_rms(x_ref[...]) * ng_ref[...] * (1.0 + _mod_rows(sc_ref, rep)) + _mod_rows(sh_ref, rep)).astype(BF16)
    cos, sin = cos_ref[...], sin_ref[...]

    pq = _dot(h, w_ref[:, PW_Q:PW_KV])
    for hd in range(N_HEADS):
        hs = slice(hd * HEAD_DIM, (hd + 1) * HEAD_DIM)
        blk = _rope(pq[:, hs], cos, sin) * (SCALE * LOG2E)
        q_ref[:, hs] = blk.astype(q_ref.dtype)
        qt_ref[hs, :] = blk.T.astype(BF16)

    pkv = _dot(h, w_ref[:, PW_KV:PW_CONV])
    for slab, (o_ref, kb_ref) in enumerate(((cmp_ref, None), (slc_ref, ksb_ref), (win_ref, kwb_ref))):
        base = slab * SLAB_W
        for g in range(N_KV):
            gs = slice(g * HEAD_DIM, (g + 1) * HEAD_DIM)
            k = pkv[:, base + g * HEAD_DIM:base + (g + 1) * HEAD_DIM]
            v = pkv[:, base + KV_W + g * HEAD_DIM:base + KV_W + (g + 1) * HEAD_DIM]
            if kb_ref is not None:
                k = _rope(k, cos, sin)
                kb_ref[:, gs] = k.astype(BF16)
                v_t = v.T.astype(BF16)
                if slab == 1:
                    vst_ref[0, gs, :] = v_t
                else:
                    for blk in range(tm // Q_BLOCK):
                        vwt_ref[blk, gs, :] = v_t[:, blk * Q_BLOCK:(blk + 1) * Q_BLOCK]
            o_ref[pl.ds(g, tm, stride=KV_ROWS), :] = k
            o_ref[pl.ds(N_KV + g, tm, stride=KV_ROWS), :] = v

    gate = jax.nn.sigmoid(_dot(h, w_ref[:, PW_GATE:PW_TOTAL]))
    gate_ref[...] = gate
    gatet_ref[...] = gate.T

    pc = _dot(h, w_ref[:, PW_CONV:PW_GATE])
    u = pc[:, 0:CONV_W] * pc[:, 2 * CONV_W:3 * CONV_W]
    c_out = pc[:, CONV_W:2 * CONV_W]
    row = lax.broadcasted_iota(jnp.int32, (tm, CONV_W), 0)
    if carry:
        @pl.when(i == 0)
        def _():
            carry_scr[...] = jnp.zeros_like(carry_scr)
        prev1 = carry_scr[7:8, :]
        prev2 = carry_scr[6:7, :]
        rs = row
    else:
        prev1 = p1_ref[...]
        prev2 = p2_ref[...]
        rs = row & (seq_rows - 1)
    um1 = jnp.where(rs >= 1, pltpu.roll(u, 1, axis=0), prev1)
    um2 = jnp.where(rs >= 2, pltpu.roll(u, 2, axis=0), jnp.where(rs == 1, prev1, prev2))
    y = um2 * cw_ref[0:1, :] + um1 * cw_ref[1:2, :] + u * cw_ref[2:3, :] + cb_ref[...]
    ocn_ref[...] = (_rms(c_out * y) * nco_ref[...]).astype(ocn_ref.dtype)
    if carry:
        carry_scr[...] = u[tm - 8:tm, :]
        u_ref[...] = u[tm - 8:tm, :]
    else:
        u_ref[...] = u


def _proj(x, shift, scale, norm_g, w, cos, sin, prev1, prev2, conv_w, conv_b, norm_co, *, carry, seq_rows, act_dtype,
          tm=256):
    t = x.shape[0]
    mod_spec, rep = _mod_spec(shift.shape[0], t, tm, 1)
    rows = lambda wdt: pl.BlockSpec((tm, wdt), lambda i: (i, 0))
    const = lambda r, wdt: pl.BlockSpec((r, wdt), lambda i: (0, 0))
    tab_spec = rows(HEAD_DIM) if carry else const(tm, HEAD_DIM)
    prev_spec = const(8, CONV_W) if carry else rows(CONV_W)
    u_rows = 8 if carry else t
    out_shape = [jax.ShapeDtypeStruct((t, ATT_W), act_dtype)]
    out_shape += [jax.ShapeDtypeStruct((t * KV_ROWS, HEAD_DIM), F32)] * 3
    per_ck = P_CK // tm
    kvb = jax.ShapeDtypeStruct((t, KV_W), BF16)
    out_shape += [kvb, jax.ShapeDtypeStruct((t // P_CK, KV_W, P_CK), BF16), kvb,
                  jax.ShapeDtypeStruct((t // Q_BLOCK, KV_W, Q_BLOCK), BF16)]
    out_shape += [jax.ShapeDtypeStruct((t, GATE_PAD), F32), jax.ShapeDtypeStruct((t, CONV_W), act_dtype),
                  jax.ShapeDtypeStruct((u_rows, CONV_W), F32), jax.ShapeDtypeStruct((ATT_W, t), BF16),
                  jax.ShapeDtypeStruct((GATE_PAD, t), F32)]
    lin_spec = pl.BlockSpec((tm * KV_ROWS, HEAD_DIM), lambda i: (i, 0))
    vst_spec = pl.BlockSpec((1, KV_W, tm), lambda i: (i // per_ck, 0, i % per_ck))
    vwt_spec = pl.BlockSpec((tm // Q_BLOCK, KV_W, Q_BLOCK), lambda i: (i, 0, 0))
    out_specs = [rows(ATT_W)] + [lin_spec] * 3 + [rows(KV_W), vst_spec, rows(KV_W), vwt_spec]
    out_specs += [rows(GATE_PAD), rows(CONV_W), const(8, CONV_W) if carry else rows(CONV_W),
                  pl.BlockSpec((ATT_W, tm), lambda i: (0, i)), pl.BlockSpec((GATE_PAD, tm), lambda i: (0, i))]
    return pl.pallas_call(
        functools.partial(_proj_kernel, tm=tm, seq_rows=seq_rows, carry=carry, rep=rep),
        grid=(t // tm,),
        in_specs=[rows(D_MODEL), mod_spec, mod_spec, const(1, D_MODEL),
                  pl.BlockSpec(memory_space=pltpu.VMEM),
                  tab_spec, tab_spec, prev_spec, prev_spec, const(CONV_K, CONV_W), const(1, CONV_W),
                  const(1, CONV_W)],
        out_specs=out_specs,
        out_shape=out_shape,
        scratch_shapes=[pltpu.VMEM((8, CONV_W), F32)],
        compiler_params=_cparams("arbitrary"),
        name="proj",
    )(x, shift, scale, norm_g, w, cos, sin, prev1, prev2, conv_w, conv_b, norm_co)


def _cmp_kernel(x_ref, w_ref, o_ref, pad_scr, *, n_j, tm):
    pitch = n_j * KV_ROWS
    ppitch = pitch + SUBLANES

    def repitch(c, _):
        src = pl.multiple_of(c * pitch, SUBLANES)
        dst = pl.multiple_of(c * ppitch, SUBLANES)
        pad_scr[pl.ds(dst, pitch), :] = x_ref[pl.ds(src, pitch), :]
        return 0

    lax.fori_loop(0, tm, repitch, 0, unroll=8)
    flat = pad_scr
    for kv in range(2):
        for g in range(N_KV):
            xs = jnp.concatenate([flat[pl.ds(j * KV_ROWS + kv * N_KV + g, tm, stride=ppitch), :] for j in range(n_j)],
                                 axis=1)
            col = (kv * N_KV + g) * 2 * HEAD_DIM
            o_ref[:, col:col + 2 * HEAD_DIM] = _dot(xs.astype(BF16), w_ref[kv])


def _compress_products(x, w, n_j, tm):
    pitch = n_j * KV_ROWS
    m = x.shape[0] // pitch
    n_out = 2 * N_KV * 2 * HEAD_DIM
    tm = min(tm, m)
    return pl.pallas_call(
        functools.partial(_cmp_kernel, n_j=n_j, tm=tm),
        grid=(m // tm,),
        in_specs=[pl.BlockSpec((tm * pitch, HEAD_DIM), lambda i: (i, 0)),
                  pl.BlockSpec((2, n_j * HEAD_DIM, 2 * HEAD_DIM), lambda i: (0, 0, 0))],
        out_specs=pl.BlockSpec((tm, n_out), lambda i: (i, 0)),
        out_shape=jax.ShapeDtypeStruct((m, n_out), F32),
        scratch_shapes=[pltpu.VMEM((tm * (pitch + SUBLANES), HEAD_DIM), F32)],
        compiler_params=_cparams("arbitrary"),
        name="compress",
    )(x, w)


def _r_cols(kv, g):
    return (kv * N_KV + g) * 2 * HEAD_DIM


P_CK = 512
P_WKEYS = WINDOW + Q_BLOCK


def _pattn_t_kernel(qt_ref, gatet_ref, rp_ref, cosc_ref, sinc_ref, ks_ref, vst_ref, kw_ref, vwt_ref, et_ref, band_ref,
                    nao_ref, o_ref, kc_scr, vct_scr, pt_scr, ot_scr, s_scr, *, n_cmp_pad, n_blocks):
    i = pl.program_id(0)
    s0 = i * Q_BLOCK
    nq = Q_BLOCK
    rows = HEADS_PER_KV * nq
    lanes4 = lambda a: jnp.concatenate([a] * HEADS_PER_KV, axis=1)

    @pl.when(i == 0)
    def _():
        for g in range(N_KV):
            gs = slice(g * HEAD_DIM, (g + 1) * HEAD_DIM)
            ck, cv = _r_cols(0, g), _r_cols(1, g)
            kc = rp_ref[0:n_cmp_pad, ck:ck + HEAD_DIM] + rp_ref[pl.ds(1, n_cmp_pad), ck + HEAD_DIM:ck + 2 * HEAD_DIM]
            kc_scr[:, gs] = _rope(kc, cosc_ref[...], sinc_ref[...]).astype(BF16)
            vc = rp_ref[0:n_cmp_pad, cv:cv + HEAD_DIM] + rp_ref[pl.ds(1, n_cmp_pad), cv + HEAD_DIM:cv + 2 * HEAD_DIM]
            vct_scr[gs, :] = vc.T.astype(BF16)
        pt_scr[...] = jnp.zeros_like(pt_scr)

    t_q = s0 + lax.broadcasted_iota(jnp.int32, (n_cmp_pad, nq), 1)
    c_end = lax.broadcasted_iota(jnp.int32, (n_cmp_pad, nq), 0) * CMP_STRIDE + (CMP_BLOCK - 1)
    bias_c = lanes4(jnp.where(c_end <= t_q, 0.0, NEG))
    j_io = lax.broadcasted_iota(jnp.int32, (P_WKEYS, nq), 0)
    bias_w = lanes4(jnp.where(j_io >= WINDOW - s0, band_ref[...], NEG))

    blocks = [jnp.maximum(i + k - WINDOW // nq, 0) for k in range(P_WKEYS // nq)]
    scores, rhs, s_cmp, s_win = [], [], [], []
    for g in range(N_KV):
        gs = slice(g * HEAD_DIM, (g + 1) * HEAD_DIM)
        q_t = jnp.concatenate([qt_ref[(g * HEADS_PER_KV + r) * HEAD_DIM:(g * HEADS_PER_KV + r + 1) * HEAD_DIM, :]
                               for r in range(HEADS_PER_KV)], axis=1)
        rhs.append(q_t)
        s_cmp.append(_dot(kc_scr[:, gs], q_t))
        k_w = jnp.concatenate([kw_ref[pl.ds(pl.multiple_of(b * nq, nq), nq), gs] for b in blocks], axis=0)
        s_win.append(_dot(k_w, q_t))

    for g in range(N_KV):
        gs = slice(g * HEAD_DIM, (g + 1) * HEAD_DIM)
        s = s_cmp[g] + bias_c
        m = jnp.max(s, axis=0, keepdims=True)
        e = jnp.exp2(s - m)
        inv = jnp.where(m > 0.5 * NEG, 1.0 / jnp.maximum(jnp.sum(e, axis=0, keepdims=True), TINY), 0.0)
        o_c = _dot(vct_scr[gs, :], e.astype(BF16)) * inv

        p = e * inv
        pt_scr[8:8 + n_cmp_pad, :] = sum(p[:, r * nq:(r + 1) * nq] for r in range(HEADS_PER_KV))
        st = lambda k: pt_scr[pl.ds(7 + k, n_blocks, stride=CHUNKS_PER_SEL), :]
        score = 0.5 * st(0) + st(1) + st(2) + st(3) + 0.5 * st(4)
        b_io = lax.broadcasted_iota(jnp.int32, (n_blocks, nq), 0)
        t_lane = s0 + lax.broadcasted_iota(jnp.int32, (n_blocks, nq), 1)
        cur = t_lane >> 6
        forced = (b_io == 0) | (b_io == cur) | (b_io == cur - 1)
        valid = b_io * SEL_BLOCK <= t_lane
        scores.append(jnp.where(forced, jnp.inf, jnp.where(valid, score, -jnp.inf)))

        v_wt = jnp.concatenate([vwt_ref[b, gs, :] for b in blocks], axis=1)
        s = s_win[g] + bias_w
        e = jnp.exp2(s - jnp.max(s, axis=0, keepdims=True))
        o_w = _dot(v_wt, e.astype(BF16)) / jnp.maximum(jnp.sum(e, axis=0, keepdims=True), TINY)

        for r in range(HEADS_PER_KV):
            hd = g * HEADS_PER_KV + r
            ls = slice(r * nq, (r + 1) * nq)
            gc = gatet_ref[hd * N_BRANCH + 0:hd * N_BRANCH + 1, :]
            gw = gatet_ref[hd * N_BRANCH + 2:hd * N_BRANCH + 3, :]
            ot_scr[hd * HEAD_DIM:(hd + 1) * HEAD_DIM, :] = gc * o_c[:, ls] + gw * o_w[:, ls]

    b_f = lax.broadcasted_iota(jnp.int32, (n_blocks, nq), 0).astype(F32)

    def pick(_, c):
        out = []
        for work, sel in c:
            m = jnp.max(work, axis=0, keepdims=True)
            idx = jnp.min(jnp.where(work == m, b_f, float(n_blocks)), axis=0, keepdims=True)
            hit = b_f == idx
            out.append((jnp.where(hit, -jnp.inf, work), jnp.where(hit, 1.0, sel)))
        return tuple(out)

    picked = lax.fori_loop(0, min(N_SEL, n_blocks), pick,
                           tuple((sc, jnp.zeros((n_blocks, nq), F32)) for sc in scores))

    for g in range(N_KV):
        sel_bias = jnp.where(picked[g][1] > 0.5, 0.0, NEG).astype(BF16)
        rhs[g] = jnp.concatenate([rhs[g], lanes4(sel_bias)], axis=0)

    def qk_scores(c, slot):
        k0 = pl.multiple_of(c * P_CK, P_CK)
        blk_hot = et_ref[pl.ds(k0, P_CK), :]
        for g in range(N_KV):
            gs = slice(g * HEAD_DIM, (g + 1) * HEAD_DIM)
            s_scr[slot, g] = _dot(jnp.concatenate([ks_ref[pl.ds(k0, P_CK), gs], blk_hot], axis=1), rhs[g])

    def softmax_pv(c, slot, carry, causal):
        if causal:
            key = c * P_CK + lax.broadcasted_iota(jnp.int32, (P_CK, rows), 0)
            t_k = s0 + (lax.broadcasted_iota(jnp.int32, (P_CK, rows), 1) & (nq - 1))
            cb = jnp.where(key <= t_k, 0.0, NEG)
        out = []
        for g in range(N_KV):
            gs = slice(g * HEAD_DIM, (g + 1) * HEAD_DIM)
            m_i, l_i, acc = carry[g]
            s = s_scr[slot, g]
            if causal:
                s = s + cb
            m_n = jnp.maximum(m_i, jnp.max(s, axis=0, keepdims=True))
            p = jnp.exp2(s - m_n)
            alpha = jnp.exp2(m_i - m_n)
            l_n = alpha * l_i + jnp.sum(p, axis=0, keepdims=True)
            out.append((m_n, l_n, alpha * acc + _dot(vst_ref[c, gs, :], p.astype(BF16))))
        return tuple(out)

    def pair(pr, carry):
        c = 2 * pr
        qk_scores(c + 1, 1)
        carry = softmax_pv(c, 0, carry, False)
        qk_scores(c + 2, 0)
        return softmax_pv(c + 1, 1, carry, False)

    def odd_tail(c, carry):
        qk_scores(c + 1, 1)
        return softmax_pv(c, 0, carry, False)

    n_chunks = (s0 + nq + P_CK - 1) // P_CK
    n_pairs = (n_chunks - 1) // 2
    init = tuple((jnp.full((1, rows), NEG, F32), jnp.zeros((1, rows), F32), jnp.zeros((HEAD_DIM, rows), F32))
                 for _ in range(N_KV))
    qk_scores(0, 0)
    carry = lax.fori_loop(0, n_pairs, pair, init)
    last = n_chunks - 1
    carry = lax.cond(last > 2 * n_pairs, lambda cr: odd_tail(last - 1, cr), lambda cr: cr, carry)
    carry = lax.cond(last > 2 * n_pairs, lambda cr: softmax_pv(last, 1, cr, True),
                     lambda cr: softmax_pv(last, 0, cr, True), carry)
    for g in range(N_KV):
        _, l_s, acc = carry[g]
        o_st = acc / jnp.maximum(l_s, TINY)
        for r in range(HEADS_PER_KV):
            hd = g * HEADS_PER_KV + r
            gsl = gatet_ref[hd * N_BRANCH + 1:hd * N_BRANCH + 2, :]
            ot_scr[hd * HEAD_DIM:(hd + 1) * HEAD_DIM, :] += gsl * o_st[:, r * nq:(r + 1) * nq]

    o_t = ot_scr[...]
    o_t = o_t * lax.rsqrt(jnp.mean(o_t * o_t, axis=0, keepdims=True) + EPS)
    o_ref[...] = (o_t.T * nao_ref[...]).astype(o_ref.dtype)


def _prompt_attention_t(q_t, gates_t, rp, cosc, sinc, ks, vs_t, kw, vw_t, e_t, band, norm_ao):
    s_len = q_t.shape[1]
    n_cmp_pad = s_len // CMP_STRIDE
    n_blocks = s_len // SEL_BLOCK
    vmem = pl.BlockSpec(memory_space=pltpu.VMEM)
    cols = lambda h: pl.BlockSpec((h, Q_BLOCK), lambda i: (0, i))
    return pl.pallas_call(
        functools.partial(_pattn_t_kernel, n_cmp_pad=n_cmp_pad, n_blocks=n_blocks),
        grid=(s_len // Q_BLOCK,),
        in_specs=[cols(ATT_W), cols(GATE_PAD), vmem, vmem, vmem, vmem, vmem, vmem, vmem, vmem, vmem,
                  pl.BlockSpec((1, ATT_W), lambda i: (0, 0))],
        out_specs=pl.BlockSpec((Q_BLOCK, ATT_W), lambda i: (i, 0)),
        out_shape=jax.ShapeDtypeStruct((s_len, ATT_W), BF16),
        scratch_shapes=[pltpu.VMEM((n_cmp_pad, KV_W), BF16), pltpu.VMEM((KV_W, n_cmp_pad), BF16),
                        pltpu.VMEM((n_cmp_pad + 16, Q_BLOCK), F32), pltpu.VMEM((ATT_W, Q_BLOCK), F32),
                        pltpu.VMEM((2, N_KV, P_CK, HEADS_PER_KV * Q_BLOCK), F32)],
        compiler_params=_cparams("arbitrary"),
        name="prompt_attn",
    )(q_t, gates_t, rp, cosc, sinc, ks, vs_t, kw, vw_t, e_t, band, norm_ao)


def _sattn_kernel(pt_ref, *refs, n_seqs, n_pages, page, n_new, past, wbuf):
    del pt_ref
    n_pg = n_seqs * n_pages
    r_pages, s_pages = refs[0:n_pg], refs[n_pg:2 * n_pg]
    (rnew_ref, q_ref, gate_ref, snew_ref, wst_ref, wnew_ref, cosc_ref, sinc_ref, e_ref, nao_ref,
     o_ref, wout_ref, r_scr, k_scr, v_scr, kw_scr, vw_scr, o_scr) = refs[2 * n_pg:]
    chunks_pp = page // CMP_STRIDE
    n_cmp = n_pages * chunks_pp
    rows = HEADS_PER_KV * n_new
    n_keys = k_scr.shape[1]
    n_wkeys = kw_scr.shape[1]
    pad = n_keys - past
    wpad = n_wkeys - wbuf
    batch = [(sq, g) for sq in range(n_seqs) for g in range(N_KV)]
    cat = lambda parts: jnp.concatenate(parts, axis=0)

    def with_zero_rows(new_rows, n_zero):
        return cat([new_rows, jnp.zeros((n_zero, HEAD_DIM), F32)]).astype(BF16)

    for sq in range(n_seqs):
        n0 = sq * n_new * KV_ROWS
        w0 = sq * wbuf * KV_ROWS
        for p in range(n_pages):
            r_scr[sq, p * chunks_pp:(p + 1) * chunks_pp, :] = r_pages[sq * n_pages + p][...]
            pg = s_pages[sq * n_pages + p]
            for g in range(N_KV):
                b = sq * N_KV + g
                k_scr[b, p * page:(p + 1) * page, :] = pg[pl.ds(g, page, stride=KV_ROWS), :].astype(BF16)
                v_scr[b, p * page:(p + 1) * page, :] = pg[pl.ds(N_KV + g, page, stride=KV_ROWS), :].astype(BF16)
        r_scr[sq, n_cmp:n_cmp + 8, :] = cat([rnew_ref[sq], jnp.zeros((7, r_scr.shape[2]), F32)])
        for g in range(N_KV):
            b = sq * N_KV + g
            k_scr[b, past:n_keys, :] = with_zero_rows(snew_ref[pl.ds(n0 + g, n_new, stride=KV_ROWS), :], pad - n_new)
            v_scr[b, past:n_keys, :] = with_zero_rows(snew_ref[pl.ds(n0 + N_KV + g, n_new, stride=KV_ROWS), :],
                                                      pad - n_new)
            kw_scr[b, 0:wbuf, :] = wst_ref[pl.ds(w0 + g, wbuf, stride=KV_ROWS), :].astype(BF16)
            vw_scr[b, 0:wbuf, :] = wst_ref[pl.ds(w0 + N_KV + g, wbuf, stride=KV_ROWS), :].astype(BF16)
            kw_scr[b, wbuf:n_wkeys, :] = with_zero_rows(wnew_ref[pl.ds(n0 + g, n_new, stride=KV_ROWS), :],
                                                        wpad - n_new)
            vw_scr[b, wbuf:n_wkeys, :] = with_zero_rows(wnew_ref[pl.ds(n0 + N_KV + g, n_new, stride=KV_ROWS), :],
                                                        wpad - n_new)
        keep = (wbuf - n_new) * KV_ROWS
        wout_ref[w0:w0 + keep, :] = wst_ref[w0 + n_new * KV_ROWS:w0 + wbuf * KV_ROWS, :]
        wout_ref[w0 + keep:w0 + wbuf * KV_ROWS, :] = wnew_ref[n0:n0 + n_new * KV_ROWS, :]

    n_b = len(batch)
    all_rows = n_b * rows
    tok = lax.broadcasted_iota(jnp.int32, (all_rows, 1), 0) & (n_new - 1)
    t_rows = past + tok

    qg, kc, vc = [], [], []
    for sq, g in batch:
        qs = slice(sq * n_new, (sq + 1) * n_new)
        qg.append(cat([q_ref[qs, (g * HEADS_PER_KV + r) * HEAD_DIM:(g * HEADS_PER_KV + r + 1) * HEAD_DIM]
                       for r in range(HEADS_PER_KV)]).astype(BF16))
        ck, cv = _r_cols(0, g), _r_cols(1, g)
        kc.append(r_scr[sq, 0:n_cmp, ck:ck + HEAD_DIM] + r_scr[sq, pl.ds(1, n_cmp), ck + HEAD_DIM:ck + 2 * HEAD_DIM])
        vc.append((r_scr[sq, 0:n_cmp, cv:cv + HEAD_DIM]
                   + r_scr[sq, pl.ds(1, n_cmp), cv + HEAD_DIM:cv + 2 * HEAD_DIM]).astype(BF16))
    cos_all, sin_all = cat([cosc_ref[...]] * n_b), cat([sinc_ref[...]] * n_b)
    kc_all = _rope(cat(kc), cos_all, sin_all).astype(BF16)
    rb = lambda b: slice(b * rows, (b + 1) * rows)

    s_c = cat([_dot_nt(qg[b], kc_all[b * n_cmp:(b + 1) * n_cmp]) for b in range(n_b)])
    s_w = cat([_dot_nt(qg[b], kw_scr[b]) for b in range(n_b)])
    s_s = cat([_dot_nt(qg[b], k_scr[b]) for b in range(n_b)])

    c_end = lax.broadcasted_iota(jnp.int32, (all_rows, n_cmp), 1) * CMP_STRIDE + (CMP_BLOCK - 1)
    p_c = _masked_softmax(s_c, c_end <= t_rows)
    p_cb = p_c.astype(BF16)
    o_c = [_dot(p_cb[rb(b)], vc[b]) for b in range(n_b)]

    sel_rows = n_b * n_new
    p_grp = cat([sum(p_c[b * rows + r * n_new:b * rows + (r + 1) * n_new] for r in range(HEADS_PER_KV))
                 for b in range(n_b)])
    lane = lax.broadcasted_iota(jnp.int32, (sel_rows, HEAD_DIM), 1)
    pch = 0.5 * (p_grp + jnp.where(lane >= 1, pltpu.roll(p_grp, 1, axis=1), 0.0))
    score = pch
    for k in range(1, CHUNKS_PER_SEL):
        score = score + pltpu.roll(pch, HEAD_DIM - k, axis=1)
    blk = lane >> 2
    t_tok = past + (lax.broadcasted_iota(jnp.int32, (sel_rows, HEAD_DIM), 0) & (n_new - 1))
    cur = t_tok >> 6
    forced = (blk == 0) | (blk == cur) | (blk == cur - 1)
    score = jnp.where(forced, jnp.inf, score)
    n_pb = past // SEL_BLOCK
    ahead = jnp.zeros((sel_rows, HEAD_DIM), F32)
    for k in range(1, n_pb):
        other = pltpu.roll(score, CHUNKS_PER_SEL * k, axis=1)
        wins = (other > score) | ((other == score) & (blk >= k))
        ahead = ahead + jnp.where(wins, 1.0, 0.0)
    sel = jnp.where(((lane & (CHUNKS_PER_SEL - 1)) == 0) & (ahead < N_SEL - 1), 1.0, 0.0).astype(BF16)
    sel_keys = _dot(sel, e_ref[...])
    new_ok = (lax.broadcasted_iota(jnp.int32, (sel_rows, pad), 1)
              <= (lax.broadcasted_iota(jnp.int32, (sel_rows, pad), 0) & (n_new - 1)))
    bias = jnp.concatenate([jnp.where(sel_keys > 0.5, 0.0, NEG), jnp.where(new_ok, 0.0, NEG)], axis=1)
    bias = cat([bias[b * n_new:(b + 1) * n_new] for b in range(n_b) for _ in range(HEADS_PER_KV)])

    s_s = s_s + bias
    e_s = jnp.exp2(s_s - jnp.max(s_s, axis=-1, keepdims=True))
    p_s = (e_s / jnp.maximum(jnp.sum(e_s, axis=-1, keepdims=True), TINY)).astype(BF16)
    o_s = [_dot(p_s[rb(b)], v_scr[b]) for b in range(n_b)]

    j_io = lax.broadcasted_iota(jnp.int32, (all_rows, n_wkeys), 1)
    m_w = (((j_io < wbuf) & (j_io > tok + (wbuf - WINDOW)) & (j_io >= wbuf - past))
           | ((j_io >= wbuf) & (j_io - wbuf <= tok)))
    p_w = _masked_softmax(s_w, m_w).astype(BF16)
    o_w = [_dot(p_w[rb(b)], vw_scr[b]) for b in range(n_b)]

    for b, (sq, g) in enumerate(batch):
        qs = slice(sq * n_new, (sq + 1) * n_new)
        for r in range(HEADS_PER_KV):
            hd = g * HEADS_PER_KV + r
            rs = slice(r * n_new, (r + 1) * n_new)
            gc = gate_ref[qs, hd * N_BRANCH + 0:hd * N_BRANCH + 1]
            gsl = gate_ref[qs, hd * N_BRANCH + 1:hd * N_BRANCH + 2]
            gw = gate_ref[qs, hd * N_BRANCH + 2:hd * N_BRANCH + 3]
            o_scr[qs, hd * HEAD_DIM:(hd + 1) * HEAD_DIM] = gc * o_c[b][rs] + gsl * o_s[b][rs] + gw * o_w[b][rs]

    o_ref[...] = _rms(o_scr[...]) * nao_ref[...]


S_SEQS = 2


def _sample_attention(page_table, r_all, r_new, q, gates, slc_cache, slc_new, win_state, win_new, cosc, sinc, e_mat,
                      norm_ao, *, n_new, page, wbuf):
    n_seq, n_pages = page_table.shape
    past = n_pages * page
    chunks_pp = page // CMP_STRIDE
    n_cmp = n_pages * chunks_pp
    assert n_cmp == HEAD_DIM and n_new == 8 and past % SEL_BLOCK == 0 and n_new <= SEL_BLOCK
    assert (past + n_new - 1) // SEL_BLOCK == past // SEL_BLOCK and wbuf == WINDOW and past >= WINDOW
    n_keys = past + HEAD_DIM
    n_wkeys = wbuf + HEAD_DIM
    r_w = r_all.shape[1]

    ns = S_SEQS
    assert n_seq % ns == 0
    page_map = lambda sq, p: (lambda b, pt: (pt[b * ns + sq, p], 0))
    in_specs = [pl.BlockSpec((chunks_pp, r_w), page_map(sq, p)) for sq in range(ns) for p in range(n_pages)]
    in_specs += [pl.BlockSpec((page * KV_ROWS, HEAD_DIM), page_map(sq, p)) for sq in range(ns) for p in range(n_pages)]
    seq_rows = lambda wdt: pl.BlockSpec((ns * n_new, wdt), lambda b, pt: (b, 0))
    kv_rows = lambda n_tok: pl.BlockSpec((ns * n_tok * KV_ROWS, HEAD_DIM), lambda b, pt: (b, 0))
    const = lambda shape: pl.BlockSpec(shape, lambda b, pt: (0,) * len(shape))
    in_specs += [pl.BlockSpec((ns, 1, r_w), lambda b, pt: (b, 0, 0)), seq_rows(ATT_W), seq_rows(GATE_PAD),
                 kv_rows(n_new), kv_rows(wbuf), kv_rows(n_new),
                 const((n_cmp, HEAD_DIM)), const((n_cmp, HEAD_DIM)), const((HEAD_DIM, past)), const((1, ATT_W))]
    grid_spec = pltpu.PrefetchScalarGridSpec(
        num_scalar_prefetch=1,
        grid=(n_seq // ns,),
        in_specs=in_specs,
        out_specs=[seq_rows(ATT_W), kv_rows(wbuf)],
        scratch_shapes=[pltpu.VMEM((ns, n_cmp + 8, r_w), F32), pltpu.VMEM((ns * N_KV, n_keys, HEAD_DIM), BF16),
                        pltpu.VMEM((ns * N_KV, n_keys, HEAD_DIM), BF16), pltpu.VMEM((ns * N_KV, n_wkeys, HEAD_DIM), BF16),
                        pltpu.VMEM((ns * N_KV, n_wkeys, HEAD_DIM), BF16), pltpu.VMEM((ns * n_new, ATT_W), F32)],
    )
    return pl.pallas_call(
        functools.partial(_sattn_kernel, n_seqs=ns, n_pages=n_pages, page=page, n_new=n_new, past=past, wbuf=wbuf),
        grid_spec=grid_spec,
        out_shape=[jax.ShapeDtypeStruct((n_seq * n_new, ATT_W), F32),
                   jax.ShapeDtypeStruct((n_seq * wbuf * KV_ROWS, HEAD_DIM), F32)],
        compiler_params=_cparams("arbitrary"),
        name="sample_attn",
    )(page_table, *([r_all] * (ns * n_pages)), *([slc_cache] * (ns * n_pages)), r_new, q, gates, slc_new, win_state,
      win_new, cosc, sinc, e_mat, norm_ao)


def _outproj_kernel(x_ref, oa_ref, oc_ref, gt_ref, w_ref, o_ref, *, rep):
    y = _dot(oa_ref[...].astype(BF16), w_ref[0:ATT_W, :]) + _dot(oc_ref[...].astype(BF16), w_ref[ATT_W:D_MODEL, :])
    o_ref[...] = x_ref[...] + (1.0 + _mod_rows(gt_ref, rep)) * y


def _outproj(x, oa, oc, gate, w, tm=512):
    t = x.shape[0]
    mod_spec, rep = _mod_spec(gate.shape[0], t, tm, 1)
    rows = lambda wdt: pl.BlockSpec((tm, wdt), lambda i: (i, 0))
    return pl.pallas_call(
        functools.partial(_outproj_kernel, rep=rep),
        grid=(t // tm,),
        in_specs=[rows(D_MODEL), rows(ATT_W), rows(CONV_W), mod_spec, pl.BlockSpec(memory_space=pltpu.VMEM)],
        out_specs=rows(D_MODEL),
        out_shape=jax.ShapeDtypeStruct((t, D_MODEL), F32),
        compiler_params=_cparams("arbitrary"),
        name="outproj",
    )(x, oa, oc, gate, w)


def _rope_tables(pos):
    half = HEAD_DIM // 2
    inv = ROPE_THETA ** (-jnp.arange(half, dtype=F32) * 2.0 / HEAD_DIM)
    ang = pos.astype(F32)[:, None] * inv[None, :]
    cos, sin = jnp.cos(ang), jnp.sin(ang)
    return jnp.concatenate([cos, cos], axis=1), jnp.concatenate([-sin, sin], axis=1)


def _pack_w_in(w_in):
    off_kv = ATT_W
    off_g = off_kv + 3 * SLAB_W
    off_c = off_g + N_HEADS * N_BRANCH
    gate_cols = jnp.pad(w_in[:, off_g:off_c], ((0, 0), (0, GATE_PAD - N_HEADS * N_BRANCH)))
    return jnp.concatenate([w_in[:, 0:off_g], w_in[:, off_c:off_c + 3 * CONV_W], gate_cols], axis=1).astype(BF16)


def _pack_w_cmp(w_ck, w_cv, n_j):
    def one(w):
        lo = w[0:n_j].reshape(n_j * HEAD_DIM, HEAD_DIM)
        hi = w[CMP_STRIDE:CMP_STRIDE + n_j].reshape(n_j * HEAD_DIM, HEAD_DIM)
        return jnp.concatenate([lo, hi], axis=1)
    return jnp.stack([one(w_ck), one(w_cv)]).astype(BF16)


def kernel(x_prompt, x_sample, c_prompt, c_sample, cache_cmp_kv, cache_slc_kv, state_win_kv, state_conv, page_table,
           w_ada, b_ada, norm_ffn1, ffn1_gate, ffn1_up, ffn1_down, norm_mix, w_in, w_cmp_k, w_cmp_v, conv_w, conv_b,
           norm_att_out, norm_conv_out, w_out, norm_ffn2, ffn2_gate, ffn2_up, ffn2_down, norm_final):
    n_p, s_len, _ = x_prompt.shape
    n_seq, n_new, _ = x_sample.shape
    depth = w_ada.shape[0]
    assert n_p == 1 and depth == 1
    n_pages = page_table.shape[1]
    page = cache_slc_kv.shape[2]
    n_phys = cache_slc_kv.shape[1]
    past = n_pages * page
    wbuf = state_win_kv.shape[2]
    keep_p = min(WINDOW, s_len)
    t_s = n_seq * n_new
    l = 0

    c_all = jnp.concatenate([c_sample, c_prompt, jnp.zeros((8 - n_p, D_MODEL), F32)], axis=0)
    mod = _ada(c_all, w_ada[l], b_ada[l])
    mod_p = [mod[n_seq:n_seq + 1, k * D_MODEL:(k + 1) * D_MODEL] for k in range(N_MOD)]
    mod_s = [mod[0:n_seq, k * D_MODEL:(k + 1) * D_MODEL] for k in range(N_MOD)]

    row = lambda v: v.reshape(1, -1)
    f1 = (ffn1_gate[l].astype(BF16), ffn1_up[l].astype(BF16), ffn1_down[l].astype(BF16))
    f2 = (ffn2_gate[l].astype(BF16), ffn2_up[l].astype(BF16), ffn2_down[l].astype(BF16))
    w_proj = _pack_w_in(w_in[l])
    w_o = w_out[l].astype(BF16)
    nfin = row(norm_final)

    xp = x_prompt.reshape(s_len, D_MODEL)
    xs = x_sample.reshape(t_s, D_MODEL)

    xp = _ffn(xp, mod_p[0], mod_p[1], mod_p[2], row(norm_ffn1[l]), nfin, *f1, final_norm=False)
    xs = _ffn(xs, mod_s[0], mod_s[1], mod_s[2], row(norm_ffn1[l]), nfin, *f1, final_norm=False)

    cos_p, sin_p = _rope_tables(jnp.arange(s_len))
    cos_s, sin_s = _rope_tables(past + jnp.arange(n_new))
    tm_s = 256
    cos_s, sin_s = jnp.tile(cos_s, (tm_s // n_new, 1)), jnp.tile(sin_s, (tm_s // n_new, 1))
    zero8 = jnp.zeros((8, CONV_W), F32)
    conv_args = (conv_w[l], row(conv_b[l]), row(norm_conv_out[l]))
    (_, cmp_p, slc_p, win_p, ksb_p, vst_p, kwb_p, vwt_p, _, ocn_p, utail_p, qt_p, gatet_p) = _proj(
        xp, mod_p[3], mod_p[4], row(norm_mix[l]), w_proj, cos_p, sin_p, zero8, zero8, *conv_args,
        carry=True, seq_rows=s_len, act_dtype=BF16)
    prev1 = jnp.repeat(state_conv[l][:, CONV_K - 2], n_new, axis=0)
    prev2 = jnp.repeat(state_conv[l][:, CONV_K - 3], n_new, axis=0)
    (q_s, cmp_s, slc_s, win_s, _, _, _, _, gate_s, ocn_s, u_s, _, _) = _proj(
        xs, mod_s[3], mod_s[4], row(norm_mix[l]), w_proj, cos_s, sin_s, prev1, prev2, *conv_args,
        carry=False, seq_rows=n_new, act_dtype=F32, tm=tm_s)

    n_j = CMP_STRIDE
    w_c = _pack_w_cmp(w_cmp_k[l], w_cmp_v[l], n_j)
    lin = lambda a: a.reshape(-1, HEAD_DIM)
    r_p = _compress_products(cmp_p, w_c, n_j, tm=256)
    r_cache = _compress_products(lin(cache_cmp_kv), w_c, n_j, tm=256)
    r_new = _compress_products(cmp_s, _pack_w_cmp(w_cmp_k[l], w_cmp_v[l], n_new), n_new, tm=n_seq)

    n_cmp_pad = s_len // CMP_STRIDE
    cosc, sinc = _rope_tables(jnp.arange(n_cmp_pad) * CMP_STRIDE + (CMP_BLOCK - 1))
    r_p = jnp.pad(r_p, ((0, 8), (0, 0)))
    n_blocks = s_len // SEL_BLOCK
    e_t = ((jnp.arange(s_len) // SEL_BLOCK)[:, None] == jnp.arange(n_blocks)[None, :]).astype(BF16)
    j_w, ti_w = jnp.arange(P_WKEYS)[:, None], jnp.arange(Q_BLOCK)[None, :]
    band = jnp.where((j_w > ti_w) & (j_w <= ti_w + WINDOW), 0.0, NEG).astype(F32)
    oa_p = _prompt_attention_t(qt_p, gatet_p, r_p, cosc, sinc, ksb_p, vst_p, kwb_p, vwt_p, e_t, band,
                               row(norm_att_out[l]))

    n_cmp_s = past // CMP_STRIDE
    cosc_s, sinc_s = _rope_tables(jnp.arange(n_cmp_s) * CMP_STRIDE + (CMP_BLOCK - 1))
    e_s =(jnp.arange(HEAD_DIM)[:, None] == (jnp.arange(past) // SEL_BLOCK * CHUNKS_PER_SEL)[None, :]).astype(BF16)
    oa_s, win_new_state = _sample_attention(
        page_table, r_cache, r_new.reshape(n_seq, 1, -1), q_s, gate_s,
        lin(cache_slc_kv), slc_s, lin(state_win_kv), win_s,
        cosc_s, sinc_s, e_s, row(norm_att_out[l]), n_new=n_new, page=page, wbuf=wbuf)

    xp = _outproj(xp, oa_p, ocn_p, mod_p[5], w_o)
    xs = _outproj(xs, oa_s, ocn_s, mod_s[5], w_o)
    yp = _ffn(xp, mod_p[6], mod_p[7], mod_p[8], row(norm_ffn2[l]), nfin, *f2, final_norm=True)
    ys = _ffn(xs, mod_s[6], mod_s[7], mod_s[8], row(norm_ffn2[l]), nfin, *f2, final_norm=True)

    kv6 = lambda a, n, s: a.reshape(1, n, s, 2, N_KV, HEAD_DIM)
    return (yp.reshape(n_p, s_len, D_MODEL), ys.reshape(n_seq, n_new, D_MODEL),
            kv6(cmp_p, n_p, s_len), kv6(slc_p, n_p, s_len), kv6(win_p[(s_len - keep_p) * KV_ROWS:], n_p, keep_p),
            utail_p[8 - (CONV_K - 1):].reshape(1, n_p, CONV_K - 1, CONV_W),
            kv6(cmp_s, n_seq, n_new), kv6(slc_s, n_seq, n_new), kv6(win_new_state, n_seq, wbuf),
            u_s.reshape(n_seq, n_new, CONV_W)[:, n_new - (CONV_K - 1):].reshape(1, n_seq, CONV_K - 1, CONV_W))
```

```python
import functools

import jax
import jax.numpy as jnp
from jax import lax
from jax.experimental import pallas as pl
from jax.experimental.pallas import tpu as pltpu

F32 = jnp.float32
BF16 = jnp.bfloat16

D_MODEL = 2048
HEAD_DIM = 128
N_HEADS = 8
N_KV = 2
HEADS_PER_KV = N_HEADS // N_KV
ATT_W = N_HEADS * HEAD_DIM
KV_W = N_KV * HEAD_DIM
CONV_W = D_MODEL - ATT_W
CONV_K = 3
CMP_BLOCK = 32
CMP_STRIDE = 16
SEL_BLOCK = 64
N_SEL = 16
WINDOW = 512
Q_BLOCK = 128
N_BRANCH = 3
N_MOD = 9
ROPE_THETA = 10000.0
EPS = 1e-6
NEG = -1e30
TINY = 1e-30
SCALE = HEAD_DIM ** -0.5
LOG2E = 1.4426950408889634
SLAB_W = 2 * KV_W
KV_ROWS = 2 * N_KV
GATE_PAD = 128
SUBLANES = 8
CHUNKS_PER_SEL = SEL_BLOCK // CMP_STRIDE

VMEM_LIMIT = 56 * 1024 * 1024


def _cparams(*sem):
    return pltpu.CompilerParams(dimension_semantics=sem, vmem_limit_bytes=VMEM_LIMIT)


def _dot(a, b):
    return jnp.dot(a, b, preferred_element_type=F32)


def _dot_nt(a, b):
    return lax.dot_general(a, b, (((1,), (1,)), ((), ())), preferred_element_type=F32)


def _rms(x):
    return x * lax.rsqrt(jnp.mean(x * x, axis=-1, keepdims=True) + EPS)


def _silu(x):
    return x * jax.nn.sigmoid(x)


def _rope(x, cos, sin_signed):
    return x * cos + pltpu.roll(x, HEAD_DIM // 2, axis=1) * sin_signed


def _masked_softmax(s, mask):
    s = jnp.where(mask, s, NEG)
    m = jnp.max(s, axis=-1, keepdims=True)
    e = jnp.where(mask, jnp.exp2(s - m), 0.0)
    return e / jnp.maximum(jnp.sum(e, axis=-1, keepdims=True), TINY)


def _mod_rows(ref, rep):
    m = ref[...]
    return m if rep == 1 else jnp.repeat(m, rep, axis=0)


def _mod_spec(mrows, t, tm, n_grid_axes):
    rep = 1 if mrows == 1 else t // mrows
    shape = (1, D_MODEL) if mrows == 1 else (tm // rep, D_MODEL)
    first = (lambda i: 0) if mrows == 1 else (lambda i: i)
    index = (lambda i: (first(i), 0)) if n_grid_axes == 1 else (lambda i, j: (first(i), 0))
    return pl.BlockSpec(shape, index), rep


def _ada_kernel(c_ref, w_ref, b_ref, o_ref):
    a = _silu(c_ref[...]).astype(BF16)
    o_ref[...] = _dot(a, w_ref[...].astype(BF16)) + b_ref[...]


def _ada(c, w, b, tn=1024):
    m, n = c.shape[0], w.shape[1]
    return pl.pallas_call(
        _ada_kernel,
        grid=(n // tn,),
        in_specs=[pl.BlockSpec((m, D_MODEL), lambda j: (0, 0)),
                  pl.BlockSpec((D_MODEL, tn), lambda j: (0, j)),
                  pl.BlockSpec((1, tn), lambda j: (0, j))],
        out_specs=pl.BlockSpec((m, tn), lambda j: (0, j)),
        out_shape=jax.ShapeDtypeStruct((m, n), F32),
        compiler_params=_cparams("arbitrary"),
        name="ada",
    )(c, w, b.reshape(1, n))


def _ffn_kernel(x_ref, sh_ref, sc_ref, gt_ref, ng_ref, nf_ref, wg_ref, wu_ref, wd_ref, o_ref, h_scr, *, n_f, final_norm,
                rep):
    j = pl.program_id(1)
    mod = lambda ref: _mod_rows(ref, rep)

    @pl.when(j == 0)
    def _():
        h = _rms(x_ref[...]) * ng_ref[...] * (1.0 + mod(sc_ref)) + mod(sh_ref)
        h_scr[...] = h.astype(BF16)
        o_ref[...] = jnp.zeros_like(o_ref)

    h = h_scr[...]
    a = (_silu(_dot(h, wg_ref[...])) * _dot(h, wu_ref[...])).astype(BF16)
    o_ref[...] += _dot(a, wd_ref[...])

    @pl.when(j == n_f - 1)
    def _():
        out = x_ref[...] + 0.5 * (1.0 + mod(gt_ref)) * o_ref[...]
        if final_norm:
            out = _rms(out) * nf_ref[...]
        o_ref[...] = out


def _ffn(x, shift, scale, gate, norm_g, norm_final, wg, wu, wd, *, final_norm, tm=512, tf=512):
    t = x.shape[0]
    d_ff = wg.shape[1]
    n_f = d_ff // tf
    mod_spec, rep = _mod_spec(shift.shape[0], t, tm, 2)
    row_spec = pl.BlockSpec((tm, D_MODEL), lambda i, j: (i, 0))
    vec_spec = pl.BlockSpec((1, D_MODEL), lambda i, j: (0, 0))
    return pl.pallas_call(
        functools.partial(_ffn_kernel, n_f=n_f, final_norm=final_norm, rep=rep),
        grid=(t // tm, n_f),
        in_specs=[row_spec, mod_spec, mod_spec, mod_spec, vec_spec, vec_spec,
                  pl.BlockSpec((D_MODEL, tf), lambda i, j: (0, j)),
                  pl.BlockSpec((D_MODEL, tf), lambda i, j: (0, j)),
                  pl.BlockSpec((tf, D_MODEL), lambda i, j: (j, 0))],
        out_specs=row_spec,
        out_shape=jax.ShapeDtypeStruct((t, D_MODEL), F32),
        scratch_shapes=[pltpu.VMEM((tm, D_MODEL), BF16)],
        compiler_params=_cparams("arbitrary", "arbitrary"),
        name="ffn",
    )(x, shift, scale, gate, norm_g, norm_final, wg, wu, wd)


FFN_ROW_CHUNK = 128


def _ffn_stream_kernel(x_ref, sh_ref, sc_ref, gt_ref, ng_ref, nf_ref, wg_ref, wu_ref, wd_ref,
                       o_ref, wgb_ref, wub_ref, wdb_ref, h_scr, *, n_f, final_norm, rep):
    j = pl.program_id(0)
    t = x_ref.shape[0]
    rc = FFN_ROW_CHUNK

    def rows_of(ref, c, width):
        return jnp.repeat(ref[pl.ds(pl.multiple_of(c * (width // rep), width // rep), width // rep), :], rep, axis=0)

    @pl.when(j == 0)
    def _():
        def body(c, _):
            r0 = pl.multiple_of(c * rc, rc)
            h = _rms(x_ref[pl.ds(r0, rc), :]) * ng_ref[...] * (1.0 + rows_of(sc_ref, c, rc)) + rows_of(sh_ref, c, rc)
            h_scr[pl.ds(r0, rc), :] = h.astype(BF16)
            return 0
        lax.fori_loop(0, t // rc, body, 0)
        o_ref[...] = jnp.zeros_like(o_ref)

    wg, wu, wd = wg_ref[...].astype(BF16), wu_ref[...].astype(BF16), wd_ref[...].astype(BF16)
    wgb_ref[...] = wg
    wub_ref[...] = wu
    wdb_ref[...] = wd
    h = h_scr[...]
    a = (_silu(_dot(h, wg)) * _dot(h, wu)).astype(BF16)
    o_ref[...] += _dot(a, wd)

    @pl.when(j == n_f - 1)
    def _():
        def body(c, _):
            r0 = pl.multiple_of(c * rc, rc)
            out = x_ref[pl.ds(r0, rc), :] + 0.5 * (1.0 + rows_of(gt_ref, c, rc)) * o_ref[pl.ds(r0, rc), :]
            if final_norm:
                out = _rms(out) * nf_ref[...]
            o_ref[pl.ds(r0, rc), :] = out
            return 0
        lax.fori_loop(0, t // rc, body, 0)


def _ffn_stream(x, shift, scale, gate, norm_g, norm_final, wg, wu, wd, *, final_norm, tf=256):
    t = x.shape[0]
    d_ff = wg.shape[1]
    n_f = d_ff // tf
    rep = t // shift.shape[0]
    assert t % FFN_ROW_CHUNK == 0 and FFN_ROW_CHUNK % rep == 0
    vmem = pl.BlockSpec(memory_space=pltpu.VMEM)
    col_tile = pl.BlockSpec((D_MODEL, tf), lambda j: (0, j))
    row_tile = pl.BlockSpec((tf, D_MODEL), lambda j: (j, 0))
    return pl.pallas_call(
        functools.partial(_ffn_stream_kernel, n_f=n_f, final_norm=final_norm, rep=rep),
        grid=(n_f,),
        in_specs=[vmem, vmem, vmem, vmem, vmem, vmem, col_tile, col_tile, row_tile],
        out_specs=[pl.BlockSpec((t, D_MODEL), lambda j: (0, 0)), col_tile, col_tile, row_tile],
        out_shape=[jax.ShapeDtypeStruct((t, D_MODEL), F32), jax.ShapeDtypeStruct(wg.shape, BF16),
                   jax.ShapeDtypeStruct(wu.shape, BF16), jax.ShapeDtypeStruct(wd.shape, BF16)],
        scratch_shapes=[pltpu.VMEM((t, D_MODEL), BF16)],
        compiler_params=_cparams("arbitrary"),
        name="ffn_stream",
    )(x, shift, scale, gate, norm_g, norm_final, wg, wu, wd)


PW_Q = 0
PW_KV = PW_Q + ATT_W
PW_CONV = PW_KV + 3 * SLAB_W
PW_GATE = PW_CONV + 3 * CONV_W
PW_TOTAL = PW_GATE + GATE_PAD


def _proj_kernel(x_ref, sh_ref, sc_ref, ng_ref, w_ref, cos_ref, sin_ref, p1_ref, p2_ref, cw_ref, cb_ref, nco_ref,
                 q_ref, cmp_ref, slc_ref, win_ref, ksb_ref, vst_ref, kwb_ref, vwt_ref, gate_ref, ocn_ref, u_ref, qt_ref,
                 gatet_ref, carry_scr, *, tm, seq_rows, carry, rep):
    i = pl.program_id(0)
    h = (_rms(x_ref[...]) * ng_ref[...] * (1.0 + _mod_rows(sc_ref, rep)) + _mod_rows(sh_ref, rep)).astype(BF16)
    cos, sin = cos_ref[...], sin_ref[...]

    pq = _dot(h, w_ref[:, PW_Q:PW_KV])
    for hd in range(N_HEADS):
        hs = slice(hd * HEAD_DIM, (hd + 1) * HEAD_DIM)
        blk = _rope(pq[:, hs], cos, sin) * (SCALE * LOG2E)
        q_ref[:, hs] = blk.astype(q_ref.dtype)
        qt_ref[hs, :] = blk.T.astype(BF16)

    pkv = _dot(h, w_ref[:, PW_KV:PW_CONV])
    for slab, (o_ref, kb_ref) in enumerate(((cmp_ref, None), (slc_ref, ksb_ref), (win_ref, kwb_ref))):
        base = slab * SLAB_W
        for g in range(N_KV):
            gs = slice(g * HEAD_DIM, (g + 1) * HEAD_DIM)
            k = pkv[:, base + g * HEAD_DIM:base + (g + 1) * HEAD_DIM]
            v = pkv[:, base + KV_W + g * HEAD_DIM:base + KV_W + (g + 1) * HEAD_DIM]
            if kb_ref is not None:
                k = _rope(k, cos, sin)
                kb_ref[:, gs] = k.astype(BF16)
                v_t = v.T.astype(BF16)
                if slab == 1:
                    vst_ref[0, gs, :] = v_t
                else:
                    for blk in range(tm // Q_BLOCK):
                        vwt_ref[blk, gs, :] = v_t[:, blk * Q_BLOCK:(blk + 1) * Q_BLOCK]
            o_ref[pl.ds(g, tm, stride=KV_ROWS), :] = k
            o_ref[pl.ds(N_KV + g, tm, stride=KV_ROWS), :] = v

    gate = jax.nn.sigmoid(_dot(h, w_ref[:, PW_GATE:PW_TOTAL]))
    gate_ref[...] = gate
    gatet_ref[...] = gate.T

    pc = _dot(h, w_ref[:, PW_CONV:PW_GATE])
    u = pc[:, 0:CONV_W] * pc[:, 2 * CONV_W:3 * CONV_W]
    c_out = pc[:, CONV_W:2 * CONV_W]
    row = lax.broadcasted_iota(jnp.int32, (tm, CONV_W), 0)
    if carry:
        @pl.when(i == 0)
        def _():
            carry_scr[...] = jnp.zeros_like(carry_scr)
        prev1 = carry_scr[7:8, :]
        prev2 = carry_scr[6:7, :]
        rs = row
    else:
        prev1 = p1_ref[...]
        prev2 = p2_ref[...]
        rs = row & (seq_rows - 1)
    um1 = jnp.where(rs >= 1, pltpu.roll(u, 1, axis=0), prev1)
    um2 = jnp.where(rs >= 2, pltpu.roll(u, 2, axis=0), jnp.where(rs == 1, prev1, prev2))
    y = um2 * cw_ref[0:1, :] + um1 * cw_ref[1:2, :] + u * cw_ref[2:3, :] + cb_ref[...]
    ocn_ref[...] = (_rms(c_out * y) * nco_ref[...]).astype(ocn_ref.dtype)
    if carry:
        carry_scr[...] = u[tm - 8:tm, :]
        u_ref[...] = u[tm - 8:tm, :]
    else:
        u_ref[...] = u


def _proj(x, shift, scale, norm_g, w, cos, sin, prev1, prev2, conv_w, conv_b, norm_co, *, carry, seq_rows, act_dtype,
          tm=256):
    t = x.shape[0]
    mod_spec, rep = _mod_spec(shift.shape[0], t, tm, 1)
    rows = lambda wdt: pl.BlockSpec((tm, wdt), lambda i: (i, 0))
    const = lambda r, wdt: pl.BlockSpec((r, wdt), lambda i: (0, 0))
    tab_spec = rows(HEAD_DIM) if carry else const(tm, HEAD_DIM)
    prev_spec = const(8, CONV_W) if carry else rows(CONV_W)
    u_rows = 8 if carry else t
    out_shape = [jax.ShapeDtypeStruct((t, ATT_W), act_dtype)]
    out_shape += [jax.ShapeDtypeStruct((t * KV_ROWS, HEAD_DIM), F32)] * 3
    per_ck = P_CK // tm
    kvb = jax.ShapeDtypeStruct((t, KV_W), BF16)
    out_shape += [kvb, jax.ShapeDtypeStruct((t // P_CK, KV_W, P_CK), BF16), kvb,
                  jax.ShapeDtypeStruct((t // Q_BLOCK, KV_W, Q_BLOCK), BF16)]
    out_shape += [jax.ShapeDtypeStruct((t, GATE_PAD), F32), jax.ShapeDtypeStruct((t, CONV_W), act_dtype),
                  jax.ShapeDtypeStruct((u_rows, CONV_W), F32), jax.ShapeDtypeStruct((ATT_W, t), BF16),
                  jax.ShapeDtypeStruct((GATE_PAD, t), F32)]
    lin_spec = pl.BlockSpec((tm * KV_ROWS, HEAD_DIM), lambda i: (i, 0))
    vst_spec = pl.BlockSpec((1, KV_W, tm), lambda i: (i // per_ck, 0, i % per_ck))
    vwt_spec = pl.BlockSpec((tm // Q_BLOCK, KV_W, Q_BLOCK), lambda i: (i, 0, 0))
    out_specs = [rows(ATT_W)] + [lin_spec] * 3 + [rows(KV_W), vst_spec, rows(KV_W), vwt_spec]
    out_specs += [rows(GATE_PAD), rows(CONV_W), const(8, CONV_W) if carry else rows(CONV_W),
                  pl.BlockSpec((ATT_W, tm), lambda i: (0, i)), pl.BlockSpec((GATE_PAD, tm), lambda i: (0, i))]
    return pl.pallas_call(
        functools.partial(_proj_kernel, tm=tm, seq_rows=seq_rows, carry=carry, rep=rep),
        grid=(t // tm,),
        in_specs=[rows(D_MODEL), mod_spec, mod_spec, const(1, D_MODEL),
                  pl.BlockSpec(memory_space=pltpu.VMEM),
                  tab_spec, tab_spec, prev_spec, prev_spec, const(CONV_K, CONV_W), const(1, CONV_W),
                  const(1, CONV_W)],
        out_specs=out_specs,
        out_shape=out_shape,
        scratch_shapes=[pltpu.VMEM((8, CONV_W), F32)],
        compiler_params=_cparams("arbitrary"),
        name="proj",
    )(x, shift, scale, norm_g, w, cos, sin, prev1, prev2, conv_w, conv_b, norm_co)


def _cmp_kernel(x_ref, w_ref, o_ref, pad_scr, *, n_j, tm):
    pitch = n_j * KV_ROWS
    ppitch = pitch + SUBLANES

    def repitch(c, _):
        src = pl.multiple_of(c * pitch, SUBLANES)
        dst = pl.multiple_of(c * ppitch, SUBLANES)
        pad_scr[pl.ds(dst, pitch), :] = x_ref[pl.ds(src, pitch), :]
        return 0

    lax.fori_loop(0, tm, repitch, 0, unroll=8)
    flat = pad_scr
    for kv in range(2):
        for g in range(N_KV):
            xs = jnp.concatenate([flat[pl.ds(j * KV_ROWS + kv * N_KV + g, tm, stride=ppitch), :] for j in range(n_j)],
                                 axis=1)
            col = (kv * N_KV + g) * 2 * HEAD_DIM
            o_ref[:, col:col + 2 * HEAD_DIM] = _dot(xs.astype(BF16), w_ref[kv])


def _compress_products(x, w, n_j, tm):
    pitch = n_j * KV_ROWS
    m = x.shape[0] // pitch
    n_out = 2 * N_KV * 2 * HEAD_DIM
    tm = min(tm, m)
    return pl.pallas_call(
        functools.partial(_cmp_kernel, n_j=n_j, tm=tm),
        grid=(m // tm,),
        in_specs=[pl.BlockSpec((tm * pitch, HEAD_DIM), lambda i: (i, 0)),
                  pl.BlockSpec((2, n_j * HEAD_DIM, 2 * HEAD_DIM), lambda i: (0, 0, 0))],
        out_specs=pl.BlockSpec((tm, n_out), lambda i: (i, 0)),
        out_shape=jax.ShapeDtypeStruct((m, n_out), F32),
        scratch_shapes=[pltpu.VMEM((tm * (pitch + SUBLANES), HEAD_DIM), F32)],
        compiler_params=_cparams("arbitrary"),
        name="compress",
    )(x, w)


def _r_cols(kv, g):
    return (kv * N_KV + g) * 2 * HEAD_DIM


P_CK = 512
P_WKEYS = WINDOW + Q_BLOCK


def _pattn_t_kernel(qt_ref, gatet_ref, rp_ref, cosc_ref, sinc_ref, ks_ref, vst_ref, kw_ref, vwt_ref, et_ref, band_ref,
                    nao_ref, o_ref, kc_scr, vct_scr, pt_scr, ot_scr, s_scr, *, n_cmp_pad, n_blocks):
    i = pl.program_id(0)
    s0 = i * Q_BLOCK
    nq = Q_BLOCK
    rows = HEADS_PER_KV * nq
    lanes4 = lambda a: jnp.concatenate([a] * HEADS_PER_KV, axis=1)

    @pl.when(i == 0)
    def _():
        for g in range(N_KV):
            gs = slice(g * HEAD_DIM, (g + 1) * HEAD_DIM)
            ck, cv = _r_cols(0, g), _r_cols(1, g)
            kc = rp_ref[0:n_cmp_pad, ck:ck + HEAD_DIM] + rp_ref[pl.ds(1, n_cmp_pad), ck + HEAD_DIM:ck + 2 * HEAD_DIM]
            kc_scr[:, gs] = _rope(kc, cosc_ref[...], sinc_ref[...]).astype(BF16)
            vc = rp_ref[0:n_cmp_pad, cv:cv + HEAD_DIM] + rp_ref[pl.ds(1, n_cmp_pad), cv + HEAD_DIM:cv + 2 * HEAD_DIM]
            vct_scr[gs, :] = vc.T.astype(BF16)
        pt_scr[...] = jnp.zeros_like(pt_scr)

    t_q = s0 + lax.broadcasted_iota(jnp.int32, (n_cmp_pad, nq), 1)
    c_end = lax.broadcasted_iota(jnp.int32, (n_cmp_pad, nq), 0) * CMP_STRIDE + (CMP_BLOCK - 1)
    bias_c = lanes4(jnp.where(c_end <= t_q, 0.0, NEG))
    j_io = lax.broadcasted_iota(jnp.int32, (P_WKEYS, nq), 0)
    bias_w = lanes4(jnp.where(j_io >= WINDOW - s0, band_ref[...], NEG))

    blocks = [jnp.maximum(i + k - WINDOW // nq, 0) for k in range(P_WKEYS // nq)]
    scores, rhs, s_cmp, s_win = [], [], [], []
    for g in range(N_KV):
        gs = slice(g * HEAD_DIM, (g + 1) * HEAD_DIM)
        q_t = jnp.concatenate([qt_ref[(g * HEADS_PER_KV + r) * HEAD_DIM:(g * HEADS_PER_KV + r + 1) * HEAD_DIM, :]
                               for r in range(HEADS_PER_KV)], axis=1)
        rhs.append(q_t)
        s_cmp.append(_dot(kc_scr[:, gs], q_t))
        k_w = jnp.concatenate([kw_ref[pl.ds(pl.multiple_of(b * nq, nq), nq), gs] for b in blocks], axis=0)
        s_win.append(_dot(k_w, q_t))

    for g in range(N_KV):
        gs = slice(g * HEAD_DIM, (g + 1) * HEAD_DIM)
        s = s_cmp[g] + bias_c
        m = jnp.max(s, axis=0, keepdims=True)
        e = jnp.exp2(s - m)
        inv = jnp.where(m > 0.5 * NEG, 1.0 / jnp.maximum(jnp.sum(e, axis=0, keepdims=True), TINY), 0.0)
        o_c = _dot(vct_scr[gs, :], e.astype(BF16)) * inv

        p = e * inv
        pt_scr[8:8 + n_cmp_pad, :] = sum(p[:, r * nq:(r + 1) * nq] for r in range(HEADS_PER_KV))
        st = lambda k: pt_scr[pl.ds(7 + k, n_blocks, stride=CHUNKS_PER_SEL), :]
        score = 0.5 * st(0) + st(1) + st(2) + st(3) + 0.5 * st(4)
        b_io = lax.broadcasted_iota(jnp.int32, (n_blocks, nq), 0)
        t_lane = s0 + lax.broadcasted_iota(jnp.int32, (n_blocks, nq), 1)
        cur = t_lane >> 6
        forced = (b_io == 0) | (b_io == cur) | (b_io == cur - 1)
        valid = b_io * SEL_BLOCK <= t_lane
        scores.append(jnp.where(forced, jnp.inf, jnp.where(valid, score, -jnp.inf)))

        v_wt = jnp.concatenate([vwt_ref[b, gs, :] for b in blocks], axis=1)
        s = s_win[g] + bias_w
        e = jnp.exp2(s - jnp.max(s, axis=0, keepdims=True))
        o_w = _dot(v_wt, e.astype(BF16)) / jnp.maximum(jnp.sum(e, axis=0, keepdims=True), TINY)

        for r in range(HEADS_PER_KV):
            hd = g * HEADS_PER_KV + r
            ls = slice(r * nq, (r + 1) * nq)
            gc = gatet_ref[hd * N_BRANCH + 0:hd * N_BRANCH + 1, :]
            gw = gatet_ref[hd * N_BRANCH + 2:hd * N_BRANCH + 3, :]
            ot_scr[hd * HEAD_DIM:(hd + 1) * HEAD_DIM, :] = gc * o_c[:, ls] + gw * o_w[:, ls]

    b_f = lax.broadcasted_iota(jnp.int32, (n_blocks, nq), 0).astype(F32)

    def pick(_, c):
        out = []
        for work, sel in c:
            m = jnp.max(work, axis=0, keepdims=True)
            idx = jnp.min(jnp.where(work == m, b_f, float(n_blocks)), axis=0, keepdims=True)
            hit = b_f == idx
            out.append((jnp.where(hit, -jnp.inf, work), jnp.where(hit, 1.0, sel)))
        return tuple(out)

    picked = lax.fori_loop(0, min(N_SEL, n_blocks), pick,
                           tuple((sc, jnp.zeros((n_blocks, nq), F32)) for sc in scores))

    for g in range(N_KV):
        sel_bias = jnp.where(picked[g][1] > 0.5, 0.0, NEG).astype(BF16)
        rhs[g] = jnp.concatenate([rhs[g], lanes4(sel_bias)], axis=0)

    def qk_scores(c, slot):
        k0 = pl.multiple_of(c * P_CK, P_CK)
        blk_hot = et_ref[pl.ds(k0, P_CK), :]
        for g in range(N_KV):
            gs = slice(g * HEAD_DIM, (g + 1) * HEAD_DIM)
            s_scr[slot, g] = _dot(jnp.concatenate([ks_ref[pl.ds(k0, P_CK), gs], blk_hot], axis=1), rhs[g])

    def softmax_pv(c, slot, carry, causal):
        if causal:
            key = c * P_CK + lax.broadcasted_iota(jnp.int32, (P_CK, rows), 0)
            t_k = s0 + (lax.broadcasted_iota(jnp.int32, (P_CK, rows), 1) & (nq - 1))
            cb = jnp.where(key <= t_k, 0.0, NEG)
        out = []
        for g in range(N_KV):
            gs = slice(g * HEAD_DIM, (g + 1) * HEAD_DIM)
            m_i, l_i, acc = carry[g]
            s = s_scr[slot, g]
            if causal:
                s = s + cb
            m_n = jnp.maximum(m_i, jnp.max(s, axis=0, keepdims=True))
            p = jnp.exp2(s - m_n)
            alpha = jnp.exp2(m_i - m_n)
            l_n = alpha * l_i + jnp.sum(p, axis=0, keepdims=True)
            out.append((m_n, l_n, alpha * acc + _dot(vst_ref[c, gs, :], p.astype(BF16))))
        return tuple(out)

    def pair(pr, carry):
        c = 2 * pr
        qk_scores(c + 1, 1)
        carry = softmax_pv(c, 0, carry, False)
        qk_scores(c + 2, 0)
        return softmax_pv(c + 1, 1, carry, False)

    def odd_tail(c, carry):
        qk_scores(c + 1, 1)
        return softmax_pv(c, 0, carry, False)

    n_chunks = (s0 + nq + P_CK - 1) // P_CK
    n_pairs = (n_chunks - 1) // 2
    init = tuple((jnp.full((1, rows), NEG, F32), jnp.zeros((1, rows), F32), jnp.zeros((HEAD_DIM, rows), F32))
                 for _ in range(N_KV))
    qk_scores(0, 0)
    carry = lax.fori_loop(0, n_pairs, pair, init)
    last = n_chunks - 1
    carry = lax.cond(last > 2 * n_pairs, lambda cr: odd_tail(last - 1, cr), lambda cr: cr, carry)
    carry = lax.cond(last > 2 * n_pairs, lambda cr: softmax_pv(last, 1, cr, True),
                     lambda cr: softmax_pv(last, 0, cr, True), carry)
    for g in range(N_KV):
        _, l_s, acc = carry[g]
        o_st = acc / jnp.maximum(l_s, TINY)
        for r in range(HEADS_PER_KV):
            hd = g * HEADS_PER_KV + r
            gsl = gatet_ref[hd * N_BRANCH + 1:hd * N_BRANCH + 2, :]
            ot_scr[hd * HEAD_DIM:(hd + 1) * HEAD_DIM, :] += gsl * o_st[:, r * nq:(r + 1) * nq]

    o_t = ot_scr[...]
    o_t = o_t * lax.rsqrt(jnp.mean(o_t * o_t, axis=0, keepdims=True) + EPS)
    o_ref[...] = (o_t.T * nao_ref[...]).astype(o_ref.dtype)


def _prompt_attention_t(q_t, gates_t, rp, cosc, sinc, ks, vs_t, kw, vw_t, e_t, band, norm_ao):
    s_len = q_t.shape[1]
    n_cmp_pad = s_len // CMP_STRIDE
    n_blocks = s_len // SEL_BLOCK
    vmem = pl.BlockSpec(memory_space=pltpu.VMEM)
    cols = lambda h: pl.BlockSpec((h, Q_BLOCK), lambda i: (0, i))
    return pl.pallas_call(
        functools.partial(_pattn_t_kernel, n_cmp_pad=n_cmp_pad, n_blocks=n_blocks),
        grid=(s_len // Q_BLOCK,),
        in_specs=[cols(ATT_W), cols(GATE_PAD), vmem, vmem, vmem, vmem, vmem, vmem, vmem, vmem, vmem,
                  pl.BlockSpec((1, ATT_W), lambda i: (0, 0))],
        out_specs=pl.BlockSpec((Q_BLOCK, ATT_W), lambda i: (i, 0)),
        out_shape=jax.ShapeDtypeStruct((s_len, ATT_W), BF16),
        scratch_shapes=[pltpu.VMEM((n_cmp_pad, KV_W), BF16), pltpu.VMEM((KV_W, n_cmp_pad), BF16),
                        pltpu.VMEM((n_cmp_pad + 16, Q_BLOCK), F32), pltpu.VMEM((ATT_W, Q_BLOCK), F32),
                        pltpu.VMEM((2, N_KV, P_CK, HEADS_PER_KV * Q_BLOCK), F32)],
        compiler_params=_cparams("arbitrary"),
        name="prompt_attn",
    )(q_t, gates_t, rp, cosc, sinc, ks, vs_t, kw, vw_t, e_t, band, norm_ao)


def _sattn_kernel(pt_ref, *refs, n_seqs, n_pages, page, n_new, past, wbuf):
    del pt_ref
    n_pg = n_seqs * n_pages
    r_pages, s_pages = refs[0:n_pg], refs[n_pg:2 * n_pg]
    (rnew_ref, q_ref, gate_ref, snew_ref, wst_ref, wnew_ref, cosc_ref, sinc_ref, e_ref, nao_ref,
     o_ref, wout_ref, r_scr, k_scr, v_scr, kw_scr, vw_scr, o_scr) = refs[2 * n_pg:]
    chunks_pp = page // CMP_STRIDE
    n_cmp = n_pages * chunks_pp
    rows = HEADS_PER_KV * n_new
    n_keys = k_scr.shape[1]
    n_wkeys = kw_scr.shape[1]
    pad = n_keys - past
    wpad = n_wkeys - wbuf
    batch = [(sq, g) for sq in range(n_seqs) for g in range(N_KV)]
    cat = lambda parts: jnp.concatenate(parts, axis=0)

    def with_zero_rows(new_rows, n_zero):
        return cat([new_rows, jnp.zeros((n_zero, HEAD_DIM), F32)]).astype(BF16)

    for sq in range(n_seqs):
        n0 = sq * n_new * KV_ROWS
        w0 = sq * wbuf * KV_ROWS
        for p in range(n_pages):
            r_scr[sq, p * chunks_pp:(p + 1) * chunks_pp, :] = r_pages[sq * n_pages + p][...]
            pg = s_pages[sq * n_pages + p]
            for g in range(N_KV):
                b = sq * N_KV + g
                k_scr[b, p * page:(p + 1) * page, :] = pg[pl.ds(g, page, stride=KV_ROWS), :].astype(BF16)
                v_scr[b, p * page:(p + 1) * page, :] = pg[pl.ds(N_KV + g, page, stride=KV_ROWS), :].astype(BF16)
        r_scr[sq, n_cmp:n_cmp + 8, :] = cat([rnew_ref[sq], jnp.zeros((7, r_scr.shape[2]), F32)])
        for g in range(N_KV):
            b = sq * N_KV + g
            k_scr[b, past:n_keys, :] = with_zero_rows(snew_ref[pl.ds(n0 + g, n_new, stride=KV_ROWS), :], pad - n_new)
            v_scr[b, past:n_keys, :] = with_zero_rows(snew_ref[pl.ds(n0 + N_KV + g, n_new, stride=KV_ROWS), :],
                                                      pad - n_new)
            kw_scr[b, 0:wbuf, :] = wst_ref[pl.ds(w0 + g, wbuf, stride=KV_ROWS), :].astype(BF16)
            vw_scr[b, 0:wbuf, :] = wst_ref[pl.ds(w0 + N_KV + g, wbuf, stride=KV_ROWS), :].astype(BF16)
            kw_scr[b, wbuf:n_wkeys, :] = with_zero_rows(wnew_ref[pl.ds(n0 + g, n_new, stride=KV_ROWS), :],
                                                        wpad - n_new)
            vw_scr[b, wbuf:n_wkeys, :] = with_zero_rows(wnew_ref[pl.ds(n0 + N_KV + g, n_new, stride=KV_ROWS), :],
                                                        wpad - n_new)
        keep = (wbuf - n_new) * KV_ROWS
        wout_ref[w0:w0 + keep, :] = wst_ref[w0 + n_new * KV_ROWS:w0 + wbuf * KV_ROWS, :]
        wout_ref[w0 + keep:w0 + wbuf * KV_ROWS, :] = wnew_ref[n0:n0 + n_new * KV_ROWS, :]

    n_b = len(batch)
    all_rows = n_b * rows
    tok = lax.broadcasted_iota(jnp.int32, (all_rows, 1), 0) & (n_new - 1)
    t_rows = past + tok

    qg, kc, vc = [], [], []
    for sq, g in batch:
        qs = slice(sq * n_new, (sq + 1) * n_new)
        qg.append(cat([q_ref[qs, (g * HEADS_PER_KV + r) * HEAD_DIM:(g * HEADS_PER_KV + r + 1) * HEAD_DIM]
                       for r in range(HEADS_PER_KV)]).astype(BF16))
        ck, cv = _r_cols(0, g), _r_cols(1, g)
        kc.append(r_scr[sq, 0:n_cmp, ck:ck + HEAD_DIM] + r_scr[sq, pl.ds(1, n_cmp), ck + HEAD_DIM:ck + 2 * HEAD_DIM])
        vc.append((r_scr[sq, 0:n_cmp, cv:cv + HEAD_DIM]
                   + r_scr[sq, pl.ds(1, n_cmp), cv + HEAD_DIM:cv + 2 * HEAD_DIM]).astype(BF16))
    cos_all, sin_all = cat([cosc_ref[...]] * n_b), cat([sinc_ref[...]] * n_b)
    kc_all = _rope(cat(kc), cos_all, sin_all).astype(BF16)
    rb = lambda b: slice(b * rows, (b + 1) * rows)

    s_c = cat([_dot_nt(qg[b], kc_all[b * n_cmp:(b + 1) * n_cmp]) for b in range(n_b)])
    s_w = cat([_dot_nt(qg[b], kw_scr[b]) for b in range(n_b)])
    s_s = cat([_dot_nt(qg[b], k_scr[b]) for b in range(n_b)])

    c_end = lax.broadcasted_iota(jnp.int32, (all_rows, n_cmp), 1) * CMP_STRIDE + (CMP_BLOCK - 1)
    p_c = _masked_softmax(s_c, c_end <= t_rows)
    p_cb = p_c.astype(BF16)
    o_c = [_dot(p_cb[rb(b)], vc[b]) for b in range(n_b)]

    sel_rows = n_b * n_new
    p_grp = cat([sum(p_c[b * rows + r * n_new:b * rows + (r + 1) * n_new] for r in range(HEADS_PER_KV))
                 for b in range(n_b)])
    lane = lax.broadcasted_iota(jnp.int32, (sel_rows, HEAD_DIM), 1)
    pch = 0.5 * (p_grp + jnp.where(lane >= 1, pltpu.roll(p_grp, 1, axis=1), 0.0))
    score = pch
    for k in range(1, CHUNKS_PER_SEL):
        score = score + pltpu.roll(pch, HEAD_DIM - k, axis=1)
    blk = lane >> 2
    t_tok = past + (lax.broadcasted_iota(jnp.int32, (sel_rows, HEAD_DIM), 0) & (n_new - 1))
    cur = t_tok >> 6
    forced = (blk == 0) | (blk == cur) | (blk == cur - 1)
    score = jnp.where(forced, jnp.inf, score)
    n_pb = past // SEL_BLOCK
    ahead = jnp.zeros((sel_rows, HEAD_DIM), F32)
    for k in range(1, n_pb):
        other = pltpu.roll(score, CHUNKS_PER_SEL * k, axis=1)
        wins = (other > score) | ((other == score) & (blk >= k))
        ahead = ahead + jnp.where(wins, 1.0, 0.0)
    sel = jnp.where(((lane & (CHUNKS_PER_SEL - 1)) == 0) & (ahead < N_SEL - 1), 1.0, 0.0).astype(BF16)
    sel_keys = _dot(sel, e_ref[...])
    new_ok = (lax.broadcasted_iota(jnp.int32, (sel_rows, pad), 1)
              <= (lax.broadcasted_iota(jnp.int32, (sel_rows, pad), 0) & (n_new - 1)))
    bias = jnp.concatenate([jnp.where(sel_keys > 0.5, 0.0, NEG), jnp.where(new_ok, 0.0, NEG)], axis=1)
    bias = cat([bias[b * n_new:(b + 1) * n_new] for b in range(n_b) for _ in range(HEADS_PER_KV)])

    s_s = s_s + bias
    e_s = jnp.exp2(s_s - jnp.max(s_s, axis=-1, keepdims=True))
    p_s = (e_s / jnp.maximum(jnp.sum(e_s, axis=-1, keepdims=True), TINY)).astype(BF16)
    o_s = [_dot(p_s[rb(b)], v_scr[b]) for b in range(n_b)]

    j_io = lax.broadcasted_iota(jnp.int32, (all_rows, n_wkeys), 1)
    m_w = (((j_io < wbuf) & (j_io > tok + (wbuf - WINDOW)) & (j_io >= wbuf - past))
           | ((j_io >= wbuf) & (j_io - wbuf <= tok)))
    p_w = _masked_softmax(s_w, m_w).astype(BF16)
    o_w = [_dot(p_w[rb(b)], vw_scr[b]) for b in range(n_b)]

    for b, (sq, g) in enumerate(batch):
        qs = slice(sq * n_new, (sq + 1) * n_new)
        for r in range(HEADS_PER_KV):
            hd = g * HEADS_PER_KV + r
            rs = slice(r * n_new, (r + 1) * n_new)
            gc = gate_ref[qs, hd * N_BRANCH + 0:hd * N_BRANCH + 1]
            gsl = gate_ref[qs, hd * N_BRANCH + 1:hd * N_BRANCH + 2]
            gw = gate_ref[qs, hd * N_BRANCH + 2:hd * N_BRANCH + 3]
            o_scr[qs, hd * HEAD_DIM:(hd + 1) * HEAD_DIM] = gc * o_c[b][rs] + gsl * o_s[b][rs] + gw * o_w[b][rs]

    o_ref[...] = _rms(o_scr[...]) * nao_ref[...]


S_SEQS = 2


def _sample_attention(page_table, r_all, r_new, q, gates, slc_cache, slc_new, win_state, win_new, cosc, sinc, e_mat,
                      norm_ao, *, n_new, page, wbuf):
    n_seq, n_pages = page_table.shape
    past = n_pages * page
    chunks_pp = page // CMP_STRIDE
    n_cmp = n_pages * chunks_pp
    assert n_cmp == HEAD_DIM and n_new == 8 and past % SEL_BLOCK == 0 and n_new <= SEL_BLOCK
    assert (past + n_new - 1) // SEL_BLOCK == past // SEL_BLOCK and wbuf == WINDOW and past >= WINDOW
    n_keys = past + HEAD_DIM
    n_wkeys = wbuf + HEAD_DIM
    r_w = r_all.shape[1]

    ns = S_SEQS
    assert n_seq % ns == 0
    page_map = lambda sq, p: (lambda b, pt: (pt[b * ns + sq, p], 0))
    in_specs = [pl.BlockSpec((chunks_pp, r_w), page_map(sq, p)) for sq in range(ns) for p in range(n_pages)]
    in_specs += [pl.BlockSpec((page * KV_ROWS, HEAD_DIM), page_map(sq, p)) for sq in range(ns) for p in range(n_pages)]
    seq_rows = lambda wdt: pl.BlockSpec((ns * n_new, wdt), lambda b, pt: (b, 0))
    kv_rows = lambda n_tok: pl.BlockSpec((ns * n_tok * KV_ROWS, HEAD_DIM), lambda b, pt: (b, 0))
    const = lambda shape: pl.BlockSpec(shape, lambda b, pt: (0,) * len(shape))
    in_specs += [pl.BlockSpec((ns, 1, r_w), lambda b, pt: (b, 0, 0)), seq_rows(ATT_W), seq_rows(GATE_PAD),
                 kv_rows(n_new), kv_rows(wbuf), kv_rows(n_new),
                 const((n_cmp, HEAD_DIM)), const((n_cmp, HEAD_DIM)), const((HEAD_DIM, past)), const((1, ATT_W))]
    grid_spec = pltpu.PrefetchScalarGridSpec(
        num_scalar_prefetch=1,
        grid=(n_seq // ns,),
        in_specs=in_specs,
        out_specs=[seq_rows(ATT_W), kv_rows(wbuf)],
        scratch_shapes=[pltpu.VMEM((ns, n_cmp + 8, r_w), F32), pltpu.VMEM((ns * N_KV, n_keys, HEAD_DIM), BF16),
                        pltpu.VMEM((ns * N_KV, n_keys, HEAD_DIM), BF16), pltpu.VMEM((ns * N_KV, n_wkeys, HEAD_DIM), BF16),
                        pltpu.VMEM((ns * N_KV, n_wkeys, HEAD_DIM), BF16), pltpu.VMEM((ns * n_new, ATT_W), F32)],
    )
    return pl.pallas_call(
        functools.partial(_sattn_kernel, n_seqs=ns, n_pages=n_pages, page=page, n_new=n_new, past=past, wbuf=wbuf),
        grid_spec=grid_spec,
        out_shape=[jax.ShapeDtypeStruct((n_seq * n_new, ATT_W), F32),
                   jax.ShapeDtypeStruct((n_seq * wbuf * KV_ROWS, HEAD_DIM), F32)],
        compiler_params=_cparams("arbitrary"),
        name="sample_attn",
    )(page_table, *([r_all] * (ns * n_pages)), *([slc_cache] * (ns * n_pages)), r_new, q, gates, slc_new, win_state,
      win_new, cosc, sinc, e_mat, norm_ao)


def _outproj_kernel(x_ref, oa_ref, oc_ref, gt_ref, w_ref, o_ref, *, rep):
    y = _dot(oa_ref[...].astype(BF16), w_ref[0:ATT_W, :]) + _dot(oc_ref[...].astype(BF16), w_ref[ATT_W:D_MODEL, :])
    o_ref[...] = x_ref[...] + (1.0 + _mod_rows(gt_ref, rep)) * y


def _outproj(x, oa, oc, gate, w, tm=512):
    t = x.shape[0]
    mod_spec, rep = _mod_spec(gate.shape[0], t, tm, 1)
    rows = lambda wdt: pl.BlockSpec((tm, wdt), lambda i: (i, 0))
    return pl.pallas_call(
        functools.partial(_outproj_kernel, rep=rep),
        grid=(t // tm,),
        in_specs=[rows(D_MODEL), rows(ATT_W), rows(CONV_W), mod_spec, pl.BlockSpec(memory_space=pltpu.VMEM)],
        out_specs=rows(D_MODEL),
        out_shape=jax.ShapeDtypeStruct((t, D_MODEL), F32),
        compiler_params=_cparams("arbitrary"),
        name="outproj",
    )(x, oa, oc, gate, w)


def _rope_tables(pos):
    half = HEAD_DIM // 2
    inv = ROPE_THETA ** (-jnp.arange(half, dtype=F32) * 2.0 / HEAD_DIM)
    ang = pos.astype(F32)[:, None] * inv[None, :]
    cos, sin = jnp.cos(ang), jnp.sin(ang)
    return jnp.concatenate([cos, cos], axis=1), jnp.concatenate([-sin, sin], axis=1)


def _pack_w_in(w_in):
    off_kv = ATT_W
    off_g = off_kv + 3 * SLAB_W
    off_c = off_g + N_HEADS * N_BRANCH
    gate_cols = jnp.pad(w_in[:, off_g:off_c], ((0, 0), (0, GATE_PAD - N_HEADS * N_BRANCH)))
    return jnp.concatenate([w_in[:, 0:off_g], w_in[:, off_c:off_c + 3 * CONV_W], gate_cols], axis=1).astype(BF16)


def _pack_w_cmp(w_ck, w_cv, n_j):
    def one(w):
        lo = w[0:n_j].reshape(n_j * HEAD_DIM, HEAD_DIM)
        hi = w[CMP_STRIDE:CMP_STRIDE + n_j].reshape(n_j * HEAD_DIM, HEAD_DIM)
        return jnp.concatenate([lo, hi], axis=1)
    return jnp.stack([one(w_ck), one(w_cv)]).astype(BF16)


def kernel(x_prompt, x_sample, c_prompt, c_sample, cache_cmp_kv, cache_slc_kv, state_win_kv, state_conv, page_table,
           w_ada, b_ada, norm_ffn1, ffn1_gate, ffn1_up, ffn1_down, norm_mix, w_in, w_cmp_k, w_cmp_v, conv_w, conv_b,
           norm_att_out, norm_conv_out, w_out, norm_ffn2, ffn2_gate, ffn2_up, ffn2_down, norm_final):
    n_p, s_len, _ = x_prompt.shape
    n_seq, n_new, _ = x_sample.shape
    depth = w_ada.shape[0]
    assert n_p == 1 and depth == 1
    n_pages = page_table.shape[1]
    page = cache_slc_kv.shape[2]
    n_phys = cache_slc_kv.shape[1]
    past = n_pages * page
    wbuf = state_win_kv.shape[2]
    keep_p = min(WINDOW, s_len)
    t_s = n_seq * n_new
    l = 0

    c_all = jnp.concatenate([c_sample, c_prompt, jnp.zeros((8 - n_p, D_MODEL), F32)], axis=0)
    mod = _ada(c_all, w_ada[l], b_ada[l])
    mod_p = [mod[n_seq:n_seq + 1, k * D_MODEL:(k + 1) * D_MODEL] for k in range(N_MOD)]
    mod_s = [mod[0:n_seq, k * D_MODEL:(k + 1) * D_MODEL] for k in range(N_MOD)]

    row = lambda v: v.reshape(1, -1)
    w_proj = _pack_w_in(w_in[l])
    w_o = w_out[l].astype(BF16)
    nfin = row(norm_final)

    xp = x_prompt.reshape(s_len, D_MODEL)
    xs = x_sample.reshape(t_s, D_MODEL)

    xs, *f1 = _ffn_stream(xs, mod_s[0], mod_s[1], mod_s[2], row(norm_ffn1[l]), nfin, ffn1_gate[l], ffn1_up[l],
                          ffn1_down[l], final_norm=False)
    xp = _ffn(xp, mod_p[0], mod_p[1], mod_p[2], row(norm_ffn1[l]), nfin, *f1, final_norm=False)

    cos_p, sin_p = _rope_tables(jnp.arange(s_len))
    cos_s, sin_s = _rope_tables(past + jnp.arange(n_new))
    tm_s = 256
    cos_s, sin_s = jnp.tile(cos_s, (tm_s // n_new, 1)), jnp.tile(sin_s, (tm_s // n_new, 1))
    zero8 = jnp.zeros((8, CONV_W), F32)
    conv_args = (conv_w[l], row(conv_b[l]), row(norm_conv_out[l]))
    (_, cmp_p, slc_p, win_p, ksb_p, vst_p, kwb_p, vwt_p, _, ocn_p, utail_p, qt_p, gatet_p) = _proj(
        xp, mod_p[3], mod_p[4], row(norm_mix[l]), w_proj, cos_p, sin_p, zero8, zero8, *conv_args,
        carry=True, seq_rows=s_len, act_dtype=BF16)
    prev1 = jnp.repeat(state_conv[l][:, CONV_K - 2], n_new, axis=0)
    prev2 = jnp.repeat(state_conv[l][:, CONV_K - 3], n_new, axis=0)
    (q_s, cmp_s, slc_s, win_s, _, _, _, _, gate_s, ocn_s, u_s, _, _) = _proj(
        xs, mod_s[3], mod_s[4], row(norm_mix[l]), w_proj, cos_s, sin_s, prev1, prev2, *conv_args,
        carry=False, seq_rows=n_new, act_dtype=F32, tm=tm_s)

    n_j = CMP_STRIDE
    w_c = _pack_w_cmp(w_cmp_k[l], w_cmp_v[l], n_j)
    lin = lambda a: a.reshape(-1, HEAD_DIM)
    r_p = _compress_products(cmp_p, w_c, n_j, tm=256)
    r_cache = _compress_products(lin(cache_cmp_kv), w_c, n_j, tm=256)
    r_new = _compress_products(cmp_s, _pack_w_cmp(w_cmp_k[l], w_cmp_v[l], n_new), n_new, tm=n_seq)

    n_cmp_pad = s_len // CMP_STRIDE
    cosc, sinc = _rope_tables(jnp.arange(n_cmp_pad) * CMP_STRIDE + (CMP_BLOCK - 1))
    r_p = jnp.pad(r_p, ((0, 8), (0, 0)))
    n_blocks = s_len // SEL_BLOCK
    e_t = ((jnp.arange(s_len) // SEL_BLOCK)[:, None] == jnp.arange(n_blocks)[None, :]).astype(BF16)
    j_w, ti_w = jnp.arange(P_WKEYS)[:, None], jnp.arange(Q_BLOCK)[None, :]
    band = jnp.where((j_w > ti_w) & (j_w <= ti_w + WINDOW), 0.0, NEG).astype(F32)
    oa_p = _prompt_attention_t(qt_p, gatet_p, r_p, cosc, sinc, ksb_p, vst_p, kwb_p, vwt_p, e_t, band,
                               row(norm_att_out[l]))

    n_cmp_s = past // CMP_STRIDE
    cosc_s, sinc_s = _rope_tables(jnp.arange(n_cmp_s) * CMP_STRIDE + (CMP_BLOCK - 1))
    e_s =(jnp.arange(HEAD_DIM)[:, None] == (jnp.arange(past) // SEL_BLOCK * CHUNKS_PER_SEL)[None, :]).astype(BF16)
    oa_s, win_new_state = _sample_attention(
        page_table, r_cache, r_new.reshape(n_seq, 1, -1), q_s, gate_s,
        lin(cache_slc_kv), slc_s, lin(state_win_kv), win_s,
        cosc_s, sinc_s, e_s, row(norm_att_out[l]), n_new=n_new, page=page, wbuf=wbuf)

    xp = _outproj(xp, oa_p, ocn_p, mod_p[5], w_o)
    xs = _outproj(xs, oa_s, ocn_s, mod_s[5], w_o)
    ys, *f2 = _ffn_stream(xs, mod_s[6], mod_s[7], mod_s[8], row(norm_ffn2[l]), nfin, ffn2_gate[l], ffn2_up[l],
                          ffn2_down[l], final_norm=True)
    yp = _ffn(xp, mod_p[6], mod_p[7], mod_p[8], row(norm_ffn2[l]), nfin, *f2, final_norm=True)

    kv6 = lambda a, n, s: a.reshape(1, n, s, 2, N_KV, HEAD_DIM)
    return (yp.reshape(n_p, s_len, D_MODEL), ys.reshape(n_seq, n_new, D_MODEL),
            kv6(cmp_p, n_p, s_len), kv6(slc_p, n_p, s_len), kv6(win_p[(s_len - keep_p) * KV_ROWS:], n_p, keep_p),
            utail_p[8 - (CONV_K - 1):].reshape(1, n_p, CONV_K - 1, CONV_W),
            kv6(cmp_s, n_seq, n_new), kv6(slc_s, n_seq, n_new), kv6(win_new_state, n_seq, wbuf),
            u_s.reshape(n_seq, n_new, CONV_W)[:, n_new - (CONV_K - 1):].reshape(1, n_seq, CONV_K - 1, CONV_W))
```

```python
import functools

import jax
import jax.numpy as jnp
import numpy as np
from jax import lax
from jax.experimental import pallas as pl
from jax.experimental.pallas import tpu as pltpu

F32 = jnp.float32
BF16 = jnp.bfloat16

D_MODEL = 2048
HEAD_DIM = 128
N_HEADS = 8
N_KV = 2
HEADS_PER_KV = N_HEADS // N_KV
ATT_W = N_HEADS * HEAD_DIM
KV_W = N_KV * HEAD_DIM
CONV_W = D_MODEL - ATT_W
CONV_K = 3
CMP_BLOCK = 32
CMP_STRIDE = 16
SEL_BLOCK = 64
N_SEL = 16
WINDOW = 512
Q_BLOCK = 128
N_BRANCH = 3
N_MOD = 9
ROPE_THETA = 10000.0
EPS = 1e-6
NEG = -1e30
TINY = 1e-30
SCALE = HEAD_DIM ** -0.5
LOG2E = 1.4426950408889634
SLAB_W = 2 * KV_W
KV_ROWS = 2 * N_KV
GATE_PAD = 128
SUBLANES = 8
CHUNKS_PER_SEL = SEL_BLOCK // CMP_STRIDE

VMEM_LIMIT = 56 * 1024 * 1024


def _cparams(*sem):
    return pltpu.CompilerParams(dimension_semantics=sem, vmem_limit_bytes=VMEM_LIMIT)


def _dot(a, b):
    return jnp.dot(a, b, preferred_element_type=F32)


def _dot_nt(a, b):
    return lax.dot_general(a, b, (((1,), (1,)), ((), ())), preferred_element_type=F32)


def _rms(x):
    return x * lax.rsqrt(jnp.mean(x * x, axis=-1, keepdims=True) + EPS)


def _silu(x):
    return x * jax.nn.sigmoid(x)


def _rope(x, cos, sin_signed):
    return x * cos + pltpu.roll(x, HEAD_DIM // 2, axis=1) * sin_signed


def _masked_softmax(s, mask):
    s = jnp.where(mask, s, NEG)
    m = jnp.max(s, axis=-1, keepdims=True)
    e = jnp.where(mask, jnp.exp2(s - m), 0.0)
    return e / jnp.maximum(jnp.sum(e, axis=-1, keepdims=True), TINY)


def _mod_rows(ref, rep):
    m = ref[...]
    return m if rep == 1 else jnp.repeat(m, rep, axis=0)


def _mod_spec(mrows, t, tm, n_grid_axes):
    rep = 1 if mrows == 1 else t // mrows
    shape = (1, D_MODEL) if mrows == 1 else (tm // rep, D_MODEL)
    first = (lambda i: 0) if mrows == 1 else (lambda i: i)
    index = (lambda i: (first(i), 0)) if n_grid_axes == 1 else (lambda i, j: (first(i), 0))
    return pl.BlockSpec(shape, index), rep


def _ada_kernel(c_ref, w_ref, b_ref, o_ref):
    a = _silu(c_ref[...]).astype(BF16)
    o_ref[...] = _dot(a, w_ref[...].astype(BF16)) + b_ref[...]


def _ada(c, w, b, tn=1024):
    m, n = c.shape[0], w.shape[1]
    return pl.pallas_call(
        _ada_kernel,
        grid=(n // tn,),
        in_specs=[pl.BlockSpec((m, D_MODEL), lambda j: (0, 0)),
                  pl.BlockSpec((D_MODEL, tn), lambda j: (0, j)),
                  pl.BlockSpec((1, tn), lambda j: (0, j))],
        out_specs=pl.BlockSpec((m, tn), lambda j: (0, j)),
        out_shape=jax.ShapeDtypeStruct((m, n), F32),
        compiler_params=_cparams("arbitrary"),
        name="ada",
    )(c, w, b.reshape(1, n))


def _ffn_kernel(x_ref, sh_ref, sc_ref, gt_ref, ng_ref, nf_ref, wg_ref, wu_ref, wd_ref, o_ref, h_scr, *, n_f, final_norm,
                rep):
    j = pl.program_id(1)
    mod = lambda ref: _mod_rows(ref, rep)

    @pl.when(j == 0)
    def _():
        h = _rms(x_ref[...]) * ng_ref[...] * (1.0 + mod(sc_ref)) + mod(sh_ref)
        h_scr[...] = h.astype(BF16)
        o_ref[...] = jnp.zeros_like(o_ref)

    h = h_scr[...]
    a = (_silu(_dot(h, wg_ref[...])) * _dot(h, wu_ref[...])).astype(BF16)
    o_ref[...] += _dot(a, wd_ref[...])

    @pl.when(j == n_f - 1)
    def _():
        out = x_ref[...] + 0.5 * (1.0 + mod(gt_ref)) * o_ref[...]
        if final_norm:
            out = _rms(out) * nf_ref[...]
        o_ref[...] = out


def _ffn(x, shift, scale, gate, norm_g, norm_final, wg, wu, wd, *, final_norm, tm=512, tf=512):
    t = x.shape[0]
    d_ff = wg.shape[1]
    n_f = d_ff // tf
    mod_spec, rep = _mod_spec(shift.shape[0], t, tm, 2)
    row_spec = pl.BlockSpec((tm, D_MODEL), lambda i, j: (i, 0))
    vec_spec = pl.BlockSpec((1, D_MODEL), lambda i, j: (0, 0))
    return pl.pallas_call(
        functools.partial(_ffn_kernel, n_f=n_f, final_norm=final_norm, rep=rep),
        grid=(t // tm, n_f),
        in_specs=[row_spec, mod_spec, mod_spec, mod_spec, vec_spec, vec_spec,
                  pl.BlockSpec((D_MODEL, tf), lambda i, j: (0, j)),
                  pl.BlockSpec((D_MODEL, tf), lambda i, j: (0, j)),
                  pl.BlockSpec((tf, D_MODEL), lambda i, j: (j, 0))],
        out_specs=row_spec,
        out_shape=jax.ShapeDtypeStruct((t, D_MODEL), F32),
        scratch_shapes=[pltpu.VMEM((tm, D_MODEL), BF16)],
        compiler_params=_cparams("arbitrary", "arbitrary"),
        name="ffn",
    )(x, shift, scale, gate, norm_g, norm_final, wg, wu, wd)


FFN_ROW_CHUNK = 128


def _ffn_stream_kernel(x_ref, sh_ref, sc_ref, gt_ref, ng_ref, nf_ref, wg_ref, wu_ref, wd_ref,
                       o_ref, wgb_ref, wub_ref, wdb_ref, h_scr, *, n_f, final_norm, rep):
    j = pl.program_id(0)
    t = x_ref.shape[0]
    rc = FFN_ROW_CHUNK

    def rows_of(ref, c, width):
        return jnp.repeat(ref[pl.ds(pl.multiple_of(c * (width // rep), width // rep), width // rep), :], rep, axis=0)

    @pl.when(j == 0)
    def _():
        def body(c, _):
            r0 = pl.multiple_of(c * rc, rc)
            h = _rms(x_ref[pl.ds(r0, rc), :]) * ng_ref[...] * (1.0 + rows_of(sc_ref, c, rc)) + rows_of(sh_ref, c, rc)
            h_scr[pl.ds(r0, rc), :] = h.astype(BF16)
            return 0
        lax.fori_loop(0, t // rc, body, 0)
        o_ref[...] = jnp.zeros_like(o_ref)

    wg, wu, wd = wg_ref[...].astype(BF16), wu_ref[...].astype(BF16), wd_ref[...].astype(BF16)
    wgb_ref[...] = wg
    wub_ref[...] = wu
    wdb_ref[...] = wd
    h = h_scr[...]
    a = (_silu(_dot(h, wg)) * _dot(h, wu)).astype(BF16)
    o_ref[...] += _dot(a, wd)

    @pl.when(j == n_f - 1)
    def _():
        def body(c, _):
            r0 = pl.multiple_of(c * rc, rc)
            out = x_ref[pl.ds(r0, rc), :] + 0.5 * (1.0 + rows_of(gt_ref, c, rc)) * o_ref[pl.ds(r0, rc), :]
            if final_norm:
                out = _rms(out) * nf_ref[...]
            o_ref[pl.ds(r0, rc), :] = out
            return 0
        lax.fori_loop(0, t // rc, body, 0)


def _ffn_stream(x, shift, scale, gate, norm_g, norm_final, wg, wu, wd, *, final_norm, tf=256):
    t = x.shape[0]
    d_ff = wg.shape[1]
    n_f = d_ff // tf
    rep = t // shift.shape[0]
    assert t % FFN_ROW_CHUNK == 0 and FFN_ROW_CHUNK % rep == 0
    vmem = pl.BlockSpec(memory_space=pltpu.VMEM)
    col_tile = pl.BlockSpec((D_MODEL, tf), lambda j: (0, j))
    row_tile = pl.BlockSpec((tf, D_MODEL), lambda j: (j, 0))
    return pl.pallas_call(
        functools.partial(_ffn_stream_kernel, n_f=n_f, final_norm=final_norm, rep=rep),
        grid=(n_f,),
        in_specs=[vmem, vmem, vmem, vmem, vmem, vmem, col_tile, col_tile, row_tile],
        out_specs=[pl.BlockSpec((t, D_MODEL), lambda j: (0, 0)), col_tile, col_tile, row_tile],
        out_shape=[jax.ShapeDtypeStruct((t, D_MODEL), F32), jax.ShapeDtypeStruct(wg.shape, BF16),
                   jax.ShapeDtypeStruct(wu.shape, BF16), jax.ShapeDtypeStruct(wd.shape, BF16)],
        scratch_shapes=[pltpu.VMEM((t, D_MODEL), BF16)],
        compiler_params=_cparams("arbitrary"),
        name="ffn_stream",
    )(x, shift, scale, gate, norm_g, norm_final, wg, wu, wd)


def _proj_kernel(x_ref, sh_ref, sc_ref, ng_ref, wa_ref, wc_ref, wg_ref, cos_ref, sin_ref, p1_ref, p2_ref, cw_ref,
                 cb_ref, nco_ref, *rest, tm, seq_rows, carry, rep):
    if carry:
        q_ref, cmp_ref, slc_ref, win_ref, ksb_ref, vst_ref, kwb_ref, vwt_ref, gate_ref, ocn_ref, u_ref, carry_scr = rest
    else:
        q_ref, cmp_ref, slc_ref, win_ref, gate_ref, ocn_ref, u_ref, carry_scr = rest
        ksb_ref = kwb_ref = None
    i = pl.program_id(0)
    h = (_rms(x_ref[...]) * ng_ref[...] * (1.0 + _mod_rows(sc_ref, rep)) + _mod_rows(sh_ref, rep)).astype(BF16)
    cos, sin = cos_ref[...], sin_ref[...]

    pq = _dot(h, wa_ref[:, 0:ATT_W])
    for hd in range(N_HEADS):
        hs = slice(hd * HEAD_DIM, (hd + 1) * HEAD_DIM)
        blk = _rope(pq[:, hs], cos, sin) * (SCALE * LOG2E)
        if carry:
            q_ref[hs, :] = blk.T.astype(q_ref.dtype)
        else:
            q_ref[:, hs] = blk.astype(q_ref.dtype)

    pkv = _dot(h, wa_ref[:, ATT_W:ATT_W + 3 * SLAB_W])
    for slab, (o_ref, kb_ref) in enumerate(((cmp_ref, None), (slc_ref, ksb_ref), (win_ref, kwb_ref))):
        base = slab * SLAB_W
        for g in range(N_KV):
            gs = slice(g * HEAD_DIM, (g + 1) * HEAD_DIM)
            k = pkv[:, base + g * HEAD_DIM:base + (g + 1) * HEAD_DIM]
            v = pkv[:, base + KV_W + g * HEAD_DIM:base + KV_W + (g + 1) * HEAD_DIM]
            if slab > 0:
                k = _rope(k, cos, sin)
            if kb_ref is not None:
                kb_ref[:, gs] = k.astype(BF16)
                v_t = v.T.astype(BF16)
                if slab == 1:
                    vst_ref[0, gs, :] = v_t
                else:
                    for blk in range(tm // Q_BLOCK):
                        vwt_ref[blk, gs, :] = v_t[:, blk * Q_BLOCK:(blk + 1) * Q_BLOCK]
            o_ref[pl.ds(g, tm, stride=KV_ROWS), :] = k
            o_ref[pl.ds(N_KV + g, tm, stride=KV_ROWS), :] = v

    gate = jax.nn.sigmoid(_dot(h, wg_ref[...]))
    gate_ref[...] = gate.T if carry else gate

    pc = _dot(h, wc_ref[...])
    u = pc[:, 0:CONV_W] * pc[:, 2 * CONV_W:3 * CONV_W]
    c_out = pc[:, CONV_W:2 * CONV_W]
    row = lax.broadcasted_iota(jnp.int32, (tm, CONV_W), 0)
    if carry:
        @pl.when(i == 0)
        def _():
            carry_scr[...] = jnp.zeros_like(carry_scr)
        prev1 = carry_scr[7:8, :]
        prev2 = carry_scr[6:7, :]
        rs = row
    else:
        prev1 = p1_ref[...]
        prev2 = p2_ref[...]
        rs = row & (seq_rows - 1)
    um1 = jnp.where(rs >= 1, pltpu.roll(u, 1, axis=0), prev1)
    um2 = jnp.where(rs >= 2, pltpu.roll(u, 2, axis=0), jnp.where(rs == 1, prev1, prev2))
    y = um2 * cw_ref[0:1, :] + um1 * cw_ref[1:2, :] + u * cw_ref[2:3, :] + cb_ref[...]
    ocn_ref[...] = (_rms(c_out * y) * nco_ref[...]).astype(ocn_ref.dtype)
    if carry:
        carry_scr[...] = u[tm - 8:tm, :]
        u_ref[...] = u[tm - 8:tm, :]
    else:
        u_ref[...] = u


def _proj(x, shift, scale, norm_g, w_groups, cos, sin, prev1, prev2, conv_w, conv_b, norm_co, *, carry, seq_rows, tm=256):
    t = x.shape[0]
    mod_spec, rep = _mod_spec(shift.shape[0], t, tm, 1)
    rows = lambda wdt: pl.BlockSpec((tm, wdt), lambda i: (i, 0))
    cols = lambda h: pl.BlockSpec((h, tm), lambda i: (0, i))
    const = lambda r, wdt: pl.BlockSpec((r, wdt), lambda i: (0, 0))
    vmem = pl.BlockSpec(memory_space=pltpu.VMEM)
    sds = jax.ShapeDtypeStruct
    lin = (sds((t * KV_ROWS, HEAD_DIM), F32), pl.BlockSpec((tm * KV_ROWS, HEAD_DIM), lambda i: (i, 0)))
    if carry:
        per_ck = P_CK // tm
        kvb = (sds((t, KV_W), BF16), rows(KV_W))
        outs = [(sds((ATT_W, t), BF16), cols(ATT_W)), lin, lin, lin, kvb,
                (sds((t // P_CK, KV_W, P_CK), BF16), pl.BlockSpec((1, KV_W, tm), lambda i: (i // per_ck, 0, i % per_ck))),
                kvb,
                (sds((t // Q_BLOCK, KV_W, Q_BLOCK), BF16), pl.BlockSpec((tm // Q_BLOCK, KV_W, Q_BLOCK), lambda i: (i, 0, 0))),
                (sds((GATE_PAD, t), F32), cols(GATE_PAD)), (sds((t, CONV_W), BF16), rows(CONV_W)),
                (sds((8, CONV_W), F32), const(8, CONV_W))]
    else:
        outs = [(sds((t, ATT_W), F32), rows(ATT_W)), lin, lin, lin, (sds((t, GATE_PAD), F32), rows(GATE_PAD)),
                (sds((t, CONV_W), F32), rows(CONV_W)), (sds((t, CONV_W), F32), rows(CONV_W))]
    tab_spec = rows(HEAD_DIM) if carry else const(tm, HEAD_DIM)
    prev_spec = const(8, CONV_W) if carry else rows(CONV_W)
    return pl.pallas_call(
        functools.partial(_proj_kernel, tm=tm, seq_rows=seq_rows, carry=carry, rep=rep),
        grid=(t // tm,),
        in_specs=[rows(D_MODEL), mod_spec, mod_spec, const(1, D_MODEL), vmem, vmem, vmem,
                  tab_spec, tab_spec, prev_spec, prev_spec, const(CONV_K, CONV_W), const(1, CONV_W),
                  const(1, CONV_W)],
        out_specs=[spec for _, spec in outs],
        out_shape=[shape for shape, _ in outs],
        scratch_shapes=[pltpu.VMEM((8, CONV_W), F32)],
        compiler_params=_cparams("arbitrary"),
        name="proj",
    )(x, shift, scale, norm_g, *w_groups, cos, sin, prev1, prev2, conv_w, conv_b, norm_co)


def _cmp_kernel(x_ref, w_ref, o_ref, pad_scr, *, n_j, tm):
    pitch = n_j * KV_ROWS
    ppitch = pitch + SUBLANES

    def repitch(c, _):
        src = pl.multiple_of(c * pitch, SUBLANES)
        dst = pl.multiple_of(c * ppitch, SUBLANES)
        pad_scr[pl.ds(dst, pitch), :] = x_ref[pl.ds(src, pitch), :]
        return 0

    lax.fori_loop(0, tm, repitch, 0, unroll=8)
    flat = pad_scr
    for kv in range(2):
        for g in range(N_KV):
            xs = jnp.concatenate([flat[pl.ds(j * KV_ROWS + kv * N_KV + g, tm, stride=ppitch), :] for j in range(n_j)],
                                 axis=1)
            col = (kv * N_KV + g) * 2 * HEAD_DIM
            o_ref[:, col:col + 2 * HEAD_DIM] = _dot(xs.astype(BF16), w_ref[kv])


def _compress_products(x, w, n_j, tm):
    pitch = n_j * KV_ROWS
    m = x.shape[0] // pitch
    n_out = 2 * N_KV * 2 * HEAD_DIM
    tm = min(tm, m)
    return pl.pallas_call(
        functools.partial(_cmp_kernel, n_j=n_j, tm=tm),
        grid=(m // tm,),
        in_specs=[pl.BlockSpec((tm * pitch, HEAD_DIM), lambda i: (i, 0)),
                  pl.BlockSpec((2, n_j * HEAD_DIM, 2 * HEAD_DIM), lambda i: (0, 0, 0))],
        out_specs=pl.BlockSpec((tm, n_out), lambda i: (i, 0)),
        out_shape=jax.ShapeDtypeStruct((m, n_out), F32),
        scratch_shapes=[pltpu.VMEM((tm * (pitch + SUBLANES), HEAD_DIM), F32)],
        compiler_params=_cparams("arbitrary"),
        name="compress",
    )(x, w)


def _r_cols(kv, g):
    return (kv * N_KV + g) * 2 * HEAD_DIM


P_CK = 512
P_WKEYS = WINDOW + Q_BLOCK


def _pattn_t_kernel(qt_ref, gatet_ref, rp_ref, cosc_ref, sinc_ref, ks_ref, vst_ref, kw_ref, vwt_ref, et_ref, band_ref,
                    nao_ref, o_ref, kc_scr, vct_scr, pt_scr, ot_scr, s_scr, *, n_cmp_pad, n_blocks):
    i = pl.program_id(0)
    s0 = i * Q_BLOCK
    nq = Q_BLOCK
    rows = HEADS_PER_KV * nq
    lanes4 = lambda a: jnp.concatenate([a] * HEADS_PER_KV, axis=1)

    @pl.when(i == 0)
    def _():
        for g in range(N_KV):
            gs = slice(g * HEAD_DIM, (g + 1) * HEAD_DIM)
            ck, cv = _r_cols(0, g), _r_cols(1, g)
            kc = rp_ref[0:n_cmp_pad, ck:ck + HEAD_DIM] + rp_ref[pl.ds(1, n_cmp_pad), ck + HEAD_DIM:ck + 2 * HEAD_DIM]
            kc_scr[:, gs] = _rope(kc, cosc_ref[...], sinc_ref[...]).astype(BF16)
            vc = rp_ref[0:n_cmp_pad, cv:cv + HEAD_DIM] + rp_ref[pl.ds(1, n_cmp_pad), cv + HEAD_DIM:cv + 2 * HEAD_DIM]
            vct_scr[gs, :] = vc.T.astype(BF16)
        pt_scr[...] = jnp.zeros_like(pt_scr)

    t_q = s0 + lax.broadcasted_iota(jnp.int32, (n_cmp_pad, nq), 1)
    c_end = lax.broadcasted_iota(jnp.int32, (n_cmp_pad, nq), 0) * CMP_STRIDE + (CMP_BLOCK - 1)
    bias_c = lanes4(jnp.where(c_end <= t_q, 0.0, NEG))
    j_io = lax.broadcasted_iota(jnp.int32, (P_WKEYS, nq), 0)
    bias_w = lanes4(jnp.where(j_io >= WINDOW - s0, band_ref[...], NEG))

    blocks = [jnp.maximum(i + k - WINDOW // nq, 0) for k in range(P_WKEYS // nq)]
    scores, rhs, s_cmp, s_win = [], [], [], []
    for g in range(N_KV):
        gs = slice(g * HEAD_DIM, (g + 1) * HEAD_DIM)
        q_t = jnp.concatenate([qt_ref[(g * HEADS_PER_KV + r) * HEAD_DIM:(g * HEADS_PER_KV + r + 1) * HEAD_DIM, :]
                               for r in range(HEADS_PER_KV)], axis=1)
        rhs.append(q_t)
        s_cmp.append(_dot(kc_scr[:, gs], q_t))
        k_w = jnp.concatenate([kw_ref[pl.ds(pl.multiple_of(b * nq, nq), nq), gs] for b in blocks], axis=0)
        s_win.append(_dot(k_w, q_t))

    for g in range(N_KV):
        gs = slice(g * HEAD_DIM, (g + 1) * HEAD_DIM)
        s = s_cmp[g] + bias_c
        m = jnp.max(s, axis=0, keepdims=True)
        e = jnp.exp2(s - m)
        inv = jnp.where(m > 0.5 * NEG, 1.0 / jnp.maximum(jnp.sum(e, axis=0, keepdims=True), TINY), 0.0)
        o_c = _dot(vct_scr[gs, :], e.astype(BF16)) * inv

        p = e * inv
        pt_scr[8:8 + n_cmp_pad, :] = sum(p[:, r * nq:(r + 1) * nq] for r in range(HEADS_PER_KV))
        st = lambda k: pt_scr[pl.ds(7 + k, n_blocks, stride=CHUNKS_PER_SEL), :]
        score = 0.5 * st(0) + st(1) + st(2) + st(3) + 0.5 * st(4)
        b_io = lax.broadcasted_iota(jnp.int32, (n_blocks, nq), 0)
        t_lane = s0 + lax.broadcasted_iota(jnp.int32, (n_blocks, nq), 1)
        cur = t_lane >> 6
        forced = (b_io == 0) | (b_io == cur) | (b_io == cur - 1)
        valid = b_io * SEL_BLOCK <= t_lane
        scores.append(jnp.where(forced, jnp.inf, jnp.where(valid, score, -jnp.inf)))

        v_wt = jnp.concatenate([vwt_ref[b, gs, :] for b in blocks], axis=1)
        s = s_win[g] + bias_w
        e = jnp.exp2(s - jnp.max(s, axis=0, keepdims=True))
        o_w = _dot(v_wt, e.astype(BF16)) / jnp.maximum(jnp.sum(e, axis=0, keepdims=True), TINY)

        for r in range(HEADS_PER_KV):
            hd = g * HEADS_PER_KV + r
            ls = slice(r * nq, (r + 1) * nq)
            gc = gatet_ref[hd * N_BRANCH + 0:hd * N_BRANCH + 1, :]
            gw = gatet_ref[hd * N_BRANCH + 2:hd * N_BRANCH + 3, :]
            ot_scr[hd * HEAD_DIM:(hd + 1) * HEAD_DIM, :] = gc * o_c[:, ls] + gw * o_w[:, ls]

    b_f = lax.broadcasted_iota(jnp.int32, (n_blocks, nq), 0).astype(F32)

    def pick(_, c):
        out = []
        for work, sel in c:
            m = jnp.max(work, axis=0, keepdims=True)
            idx = jnp.min(jnp.where(work == m, b_f, float(n_blocks)), axis=0, keepdims=True)
            hit = b_f == idx
            out.append((jnp.where(hit, -jnp.inf, work), jnp.where(hit, 1.0, sel)))
        return tuple(out)

    picked = lax.fori_loop(0, min(N_SEL, n_blocks), pick,
                           tuple((sc, jnp.zeros((n_blocks, nq), F32)) for sc in scores))

    for g in range(N_KV):
        sel_bias = jnp.where(picked[g][1] > 0.5, 0.0, NEG).astype(BF16)
        rhs[g] = jnp.concatenate([rhs[g], lanes4(sel_bias)], axis=0)

    def qk_scores(c, slot):
        k0 = pl.multiple_of(c * P_CK, P_CK)
        blk_hot = et_ref[pl.ds(k0, P_CK), :]
        for g in range(N_KV):
            gs = slice(g * HEAD_DIM, (g + 1) * HEAD_DIM)
            s_scr[slot, g] = _dot(jnp.concatenate([ks_ref[pl.ds(k0, P_CK), gs], blk_hot], axis=1), rhs[g])

    def softmax_pv(c, slot, carry, causal):
        if causal:
            key = c * P_CK + lax.broadcasted_iota(jnp.int32, (P_CK, rows), 0)
            t_k = s0 + (lax.broadcasted_iota(jnp.int32, (P_CK, rows), 1) & (nq - 1))
            cb = jnp.where(key <= t_k, 0.0, NEG)
        out = []
        for g in range(N_KV):
            gs = slice(g * HEAD_DIM, (g + 1) * HEAD_DIM)
            m_i, l_i, acc = carry[g]
            s = s_scr[slot, g]
            if causal:
                s = s + cb
            m_n = jnp.maximum(m_i, jnp.max(s, axis=0, keepdims=True))
            p = jnp.exp2(s - m_n)
            alpha = jnp.exp2(m_i - m_n)
            l_n = alpha * l_i + jnp.sum(p, axis=0, keepdims=True)
            out.append((m_n, l_n, alpha * acc + _dot(vst_ref[c, gs, :], p.astype(BF16))))
        return tuple(out)

    def pair(pr, carry):
        c = 2 * pr
        qk_scores(c + 1, 1)
        carry = softmax_pv(c, 0, carry, False)
        qk_scores(c + 2, 0)
        return softmax_pv(c + 1, 1, carry, False)

    def odd_tail(c, carry):
        qk_scores(c + 1, 1)
        return softmax_pv(c, 0, carry, False)

    n_chunks = (s0 + nq + P_CK - 1) // P_CK
    n_pairs = (n_chunks - 1) // 2
    init = tuple((jnp.full((1, rows), NEG, F32), jnp.zeros((1, rows), F32), jnp.zeros((HEAD_DIM, rows), F32))
                 for _ in range(N_KV))
    qk_scores(0, 0)
    carry = lax.fori_loop(0, n_pairs, pair, init)
    last = n_chunks - 1
    carry = lax.cond(last > 2 * n_pairs, lambda cr: odd_tail(last - 1, cr), lambda cr: cr, carry)
    carry = lax.cond(last > 2 * n_pairs, lambda cr: softmax_pv(last, 1, cr, True),
                     lambda cr: softmax_pv(last, 0, cr, True), carry)
    for g in range(N_KV):
        _, l_s, acc = carry[g]
        o_st = acc / jnp.maximum(l_s, TINY)
        for r in range(HEADS_PER_KV):
            hd = g * HEADS_PER_KV + r
            gsl = gatet_ref[hd * N_BRANCH + 1:hd * N_BRANCH + 2, :]
            ot_scr[hd * HEAD_DIM:(hd + 1) * HEAD_DIM, :] += gsl * o_st[:, r * nq:(r + 1) * nq]

    o_t = ot_scr[...]
    o_t = o_t * lax.rsqrt(jnp.mean(o_t * o_t, axis=0, keepdims=True) + EPS)
    o_ref[...] = (o_t.T * nao_ref[...]).astype(o_ref.dtype)


def _prompt_attention_t(q_t, gates_t, rp, cosc, sinc, ks, vs_t, kw, vw_t, e_t, band, norm_ao):
    s_len = q_t.shape[1]
    n_cmp_pad = s_len // CMP_STRIDE
    n_blocks = s_len // SEL_BLOCK
    vmem = pl.BlockSpec(memory_space=pltpu.VMEM)
    cols = lambda h: pl.BlockSpec((h, Q_BLOCK), lambda i: (0, i))
    return pl.pallas_call(
        functools.partial(_pattn_t_kernel, n_cmp_pad=n_cmp_pad, n_blocks=n_blocks),
        grid=(s_len // Q_BLOCK,),
        in_specs=[cols(ATT_W), cols(GATE_PAD), vmem, vmem, vmem, vmem, vmem, vmem, vmem, vmem, vmem,
                  pl.BlockSpec((1, ATT_W), lambda i: (0, 0))],
        out_specs=pl.BlockSpec((Q_BLOCK, ATT_W), lambda i: (i, 0)),
        out_shape=jax.ShapeDtypeStruct((s_len, ATT_W), BF16),
        scratch_shapes=[pltpu.VMEM((n_cmp_pad, KV_W), BF16), pltpu.VMEM((KV_W, n_cmp_pad), BF16),
                        pltpu.VMEM((n_cmp_pad + 16, Q_BLOCK), F32), pltpu.VMEM((ATT_W, Q_BLOCK), F32),
                        pltpu.VMEM((2, N_KV, P_CK, HEADS_PER_KV * Q_BLOCK), F32)],
        compiler_params=_cparams("arbitrary"),
        name="prompt_attn",
    )(q_t, gates_t, rp, cosc, sinc, ks, vs_t, kw, vw_t, e_t, band, norm_ao)


def _sattn_kernel(pt_ref, *refs, n_seqs, n_pages, page, n_new, past, wbuf):
    del pt_ref
    n_pg = n_seqs * n_pages
    r_pages, s_pages = refs[0:n_pg], refs[n_pg:2 * n_pg]
    (rnew_ref, q_ref, gate_ref, snew_ref, wst_ref, wnew_ref, cosc_ref, sinc_ref, e_ref, nao_ref,
     o_ref, wout_ref, r_scr, k_scr, v_scr, kw_scr, vw_scr, o_scr) = refs[2 * n_pg:]
    chunks_pp = page // CMP_STRIDE
    n_cmp = n_pages * chunks_pp
    rows = HEADS_PER_KV * n_new
    n_keys = k_scr.shape[1]
    n_wkeys = kw_scr.shape[1]
    pad = n_keys - past
    wpad = n_wkeys - wbuf
    batch = [(sq, g) for sq in range(n_seqs) for g in range(N_KV)]
    cat = lambda parts: jnp.concatenate(parts, axis=0)

    def with_zero_rows(new_rows, n_zero):
        return cat([new_rows, jnp.zeros((n_zero, HEAD_DIM), F32)]).astype(BF16)

    for sq in range(n_seqs):
        n0 = sq * n_new * KV_ROWS
        w0 = sq * wbuf * KV_ROWS
        for p in range(n_pages):
            r_scr[sq, p * chunks_pp:(p + 1) * chunks_pp, :] = r_pages[sq * n_pages + p][...]
            pg = s_pages[sq * n_pages + p]
            for g in range(N_KV):
                b = sq * N_KV + g
                k_scr[b, p * page:(p + 1) * page, :] = pg[pl.ds(g, page, stride=KV_ROWS), :].astype(BF16)
                v_scr[b, p * page:(p + 1) * page, :] = pg[pl.ds(N_KV + g, page, stride=KV_ROWS), :].astype(BF16)
        r_scr[sq, n_cmp:n_cmp + 8, :] = cat([rnew_ref[sq], jnp.zeros((7, r_scr.shape[2]), F32)])
        for g in range(N_KV):
            b = sq * N_KV + g
            k_scr[b, past:n_keys, :] = with_zero_rows(snew_ref[pl.ds(n0 + g, n_new, stride=KV_ROWS), :], pad - n_new)
            v_scr[b, past:n_keys, :] = with_zero_rows(snew_ref[pl.ds(n0 + N_KV + g, n_new, stride=KV_ROWS), :],
                                                      pad - n_new)
            kw_scr[b, 0:wbuf, :] = wst_ref[pl.ds(w0 + g, wbuf, stride=KV_ROWS), :].astype(BF16)
            vw_scr[b, 0:wbuf, :] = wst_ref[pl.ds(w0 + N_KV + g, wbuf, stride=KV_ROWS), :].astype(BF16)
            kw_scr[b, wbuf:n_wkeys, :] = with_zero_rows(wnew_ref[pl.ds(n0 + g, n_new, stride=KV_ROWS), :],
                                                        wpad - n_new)
            vw_scr[b, wbuf:n_wkeys, :] = with_zero_rows(wnew_ref[pl.ds(n0 + N_KV + g, n_new, stride=KV_ROWS), :],
                                                        wpad - n_new)
        keep = (wbuf - n_new) * KV_ROWS
        wout_ref[w0:w0 + keep, :] = wst_ref[w0 + n_new * KV_ROWS:w0 + wbuf * KV_ROWS, :]
        wout_ref[w0 + keep:w0 + wbuf * KV_ROWS, :] = wnew_ref[n0:n0 + n_new * KV_ROWS, :]

    n_b = len(batch)
    all_rows = n_b * rows
    tok = lax.broadcasted_iota(jnp.int32, (all_rows, 1), 0) & (n_new - 1)
    t_rows = past + tok

    qg, kc, vc = [], [], []
    for sq, g in batch:
        qs = slice(sq * n_new, (sq + 1) * n_new)
        qg.append(cat([q_ref[qs, (g * HEADS_PER_KV + r) * HEAD_DIM:(g * HEADS_PER_KV + r + 1) * HEAD_DIM]
                       for r in range(HEADS_PER_KV)]).astype(BF16))
        ck, cv = _r_cols(0, g), _r_cols(1, g)
        kc.append(r_scr[sq, 0:n_cmp, ck:ck + HEAD_DIM] + r_scr[sq, pl.ds(1, n_cmp), ck + HEAD_DIM:ck + 2 * HEAD_DIM])
        vc.append((r_scr[sq, 0:n_cmp, cv:cv + HEAD_DIM]
                   + r_scr[sq, pl.ds(1, n_cmp), cv + HEAD_DIM:cv + 2 * HEAD_DIM]).astype(BF16))
    cos_all, sin_all = cat([cosc_ref[...]] * n_b), cat([sinc_ref[...]] * n_b)
    kc_all = _rope(cat(kc), cos_all, sin_all).astype(BF16)
    rb = lambda b: slice(b * rows, (b + 1) * rows)

    s_c = cat([_dot_nt(qg[b], kc_all[b * n_cmp:(b + 1) * n_cmp]) for b in range(n_b)])
    s_w = cat([_dot_nt(qg[b], kw_scr[b]) for b in range(n_b)])
    s_s = cat([_dot_nt(qg[b], k_scr[b]) for b in range(n_b)])

    c_end = lax.broadcasted_iota(jnp.int32, (all_rows, n_cmp), 1) * CMP_STRIDE + (CMP_BLOCK - 1)
    p_c = _masked_softmax(s_c, c_end <= t_rows)
    p_cb = p_c.astype(BF16)
    o_c = [_dot(p_cb[rb(b)], vc[b]) for b in range(n_b)]

    sel_rows = n_b * n_new
    p_grp = cat([sum(p_c[b * rows + r * n_new:b * rows + (r + 1) * n_new] for r in range(HEADS_PER_KV))
                 for b in range(n_b)])
    lane = lax.broadcasted_iota(jnp.int32, (sel_rows, HEAD_DIM), 1)
    pch = 0.5 * (p_grp + jnp.where(lane >= 1, pltpu.roll(p_grp, 1, axis=1), 0.0))
    score = pch
    for k in range(1, CHUNKS_PER_SEL):
        score = score + pltpu.roll(pch, HEAD_DIM - k, axis=1)
    blk = lane >> 2
    t_tok = past + (lax.broadcasted_iota(jnp.int32, (sel_rows, HEAD_DIM), 0) & (n_new - 1))
    cur = t_tok >> 6
    forced = (blk == 0) | (blk == cur) | (blk == cur - 1)
    score = jnp.where(forced, jnp.inf, score)
    n_pb = past // SEL_BLOCK
    ahead = jnp.zeros((sel_rows, HEAD_DIM), F32)
    for k in range(1, n_pb):
        other = pltpu.roll(score, CHUNKS_PER_SEL * k, axis=1)
        wins = (other > score) | ((other == score) & (blk >= k))
        ahead = ahead + jnp.where(wins, 1.0, 0.0)
    sel = jnp.where(((lane & (CHUNKS_PER_SEL - 1)) == 0) & (ahead < N_SEL - 1), 1.0, 0.0).astype(BF16)
    sel_keys = _dot(sel, e_ref[...])
    new_ok = (lax.broadcasted_iota(jnp.int32, (sel_rows, pad), 1)
              <= (lax.broadcasted_iota(jnp.int32, (sel_rows, pad), 0) & (n_new - 1)))
    bias = jnp.concatenate([jnp.where(sel_keys > 0.5, 0.0, NEG), jnp.where(new_ok, 0.0, NEG)], axis=1)
    bias = cat([bias[b * n_new:(b + 1) * n_new] for b in range(n_b) for _ in range(HEADS_PER_KV)])

    s_s = s_s + bias
    e_s = jnp.exp2(s_s - jnp.max(s_s, axis=-1, keepdims=True))
    p_s = (e_s / jnp.maximum(jnp.sum(e_s, axis=-1, keepdims=True), TINY)).astype(BF16)
    o_s = [_dot(p_s[rb(b)], v_scr[b]) for b in range(n_b)]

    j_io = lax.broadcasted_iota(jnp.int32, (all_rows, n_wkeys), 1)
    m_w = (((j_io < wbuf) & (j_io > tok + (wbuf - WINDOW)) & (j_io >= wbuf - past))
           | ((j_io >= wbuf) & (j_io - wbuf <= tok)))
    p_w = _masked_softmax(s_w, m_w).astype(BF16)
    o_w = [_dot(p_w[rb(b)], vw_scr[b]) for b in range(n_b)]

    for b, (sq, g) in enumerate(batch):
        qs = slice(sq * n_new, (sq + 1) * n_new)
        for r in range(HEADS_PER_KV):
            hd = g * HEADS_PER_KV + r
            rs = slice(r * n_new, (r + 1) * n_new)
            gc = gate_ref[qs, hd * N_BRANCH + 0:hd * N_BRANCH + 1]
            gsl = gate_ref[qs, hd * N_BRANCH + 1:hd * N_BRANCH + 2]
            gw = gate_ref[qs, hd * N_BRANCH + 2:hd * N_BRANCH + 3]
            o_scr[qs, hd * HEAD_DIM:(hd + 1) * HEAD_DIM] = gc * o_c[b][rs] + gsl * o_s[b][rs] + gw * o_w[b][rs]

    o_ref[...] = _rms(o_scr[...]) * nao_ref[...]


S_SEQS = 2


def _sample_attention(page_table, r_all, r_new, q, gates, slc_cache, slc_new, win_state, win_new, cosc, sinc, e_mat,
                      norm_ao, *, n_new, page, wbuf):
    n_seq, n_pages = page_table.shape
    past = n_pages * page
    chunks_pp = page // CMP_STRIDE
    n_cmp = n_pages * chunks_pp
    assert n_cmp == HEAD_DIM and n_new == 8 and past % SEL_BLOCK == 0 and n_new <= SEL_BLOCK
    assert (past + n_new - 1) // SEL_BLOCK == past // SEL_BLOCK and wbuf == WINDOW and past >= WINDOW
    n_keys = past + HEAD_DIM
    n_wkeys = wbuf + HEAD_DIM
    r_w = r_all.shape[1]

    ns = S_SEQS
    assert n_seq % ns == 0
    page_map = lambda sq, p: (lambda b, pt: (pt[b * ns + sq, p], 0))
    in_specs = [pl.BlockSpec((chunks_pp, r_w), page_map(sq, p)) for sq in range(ns) for p in range(n_pages)]
    in_specs += [pl.BlockSpec((page * KV_ROWS, HEAD_DIM), page_map(sq, p)) for sq in range(ns) for p in range(n_pages)]
    seq_rows = lambda wdt: pl.BlockSpec((ns * n_new, wdt), lambda b, pt: (b, 0))
    kv_rows = lambda n_tok: pl.BlockSpec((ns * n_tok * KV_ROWS, HEAD_DIM), lambda b, pt: (b, 0))
    const = lambda shape: pl.BlockSpec(shape, lambda b, pt: (0,) * len(shape))
    in_specs += [pl.BlockSpec((ns, 1, r_w), lambda b, pt: (b, 0, 0)), seq_rows(ATT_W), seq_rows(GATE_PAD),
                 kv_rows(n_new), kv_rows(wbuf), kv_rows(n_new),
                 const((n_cmp, HEAD_DIM)), const((n_cmp, HEAD_DIM)), const((HEAD_DIM, past)), const((1, ATT_W))]
    grid_spec = pltpu.PrefetchScalarGridSpec(
        num_scalar_prefetch=1,
        grid=(n_seq // ns,),
        in_specs=in_specs,
        out_specs=[seq_rows(ATT_W), kv_rows(wbuf)],
        scratch_shapes=[pltpu.VMEM((ns, n_cmp + 8, r_w), F32), pltpu.VMEM((ns * N_KV, n_keys, HEAD_DIM), BF16),
                        pltpu.VMEM((ns * N_KV, n_keys, HEAD_DIM), BF16), pltpu.VMEM((ns * N_KV, n_wkeys, HEAD_DIM), BF16),
                        pltpu.VMEM((ns * N_KV, n_wkeys, HEAD_DIM), BF16), pltpu.VMEM((ns * n_new, ATT_W), F32)],
    )
    return pl.pallas_call(
        functools.partial(_sattn_kernel, n_seqs=ns, n_pages=n_pages, page=page, n_new=n_new, past=past, wbuf=wbuf),
        grid_spec=grid_spec,
        out_shape=[jax.ShapeDtypeStruct((n_seq * n_new, ATT_W), F32),
                   jax.ShapeDtypeStruct((n_seq * wbuf * KV_ROWS, HEAD_DIM), F32)],
        compiler_params=_cparams("arbitrary"),
        name="sample_attn",
    )(page_table, *([r_all] * (ns * n_pages)), *([slc_cache] * (ns * n_pages)), r_new, q, gates, slc_new, win_state,
      win_new, cosc, sinc, e_mat, norm_ao)


def _outproj_kernel(x_ref, oa_ref, oc_ref, gt_ref, w_ref, o_ref, *, rep):
    y = _dot(oa_ref[...].astype(BF16), w_ref[0:ATT_W, :]) + _dot(oc_ref[...].astype(BF16), w_ref[ATT_W:D_MODEL, :])
    o_ref[...] = x_ref[...] + (1.0 + _mod_rows(gt_ref, rep)) * y


def _outproj(x, oa, oc, gate, w, tm=512):
    t = x.shape[0]
    mod_spec, rep = _mod_spec(gate.shape[0], t, tm, 1)
    rows = lambda wdt: pl.BlockSpec((tm, wdt), lambda i: (i, 0))
    return pl.pallas_call(
        functools.partial(_outproj_kernel, rep=rep),
        grid=(t // tm,),
        in_specs=[rows(D_MODEL), rows(ATT_W), rows(CONV_W), mod_spec, pl.BlockSpec(memory_space=pltpu.VMEM)],
        out_specs=rows(D_MODEL),
        out_shape=jax.ShapeDtypeStruct((t, D_MODEL), F32),
        compiler_params=_cparams("arbitrary"),
        name="outproj",
    )(x, oa, oc, gate, w)


def _rope_tables(pos):
    half = HEAD_DIM // 2
    inv = np.float32(ROPE_THETA) ** (-np.arange(half, dtype=np.float32) * np.float32(2.0) / np.float32(HEAD_DIM))
    ang = np.asarray(pos, np.float32)[:, None] * inv[None, :]
    cos, sin = np.cos(ang), np.sin(ang)
    return (jnp.asarray(np.concatenate([cos, cos], axis=1), F32),
            jnp.asarray(np.concatenate([-sin, sin], axis=1), F32))


def _pack_w_in(w_in):
    off_g = ATT_W + 3 * SLAB_W
    off_c = off_g + N_HEADS * N_BRANCH
    gate_cols = jnp.pad(w_in[:, off_g:off_c], ((0, 0), (0, GATE_PAD - N_HEADS * N_BRANCH)))
    return (w_in[:, 0:off_g].astype(BF16), w_in[:, off_c:off_c + 3 * CONV_W].astype(BF16), gate_cols.astype(BF16))


def _pack_w_cmp(w_ck, w_cv, n_j):
    def one(w):
        lo = w[0:n_j].reshape(n_j * HEAD_DIM, HEAD_DIM)
        hi = w[CMP_STRIDE:CMP_STRIDE + n_j].reshape(n_j * HEAD_DIM, HEAD_DIM)
        return jnp.concatenate([lo, hi], axis=1)
    return jnp.stack([one(w_ck), one(w_cv)]).astype(BF16)


def kernel(x_prompt, x_sample, c_prompt, c_sample, cache_cmp_kv, cache_slc_kv, state_win_kv, state_conv, page_table,
           w_ada, b_ada, norm_ffn1, ffn1_gate, ffn1_up, ffn1_down, norm_mix, w_in, w_cmp_k, w_cmp_v, conv_w, conv_b,
           norm_att_out, norm_conv_out, w_out, norm_ffn2, ffn2_gate, ffn2_up, ffn2_down, norm_final):
    n_p, s_len, _ = x_prompt.shape
    n_seq, n_new, _ = x_sample.shape
    depth = w_ada.shape[0]
    assert n_p == 1 and depth == 1
    n_pages = page_table.shape[1]
    page = cache_slc_kv.shape[2]
    n_phys = cache_slc_kv.shape[1]
    past = n_pages * page
    wbuf = state_win_kv.shape[2]
    keep_p = min(WINDOW, s_len)
    t_s = n_seq * n_new
    l = 0

    c_all = jnp.concatenate([c_sample, c_prompt, jnp.zeros((8 - n_p, D_MODEL), F32)], axis=0)
    mod = _ada(c_all, w_ada[l], b_ada[l])
    mod_p = [mod[n_seq:n_seq + 1, k * D_MODEL:(k + 1) * D_MODEL] for k in range(N_MOD)]
    mod_s = [mod[0:n_seq, k * D_MODEL:(k + 1) * D_MODEL] for k in range(N_MOD)]

    row = lambda v: v.reshape(1, -1)
    w_proj = _pack_w_in(w_in[l])
    w_o = w_out[l].astype(BF16)
    nfin = row(norm_final)

    xp = x_prompt.reshape(s_len, D_MODEL)
    xs = x_sample.reshape(t_s, D_MODEL)

    xs, *f1 = _ffn_stream(xs, mod_s[0], mod_s[1], mod_s[2], row(norm_ffn1[l]), nfin, ffn1_gate[l], ffn1_up[l],
                          ffn1_down[l], final_norm=False)
    xp = _ffn(xp, mod_p[0], mod_p[1], mod_p[2], row(norm_ffn1[l]), nfin, *f1, final_norm=False)

    cos_p, sin_p = _rope_tables(np.arange(s_len))
    tm_s = 256
    cos_s, sin_s = _rope_tables(np.tile(past + np.arange(n_new), tm_s // n_new))
    zero8 = jnp.zeros((8, CONV_W), F32)
    conv_args = (conv_w[l], row(conv_b[l]), row(norm_conv_out[l]))
    (qt_p, cmp_p, slc_p, win_p, ksb_p, vst_p, kwb_p, vwt_p, gatet_p, ocn_p, utail_p) = _proj(
        xp, mod_p[3], mod_p[4], row(norm_mix[l]), w_proj, cos_p, sin_p, zero8, zero8, *conv_args,
        carry=True, seq_rows=s_len)
    prev1 = jnp.repeat(state_conv[l][:, CONV_K - 2], n_new, axis=0)
    prev2 = jnp.repeat(state_conv[l][:, CONV_K - 3], n_new, axis=0)
    (q_s, cmp_s, slc_s, win_s, gate_s, ocn_s, u_s) = _proj(
        xs, mod_s[3], mod_s[4], row(norm_mix[l]), w_proj, cos_s, sin_s, prev1, prev2, *conv_args,
        carry=False, seq_rows=n_new, tm=tm_s)

    n_j = CMP_STRIDE
    w_c = _pack_w_cmp(w_cmp_k[l], w_cmp_v[l], n_j)
    lin = lambda a: a.reshape(-1, HEAD_DIM)
    r_p = _compress_products(cmp_p, w_c, n_j, tm=256)
    r_cache = _compress_products(lin(cache_cmp_kv), w_c, n_j, tm=256)
    r_new = _compress_products(cmp_s, _pack_w_cmp(w_cmp_k[l], w_cmp_v[l], n_new), n_new, tm=n_seq)

    n_cmp_pad = s_len // CMP_STRIDE
    cosc, sinc = _rope_tables(np.arange(n_cmp_pad) * CMP_STRIDE + (CMP_BLOCK - 1))
    r_p = jnp.pad(r_p, ((0, 8), (0, 0)))
    n_blocks = s_len // SEL_BLOCK
    e_t = ((jnp.arange(s_len) // SEL_BLOCK)[:, None] == jnp.arange(n_blocks)[None, :]).astype(BF16)
    j_w, ti_w = jnp.arange(P_WKEYS)[:, None], jnp.arange(Q_BLOCK)[None, :]
    band = jnp.where((j_w > ti_w) & (j_w <= ti_w + WINDOW), 0.0, NEG).astype(F32)
    oa_p = _prompt_attention_t(qt_p, gatet_p, r_p, cosc, sinc, ksb_p, vst_p, kwb_p, vwt_p, e_t, band,
                               row(norm_att_out[l]))

    n_cmp_s = past // CMP_STRIDE
    cosc_s, sinc_s = _rope_tables(np.arange(n_cmp_s) * CMP_STRIDE + (CMP_BLOCK - 1))
    e_s =(jnp.arange(HEAD_DIM)[:, None] == (jnp.arange(past) // SEL_BLOCK * CHUNKS_PER_SEL)[None, :]).astype(BF16)
    oa_s, win_new_state = _sample_attention(
        page_table, r_cache, r_new.reshape(n_seq, 1, -1), q_s, gate_s,
        lin(cache_slc_kv), slc_s, lin(state_win_kv), win_s,
        cosc_s, sinc_s, e_s, row(norm_att_out[l]), n_new=n_new, page=page, wbuf=wbuf)

    xp = _outproj(xp, oa_p, ocn_p, mod_p[5], w_o)
    xs = _outproj(xs, oa_s, ocn_s, mod_s[5], w_o)
    ys, *f2 = _ffn_stream(xs, mod_s[6], mod_s[7], mod_s[8], row(norm_ffn2[l]), nfin, ffn2_gate[l], ffn2_up[l],
                          ffn2_down[l], final_norm=True)
    yp = _ffn(xp, mod_p[6], mod_p[7], mod_p[8], row(norm_ffn2[l]), nfin, *f2, final_norm=True)

    kv6 = lambda a, n, s: a.reshape(1, n, s, 2, N_KV, HEAD_DIM)
    return (yp.reshape(n_p, s_len, D_MODEL), ys.reshape(n_seq, n_new, D_MODEL),
            kv6(cmp_p, n_p, s_len), kv6(slc_p, n_p, s_len), kv6(win_p[(s_len - keep_p) * KV_ROWS:], n_p, keep_p),
            utail_p[8 - (CONV_K - 1):].reshape(1, n_p, CONV_K - 1, CONV_W),
            kv6(cmp_s, n_seq, n_new), kv6(slc_s, n_seq, n_new), kv6(win_new_state, n_seq, wbuf),
            u_s.reshape(n_seq, n_new, CONV_W)[:, n_new - (CONV_K - 1):].reshape(1, n_seq, CONV_K - 1, CONV_W))
```

```python
import functools

import jax
import jax.numpy as jnp
import numpy as np
from jax import lax
from jax.experimental import pallas as pl
from jax.experimental.pallas import tpu as pltpu

F32 = jnp.float32
BF16 = jnp.bfloat16

D_MODEL = 2048
HEAD_DIM = 128
N_HEADS = 8
N_KV = 2
HEADS_PER_KV = N_HEADS // N_KV
ATT_W = N_HEADS * HEAD_DIM
KV_W = N_KV * HEAD_DIM
CONV_W = D_MODEL - ATT_W
CONV_K = 3
CMP_BLOCK = 32
CMP_STRIDE = 16
SEL_BLOCK = 64
N_SEL = 16
WINDOW = 512
Q_BLOCK = 128
N_BRANCH = 3
N_MOD = 9
ROPE_THETA = 10000.0
EPS = 1e-6
NEG = -1e30
TINY = 1e-30
SCALE = HEAD_DIM ** -0.5
LOG2E = 1.4426950408889634
SLAB_W = 2 * KV_W
KV_ROWS = 2 * N_KV
GATE_PAD = 128
SUBLANES = 8
CHUNKS_PER_SEL = SEL_BLOCK // CMP_STRIDE

VMEM_LIMIT = 56 * 1024 * 1024


def _cparams(*sem):
    return pltpu.CompilerParams(dimension_semantics=sem, vmem_limit_bytes=VMEM_LIMIT)


def _dot(a, b):
    return jnp.dot(a, b, preferred_element_type=F32)


def _dot_nt(a, b):
    return lax.dot_general(a, b, (((1,), (1,)), ((), ())), preferred_element_type=F32)


def _rms(x):
    return x * lax.rsqrt(jnp.mean(x * x, axis=-1, keepdims=True) + EPS)


def _silu(x):
    return x * jax.nn.sigmoid(x)


def _rope(x, cos, sin_signed):
    return x * cos + pltpu.roll(x, HEAD_DIM // 2, axis=1) * sin_signed


def _masked_softmax(s, mask):
    s = jnp.where(mask, s, NEG)
    m = jnp.max(s, axis=-1, keepdims=True)
    e = jnp.where(mask, jnp.exp2(s - m), 0.0)
    return e / jnp.maximum(jnp.sum(e, axis=-1, keepdims=True), TINY)


def _mod_rows(ref, rep):
    m = ref[...]
    return m if rep == 1 else jnp.repeat(m, rep, axis=0)


def _mod_spec(mrows, t, tm, n_grid_axes):
    rep = 1 if mrows == 1 else t // mrows
    shape = (1, D_MODEL) if mrows == 1 else (tm // rep, D_MODEL)
    first = (lambda i: 0) if mrows == 1 else (lambda i: i)
    index = (lambda i: (first(i), 0)) if n_grid_axes == 1 else (lambda i, j: (first(i), 0))
    return pl.BlockSpec(shape, index), rep


def _ada_kernel(c_ref, w_ref, b_ref, o_ref):
    a = _silu(c_ref[...]).astype(BF16)
    o_ref[...] = _dot(a, w_ref[...].astype(BF16)) + b_ref[...]


def _ada(c, w, b, tn=1024):
    m, n = c.shape[0], w.shape[1]
    return pl.pallas_call(
        _ada_kernel,
        grid=(n // tn,),
        in_specs=[pl.BlockSpec((m, D_MODEL), lambda j: (0, 0)),
                  pl.BlockSpec((D_MODEL, tn), lambda j: (0, j)),
                  pl.BlockSpec((1, tn), lambda j: (0, j))],
        out_specs=pl.BlockSpec((m, tn), lambda j: (0, j)),
        out_shape=jax.ShapeDtypeStruct((m, n), F32),
        compiler_params=_cparams("arbitrary"),
        name="ada",
    )(c, w, b.reshape(1, n))


def _ffn_kernel(x_ref, sh_ref, sc_ref, gt_ref, ng_ref, nf_ref, wg_ref, wu_ref, wd_ref, o_ref, h_scr, *, n_f, final_norm,
                rep):
    j = pl.program_id(1)
    mod = lambda ref: _mod_rows(ref, rep)

    @pl.when(j == 0)
    def _():
        h = _rms(x_ref[...]) * (ng_ref[...] * (1.0 + mod(sc_ref))) + mod(sh_ref)
        h_scr[...] = h.astype(BF16)
        o_ref[...] = jnp.zeros_like(o_ref)

    h = h_scr[...]
    a = (_silu(_dot(h, wg_ref[...])) * _dot(h, wu_ref[...])).astype(BF16)
    o_ref[...] += _dot(a, wd_ref[...])

    @pl.when(j == n_f - 1)
    def _():
        out = x_ref[...] + 0.5 * (1.0 + mod(gt_ref)) * o_ref[...]
        if final_norm:
            out = _rms(out) * nf_ref[...]
        o_ref[...] = out


def _ffn(x, shift, scale, gate, norm_g, norm_final, wg, wu, wd, *, final_norm, tm=512, tf=512):
    t = x.shape[0]
    d_ff = wg.shape[1]
    n_f = d_ff // tf
    mod_spec, rep = _mod_spec(shift.shape[0], t, tm, 2)
    row_spec = pl.BlockSpec((tm, D_MODEL), lambda i, j: (i, 0))
    vec_spec = pl.BlockSpec((1, D_MODEL), lambda i, j: (0, 0))
    return pl.pallas_call(
        functools.partial(_ffn_kernel, n_f=n_f, final_norm=final_norm, rep=rep),
        grid=(t // tm, n_f),
        in_specs=[row_spec, mod_spec, mod_spec, mod_spec, vec_spec, vec_spec,
                  pl.BlockSpec((D_MODEL, tf), lambda i, j: (0, j)),
                  pl.BlockSpec((D_MODEL, tf), lambda i, j: (0, j)),
                  pl.BlockSpec((tf, D_MODEL), lambda i, j: (j, 0))],
        out_specs=row_spec,
        out_shape=jax.ShapeDtypeStruct((t, D_MODEL), F32),
        scratch_shapes=[pltpu.VMEM((tm, D_MODEL), BF16)],
        compiler_params=_cparams("arbitrary", "arbitrary"),
        name="ffn",
    )(x, shift, scale, gate, norm_g, norm_final, wg, wu, wd)


FFN_ROW_CHUNK = 128


def _ffn_stream_kernel(x_ref, sh_ref, sc_ref, gt_ref, ng_ref, nf_ref, wg_ref, wu_ref, wd_ref,
                       o_ref, wgb_ref, wub_ref, wdb_ref, h_scr, *, n_f, final_norm, rep):
    j = pl.program_id(0)
    t = x_ref.shape[0]
    rc = FFN_ROW_CHUNK

    def rows_of(ref, c, width):
        return jnp.repeat(ref[pl.ds(pl.multiple_of(c * (width // rep), width // rep), width // rep), :], rep, axis=0)

    @pl.when(j == 0)
    def _():
        def body(c, _):
            r0 = pl.multiple_of(c * rc, rc)
            h = _rms(x_ref[pl.ds(r0, rc), :]) * ng_ref[...] * (1.0 + rows_of(sc_ref, c, rc)) + rows_of(sh_ref, c, rc)
            h_scr[pl.ds(r0, rc), :] = h.astype(BF16)
            return 0
        lax.fori_loop(0, t // rc, body, 0)
        o_ref[...] = jnp.zeros_like(o_ref)

    wg, wu, wd = wg_ref[...].astype(BF16), wu_ref[...].astype(BF16), wd_ref[...].astype(BF16)
    wgb_ref[...] = wg
    wub_ref[...] = wu
    wdb_ref[...] = wd
    h = h_scr[...]
    a = (_silu(_dot(h, wg)) * _dot(h, wu)).astype(BF16)
    o_ref[...] += _dot(a, wd)

    @pl.when(j == n_f - 1)
    def _():
        def body(c, _):
            r0 = pl.multiple_of(c * rc, rc)
            out = x_ref[pl.ds(r0, rc), :] + 0.5 * (1.0 + rows_of(gt_ref, c, rc)) * o_ref[pl.ds(r0, rc), :]
            if final_norm:
                out = _rms(out) * nf_ref[...]
            o_ref[pl.ds(r0, rc), :] = out
            return 0
        lax.fori_loop(0, t // rc, body, 0)


def _ffn_stream(x, shift, scale, gate, norm_g, norm_final, wg, wu, wd, *, final_norm, tf=256):
    t = x.shape[0]
    d_ff = wg.shape[1]
    n_f = d_ff // tf
    rep = t // shift.shape[0]
    assert t % FFN_ROW_CHUNK == 0 and FFN_ROW_CHUNK % rep == 0
    vmem = pl.BlockSpec(memory_space=pltpu.VMEM)
    col_tile = pl.BlockSpec((D_MODEL, tf), lambda j: (0, j))
    row_tile = pl.BlockSpec((tf, D_MODEL), lambda j: (j, 0))
    return pl.pallas_call(
        functools.partial(_ffn_stream_kernel, n_f=n_f, final_norm=final_norm, rep=rep),
        grid=(n_f,),
        in_specs=[vmem, vmem, vmem, vmem, vmem, vmem, col_tile, col_tile, row_tile],
        out_specs=[pl.BlockSpec((t, D_MODEL), lambda j: (0, 0)), col_tile, col_tile, row_tile],
        out_shape=[jax.ShapeDtypeStruct((t, D_MODEL), F32), jax.ShapeDtypeStruct(wg.shape, BF16),
                   jax.ShapeDtypeStruct(wu.shape, BF16), jax.ShapeDtypeStruct(wd.shape, BF16)],
        scratch_shapes=[pltpu.VMEM((t, D_MODEL), BF16)],
        compiler_params=_cparams("arbitrary"),
        name="ffn_stream",
    )(x, shift, scale, gate, norm_g, norm_final, wg, wu, wd)


def _proj_kernel(x_ref, sh_ref, sc_ref, ng_ref, wa_ref, wc_ref, wg_ref, cos_ref, sin_ref, p1_ref, p2_ref, cw_ref,
                 cb_ref, nco_ref, *rest, tm, seq_rows, carry, rep):
    if carry:
        q_ref, cmp_ref, slc_ref, win_ref, ksb_ref, vst_ref, kwb_ref, vwt_ref, gate_ref, ocn_ref, u_ref, carry_scr = rest
    else:
        q_ref, cmp_ref, slc_ref, win_ref, gate_ref, ocn_ref, u_ref, carry_scr = rest
        ksb_ref = kwb_ref = None
    i = pl.program_id(0)
    h = (_rms(x_ref[...]) * ng_ref[...] * (1.0 + _mod_rows(sc_ref, rep)) + _mod_rows(sh_ref, rep)).astype(BF16)
    cos, sin = cos_ref[...], sin_ref[...]

    pq = _dot(h, wa_ref[:, 0:ATT_W])
    for hd in range(N_HEADS):
        hs = slice(hd * HEAD_DIM, (hd + 1) * HEAD_DIM)
        blk = _rope(pq[:, hs], cos, sin) * (SCALE * LOG2E)
        if carry:
            q_ref[hs, :] = blk.T.astype(q_ref.dtype)
        else:
            q_ref[:, hs] = blk.astype(q_ref.dtype)

    pkv = _dot(h, wa_ref[:, ATT_W:ATT_W + 3 * SLAB_W])
    for slab, (o_ref, kb_ref) in enumerate(((cmp_ref, None), (slc_ref, ksb_ref), (win_ref, kwb_ref))):
        base = slab * SLAB_W
        for g in range(N_KV):
            gs = slice(g * HEAD_DIM, (g + 1) * HEAD_DIM)
            k = pkv[:, base + g * HEAD_DIM:base + (g + 1) * HEAD_DIM]
            v = pkv[:, base + KV_W + g * HEAD_DIM:base + KV_W + (g + 1) * HEAD_DIM]
            if slab > 0:
                k = _rope(k, cos, sin)
            if kb_ref is not None:
                kb_ref[:, gs] = k.astype(BF16)
                v_t = v.T.astype(BF16)
                if slab == 1:
                    vst_ref[0, gs, :] = v_t
                else:
                    for blk in range(tm // Q_BLOCK):
                        vwt_ref[blk, gs, :] = v_t[:, blk * Q_BLOCK:(blk + 1) * Q_BLOCK]
            o_ref[pl.ds(g, tm, stride=KV_ROWS), :] = k
            o_ref[pl.ds(N_KV + g, tm, stride=KV_ROWS), :] = v

    gate = jax.nn.sigmoid(_dot(h, wg_ref[...]))
    gate_ref[...] = gate.T if carry else gate

    pc = _dot(h, wc_ref[...])
    u = pc[:, 0:CONV_W] * pc[:, 2 * CONV_W:3 * CONV_W]
    c_out = pc[:, CONV_W:2 * CONV_W]
    row = lax.broadcasted_iota(jnp.int32, (tm, CONV_W), 0)
    if carry:
        @pl.when(i == 0)
        def _():
            carry_scr[...] = jnp.zeros_like(carry_scr)
        prev1 = carry_scr[7:8, :]
        prev2 = carry_scr[6:7, :]
        rs = row
    else:
        prev1 = p1_ref[...]
        prev2 = p2_ref[...]
        rs = row & (seq_rows - 1)
    um1 = jnp.where(rs >= 1, pltpu.roll(u, 1, axis=0), prev1)
    um2 = jnp.where(rs >= 2, pltpu.roll(u, 2, axis=0), jnp.where(rs == 1, prev1, prev2))
    y = um2 * cw_ref[0:1, :] + um1 * cw_ref[1:2, :] + u * cw_ref[2:3, :] + cb_ref[...]
    ocn_ref[...] = (_rms(c_out * y) * nco_ref[...]).astype(ocn_ref.dtype)
    if carry:
        carry_scr[...] = u[tm - 8:tm, :]
        u_ref[...] = u[tm - 8:tm, :]
    else:
        u_ref[...] = u


def _proj(x, shift, scale, norm_g, w_groups, cos, sin, prev1, prev2, conv_w, conv_b, norm_co, *, carry, seq_rows, tm=256):
    t = x.shape[0]
    mod_spec, rep = _mod_spec(shift.shape[0], t, tm, 1)
    rows = lambda wdt: pl.BlockSpec((tm, wdt), lambda i: (i, 0))
    cols = lambda h: pl.BlockSpec((h, tm), lambda i: (0, i))
    const = lambda r, wdt: pl.BlockSpec((r, wdt), lambda i: (0, 0))
    vmem = pl.BlockSpec(memory_space=pltpu.VMEM)
    sds = jax.ShapeDtypeStruct
    lin = (sds((t * KV_ROWS, HEAD_DIM), F32), pl.BlockSpec((tm * KV_ROWS, HEAD_DIM), lambda i: (i, 0)))
    if carry:
        per_ck = P_CK // tm
        kvb = (sds((t, KV_W), BF16), rows(KV_W))
        outs = [(sds((ATT_W, t), BF16), cols(ATT_W)), lin, lin, lin, kvb,
                (sds((t // P_CK, KV_W, P_CK), BF16), pl.BlockSpec((1, KV_W, tm), lambda i: (i // per_ck, 0, i % per_ck))),
                kvb,
                (sds((t // Q_BLOCK, KV_W, Q_BLOCK), BF16), pl.BlockSpec((tm // Q_BLOCK, KV_W, Q_BLOCK), lambda i: (i, 0, 0))),
                (sds((GATE_PAD, t), F32), cols(GATE_PAD)), (sds((t, CONV_W), BF16), rows(CONV_W)),
                (sds((8, CONV_W), F32), const(8, CONV_W))]
    else:
        outs = [(sds((t, ATT_W), F32), rows(ATT_W)), lin, lin, lin, (sds((t, GATE_PAD), F32), rows(GATE_PAD)),
                (sds((t, CONV_W), F32), rows(CONV_W)), (sds((t, CONV_W), F32), rows(CONV_W))]
    tab_spec = rows(HEAD_DIM) if carry else const(tm, HEAD_DIM)
    prev_spec = const(8, CONV_W) if carry else rows(CONV_W)
    return pl.pallas_call(
        functools.partial(_proj_kernel, tm=tm, seq_rows=seq_rows, carry=carry, rep=rep),
        grid=(t // tm,),
        in_specs=[rows(D_MODEL), mod_spec, mod_spec, const(1, D_MODEL), vmem, vmem, vmem,
                  tab_spec, tab_spec, prev_spec, prev_spec, const(CONV_K, CONV_W), const(1, CONV_W),
                  const(1, CONV_W)],
        out_specs=[spec for _, spec in outs],
        out_shape=[shape for shape, _ in outs],
        scratch_shapes=[pltpu.VMEM((8, CONV_W), F32)],
        compiler_params=_cparams("arbitrary"),
        name="proj",
    )(x, shift, scale, norm_g, *w_groups, cos, sin, prev1, prev2, conv_w, conv_b, norm_co)


def _cmp_kernel(x_ref, w_ref, o_ref, pad_scr, *, n_j, tm):
    pitch = n_j * KV_ROWS
    ppitch = pitch + SUBLANES

    def repitch(c, _):
        src = pl.multiple_of(c * pitch, SUBLANES)
        dst = pl.multiple_of(c * ppitch, SUBLANES)
        pad_scr[pl.ds(dst, pitch), :] = x_ref[pl.ds(src, pitch), :]
        return 0

    lax.fori_loop(0, tm, repitch, 0, unroll=8)
    flat = pad_scr
    for kv in range(2):
        for g in range(N_KV):
            xs = jnp.concatenate([flat[pl.ds(j * KV_ROWS + kv * N_KV + g, tm, stride=ppitch), :] for j in range(n_j)],
                                 axis=1)
            col = (kv * N_KV + g) * 2 * HEAD_DIM
            o_ref[:, col:col + 2 * HEAD_DIM] = _dot(xs.astype(BF16), w_ref[kv])


def _compress_products(x, w, n_j, tm):
    pitch = n_j * KV_ROWS
    m = x.shape[0] // pitch
    n_out = 2 * N_KV * 2 * HEAD_DIM
    tm = min(tm, m)
    return pl.pallas_call(
        functools.partial(_cmp_kernel, n_j=n_j, tm=tm),
        grid=(m // tm,),
        in_specs=[pl.BlockSpec((tm * pitch, HEAD_DIM), lambda i: (i, 0)),
                  pl.BlockSpec((2, n_j * HEAD_DIM, 2 * HEAD_DIM), lambda i: (0, 0, 0))],
        out_specs=pl.BlockSpec((tm, n_out), lambda i: (i, 0)),
        out_shape=jax.ShapeDtypeStruct((m, n_out), F32),
        scratch_shapes=[pltpu.VMEM((tm * (pitch + SUBLANES), HEAD_DIM), F32)],
        compiler_params=_cparams("arbitrary"),
        name="compress",
    )(x, w)


def _r_cols(kv, g):
    return (kv * N_KV + g) * 2 * HEAD_DIM


P_CK = 512
P_WKEYS = WINDOW + Q_BLOCK
P_TRIP = 4
N_FORCED = 3


def _pattn_t_kernel(qt_ref, gatet_ref, rp_ref, cosc_ref, sinc_ref, ks_ref, vst_ref, kw_ref, vwt_ref, et_ref, band_ref,
                    nao_ref, o_ref, kc_scr, vct_scr, pt_scr, ot_scr, s_scr, *, n_cmp_pad, n_blocks):
    i = pl.program_id(0)
    s0 = i * Q_BLOCK
    nq = Q_BLOCK
    rows = HEADS_PER_KV * nq
    lanes4 = lambda a: jnp.concatenate([a] * HEADS_PER_KV, axis=1)

    @pl.when(i == 0)
    def _():
        for g in range(N_KV):
            gs = slice(g * HEAD_DIM, (g + 1) * HEAD_DIM)
            ck, cv = _r_cols(0, g), _r_cols(1, g)
            kc = rp_ref[0:n_cmp_pad, ck:ck + HEAD_DIM] + rp_ref[pl.ds(1, n_cmp_pad), ck + HEAD_DIM:ck + 2 * HEAD_DIM]
            kc_scr[:, gs] = _rope(kc, cosc_ref[...], sinc_ref[...]).astype(BF16)
            vc = rp_ref[0:n_cmp_pad, cv:cv + HEAD_DIM] + rp_ref[pl.ds(1, n_cmp_pad), cv + HEAD_DIM:cv + 2 * HEAD_DIM]
            vct_scr[gs, :] = vc.T.astype(BF16)
        pt_scr[...] = jnp.zeros_like(pt_scr)

    t_q = s0 + lax.broadcasted_iota(jnp.int32, (n_cmp_pad, nq), 1)
    c_end = lax.broadcasted_iota(jnp.int32, (n_cmp_pad, nq), 0) * CMP_STRIDE + (CMP_BLOCK - 1)
    bias_c = lanes4(jnp.where(c_end <= t_q, 0.0, NEG))
    j_io = lax.broadcasted_iota(jnp.int32, (P_WKEYS, nq), 0)
    bias_w = lanes4(jnp.where(j_io >= WINDOW - s0, band_ref[...], NEG))

    blocks = [jnp.maximum(i + k - WINDOW // nq, 0) for k in range(P_WKEYS // nq)]
    scores, rhs, s_cmp, s_win = [], [], [], []
    for g in range(N_KV):
        gs = slice(g * HEAD_DIM, (g + 1) * HEAD_DIM)
        q_t = jnp.concatenate([qt_ref[(g * HEADS_PER_KV + r) * HEAD_DIM:(g * HEADS_PER_KV + r + 1) * HEAD_DIM, :]
                               for r in range(HEADS_PER_KV)], axis=1)
        rhs.append(q_t)
        s_cmp.append(_dot(kc_scr[:, gs], q_t))
        k_w = jnp.concatenate([kw_ref[pl.ds(pl.multiple_of(b * nq, nq), nq), gs] for b in blocks], axis=0)
        s_win.append(_dot(k_w, q_t))

    for g in range(N_KV):
        gs = slice(g * HEAD_DIM, (g + 1) * HEAD_DIM)
        s = s_cmp[g] + bias_c
        m = jnp.max(s, axis=0, keepdims=True)
        e = jnp.exp2(s - m)
        inv = jnp.where(m > 0.5 * NEG, 1.0 / jnp.maximum(jnp.sum(e, axis=0, keepdims=True), TINY), 0.0)
        o_c = _dot(vct_scr[gs, :], e.astype(BF16)) * inv

        p = e * inv
        pt_scr[8:8 + n_cmp_pad, :] = sum(p[:, r * nq:(r + 1) * nq] for r in range(HEADS_PER_KV))
        st = lambda k: pt_scr[pl.ds(7 + k, n_blocks, stride=CHUNKS_PER_SEL), :]
        score = 0.5 * st(0) + st(1) + st(2) + st(3) + 0.5 * st(4)
        b_io = lax.broadcasted_iota(jnp.int32, (n_blocks, nq), 0)
        t_lane = s0 + lax.broadcasted_iota(jnp.int32, (n_blocks, nq), 1)
        cur = t_lane >> 6
        forced = (b_io == 0) | (b_io == cur) | (b_io == cur - 1)
        valid = b_io * SEL_BLOCK <= t_lane
        scores.append((jnp.where(valid & jnp.logical_not(forced), score, -jnp.inf), jnp.where(forced, 1.0, 0.0)))

        v_wt = jnp.concatenate([vwt_ref[b, gs, :] for b in blocks], axis=1)
        s = s_win[g] + bias_w
        e = jnp.exp2(s - jnp.max(s, axis=0, keepdims=True))
        o_w = _dot(v_wt, e.astype(BF16)) / jnp.maximum(jnp.sum(e, axis=0, keepdims=True), TINY)

        for r in range(HEADS_PER_KV):
            hd = g * HEADS_PER_KV + r
            ls = slice(r * nq, (r + 1) * nq)
            gc = gatet_ref[hd * N_BRANCH + 0:hd * N_BRANCH + 1, :]
            gw = gatet_ref[hd * N_BRANCH + 2:hd * N_BRANCH + 3, :]
            ot_scr[hd * HEAD_DIM:(hd + 1) * HEAD_DIM, :] = gc * o_c[:, ls] + gw * o_w[:, ls]

    b_f = lax.broadcasted_iota(jnp.int32, (n_blocks, nq), 0).astype(F32)

    def pick(_, c):
        out = []
        for work, sel in c:
            m = jnp.max(work, axis=0, keepdims=True)
            idx = jnp.min(jnp.where(work == m, b_f, float(n_blocks)), axis=0, keepdims=True)
            hit = b_f == idx
            out.append((jnp.where(hit, -jnp.inf, work), jnp.where(hit, 1.0, sel)))
        return tuple(out)

    picked = lax.fori_loop(0, min(N_SEL, n_blocks) - N_FORCED, pick, tuple(scores))

    for g in range(N_KV):
        sel_bias = jnp.where(picked[g][1] > 0.5, 0.0, NEG).astype(BF16)
        rhs[g] = jnp.concatenate([rhs[g], lanes4(sel_bias)], axis=0)

    def qk_scores(c, slot):
        k0 = pl.multiple_of(c * P_CK, P_CK)
        blk_hot = et_ref[pl.ds(k0, P_CK), :]
        for g in range(N_KV):
            gs = slice(g * HEAD_DIM, (g + 1) * HEAD_DIM)
            s_scr[slot, g] = _dot(jnp.concatenate([ks_ref[pl.ds(k0, P_CK), gs], blk_hot], axis=1), rhs[g])

    ones_rows = jnp.ones((2 * SUBLANES, P_CK), BF16)

    def softmax_pv(c, slot, carry, causal):
        if causal:
            key = c * P_CK + lax.broadcasted_iota(jnp.int32, (P_CK, rows), 0)
            t_k = s0 + (lax.broadcasted_iota(jnp.int32, (P_CK, rows), 1) & (nq - 1))
            cb = jnp.where(key <= t_k, 0.0, NEG)
        out = []
        for g in range(N_KV):
            gs = slice(g * HEAD_DIM, (g + 1) * HEAD_DIM)
            m_i, l_i, acc = carry[g]
            s = s_scr[slot, g]
            if causal:
                s = s + cb
            m_n = jnp.maximum(m_i, jnp.max(s, axis=0, keepdims=True))
            p = jnp.exp2(s - m_n)
            alpha = jnp.exp2(m_i - m_n)
            pv = _dot(jnp.concatenate([vst_ref[c, gs, :], ones_rows], axis=0), p.astype(BF16))
            out.append((m_n, alpha * l_i + pv[HEAD_DIM:HEAD_DIM + 1], alpha * acc + pv[0:HEAD_DIM]))
        return tuple(out)

    def ahead(c, slot, carry):
        qk_scores(c + 1, 1 - slot)
        return softmax_pv(c, slot, carry, False)

    def trip(tr, carry):
        for k in range(P_TRIP):
            carry = ahead(P_TRIP * tr + k, k & 1, carry)
        return carry

    last = (s0 + nq + P_CK - 1) // P_CK - 1
    n_trips = last // P_TRIP
    init = tuple((jnp.full((1, rows), NEG, F32), jnp.zeros((1, rows), F32), jnp.zeros((HEAD_DIM, rows), F32))
                 for _ in range(N_KV))
    qk_scores(0, 0)
    carry = lax.fori_loop(0, n_trips, trip, init)
    done = P_TRIP * n_trips
    for k in range(P_TRIP - 1):
        carry = lax.cond(last - done > k, lambda cr, k=k: ahead(done + k, k & 1, cr), lambda cr: cr, carry)
    carry = lax.cond(((last - done) & 1) == 1, lambda cr: softmax_pv(last, 1, cr, True),
                     lambda cr: softmax_pv(last, 0, cr, True), carry)
    for g in range(N_KV):
        _, l_s, acc = carry[g]
        o_st = acc / jnp.maximum(l_s, TINY)
        for r in range(HEADS_PER_KV):
            hd = g * HEADS_PER_KV + r
            gsl = gatet_ref[hd * N_BRANCH + 1:hd * N_BRANCH + 2, :]
            ot_scr[hd * HEAD_DIM:(hd + 1) * HEAD_DIM, :] += gsl * o_st[:, r * nq:(r + 1) * nq]

    o_t = ot_scr[...]
    o_t = o_t * lax.rsqrt(jnp.mean(o_t * o_t, axis=0, keepdims=True) + EPS)
    o_ref[...] = (o_t.T * nao_ref[...]).astype(o_ref.dtype)


def _prompt_attention_t(q_t, gates_t, rp, cosc, sinc, ks, vs_t, kw, vw_t, e_t, band, norm_ao):
    s_len = q_t.shape[1]
    n_cmp_pad = s_len // CMP_STRIDE
    n_blocks = s_len // SEL_BLOCK
    vmem = pl.BlockSpec(memory_space=pltpu.VMEM)
    cols = lambda h: pl.BlockSpec((h, Q_BLOCK), lambda i: (0, i))
    return pl.pallas_call(
        functools.partial(_pattn_t_kernel, n_cmp_pad=n_cmp_pad, n_blocks=n_blocks),
        grid=(s_len // Q_BLOCK,),
        in_specs=[cols(ATT_W), cols(GATE_PAD), vmem, vmem, vmem, vmem, vmem, vmem, vmem, vmem, vmem,
                  pl.BlockSpec((1, ATT_W), lambda i: (0, 0))],
        out_specs=pl.BlockSpec((Q_BLOCK, ATT_W), lambda i: (i, 0)),
        out_shape=jax.ShapeDtypeStruct((s_len, ATT_W), BF16),
        scratch_shapes=[pltpu.VMEM((n_cmp_pad, KV_W), BF16), pltpu.VMEM((KV_W, n_cmp_pad), BF16),
                        pltpu.VMEM((n_cmp_pad + 16, Q_BLOCK), F32), pltpu.VMEM((ATT_W, Q_BLOCK), F32),
                        pltpu.VMEM((2, N_KV, P_CK, HEADS_PER_KV * Q_BLOCK), F32)],
        compiler_params=_cparams("arbitrary"),
        name="prompt_attn",
    )(q_t, gates_t, rp, cosc, sinc, ks, vs_t, kw, vw_t, e_t, band, norm_ao)


def _sattn_kernel(pt_ref, *refs, n_seqs, n_pages, page, n_new, past, wbuf):
    del pt_ref
    n_pg = n_seqs * n_pages
    r_pages, s_pages = refs[0:n_pg], refs[n_pg:2 * n_pg]
    (rnew_ref, q_ref, gate_ref, snew_ref, wst_ref, wnew_ref, cosc_ref, sinc_ref, e_ref, nao_ref,
     o_ref, wout_ref, r_scr, k_scr, v_scr, kw_scr, vw_scr, o_scr) = refs[2 * n_pg:]
    chunks_pp = page // CMP_STRIDE
    n_cmp = n_pages * chunks_pp
    rows = HEADS_PER_KV * n_new
    n_keys = k_scr.shape[1]
    n_wkeys = kw_scr.shape[1]
    pad = n_keys - past
    wpad = n_wkeys - wbuf
    batch = [(sq, g) for sq in range(n_seqs) for g in range(N_KV)]
    cat = lambda parts: jnp.concatenate(parts, axis=0)

    def with_zero_rows(new_rows, n_zero):
        return cat([new_rows, jnp.zeros((n_zero, HEAD_DIM), F32)]).astype(BF16)

    for sq in range(n_seqs):
        n0 = sq * n_new * KV_ROWS
        w0 = sq * wbuf * KV_ROWS
        for p in range(n_pages):
            r_scr[sq, p * chunks_pp:(p + 1) * chunks_pp, :] = r_pages[sq * n_pages + p][...]
            pg = s_pages[sq * n_pages + p]
            for g in range(N_KV):
                b = sq * N_KV + g
                k_scr[b, p * page:(p + 1) * page, :] = pg[pl.ds(g, page, stride=KV_ROWS), :].astype(BF16)
                v_scr[b, p * page:(p + 1) * page, :] = pg[pl.ds(N_KV + g, page, stride=KV_ROWS), :].astype(BF16)
        r_scr[sq, n_cmp:n_cmp + 8, :] = cat([rnew_ref[sq], jnp.zeros((7, r_scr.shape[2]), F32)])
        for g in range(N_KV):
            b = sq * N_KV + g
            k_scr[b, past:n_keys, :] = with_zero_rows(snew_ref[pl.ds(n0 + g, n_new, stride=KV_ROWS), :], pad - n_new)
            v_scr[b, past:n_keys, :] = with_zero_rows(snew_ref[pl.ds(n0 + N_KV + g, n_new, stride=KV_ROWS), :],
                                                      pad - n_new)
            kw_scr[b, 0:wbuf, :] = wst_ref[pl.ds(w0 + g, wbuf, stride=KV_ROWS), :].astype(BF16)
            vw_scr[b, 0:wbuf, :] = wst_ref[pl.ds(w0 + N_KV + g, wbuf, stride=KV_ROWS), :].astype(BF16)
            kw_scr[b, wbuf:n_wkeys, :] = with_zero_rows(wnew_ref[pl.ds(n0 + g, n_new, stride=KV_ROWS), :],
                                                        wpad - n_new)
            vw_scr[b, wbuf:n_wkeys, :] = with_zero_rows(wnew_ref[pl.ds(n0 + N_KV + g, n_new, stride=KV_ROWS), :],
                                                        wpad - n_new)
        keep = (wbuf - n_new) * KV_ROWS
        wout_ref[w0:w0 + keep, :] = wst_ref[w0 + n_new * KV_ROWS:w0 + wbuf * KV_ROWS, :]
        wout_ref[w0 + keep:w0 + wbuf * KV_ROWS, :] = wnew_ref[n0:n0 + n_new * KV_ROWS, :]

    n_b = len(batch)
    all_rows = n_b * rows
    tok = lax.broadcasted_iota(jnp.int32, (all_rows, 1), 0) & (n_new - 1)
    t_rows = past + tok

    qg, kc, vc = [], [], []
    for sq, g in batch:
        qs = slice(sq * n_new, (sq + 1) * n_new)
        qg.append(cat([q_ref[qs, (g * HEADS_PER_KV + r) * HEAD_DIM:(g * HEADS_PER_KV + r + 1) * HEAD_DIM]
                       for r in range(HEADS_PER_KV)]).astype(BF16))
        ck, cv = _r_cols(0, g), _r_cols(1, g)
        kc.append(r_scr[sq, 0:n_cmp, ck:ck + HEAD_DIM] + r_scr[sq, pl.ds(1, n_cmp), ck + HEAD_DIM:ck + 2 * HEAD_DIM])
        vc.append((r_scr[sq, 0:n_cmp, cv:cv + HEAD_DIM]
                   + r_scr[sq, pl.ds(1, n_cmp), cv + HEAD_DIM:cv + 2 * HEAD_DIM]).astype(BF16))
    cos_all, sin_all = cat([cosc_ref[...]] * n_b), cat([sinc_ref[...]] * n_b)
    kc_all = _rope(cat(kc), cos_all, sin_all).astype(BF16)
    rb = lambda b: slice(b * rows, (b + 1) * rows)

    s_c = cat([_dot_nt(qg[b], kc_all[b * n_cmp:(b + 1) * n_cmp]) for b in range(n_b)])
    s_w = cat([_dot_nt(qg[b], kw_scr[b]) for b in range(n_b)])
    s_s = cat([_dot_nt(qg[b], k_scr[b]) for b in range(n_b)])

    c_end = lax.broadcasted_iota(jnp.int32, (all_rows, n_cmp), 1) * CMP_STRIDE + (CMP_BLOCK - 1)
    p_c = _masked_softmax(s_c, c_end <= t_rows)
    p_cb = p_c.astype(BF16)
    o_c = [_dot(p_cb[rb(b)], vc[b]) for b in range(n_b)]

    sel_rows = n_b * n_new
    p_grp = cat([sum(p_c[b * rows + r * n_new:b * rows + (r + 1) * n_new] for r in range(HEADS_PER_KV))
                 for b in range(n_b)])
    lane = lax.broadcasted_iota(jnp.int32, (sel_rows, HEAD_DIM), 1)
    pch = 0.5 * (p_grp + jnp.where(lane >= 1, pltpu.roll(p_grp, 1, axis=1), 0.0))
    score = pch
    for k in range(1, CHUNKS_PER_SEL):
        score = score + pltpu.roll(pch, HEAD_DIM - k, axis=1)
    blk = lane >> 2
    t_tok = past + (lax.broadcasted_iota(jnp.int32, (sel_rows, HEAD_DIM), 0) & (n_new - 1))
    cur = t_tok >> 6
    forced = (blk == 0) | (blk == cur) | (blk == cur - 1)
    score = jnp.where(forced, jnp.inf, score)
    n_pb = past // SEL_BLOCK
    ahead = jnp.zeros((sel_rows, HEAD_DIM), F32)
    for k in range(1, n_pb):
        other = pltpu.roll(score, CHUNKS_PER_SEL * k, axis=1)
        wins = (other > score) | ((other == score) & (blk >= k))
        ahead = ahead + jnp.where(wins, 1.0, 0.0)
    sel = jnp.where(((lane & (CHUNKS_PER_SEL - 1)) == 0) & (ahead < N_SEL - 1), 1.0, 0.0).astype(BF16)
    sel_keys = _dot(sel, e_ref[...])
    new_ok = (lax.broadcasted_iota(jnp.int32, (sel_rows, pad), 1)
              <= (lax.broadcasted_iota(jnp.int32, (sel_rows, pad), 0) & (n_new - 1)))
    bias = jnp.concatenate([jnp.where(sel_keys > 0.5, 0.0, NEG), jnp.where(new_ok, 0.0, NEG)], axis=1)
    bias = cat([bias[b * n_new:(b + 1) * n_new] for b in range(n_b) for _ in range(HEADS_PER_KV)])

    s_s = s_s + bias
    e_s = jnp.exp2(s_s - jnp.max(s_s, axis=-1, keepdims=True))
    p_s = (e_s / jnp.maximum(jnp.sum(e_s, axis=-1, keepdims=True), TINY)).astype(BF16)
    o_s = [_dot(p_s[rb(b)], v_scr[b]) for b in range(n_b)]

    j_io = lax.broadcasted_iota(jnp.int32, (all_rows, n_wkeys), 1)
    m_w = (((j_io < wbuf) & (j_io > tok + (wbuf - WINDOW)) & (j_io >= wbuf - past))
           | ((j_io >= wbuf) & (j_io - wbuf <= tok)))
    p_w = _masked_softmax(s_w, m_w).astype(BF16)
    o_w = [_dot(p_w[rb(b)], vw_scr[b]) for b in range(n_b)]

    for b, (sq, g) in enumerate(batch):
        qs = slice(sq * n_new, (sq + 1) * n_new)
        for r in range(HEADS_PER_KV):
            hd = g * HEADS_PER_KV + r
            rs = slice(r * n_new, (r + 1) * n_new)
            gc = gate_ref[qs, hd * N_BRANCH + 0:hd * N_BRANCH + 1]
            gsl = gate_ref[qs, hd * N_BRANCH + 1:hd * N_BRANCH + 2]
            gw = gate_ref[qs, hd * N_BRANCH + 2:hd * N_BRANCH + 3]
            o_scr[qs, hd * HEAD_DIM:(hd + 1) * HEAD_DIM] = gc * o_c[b][rs] + gsl * o_s[b][rs] + gw * o_w[b][rs]

    o_ref[...] = _rms(o_scr[...]) * nao_ref[...]


S_SEQS = 2


def _sample_attention(page_table, r_all, r_new, q, gates, slc_cache, slc_new, win_state, win_new, cosc, sinc, e_mat,
                      norm_ao, *, n_new, page, wbuf):
    n_seq, n_pages = page_table.shape
    past = n_pages * page
    chunks_pp = page // CMP_STRIDE
    n_cmp = n_pages * chunks_pp
    assert n_cmp == HEAD_DIM and n_new == 8 and past % SEL_BLOCK == 0 and n_new <= SEL_BLOCK
    assert (past + n_new - 1) // SEL_BLOCK == past // SEL_BLOCK and wbuf == WINDOW and past >= WINDOW
    n_keys = past + HEAD_DIM
    n_wkeys = wbuf + HEAD_DIM
    r_w = r_all.shape[1]

    ns = S_SEQS
    assert n_seq % ns == 0
    page_map = lambda sq, p: (lambda b, pt: (pt[b * ns + sq, p], 0))
    in_specs = [pl.BlockSpec((chunks_pp, r_w), page_map(sq, p)) for sq in range(ns) for p in range(n_pages)]
    in_specs += [pl.BlockSpec((page * KV_ROWS, HEAD_DIM), page_map(sq, p)) for sq in range(ns) for p in range(n_pages)]
    seq_rows = lambda wdt: pl.BlockSpec((ns * n_new, wdt), lambda b, pt: (b, 0))
    kv_rows = lambda n_tok: pl.BlockSpec((ns * n_tok * KV_ROWS, HEAD_DIM), lambda b, pt: (b, 0))
    const = lambda shape: pl.BlockSpec(shape, lambda b, pt: (0,) * len(shape))
    in_specs += [pl.BlockSpec((ns, 1, r_w), lambda b, pt: (b, 0, 0)), seq_rows(ATT_W), seq_rows(GATE_PAD),
                 kv_rows(n_new), kv_rows(wbuf), kv_rows(n_new),
                 const((n_cmp, HEAD_DIM)), const((n_cmp, HEAD_DIM)), const((HEAD_DIM, past)), const((1, ATT_W))]
    grid_spec = pltpu.PrefetchScalarGridSpec(
        num_scalar_prefetch=1,
        grid=(n_seq // ns,),
        in_specs=in_specs,
        out_specs=[seq_rows(ATT_W), kv_rows(wbuf)],
        scratch_shapes=[pltpu.VMEM((ns, n_cmp + 8, r_w), F32), pltpu.VMEM((ns * N_KV, n_keys, HEAD_DIM), BF16),
                        pltpu.VMEM((ns * N_KV, n_keys, HEAD_DIM), BF16), pltpu.VMEM((ns * N_KV, n_wkeys, HEAD_DIM), BF16),
                        pltpu.VMEM((ns * N_KV, n_wkeys, HEAD_DIM), BF16), pltpu.VMEM((ns * n_new, ATT_W), F32)],
    )
    return pl.pallas_call(
        functools.partial(_sattn_kernel, n_seqs=ns, n_pages=n_pages, page=page, n_new=n_new, past=past, wbuf=wbuf),
        grid_spec=grid_spec,
        out_shape=[jax.ShapeDtypeStruct((n_seq * n_new, ATT_W), F32),
                   jax.ShapeDtypeStruct((n_seq * wbuf * KV_ROWS, HEAD_DIM), F32)],
        compiler_params=_cparams("arbitrary"),
        name="sample_attn",
    )(page_table, *([r_all] * (ns * n_pages)), *([slc_cache] * (ns * n_pages)), r_new, q, gates, slc_new, win_state,
      win_new, cosc, sinc, e_mat, norm_ao)


def _outproj_kernel(x_ref, oa_ref, oc_ref, gt_ref, w_ref, o_ref, *, rep):
    y = _dot(oa_ref[...].astype(BF16), w_ref[0:ATT_W, :]) + _dot(oc_ref[...].astype(BF16), w_ref[ATT_W:D_MODEL, :])
    o_ref[...] = x_ref[...] + (1.0 + _mod_rows(gt_ref, rep)) * y


def _outproj(x, oa, oc, gate, w, tm=512):
    t = x.shape[0]
    mod_spec, rep = _mod_spec(gate.shape[0], t, tm, 1)
    rows = lambda wdt: pl.BlockSpec((tm, wdt), lambda i: (i, 0))
    return pl.pallas_call(
        functools.partial(_outproj_kernel, rep=rep),
        grid=(t // tm,),
        in_specs=[rows(D_MODEL), rows(ATT_W), rows(CONV_W), mod_spec, pl.BlockSpec(memory_space=pltpu.VMEM)],
        out_specs=rows(D_MODEL),
        out_shape=jax.ShapeDtypeStruct((t, D_MODEL), F32),
        compiler_params=_cparams("arbitrary"),
        name="outproj",
    )(x, oa, oc, gate, w)


def _rope_tables(pos):
    half = HEAD_DIM // 2
    inv = np.float32(ROPE_THETA) ** (-np.arange(half, dtype=np.float32) * np.float32(2.0) / np.float32(HEAD_DIM))
    ang = np.asarray(pos, np.float32)[:, None] * inv[None, :]
    cos, sin = np.cos(ang), np.sin(ang)
    return (jnp.asarray(np.concatenate([cos, cos], axis=1), F32),
            jnp.asarray(np.concatenate([-sin, sin], axis=1), F32))


def _pack_w_in(w_in):
    off_g = ATT_W + 3 * SLAB_W
    off_c = off_g + N_HEADS * N_BRANCH
    gate_cols = jnp.pad(w_in[:, off_g:off_c], ((0, 0), (0, GATE_PAD - N_HEADS * N_BRANCH)))
    return (w_in[:, 0:off_g].astype(BF16), w_in[:, off_c:off_c + 3 * CONV_W].astype(BF16), gate_cols.astype(BF16))


def _pack_w_cmp(w_ck, w_cv, n_j):
    def one(w):
        lo = w[0:n_j].reshape(n_j * HEAD_DIM, HEAD_DIM)
        hi = w[CMP_STRIDE:CMP_STRIDE + n_j].reshape(n_j * HEAD_DIM, HEAD_DIM)
        return jnp.concatenate([lo, hi], axis=1)
    return jnp.stack([one(w_ck), one(w_cv)]).astype(BF16)


def kernel(x_prompt, x_sample, c_prompt, c_sample, cache_cmp_kv, cache_slc_kv, state_win_kv, state_conv, page_table,
           w_ada, b_ada, norm_ffn1, ffn1_gate, ffn1_up, ffn1_down, norm_mix, w_in, w_cmp_k, w_cmp_v, conv_w, conv_b,
           norm_att_out, norm_conv_out, w_out, norm_ffn2, ffn2_gate, ffn2_up, ffn2_down, norm_final):
    n_p, s_len, _ = x_prompt.shape
    n_seq, n_new, _ = x_sample.shape
    depth = w_ada.shape[0]
    assert n_p == 1 and depth == 1
    n_pages = page_table.shape[1]
    page = cache_slc_kv.shape[2]
    n_phys = cache_slc_kv.shape[1]
    past = n_pages * page
    wbuf = state_win_kv.shape[2]
    keep_p = min(WINDOW, s_len)
    t_s = n_seq * n_new
    l = 0

    c_all = jnp.concatenate([c_sample, c_prompt, jnp.zeros((8 - n_p, D_MODEL), F32)], axis=0)
    mod = _ada(c_all, w_ada[l], b_ada[l])
    mod_p = [mod[n_seq:n_seq + 1, k * D_MODEL:(k + 1) * D_MODEL] for k in range(N_MOD)]
    mod_s = [mod[0:n_seq, k * D_MODEL:(k + 1) * D_MODEL] for k in range(N_MOD)]

    row = lambda v: v.reshape(1, -1)
    w_proj = _pack_w_in(w_in[l])
    w_o = w_out[l].astype(BF16)
    nfin = row(norm_final)

    xp = x_prompt.reshape(s_len, D_MODEL)
    xs = x_sample.reshape(t_s, D_MODEL)

    xs, *f1 = _ffn_stream(xs, mod_s[0], mod_s[1], mod_s[2], row(norm_ffn1[l]), nfin, ffn1_gate[l], ffn1_up[l],
                          ffn1_down[l], final_norm=False)
    xp = _ffn(xp, mod_p[0], mod_p[1], mod_p[2], row(norm_ffn1[l]), nfin, *f1, final_norm=False)

    cos_p, sin_p = _rope_tables(np.arange(s_len))
    tm_s = 256
    cos_s, sin_s = _rope_tables(np.tile(past + np.arange(n_new), tm_s // n_new))
    zero8 = jnp.zeros((8, CONV_W), F32)
    conv_args = (conv_w[l], row(conv_b[l]), row(norm_conv_out[l]))
    (qt_p, cmp_p, slc_p, win_p, ksb_p, vst_p, kwb_p, vwt_p, gatet_p, ocn_p, utail_p) = _proj(
        xp, mod_p[3], mod_p[4], row(norm_mix[l]), w_proj, cos_p, sin_p, zero8, zero8, *conv_args,
        carry=True, seq_rows=s_len)
    prev1 = jnp.repeat(state_conv[l][:, CONV_K - 2], n_new, axis=0)
    prev2 = jnp.repeat(state_conv[l][:, CONV_K - 3], n_new, axis=0)
    (q_s, cmp_s, slc_s, win_s, gate_s, ocn_s, u_s) = _proj(
        xs, mod_s[3], mod_s[4], row(norm_mix[l]), w_proj, cos_s, sin_s, prev1, prev2, *conv_args,
        carry=False, seq_rows=n_new, tm=tm_s)

    n_j = CMP_STRIDE
    w_c = _pack_w_cmp(w_cmp_k[l], w_cmp_v[l], n_j)
    lin = lambda a: a.reshape(-1, HEAD_DIM)
    r_p = _compress_products(cmp_p, w_c, n_j, tm=256)
    r_cache = _compress_products(lin(cache_cmp_kv), w_c, n_j, tm=256)
    r_new = _compress_products(cmp_s, _pack_w_cmp(w_cmp_k[l], w_cmp_v[l], n_new), n_new, tm=n_seq)

    n_cmp_pad = s_len // CMP_STRIDE
    cosc, sinc = _rope_tables(np.arange(n_cmp_pad) * CMP_STRIDE + (CMP_BLOCK - 1))
    r_p = jnp.pad(r_p, ((0, 8), (0, 0)))
    n_blocks = s_len // SEL_BLOCK
    e_t = ((jnp.arange(s_len) // SEL_BLOCK)[:, None] == jnp.arange(n_blocks)[None, :]).astype(BF16)
    j_w, ti_w = jnp.arange(P_WKEYS)[:, None], jnp.arange(Q_BLOCK)[None, :]
    band = jnp.where((j_w > ti_w) & (j_w <= ti_w + WINDOW), 0.0, NEG).astype(F32)
    oa_p = _prompt_attention_t(qt_p, gatet_p, r_p, cosc, sinc, ksb_p, vst_p, kwb_p, vwt_p, e_t, band,
                               row(norm_att_out[l]))

    n_cmp_s = past // CMP_STRIDE
    cosc_s, sinc_s = _rope_tables(np.arange(n_cmp_s) * CMP_STRIDE + (CMP_BLOCK - 1))
    e_s =(jnp.arange(HEAD_DIM)[:, None] == (jnp.arange(past) // SEL_BLOCK * CHUNKS_PER_SEL)[None, :]).astype(BF16)
    oa_s, win_new_state = _sample_attention(
        page_table, r_cache, r_new.reshape(n_seq, 1, -1), q_s, gate_s,
        lin(cache_slc_kv), slc_s, lin(state_win_kv), win_s,
        cosc_s, sinc_s, e_s, row(norm_att_out[l]), n_new=n_new, page=page, wbuf=wbuf)

    xp = _outproj(xp, oa_p, ocn_p, mod_p[5], w_o)
    xs = _outproj(xs, oa_s, ocn_s, mod_s[5], w_o)
    ys, *f2 = _ffn_stream(xs, mod_s[6], mod_s[7], mod_s[8], row(norm_ffn2[l]), nfin, ffn2_gate[l], ffn2_up[l],
                          ffn2_down[l], final_norm=True)
    yp = _ffn(xp, mod_p[6], mod_p[7], mod_p[8], row(norm_ffn2[l]), nfin, *f2, final_norm=True)

    kv6 = lambda a, n, s: a.reshape(1, n, s, 2, N_KV, HEAD_DIM)
    return (yp.reshape(n_p, s_len, D_MODEL), ys.reshape(n_seq, n_new, D_MODEL),
            kv6(cmp_p, n_p, s_len), kv6(slc_p, n_p, s_len), kv6(win_p[(s_len - keep_p) * KV_ROWS:], n_p, keep_p),
            utail_p[8 - (CONV_K - 1):].reshape(1, n_p, CONV_K - 1, CONV_W),
            kv6(cmp_s, n_seq, n_new), kv6(slc_s, n_seq, n_new), kv6(win_new_state, n_seq, wbuf),
            u_s.reshape(n_seq, n_new, CONV_W)[:, n_new - (CONV_K - 1):].reshape(1, n_seq, CONV_K - 1, CONV_W))
```

```python
import functools

import jax
import jax.numpy as jnp
import numpy as np
from jax import lax
from jax.experimental import pallas as pl
from jax.experimental.pallas import tpu as pltpu

F32 = jnp.float32
BF16 = jnp.bfloat16

D_MODEL = 2048
HEAD_DIM = 128
N_HEADS = 8
N_KV = 2
HEADS_PER_KV = N_HEADS // N_KV
ATT_W = N_HEADS * HEAD_DIM
KV_W = N_KV * HEAD_DIM
CONV_W = D_MODEL - ATT_W
CONV_K = 3
CMP_BLOCK = 32
CMP_STRIDE = 16
SEL_BLOCK = 64
N_SEL = 16
WINDOW = 512
Q_BLOCK = 128
N_BRANCH = 3
N_MOD = 9
ROPE_THETA = 10000.0
EPS = 1e-6
NEG = -1e30
TINY = 1e-30
SCALE = HEAD_DIM ** -0.5
LOG2E = 1.4426950408889634
SLAB_W = 2 * KV_W
KV_ROWS = 2 * N_KV
GATE_PAD = 128
SUBLANES = 8
CHUNKS_PER_SEL = SEL_BLOCK // CMP_STRIDE
SEL_SHIFT = SEL_BLOCK.bit_length() - 1

VMEM_LIMIT = 56 * 1024 * 1024


def _cparams(*sem):
    return pltpu.CompilerParams(dimension_semantics=sem, vmem_limit_bytes=VMEM_LIMIT)


def _dot(a, b):
    return jnp.dot(a, b, preferred_element_type=F32)


def _dot_nt(a, b):
    return lax.dot_general(a, b, (((1,), (1,)), ((), ())), preferred_element_type=F32)


def _rms(x):
    return x * lax.rsqrt(jnp.mean(x * x, axis=-1, keepdims=True) + EPS)


def _silu(x):
    return x * jax.nn.sigmoid(x)


def _rope(x, cos, sin_signed):
    return x * cos + pltpu.roll(x, HEAD_DIM // 2, axis=1) * sin_signed


def _masked_softmax(s, mask):
    s = jnp.where(mask, s, NEG)
    m = jnp.max(s, axis=-1, keepdims=True)
    e = jnp.where(mask, jnp.exp2(s - m), 0.0)
    return e / jnp.maximum(jnp.sum(e, axis=-1, keepdims=True), TINY)


def _mod_rows(ref, rep):
    m = ref[...]
    return m if rep == 1 else jnp.repeat(m, rep, axis=0)


def _mod_spec(mrows, t, tm, n_grid_axes):
    rep = 1 if mrows == 1 else t // mrows
    shape = (1, D_MODEL) if mrows == 1 else (tm // rep, D_MODEL)
    first = (lambda i: 0) if mrows == 1 else (lambda i: i)
    index = (lambda i: (first(i), 0)) if n_grid_axes == 1 else (lambda i, j: (first(i), 0))
    return pl.BlockSpec(shape, index), rep


def _ada_kernel(c_ref, w_ref, b_ref, o_ref):
    a = _silu(c_ref[...]).astype(BF16)
    o_ref[...] = _dot(a, w_ref[...].astype(BF16)) + b_ref[...]


def _ada(c, w, b, tn=1024):
    m, n = c.shape[0], w.shape[1]
    return pl.pallas_call(
        _ada_kernel,
        grid=(n // tn,),
        in_specs=[pl.BlockSpec((m, D_MODEL), lambda j: (0, 0)),
                  pl.BlockSpec((D_MODEL, tn), lambda j: (0, j)),
                  pl.BlockSpec((1, tn), lambda j: (0, j))],
        out_specs=pl.BlockSpec((m, tn), lambda j: (0, j)),
        out_shape=jax.ShapeDtypeStruct((m, n), F32),
        compiler_params=_cparams("arbitrary"),
        name="ada",
    )(c, w, b.reshape(1, n))


def _ffn_kernel(x_ref, sh_ref, sc_ref, gt_ref, ng_ref, nf_ref, wg_ref, wu_ref, wd_ref, o_ref, h_scr, *, n_f, final_norm,
                rep):
    j = pl.program_id(1)
    mod = lambda ref: _mod_rows(ref, rep)

    @pl.when(j == 0)
    def _():
        h = _rms(x_ref[...]) * (ng_ref[...] * (1.0 + mod(sc_ref))) + mod(sh_ref)
        h_scr[...] = h.astype(BF16)
        o_ref[...] = jnp.zeros_like(o_ref)

    h = h_scr[...]
    a = (_silu(_dot(h, wg_ref[...])) * _dot(h, wu_ref[...])).astype(BF16)
    o_ref[...] += _dot(a, wd_ref[...])

    @pl.when(j == n_f - 1)
    def _():
        out = x_ref[...] + (0.5 * (1.0 + mod(gt_ref))) * o_ref[...]
        if final_norm:
            out = _rms(out) * nf_ref[...]
        o_ref[...] = out


def _ffn(x, shift, scale, gate, norm_g, norm_final, wg, wu, wd, *, final_norm, tm=512, tf=512):
    t = x.shape[0]
    d_ff = wg.shape[1]
    n_f = d_ff // tf
    mod_spec, rep = _mod_spec(shift.shape[0], t, tm, 2)
    row_spec = pl.BlockSpec((tm, D_MODEL), lambda i, j: (i, 0))
    vec_spec = pl.BlockSpec((1, D_MODEL), lambda i, j: (0, 0))
    return pl.pallas_call(
        functools.partial(_ffn_kernel, n_f=n_f, final_norm=final_norm, rep=rep),
        grid=(t // tm, n_f),
        in_specs=[row_spec, mod_spec, mod_spec, mod_spec, vec_spec, vec_spec,
                  pl.BlockSpec((D_MODEL, tf), lambda i, j: (0, j)),
                  pl.BlockSpec((D_MODEL, tf), lambda i, j: (0, j)),
                  pl.BlockSpec((tf, D_MODEL), lambda i, j: (j, 0))],
        out_specs=row_spec,
        out_shape=jax.ShapeDtypeStruct((t, D_MODEL), F32),
        scratch_shapes=[pltpu.VMEM((tm, D_MODEL), BF16)],
        compiler_params=_cparams("arbitrary", "arbitrary"),
        name="ffn",
    )(x, shift, scale, gate, norm_g, norm_final, wg, wu, wd)


FFN_ROW_CHUNK = 128


def _ffn_stream_kernel(x_ref, sh_ref, sc_ref, gt_ref, ng_ref, nf_ref, wg_ref, wu_ref, wd_ref,
                       o_ref, wgb_ref, wub_ref, wdb_ref, h_scr, *, n_f, final_norm, rep):
    j = pl.program_id(0)
    t = x_ref.shape[0]
    rc = FFN_ROW_CHUNK

    def rows_of(ref, c, width):
        return jnp.repeat(ref[pl.ds(pl.multiple_of(c * (width // rep), width // rep), width // rep), :], rep, axis=0)

    @pl.when(j == 0)
    def _():
        def body(c, _):
            r0 = pl.multiple_of(c * rc, rc)
            h = _rms(x_ref[pl.ds(r0, rc), :]) * ng_ref[...] * (1.0 + rows_of(sc_ref, c, rc)) + rows_of(sh_ref, c, rc)
            h_scr[pl.ds(r0, rc), :] = h.astype(BF16)
            return 0
        lax.fori_loop(0, t // rc, body, 0)
        o_ref[...] = jnp.zeros_like(o_ref)

    wg, wu, wd = wg_ref[...].astype(BF16), wu_ref[...].astype(BF16), wd_ref[...].astype(BF16)
    wgb_ref[...] = wg
    wub_ref[...] = wu
    wdb_ref[...] = wd
    h = h_scr[...]
    a = (_silu(_dot(h, wg)) * _dot(h, wu)).astype(BF16)
    o_ref[...] += _dot(a, wd)

    @pl.when(j == n_f - 1)
    def _():
        def body(c, _):
            r0 = pl.multiple_of(c * rc, rc)
            out = x_ref[pl.ds(r0, rc), :] + 0.5 * (1.0 + rows_of(gt_ref, c, rc)) * o_ref[pl.ds(r0, rc), :]
            if final_norm:
                out = _rms(out) * nf_ref[...]
            o_ref[pl.ds(r0, rc), :] = out
            return 0
        lax.fori_loop(0, t // rc, body, 0)


def _ffn_stream(x, shift, scale, gate, norm_g, norm_final, wg, wu, wd, *, final_norm, tf=256):
    t = x.shape[0]
    d_ff = wg.shape[1]
    n_f = d_ff // tf
    rep = t // shift.shape[0]
    assert t % FFN_ROW_CHUNK == 0 and FFN_ROW_CHUNK % rep == 0
    vmem = pl.BlockSpec(memory_space=pltpu.VMEM)
    col_tile = pl.BlockSpec((D_MODEL, tf), lambda j: (0, j))
    row_tile = pl.BlockSpec((tf, D_MODEL), lambda j: (j, 0))
    return pl.pallas_call(
        functools.partial(_ffn_stream_kernel, n_f=n_f, final_norm=final_norm, rep=rep),
        grid=(n_f,),
        in_specs=[vmem, vmem, vmem, vmem, vmem, vmem, col_tile, col_tile, row_tile],
        out_specs=[pl.BlockSpec((t, D_MODEL), lambda j: (0, 0)), col_tile, col_tile, row_tile],
        out_shape=[jax.ShapeDtypeStruct((t, D_MODEL), F32), jax.ShapeDtypeStruct(wg.shape, BF16),
                   jax.ShapeDtypeStruct(wu.shape, BF16), jax.ShapeDtypeStruct(wd.shape, BF16)],
        scratch_shapes=[pltpu.VMEM((t, D_MODEL), BF16)],
        compiler_params=_cparams("arbitrary"),
        name="ffn_stream",
    )(x, shift, scale, gate, norm_g, norm_final, wg, wu, wd)


def _proj_kernel(x_ref, sh_ref, sc_ref, ng_ref, wa_ref, wc_ref, wg_ref, cos_ref, sin_ref, p1_ref, p2_ref, cw_ref,
                 cb_ref, nco_ref, *rest, tm, seq_rows, carry, rep):
    if carry:
        q_ref, cmp_ref, slc_ref, win_ref, ksb_ref, vst_ref, kwb_ref, vwt_ref, gate_ref, ocn_ref, u_ref, carry_scr = rest
    else:
        q_ref, cmp_ref, slc_ref, win_ref, gate_ref, ocn_ref, u_ref, carry_scr = rest
        ksb_ref = kwb_ref = None
    i = pl.program_id(0)
    h = (_rms(x_ref[...]) * ng_ref[...] * (1.0 + _mod_rows(sc_ref, rep)) + _mod_rows(sh_ref, rep)).astype(BF16)
    cos, sin = cos_ref[...], sin_ref[...]

    pq = _dot(h, wa_ref[:, 0:ATT_W])
    for hd in range(N_HEADS):
        hs = slice(hd * HEAD_DIM, (hd + 1) * HEAD_DIM)
        blk = _rope(pq[:, hs], cos, sin) * (SCALE * LOG2E)
        if carry:
            q_ref[hs, :] = blk.T.astype(q_ref.dtype)
        else:
            q_ref[:, hs] = blk.astype(q_ref.dtype)

    pkv = _dot(h, wa_ref[:, ATT_W:ATT_W + 3 * SLAB_W])
    for slab, (o_ref, kb_ref) in enumerate(((cmp_ref, None), (slc_ref, ksb_ref), (win_ref, kwb_ref))):
        base = slab * SLAB_W
        for g in range(N_KV):
            gs = slice(g * HEAD_DIM, (g + 1) * HEAD_DIM)
            k = pkv[:, base + g * HEAD_DIM:base + (g + 1) * HEAD_DIM]
            v = pkv[:, base + KV_W + g * HEAD_DIM:base + KV_W + (g + 1) * HEAD_DIM]
            if slab > 0:
                k = _rope(k, cos, sin)
            if kb_ref is not None:
                kb_ref[:, gs] = k.astype(BF16)
                v_t = v.T.astype(BF16)
                if slab == 1:
                    vst_ref[0, gs, :] = v_t
                else:
                    for blk in range(tm // Q_BLOCK):
                        vwt_ref[blk, gs, :] = v_t[:, blk * Q_BLOCK:(blk + 1) * Q_BLOCK]
            o_ref[pl.ds(g, tm, stride=KV_ROWS), :] = k
            o_ref[pl.ds(N_KV + g, tm, stride=KV_ROWS), :] = v

    gate = jax.nn.sigmoid(_dot(h, wg_ref[...]))
    gate_ref[...] = gate.T if carry else gate

    pc = _dot(h, wc_ref[...])
    u = pc[:, 0:CONV_W] * pc[:, 2 * CONV_W:3 * CONV_W]
    c_out = pc[:, CONV_W:2 * CONV_W]
    row = lax.broadcasted_iota(jnp.int32, (tm, CONV_W), 0)
    if carry:
        @pl.when(i == 0)
        def _():
            carry_scr[...] = jnp.zeros_like(carry_scr)
        prev1 = carry_scr[7:8, :]
        prev2 = carry_scr[6:7, :]
        rs = row
    else:
        prev1 = p1_ref[...]
        prev2 = p2_ref[...]
        rs = row & (seq_rows - 1)
    um1 = jnp.where(rs >= 1, pltpu.roll(u, 1, axis=0), prev1)
    um2 = jnp.where(rs >= 2, pltpu.roll(u, 2, axis=0), jnp.where(rs == 1, prev1, prev2))
    y = um2 * cw_ref[0:1, :] + um1 * cw_ref[1:2, :] + u * cw_ref[2:3, :] + cb_ref[...]
    ocn_ref[...] = (_rms(c_out * y) * nco_ref[...]).astype(ocn_ref.dtype)
    if carry:
        carry_scr[...] = u[tm - 8:tm, :]
        u_ref[...] = u[tm - 8:tm, :]
    else:
        u_ref[...] = u


def _proj(x, shift, scale, norm_g, w_groups, cos, sin, prev1, prev2, conv_w, conv_b, norm_co, *, carry, seq_rows, tm=256):
    t = x.shape[0]
    mod_spec, rep = _mod_spec(shift.shape[0], t, tm, 1)
    rows = lambda wdt: pl.BlockSpec((tm, wdt), lambda i: (i, 0))
    cols = lambda h: pl.BlockSpec((h, tm), lambda i: (0, i))
    const = lambda r, wdt: pl.BlockSpec((r, wdt), lambda i: (0, 0))
    vmem = pl.BlockSpec(memory_space=pltpu.VMEM)
    sds = jax.ShapeDtypeStruct
    lin = (sds((t * KV_ROWS, HEAD_DIM), F32), pl.BlockSpec((tm * KV_ROWS, HEAD_DIM), lambda i: (i, 0)))
    if carry:
        per_ck = P_CK // tm
        kvb = (sds((t, KV_W), BF16), rows(KV_W))
        outs = [(sds((ATT_W, t), BF16), cols(ATT_W)), lin, lin, lin, kvb,
                (sds((t // P_CK, KV_W, P_CK), BF16), pl.BlockSpec((1, KV_W, tm), lambda i: (i // per_ck, 0, i % per_ck))),
                kvb,
                (sds((t // Q_BLOCK, KV_W, Q_BLOCK), BF16), pl.BlockSpec((tm // Q_BLOCK, KV_W, Q_BLOCK), lambda i: (i, 0, 0))),
                (sds((GATE_PAD, t), F32), cols(GATE_PAD)), (sds((t, CONV_W), BF16), rows(CONV_W)),
                (sds((8, CONV_W), F32), const(8, CONV_W))]
    else:
        outs = [(sds((t, ATT_W), F32), rows(ATT_W)), lin, lin, lin, (sds((t, GATE_PAD), F32), rows(GATE_PAD)),
                (sds((t, CONV_W), F32), rows(CONV_W)), (sds((t, CONV_W), F32), rows(CONV_W))]
    tab_spec = rows(HEAD_DIM) if carry else const(tm, HEAD_DIM)
    prev_spec = const(8, CONV_W) if carry else rows(CONV_W)
    return pl.pallas_call(
        functools.partial(_proj_kernel, tm=tm, seq_rows=seq_rows, carry=carry, rep=rep),
        grid=(t // tm,),
        in_specs=[rows(D_MODEL), mod_spec, mod_spec, const(1, D_MODEL), vmem, vmem, vmem,
                  tab_spec, tab_spec, prev_spec, prev_spec, const(CONV_K, CONV_W), const(1, CONV_W),
                  const(1, CONV_W)],
        out_specs=[spec for _, spec in outs],
        out_shape=[shape for shape, _ in outs],
        scratch_shapes=[pltpu.VMEM((8, CONV_W), F32)],
        compiler_params=_cparams("arbitrary"),
        name="proj",
    )(x, shift, scale, norm_g, *w_groups, cos, sin, prev1, prev2, conv_w, conv_b, norm_co)


def _cmp_kernel(x_ref, w_ref, o_ref, pad_scr, *, n_j, tm):
    pitch = n_j * KV_ROWS
    ppitch = pitch + SUBLANES

    def repitch(c, _):
        src = pl.multiple_of(c * pitch, SUBLANES)
        dst = pl.multiple_of(c * ppitch, SUBLANES)
        pad_scr[pl.ds(dst, pitch), :] = x_ref[pl.ds(src, pitch), :]
        return 0

    lax.fori_loop(0, tm, repitch, 0, unroll=8)
    flat = pad_scr
    for kv in range(2):
        for g in range(N_KV):
            xs = jnp.concatenate([flat[pl.ds(j * KV_ROWS + kv * N_KV + g, tm, stride=ppitch), :] for j in range(n_j)],
                                 axis=1)
            col = (kv * N_KV + g) * 2 * HEAD_DIM
            o_ref[:, col:col + 2 * HEAD_DIM] = _dot(xs.astype(BF16), w_ref[kv])


def _compress_products(x, w, n_j, tm):
    pitch = n_j * KV_ROWS
    m = x.shape[0] // pitch
    n_out = 2 * N_KV * 2 * HEAD_DIM
    tm = min(tm, m)
    return pl.pallas_call(
        functools.partial(_cmp_kernel, n_j=n_j, tm=tm),
        grid=(m // tm,),
        in_specs=[pl.BlockSpec((tm * pitch, HEAD_DIM), lambda i: (i, 0)),
                  pl.BlockSpec((2, n_j * HEAD_DIM, 2 * HEAD_DIM), lambda i: (0, 0, 0))],
        out_specs=pl.BlockSpec((tm, n_out), lambda i: (i, 0)),
        out_shape=jax.ShapeDtypeStruct((m, n_out), F32),
        scratch_shapes=[pltpu.VMEM((tm * (pitch + SUBLANES), HEAD_DIM), F32)],
        compiler_params=_cparams("arbitrary"),
        name="compress",
    )(x, w)


def _r_cols(kv, g):
    return (kv * N_KV + g) * 2 * HEAD_DIM


P_CK = 512
P_WKEYS = WINDOW + Q_BLOCK
P_TRIP = 4
N_FORCED = 3


def _pattn_t_kernel(qt_ref, gatet_ref, rp_ref, cosc_ref, sinc_ref, ks_ref, vst_ref, kw_ref, vwt_ref, et_ref, band_ref,
                    nao_ref, o_ref, kc_scr, vct_scr, pt_scr, ot_scr, s_scr, *, n_cmp_pad, n_blocks):
    i = pl.program_id(0)
    s0 = i * Q_BLOCK
    nq = Q_BLOCK
    rows = HEADS_PER_KV * nq
    lanes4 = lambda a: jnp.concatenate([a] * HEADS_PER_KV, axis=1)

    @pl.when(i == 0)
    def _():
        for g in range(N_KV):
            gs = slice(g * HEAD_DIM, (g + 1) * HEAD_DIM)
            ck, cv = _r_cols(0, g), _r_cols(1, g)
            kc = rp_ref[0:n_cmp_pad, ck:ck + HEAD_DIM] + rp_ref[pl.ds(1, n_cmp_pad), ck + HEAD_DIM:ck + 2 * HEAD_DIM]
            kc_scr[:, gs] = _rope(kc, cosc_ref[...], sinc_ref[...]).astype(BF16)
            vc = rp_ref[0:n_cmp_pad, cv:cv + HEAD_DIM] + rp_ref[pl.ds(1, n_cmp_pad), cv + HEAD_DIM:cv + 2 * HEAD_DIM]
            vct_scr[gs, :] = vc.T.astype(BF16)
        pt_scr[...] = jnp.zeros_like(pt_scr)

    t_q = s0 + lax.broadcasted_iota(jnp.int32, (n_cmp_pad, nq), 1)
    c_end = lax.broadcasted_iota(jnp.int32, (n_cmp_pad, nq), 0) * CMP_STRIDE + (CMP_BLOCK - 1)
    bias_c = lanes4(jnp.where(c_end <= t_q, 0.0, NEG))
    j_io = lax.broadcasted_iota(jnp.int32, (P_WKEYS, nq), 0)
    bias_w = lanes4(jnp.where(j_io >= WINDOW - s0, band_ref[...], NEG))

    blocks = [jnp.maximum(i + k - WINDOW // nq, 0) for k in range(P_WKEYS // nq)]
    scores, rhs, s_cmp, s_win = [], [], [], []
    for g in range(N_KV):
        gs = slice(g * HEAD_DIM, (g + 1) * HEAD_DIM)
        q_t = jnp.concatenate([qt_ref[(g * HEADS_PER_KV + r) * HEAD_DIM:(g * HEADS_PER_KV + r + 1) * HEAD_DIM, :]
                               for r in range(HEADS_PER_KV)], axis=1)
        rhs.append(q_t)
        s_cmp.append(_dot(kc_scr[:, gs], q_t))
        k_w = jnp.concatenate([kw_ref[pl.ds(pl.multiple_of(b * nq, nq), nq), gs] for b in blocks], axis=0)
        s_win.append(_dot(k_w, q_t))

    for g in range(N_KV):
        gs = slice(g * HEAD_DIM, (g + 1) * HEAD_DIM)
        s = s_cmp[g] + bias_c
        m = jnp.max(s, axis=0, keepdims=True)
        e = jnp.exp2(s - m)
        inv = jnp.where(m > 0.5 * NEG, 1.0 / jnp.maximum(jnp.sum(e, axis=0, keepdims=True), TINY), 0.0)
        o_c = _dot(vct_scr[gs, :], e.astype(BF16)) * inv

        p = e * inv
        pt_scr[SUBLANES:SUBLANES + n_cmp_pad, :] = sum(p[:, r * nq:(r + 1) * nq] for r in range(HEADS_PER_KV))
        st = lambda k: pt_scr[pl.ds(SUBLANES - 1 + k, n_blocks, stride=CHUNKS_PER_SEL), :]
        score = 0.5 * st(0) + st(1) + st(2) + st(3) + 0.5 * st(4)
        b_io = lax.broadcasted_iota(jnp.int32, (n_blocks, nq), 0)
        t_lane = s0 + lax.broadcasted_iota(jnp.int32, (n_blocks, nq), 1)
        cur = t_lane >> SEL_SHIFT
        forced = (b_io == 0) | (b_io == cur) | (b_io == cur - 1)
        valid = b_io * SEL_BLOCK <= t_lane
        scores.append((jnp.where(valid & jnp.logical_not(forced), score, -jnp.inf), jnp.where(forced, 1.0, 0.0)))

        v_wt = jnp.concatenate([vwt_ref[b, gs, :] for b in blocks], axis=1)
        s = s_win[g] + bias_w
        e = jnp.exp2(s - jnp.max(s, axis=0, keepdims=True))
        o_w = _dot(v_wt, e.astype(BF16)) / jnp.maximum(jnp.sum(e, axis=0, keepdims=True), TINY)

        for r in range(HEADS_PER_KV):
            hd = g * HEADS_PER_KV + r
            ls = slice(r * nq, (r + 1) * nq)
            gc = gatet_ref[hd * N_BRANCH + 0:hd * N_BRANCH + 1, :]
            gw = gatet_ref[hd * N_BRANCH + 2:hd * N_BRANCH + 3, :]
            ot_scr[hd * HEAD_DIM:(hd + 1) * HEAD_DIM, :] = gc * o_c[:, ls] + gw * o_w[:, ls]

    b_f = lax.broadcasted_iota(jnp.int32, (n_blocks, nq), 0).astype(F32)

    def pick(_, c):
        out = []
        for work, sel in c:
            m = jnp.max(work, axis=0, keepdims=True)
            idx = jnp.min(jnp.where(work == m, b_f, float(n_blocks)), axis=0, keepdims=True)
            hit = b_f == idx
            out.append((jnp.where(hit, -jnp.inf, work), jnp.where(hit, 1.0, sel)))
        return tuple(out)

    picked = lax.fori_loop(0, min(N_SEL, n_blocks) - N_FORCED, pick, tuple(scores))

    for g in range(N_KV):
        sel_bias = jnp.where(picked[g][1] > 0.5, 0.0, NEG).astype(BF16)
        rhs[g] = jnp.concatenate([rhs[g], lanes4(sel_bias)], axis=0)

    def qk_scores(c, slot):
        k0 = pl.multiple_of(c * P_CK, P_CK)
        blk_hot = et_ref[pl.ds(k0, P_CK), :]
        for g in range(N_KV):
            gs = slice(g * HEAD_DIM, (g + 1) * HEAD_DIM)
            s_scr[slot, g] = _dot(jnp.concatenate([ks_ref[pl.ds(k0, P_CK), gs], blk_hot], axis=1), rhs[g])

    ones_rows = jnp.ones((2 * SUBLANES, P_CK), BF16)

    def softmax_pv(c, slot, carry, causal):
        if causal:
            key = c * P_CK + lax.broadcasted_iota(jnp.int32, (P_CK, rows), 0)
            t_k = s0 + (lax.broadcasted_iota(jnp.int32, (P_CK, rows), 1) & (nq - 1))
            cb = jnp.where(key <= t_k, 0.0, NEG)
        out = []
        for g in range(N_KV):
            gs = slice(g * HEAD_DIM, (g + 1) * HEAD_DIM)
            m_i, l_i, acc = carry[g]
            s = s_scr[slot, g]
            if causal:
                s = s + cb
            m_n = jnp.maximum(m_i, jnp.max(s, axis=0, keepdims=True))
            p = jnp.exp2(s - m_n)
            alpha = jnp.exp2(m_i - m_n)
            pv = _dot(jnp.concatenate([vst_ref[c, gs, :], ones_rows], axis=0), p.astype(BF16))
            out.append((m_n, alpha * l_i + pv[HEAD_DIM:HEAD_DIM + 1], alpha * acc + pv[0:HEAD_DIM]))
        return tuple(out)

    def ahead(c, slot, carry):
        qk_scores(c + 1, 1 - slot)
        return softmax_pv(c, slot, carry, False)

    def trip(tr, carry):
        for k in range(P_TRIP):
            carry = ahead(P_TRIP * tr + k, k & 1, carry)
        return carry

    last = (s0 + nq + P_CK - 1) // P_CK - 1
    n_trips = last // P_TRIP
    init = tuple((jnp.full((1, rows), NEG, F32), jnp.zeros((1, rows), F32), jnp.zeros((HEAD_DIM, rows), F32))
                 for _ in range(N_KV))
    qk_scores(0, 0)
    carry = lax.fori_loop(0, n_trips, trip, init)
    done = P_TRIP * n_trips
    for k in range(P_TRIP - 1):
        carry = lax.cond(last - done > k, lambda cr, k=k: ahead(done + k, k & 1, cr), lambda cr: cr, carry)
    carry = lax.cond(((last - done) & 1) == 1, lambda cr: softmax_pv(last, 1, cr, True),
                     lambda cr: softmax_pv(last, 0, cr, True), carry)
    for g in range(N_KV):
        _, l_s, acc = carry[g]
        o_st = acc / jnp.maximum(l_s, TINY)
        for r in range(HEADS_PER_KV):
            hd = g * HEADS_PER_KV + r
            gsl = gatet_ref[hd * N_BRANCH + 1:hd * N_BRANCH + 2, :]
            ot_scr[hd * HEAD_DIM:(hd + 1) * HEAD_DIM, :] += gsl * o_st[:, r * nq:(r + 1) * nq]

    o_t = ot_scr[...]
    o_t = o_t * lax.rsqrt(jnp.mean(o_t * o_t, axis=0, keepdims=True) + EPS)
    o_ref[...] = (o_t.T * nao_ref[...]).astype(o_ref.dtype)


def _prompt_attention_t(q_t, gates_t, rp, cosc, sinc, ks, vs_t, kw, vw_t, e_t, band, norm_ao):
    s_len = q_t.shape[1]
    n_cmp_pad = s_len // CMP_STRIDE
    n_blocks = s_len // SEL_BLOCK
    vmem = pl.BlockSpec(memory_space=pltpu.VMEM)
    cols = lambda h: pl.BlockSpec((h, Q_BLOCK), lambda i: (0, i))
    return pl.pallas_call(
        functools.partial(_pattn_t_kernel, n_cmp_pad=n_cmp_pad, n_blocks=n_blocks),
        grid=(s_len // Q_BLOCK,),
        in_specs=[cols(ATT_W), cols(GATE_PAD), vmem, vmem, vmem, vmem, vmem, vmem, vmem, vmem, vmem,
                  pl.BlockSpec((1, ATT_W), lambda i: (0, 0))],
        out_specs=pl.BlockSpec((Q_BLOCK, ATT_W), lambda i: (i, 0)),
        out_shape=jax.ShapeDtypeStruct((s_len, ATT_W), BF16),
        scratch_shapes=[pltpu.VMEM((n_cmp_pad, KV_W), BF16), pltpu.VMEM((KV_W, n_cmp_pad), BF16),
                        pltpu.VMEM((n_cmp_pad + 2 * SUBLANES, Q_BLOCK), F32), pltpu.VMEM((ATT_W, Q_BLOCK), F32),
                        pltpu.VMEM((2, N_KV, P_CK, HEADS_PER_KV * Q_BLOCK), F32)],
        compiler_params=_cparams("arbitrary"),
        name="prompt_attn",
    )(q_t, gates_t, rp, cosc, sinc, ks, vs_t, kw, vw_t, e_t, band, norm_ao)


def _sattn_kernel(pt_ref, *refs, n_seqs, n_pages, page, n_new, past, wbuf):
    del pt_ref
    n_pg = n_seqs * n_pages
    r_pages, s_pages = refs[0:n_pg], refs[n_pg:2 * n_pg]
    (rnew_ref, q_ref, gate_ref, snew_ref, wst_ref, wnew_ref, cosc_ref, sinc_ref, e_ref, nao_ref,
     o_ref, wout_ref, r_scr, k_scr, v_scr, kw_scr, vw_scr, o_scr) = refs[2 * n_pg:]
    chunks_pp = page // CMP_STRIDE
    n_cmp = n_pages * chunks_pp
    rows = HEADS_PER_KV * n_new
    n_keys = k_scr.shape[1]
    n_wkeys = kw_scr.shape[1]
    pad = n_keys - past
    wpad = n_wkeys - wbuf
    batch = [(sq, g) for sq in range(n_seqs) for g in range(N_KV)]
    cat = lambda parts: jnp.concatenate(parts, axis=0)

    def with_zero_rows(new_rows, n_zero):
        return cat([new_rows, jnp.zeros((n_zero, HEAD_DIM), F32)]).astype(BF16)

    for sq in range(n_seqs):
        n0 = sq * n_new * KV_ROWS
        w0 = sq * wbuf * KV_ROWS
        for p in range(n_pages):
            r_scr[sq, p * chunks_pp:(p + 1) * chunks_pp, :] = r_pages[sq * n_pages + p][...]
            pg = s_pages[sq * n_pages + p]
            for g in range(N_KV):
                b = sq * N_KV + g
                k_scr[b, p * page:(p + 1) * page, :] = pg[pl.ds(g, page, stride=KV_ROWS), :].astype(BF16)
                v_scr[b, p * page:(p + 1) * page, :] = pg[pl.ds(N_KV + g, page, stride=KV_ROWS), :].astype(BF16)
        r_scr[sq, n_cmp:n_cmp + 8, :] = cat([rnew_ref[sq], jnp.zeros((7, r_scr.shape[2]), F32)])
        for g in range(N_KV):
            b = sq * N_KV + g
            k_scr[b, past:n_keys, :] = with_zero_rows(snew_ref[pl.ds(n0 + g, n_new, stride=KV_ROWS), :], pad - n_new)
            v_scr[b, past:n_keys, :] = with_zero_rows(snew_ref[pl.ds(n0 + N_KV + g, n_new, stride=KV_ROWS), :],
                                                      pad - n_new)
            kw_scr[b, 0:wbuf, :] = wst_ref[pl.ds(w0 + g, wbuf, stride=KV_ROWS), :].astype(BF16)
            vw_scr[b, 0:wbuf, :] = wst_ref[pl.ds(w0 + N_KV + g, wbuf, stride=KV_ROWS), :].astype(BF16)
            kw_scr[b, wbuf:n_wkeys, :] = with_zero_rows(wnew_ref[pl.ds(n0 + g, n_new, stride=KV_ROWS), :],
                                                        wpad - n_new)
            vw_scr[b, wbuf:n_wkeys, :] = with_zero_rows(wnew_ref[pl.ds(n0 + N_KV + g, n_new, stride=KV_ROWS), :],
                                                        wpad - n_new)
        keep = (wbuf - n_new) * KV_ROWS
        wout_ref[w0:w0 + keep, :] = wst_ref[w0 + n_new * KV_ROWS:w0 + wbuf * KV_ROWS, :]
        wout_ref[w0 + keep:w0 + wbuf * KV_ROWS, :] = wnew_ref[n0:n0 + n_new * KV_ROWS, :]

    n_b = len(batch)
    all_rows = n_b * rows
    tok = lax.broadcasted_iota(jnp.int32, (all_rows, 1), 0) & (n_new - 1)
    t_rows = past + tok

    qg, kc, vc = [], [], []
    for sq, g in batch:
        qs = slice(sq * n_new, (sq + 1) * n_new)
        qg.append(cat([q_ref[qs, (g * HEADS_PER_KV + r) * HEAD_DIM:(g * HEADS_PER_KV + r + 1) * HEAD_DIM]
                       for r in range(HEADS_PER_KV)]).astype(BF16))
        ck, cv = _r_cols(0, g), _r_cols(1, g)
        kc.append(r_scr[sq, 0:n_cmp, ck:ck + HEAD_DIM] + r_scr[sq, pl.ds(1, n_cmp), ck + HEAD_DIM:ck + 2 * HEAD_DIM])
        vc.append((r_scr[sq, 0:n_cmp, cv:cv + HEAD_DIM]
                   + r_scr[sq, pl.ds(1, n_cmp), cv + HEAD_DIM:cv + 2 * HEAD_DIM]).astype(BF16))
    cos_all, sin_all = cat([cosc_ref[...]] * n_b), cat([sinc_ref[...]] * n_b)
    kc_all = _rope(cat(kc), cos_all, sin_all).astype(BF16)
    rb = lambda b: slice(b * rows, (b + 1) * rows)

    s_c = cat([_dot_nt(qg[b], kc_all[b * n_cmp:(b + 1) * n_cmp]) for b in range(n_b)])
    s_w = cat([_dot_nt(qg[b], kw_scr[b]) for b in range(n_b)])
    s_s = cat([_dot_nt(qg[b], k_scr[b]) for b in range(n_b)])

    c_end = lax.broadcasted_iota(jnp.int32, (all_rows, n_cmp), 1) * CMP_STRIDE + (CMP_BLOCK - 1)
    p_c = _masked_softmax(s_c, c_end <= t_rows)
    p_cb = p_c.astype(BF16)
    o_c = [_dot(p_cb[rb(b)], vc[b]) for b in range(n_b)]

    sel_rows = n_b * n_new
    p_grp = cat([sum(p_c[b * rows + r * n_new:b * rows + (r + 1) * n_new] for r in range(HEADS_PER_KV))
                 for b in range(n_b)])
    lane = lax.broadcasted_iota(jnp.int32, (sel_rows, HEAD_DIM), 1)
    pch = 0.5 * (p_grp + jnp.where(lane >= 1, pltpu.roll(p_grp, 1, axis=1), 0.0))
    score = pch
    for k in range(1, CHUNKS_PER_SEL):
        score = score + pltpu.roll(pch, HEAD_DIM - k, axis=1)
    blk = lane >> (CHUNKS_PER_SEL.bit_length() - 1)
    t_tok = past + (lax.broadcasted_iota(jnp.int32, (sel_rows, HEAD_DIM), 0) & (n_new - 1))
    cur = t_tok >> SEL_SHIFT
    forced = (blk == 0) | (blk == cur) | (blk == cur - 1)
    score = jnp.where(forced, jnp.inf, score)
    n_pb = past // SEL_BLOCK
    ahead = jnp.zeros((sel_rows, HEAD_DIM), F32)
    for k in range(1, n_pb):
        other = pltpu.roll(score, CHUNKS_PER_SEL * k, axis=1)
        wins = (other > score) | ((other == score) & (blk >= k))
        ahead = ahead + jnp.where(wins, 1.0, 0.0)
    sel = jnp.where(((lane & (CHUNKS_PER_SEL - 1)) == 0) & (ahead < N_SEL - 1), 1.0, 0.0).astype(BF16)
    sel_keys = _dot(sel, e_ref[...])
    new_ok = (lax.broadcasted_iota(jnp.int32, (sel_rows, pad), 1)
              <= (lax.broadcasted_iota(jnp.int32, (sel_rows, pad), 0) & (n_new - 1)))
    bias = jnp.concatenate([jnp.where(sel_keys > 0.5, 0.0, NEG), jnp.where(new_ok, 0.0, NEG)], axis=1)
    bias = cat([bias[b * n_new:(b + 1) * n_new] for b in range(n_b) for _ in range(HEADS_PER_KV)])

    s_s = s_s + bias
    e_s = jnp.exp2(s_s - jnp.max(s_s, axis=-1, keepdims=True))
    p_s = (e_s / jnp.maximum(jnp.sum(e_s, axis=-1, keepdims=True), TINY)).astype(BF16)
    o_s = [_dot(p_s[rb(b)], v_scr[b]) for b in range(n_b)]

    j_io = lax.broadcasted_iota(jnp.int32, (all_rows, n_wkeys), 1)
    m_w = (((j_io < wbuf) & (j_io > tok + (wbuf - WINDOW)) & (j_io >= wbuf - past))
           | ((j_io >= wbuf) & (j_io - wbuf <= tok)))
    p_w = _masked_softmax(s_w, m_w).astype(BF16)
    o_w = [_dot(p_w[rb(b)], vw_scr[b]) for b in range(n_b)]

    for b, (sq, g) in enumerate(batch):
        qs = slice(sq * n_new, (sq + 1) * n_new)
        for r in range(HEADS_PER_KV):
            hd = g * HEADS_PER_KV + r
            rs = slice(r * n_new, (r + 1) * n_new)
            gc = gate_ref[qs, hd * N_BRANCH + 0:hd * N_BRANCH + 1]
            gsl = gate_ref[qs, hd * N_BRANCH + 1:hd * N_BRANCH + 2]
            gw = gate_ref[qs, hd * N_BRANCH + 2:hd * N_BRANCH + 3]
            o_scr[qs, hd * HEAD_DIM:(hd + 1) * HEAD_DIM] = gc * o_c[b][rs] + gsl * o_s[b][rs] + gw * o_w[b][rs]

    o_ref[...] = _rms(o_scr[...]) * nao_ref[...]


S_SEQS = 2


def _sample_attention(page_table, r_all, r_new, q, gates, slc_cache, slc_new, win_state, win_new, cosc, sinc, e_mat,
                      norm_ao, *, n_new, page, wbuf):
    n_seq, n_pages = page_table.shape
    past = n_pages * page
    chunks_pp = page // CMP_STRIDE
    n_cmp = n_pages * chunks_pp
    assert n_cmp == HEAD_DIM and n_new == 8 and past % SEL_BLOCK == 0 and n_new <= SEL_BLOCK
    assert (past + n_new - 1) // SEL_BLOCK == past // SEL_BLOCK and wbuf == WINDOW and past >= WINDOW
    n_keys = past + HEAD_DIM
    n_wkeys = wbuf + HEAD_DIM
    r_w = r_all.shape[1]

    ns = S_SEQS
    assert n_seq % ns == 0
    page_map = lambda sq, p: (lambda b, pt: (pt[b * ns + sq, p], 0))
    in_specs = [pl.BlockSpec((chunks_pp, r_w), page_map(sq, p)) for sq in range(ns) for p in range(n_pages)]
    in_specs += [pl.BlockSpec((page * KV_ROWS, HEAD_DIM), page_map(sq, p)) for sq in range(ns) for p in range(n_pages)]
    seq_rows = lambda wdt: pl.BlockSpec((ns * n_new, wdt), lambda b, pt: (b, 0))
    kv_rows = lambda n_tok: pl.BlockSpec((ns * n_tok * KV_ROWS, HEAD_DIM), lambda b, pt: (b, 0))
    const = lambda shape: pl.BlockSpec(shape, lambda b, pt: (0,) * len(shape))
    in_specs += [pl.BlockSpec((ns, 1, r_w), lambda b, pt: (b, 0, 0)), seq_rows(ATT_W), seq_rows(GATE_PAD),
                 kv_rows(n_new), kv_rows(wbuf), kv_rows(n_new),
                 const((n_cmp, HEAD_DIM)), const((n_cmp, HEAD_DIM)), const((HEAD_DIM, past)), const((1, ATT_W))]
    grid_spec = pltpu.PrefetchScalarGridSpec(
        num_scalar_prefetch=1,
        grid=(n_seq // ns,),
        in_specs=in_specs,
        out_specs=[seq_rows(ATT_W), kv_rows(wbuf)],
        scratch_shapes=[pltpu.VMEM((ns, n_cmp + 8, r_w), F32), pltpu.VMEM((ns * N_KV, n_keys, HEAD_DIM), BF16),
                        pltpu.VMEM((ns * N_KV, n_keys, HEAD_DIM), BF16), pltpu.VMEM((ns * N_KV, n_wkeys, HEAD_DIM), BF16),
                        pltpu.VMEM((ns * N_KV, n_wkeys, HEAD_DIM), BF16), pltpu.VMEM((ns * n_new, ATT_W), F32)],
    )
    return pl.pallas_call(
        functools.partial(_sattn_kernel, n_seqs=ns, n_pages=n_pages, page=page, n_new=n_new, past=past, wbuf=wbuf),
        grid_spec=grid_spec,
        out_shape=[jax.ShapeDtypeStruct((n_seq * n_new, ATT_W), F32),
                   jax.ShapeDtypeStruct((n_seq * wbuf * KV_ROWS, HEAD_DIM), F32)],
        compiler_params=_cparams("arbitrary"),
        name="sample_attn",
    )(page_table, *([r_all] * (ns * n_pages)), *([slc_cache] * (ns * n_pages)), r_new, q, gates, slc_new, win_state,
      win_new, cosc, sinc, e_mat, norm_ao)


def _outproj_kernel(x_ref, oa_ref, oc_ref, gt_ref, w_ref, o_ref, *, rep):
    y = _dot(oa_ref[...].astype(BF16), w_ref[0:ATT_W, :]) + _dot(oc_ref[...].astype(BF16), w_ref[ATT_W:D_MODEL, :])
    o_ref[...] = x_ref[...] + (1.0 + _mod_rows(gt_ref, rep)) * y


def _outproj(x, oa, oc, gate, w, tm=512):
    t = x.shape[0]
    mod_spec, rep = _mod_spec(gate.shape[0], t, tm, 1)
    rows = lambda wdt: pl.BlockSpec((tm, wdt), lambda i: (i, 0))
    return pl.pallas_call(
        functools.partial(_outproj_kernel, rep=rep),
        grid=(t // tm,),
        in_specs=[rows(D_MODEL), rows(ATT_W), rows(CONV_W), mod_spec, pl.BlockSpec(memory_space=pltpu.VMEM)],
        out_specs=rows(D_MODEL),
        out_shape=jax.ShapeDtypeStruct((t, D_MODEL), F32),
        compiler_params=_cparams("arbitrary"),
        name="outproj",
    )(x, oa, oc, gate, w)


def _rope_tables(pos):
    half = HEAD_DIM // 2
    inv = np.float32(ROPE_THETA) ** (-np.arange(half, dtype=np.float32) * np.float32(2.0) / np.float32(HEAD_DIM))
    ang = np.asarray(pos, np.float32)[:, None] * inv[None, :]
    cos, sin = np.cos(ang), np.sin(ang)
    return (jnp.asarray(np.concatenate([cos, cos], axis=1), F32),
            jnp.asarray(np.concatenate([-sin, sin], axis=1), F32))


def _pack_w_in(w_in):
    off_g = ATT_W + 3 * SLAB_W
    off_c = off_g + N_HEADS * N_BRANCH
    gate_cols = jnp.pad(w_in[:, off_g:off_c], ((0, 0), (0, GATE_PAD - N_HEADS * N_BRANCH)))
    return (w_in[:, 0:off_g].astype(BF16), w_in[:, off_c:off_c + 3 * CONV_W].astype(BF16), gate_cols.astype(BF16))


def _pack_w_cmp(w_ck, w_cv, n_j):
    def one(w):
        lo = w[0:n_j].reshape(n_j * HEAD_DIM, HEAD_DIM)
        hi = w[CMP_STRIDE:CMP_STRIDE + n_j].reshape(n_j * HEAD_DIM, HEAD_DIM)
        return jnp.concatenate([lo, hi], axis=1)
    return jnp.stack([one(w_ck), one(w_cv)]).astype(BF16)


def kernel(x_prompt, x_sample, c_prompt, c_sample, cache_cmp_kv, cache_slc_kv, state_win_kv, state_conv, page_table,
           w_ada, b_ada, norm_ffn1, ffn1_gate, ffn1_up, ffn1_down, norm_mix, w_in, w_cmp_k, w_cmp_v, conv_w, conv_b,
           norm_att_out, norm_conv_out, w_out, norm_ffn2, ffn2_gate, ffn2_up, ffn2_down, norm_final):
    n_p, s_len, _ = x_prompt.shape
    n_seq, n_new, _ = x_sample.shape
    depth = w_ada.shape[0]
    assert n_p == 1 and depth == 1
    n_pages = page_table.shape[1]
    page = cache_slc_kv.shape[2]
    n_phys = cache_slc_kv.shape[1]
    past = n_pages * page
    wbuf = state_win_kv.shape[2]
    keep_p = min(WINDOW, s_len)
    t_s = n_seq * n_new
    l = 0

    c_all = jnp.concatenate([c_sample, c_prompt, jnp.zeros((8 - n_p, D_MODEL), F32)], axis=0)
    mod = _ada(c_all, w_ada[l], b_ada[l])
    mod_p = [mod[n_seq:n_seq + 1, k * D_MODEL:(k + 1) * D_MODEL] for k in range(N_MOD)]
    mod_s = [mod[0:n_seq, k * D_MODEL:(k + 1) * D_MODEL] for k in range(N_MOD)]

    row = lambda v: v.reshape(1, -1)
    w_proj = _pack_w_in(w_in[l])
    w_o = w_out[l].astype(BF16)
    nfin = row(norm_final)

    xp = x_prompt.reshape(s_len, D_MODEL)
    xs = x_sample.reshape(t_s, D_MODEL)

    xs, *f1 = _ffn_stream(xs, mod_s[0], mod_s[1], mod_s[2], row(norm_ffn1[l]), nfin, ffn1_gate[l], ffn1_up[l],
                          ffn1_down[l], final_norm=False)
    xp = _ffn(xp, mod_p[0], mod_p[1], mod_p[2], row(norm_ffn1[l]), nfin, *f1, final_norm=False)

    cos_p, sin_p = _rope_tables(np.arange(s_len))
    tm_s = 256
    cos_s, sin_s = _rope_tables(np.tile(past + np.arange(n_new), tm_s // n_new))
    zero8 = jnp.zeros((8, CONV_W), F32)
    conv_args = (conv_w[l], row(conv_b[l]), row(norm_conv_out[l]))
    (qt_p, cmp_p, slc_p, win_p, ksb_p, vst_p, kwb_p, vwt_p, gatet_p, ocn_p, utail_p) = _proj(
        xp, mod_p[3], mod_p[4], row(norm_mix[l]), w_proj, cos_p, sin_p, zero8, zero8, *conv_args,
        carry=True, seq_rows=s_len)
    prev1 = jnp.repeat(state_conv[l][:, CONV_K - 2], n_new, axis=0)
    prev2 = jnp.repeat(state_conv[l][:, CONV_K - 3], n_new, axis=0)
    (q_s, cmp_s, slc_s, win_s, gate_s, ocn_s, u_s) = _proj(
        xs, mod_s[3], mod_s[4], row(norm_mix[l]), w_proj, cos_s, sin_s, prev1, prev2, *conv_args,
        carry=False, seq_rows=n_new, tm=tm_s)

    n_j = CMP_STRIDE
    w_c = _pack_w_cmp(w_cmp_k[l], w_cmp_v[l], n_j)
    lin = lambda a: a.reshape(-1, HEAD_DIM)
    r_p = _compress_products(cmp_p, w_c, n_j, tm=256)
    r_cache = _compress_products(lin(cache_cmp_kv), w_c, n_j, tm=256)
    r_new = _compress_products(cmp_s, _pack_w_cmp(w_cmp_k[l], w_cmp_v[l], n_new), n_new, tm=n_seq)

    n_cmp_pad = s_len // CMP_STRIDE
    cosc, sinc = _rope_tables(np.arange(n_cmp_pad) * CMP_STRIDE + (CMP_BLOCK - 1))
    r_p = jnp.pad(r_p, ((0, 8), (0, 0)))
    n_blocks = s_len // SEL_BLOCK
    e_t = ((jnp.arange(s_len) // SEL_BLOCK)[:, None] == jnp.arange(n_blocks)[None, :]).astype(BF16)
    j_w, ti_w = jnp.arange(P_WKEYS)[:, None], jnp.arange(Q_BLOCK)[None, :]
    band = jnp.where((j_w > ti_w) & (j_w <= ti_w + WINDOW), 0.0, NEG).astype(F32)
    oa_p = _prompt_attention_t(qt_p, gatet_p, r_p, cosc, sinc, ksb_p, vst_p, kwb_p, vwt_p, e_t, band,
                               row(norm_att_out[l]))

    n_cmp_s = past // CMP_STRIDE
    cosc_s, sinc_s = _rope_tables(np.arange(n_cmp_s) * CMP_STRIDE + (CMP_BLOCK - 1))
    e_s =(jnp.arange(HEAD_DIM)[:, None] == (jnp.arange(past) // SEL_BLOCK * CHUNKS_PER_SEL)[None, :]).astype(BF16)
    oa_s, win_new_state = _sample_attention(
        page_table, r_cache, r_new.reshape(n_seq, 1, -1), q_s, gate_s,
        lin(cache_slc_kv), slc_s, lin(state_win_kv), win_s,
        cosc_s, sinc_s, e_s, row(norm_att_out[l]), n_new=n_new, page=page, wbuf=wbuf)

    xp = _outproj(xp, oa_p, ocn_p, mod_p[5], w_o)
    xs = _outproj(xs, oa_s, ocn_s, mod_s[5], w_o)
    ys, *f2 = _ffn_stream(xs, mod_s[6], mod_s[7], mod_s[8], row(norm_ffn2[l]), nfin, ffn2_gate[l], ffn2_up[l],
                          ffn2_down[l], final_norm=True)
    yp = _ffn(xp, mod_p[6], mod_p[7], mod_p[8], row(norm_ffn2[l]), nfin, *f2, final_norm=True)

    kv6 = lambda a, n, s: a.reshape(1, n, s, 2, N_KV, HEAD_DIM)
    return (yp.reshape(n_p, s_len, D_MODEL), ys.reshape(n_seq, n_new, D_MODEL),
            kv6(cmp_p, n_p, s_len), kv6(slc_p, n_p, s_len), kv6(win_p[(s_len - keep_p) * KV_ROWS:], n_p, keep_p),
            utail_p[8 - (CONV_K - 1):].reshape(1, n_p, CONV_K - 1, CONV_W),
            kv6(cmp_s, n_seq, n_new), kv6(slc_s, n_seq, n_new), kv6(win_new_state, n_seq, wbuf),
            u_s.reshape(n_seq, n_new, CONV_W)[:, n_new - (CONV_K - 1):].reshape(1, n_seq, CONV_K - 1, CONV_W))
```

```python
import functools

import jax
import jax.numpy as jnp
import numpy as np
from jax import lax
from jax.experimental import pallas as pl
from jax.experimental.pallas import tpu as pltpu

F32 = jnp.float32
BF16 = jnp.bfloat16

D_MODEL = 2048
HEAD_DIM = 128
N_HEADS = 8
N_KV = 2
HEADS_PER_KV = N_HEADS // N_KV
ATT_W = N_HEADS * HEAD_DIM
KV_W = N_KV * HEAD_DIM
CONV_W = D_MODEL - ATT_W
CONV_K = 3
CMP_BLOCK = 32
CMP_STRIDE = 16
SEL_BLOCK = 64
N_SEL = 16
WINDOW = 512
Q_BLOCK = 128
N_BRANCH = 3
N_MOD = 9
ROPE_THETA = 10000.0
EPS = 1e-6
NEG = -1e30
TINY = 1e-30
SCALE = HEAD_DIM ** -0.5
LOG2E = 1.4426950408889634
SLAB_W = 2 * KV_W
KV_ROWS = 2 * N_KV
GATE_PAD = 128
SUBLANES = 8
CHUNKS_PER_SEL = SEL_BLOCK // CMP_STRIDE
SEL_SHIFT = SEL_BLOCK.bit_length() - 1

VMEM_LIMIT = 56 * 1024 * 1024


def _cparams(*sem):
    return pltpu.CompilerParams(dimension_semantics=sem, vmem_limit_bytes=VMEM_LIMIT)


def _dot(a, b):
    return jnp.dot(a, b, preferred_element_type=F32)


def _dot_nt(a, b):
    return lax.dot_general(a, b, (((1,), (1,)), ((), ())), preferred_element_type=F32)


def _rms(x):
    return x * lax.rsqrt(jnp.mean(x * x, axis=-1, keepdims=True) + EPS)


def _silu(x):
    return x * jax.nn.sigmoid(x)


def _rope(x, cos, sin_signed):
    return x * cos + pltpu.roll(x, HEAD_DIM // 2, axis=1) * sin_signed


def _masked_softmax(s, mask):
    s = jnp.where(mask, s, NEG)
    m = jnp.max(s, axis=-1, keepdims=True)
    e = jnp.where(mask, jnp.exp2(s - m), 0.0)
    return e / jnp.maximum(jnp.sum(e, axis=-1, keepdims=True), TINY)


def _mod_rows(ref, rep):
    m = ref[...]
    return m if rep == 1 else jnp.repeat(m, rep, axis=0)


def _mod_spec(mrows, t, tm, n_grid_axes):
    rep = 1 if mrows == 1 else t // mrows
    shape = (1, D_MODEL) if mrows == 1 else (tm // rep, D_MODEL)
    first = (lambda i: 0) if mrows == 1 else (lambda i: i)
    index = (lambda i: (first(i), 0)) if n_grid_axes == 1 else (lambda i, j: (first(i), 0))
    return pl.BlockSpec(shape, index), rep


def _ada_kernel(c_ref, w_ref, b_ref, o_ref):
    a = _silu(c_ref[...]).astype(BF16)
    o_ref[...] = _dot(a, w_ref[...].astype(BF16)) + b_ref[...]


def _ada(c, w, b, tn=1024):
    m, n = c.shape[0], w.shape[1]
    return pl.pallas_call(
        _ada_kernel,
        grid=(n // tn,),
        in_specs=[pl.BlockSpec((m, D_MODEL), lambda j: (0, 0)),
                  pl.BlockSpec((D_MODEL, tn), lambda j: (0, j)),
                  pl.BlockSpec((1, tn), lambda j: (0, j))],
        out_specs=pl.BlockSpec((m, tn), lambda j: (0, j)),
        out_shape=jax.ShapeDtypeStruct((m, n), F32),
        compiler_params=_cparams("arbitrary"),
        name="ada",
    )(c, w, b.reshape(1, n))


def _ffn_kernel(x_ref, sh_ref, sc_ref, gt_ref, ng_ref, nf_ref, wg_ref, wu_ref, wd_ref, o_ref, h_scr, *, n_f, final_norm,
                rep):
    j = pl.program_id(1)
    mod = lambda ref: _mod_rows(ref, rep)

    @pl.when(j == 0)
    def _():
        h = _rms(x_ref[...]) * (ng_ref[...] * (1.0 + mod(sc_ref))) + mod(sh_ref)
        h_scr[...] = h.astype(BF16)
        o_ref[...] = jnp.zeros_like(o_ref)

    h = h_scr[...]
    a = (_silu(_dot(h, wg_ref[...])) * _dot(h, wu_ref[...])).astype(BF16)
    o_ref[...] += _dot(a, wd_ref[...])

    @pl.when(j == n_f - 1)
    def _():
        out = x_ref[...] + (0.5 * (1.0 + mod(gt_ref))) * o_ref[...]
        if final_norm:
            out = _rms(out) * nf_ref[...]
        o_ref[...] = out


def _ffn(x, shift, scale, gate, norm_g, norm_final, wg, wu, wd, *, final_norm, tm=512, tf=512):
    t = x.shape[0]
    d_ff = wg.shape[1]
    n_f = d_ff // tf
    mod_spec, rep = _mod_spec(shift.shape[0], t, tm, 2)
    row_spec = pl.BlockSpec((tm, D_MODEL), lambda i, j: (i, 0))
    vec_spec = pl.BlockSpec((1, D_MODEL), lambda i, j: (0, 0))
    return pl.pallas_call(
        functools.partial(_ffn_kernel, n_f=n_f, final_norm=final_norm, rep=rep),
        grid=(t // tm, n_f),
        in_specs=[row_spec, mod_spec, mod_spec, mod_spec, vec_spec, vec_spec,
                  pl.BlockSpec((D_MODEL, tf), lambda i, j: (0, j)),
                  pl.BlockSpec((D_MODEL, tf), lambda i, j: (0, j)),
                  pl.BlockSpec((tf, D_MODEL), lambda i, j: (j, 0))],
        out_specs=row_spec,
        out_shape=jax.ShapeDtypeStruct((t, D_MODEL), F32),
        scratch_shapes=[pltpu.VMEM((tm, D_MODEL), BF16)],
        compiler_params=_cparams("arbitrary", "arbitrary"),
        name="ffn",
    )(x, shift, scale, gate, norm_g, norm_final, wg, wu, wd)


FFN_ROW_CHUNK = 128


def _ffn_stream_kernel(x_ref, sh_ref, sc_ref, gt_ref, ng_ref, nf_ref, wg_ref, wu_ref, wd_ref,
                       o_ref, wgb_ref, wub_ref, wdb_ref, h_scr, *, n_f, final_norm, rep):
    j = pl.program_id(0)
    t = x_ref.shape[0]
    rc = FFN_ROW_CHUNK

    def rows_of(ref, c, width):
        return jnp.repeat(ref[pl.ds(pl.multiple_of(c * (width // rep), width // rep), width // rep), :], rep, axis=0)

    @pl.when(j == 0)
    def _():
        def body(c, _):
            r0 = pl.multiple_of(c * rc, rc)
            h = _rms(x_ref[pl.ds(r0, rc), :]) * ng_ref[...] * (1.0 + rows_of(sc_ref, c, rc)) + rows_of(sh_ref, c, rc)
            h_scr[pl.ds(r0, rc), :] = h.astype(BF16)
            return 0
        lax.fori_loop(0, t // rc, body, 0)
        o_ref[...] = jnp.zeros_like(o_ref)

    wg, wu, wd = wg_ref[...].astype(BF16), wu_ref[...].astype(BF16), wd_ref[...].astype(BF16)
    wgb_ref[...] = wg
    wub_ref[...] = wu
    wdb_ref[...] = wd
    h = h_scr[...]
    a = (_silu(_dot(h, wg)) * _dot(h, wu)).astype(BF16)
    o_ref[...] += _dot(a, wd)

    @pl.when(j == n_f - 1)
    def _():
        def body(c, _):
            r0 = pl.multiple_of(c * rc, rc)
            out = x_ref[pl.ds(r0, rc), :] + 0.5 * (1.0 + rows_of(gt_ref, c, rc)) * o_ref[pl.ds(r0, rc), :]
            if final_norm:
                out = _rms(out) * nf_ref[...]
            o_ref[pl.ds(r0, rc), :] = out
            return 0
        lax.fori_loop(0, t // rc, body, 0)


def _ffn_stream(x, shift, scale, gate, norm_g, norm_final, wg, wu, wd, *, final_norm, tf=256):
    t = x.shape[0]
    d_ff = wg.shape[1]
    n_f = d_ff // tf
    rep = t // shift.shape[0]
    assert t % FFN_ROW_CHUNK == 0 and FFN_ROW_CHUNK % rep == 0
    vmem = pl.BlockSpec(memory_space=pltpu.VMEM)
    col_tile = pl.BlockSpec((D_MODEL, tf), lambda j: (0, j))
    row_tile = pl.BlockSpec((tf, D_MODEL), lambda j: (j, 0))
    return pl.pallas_call(
        functools.partial(_ffn_stream_kernel, n_f=n_f, final_norm=final_norm, rep=rep),
        grid=(n_f,),
        in_specs=[vmem, vmem, vmem, vmem, vmem, vmem, col_tile, col_tile, row_tile],
        out_specs=[pl.BlockSpec((t, D_MODEL), lambda j: (0, 0)), col_tile, col_tile, row_tile],
        out_shape=[jax.ShapeDtypeStruct((t, D_MODEL), F32), jax.ShapeDtypeStruct(wg.shape, BF16),
                   jax.ShapeDtypeStruct(wu.shape, BF16), jax.ShapeDtypeStruct(wd.shape, BF16)],
        scratch_shapes=[pltpu.VMEM((t, D_MODEL), BF16)],
        compiler_params=_cparams("arbitrary"),
        name="ffn_stream",
    )(x, shift, scale, gate, norm_g, norm_final, wg, wu, wd)


def _proj_kernel(x_ref, sh_ref, sc_ref, ng_ref, wa_ref, wc_ref, wg_ref, cos_ref, sin_ref, p1_ref, p2_ref, cw_ref,
                 cb_ref, nco_ref, *rest, tm, seq_rows, carry, rep):
    if carry:
        q_ref, cmp_ref, slc_ref, win_ref, ksb_ref, vst_ref, kwb_ref, vwt_ref, gate_ref, ocn_ref, u_ref, carry_scr = rest
    else:
        q_ref, cmp_ref, slc_ref, win_ref, gate_ref, ocn_ref, u_ref, carry_scr = rest
        ksb_ref = kwb_ref = None
    i = pl.program_id(0)
    h = (_rms(x_ref[...]) * ng_ref[...] * (1.0 + _mod_rows(sc_ref, rep)) + _mod_rows(sh_ref, rep)).astype(BF16)
    cos, sin = cos_ref[...], sin_ref[...]

    pq = _dot(h, wa_ref[:, 0:ATT_W])
    for hd in range(N_HEADS):
        hs = slice(hd * HEAD_DIM, (hd + 1) * HEAD_DIM)
        blk = _rope(pq[:, hs], cos, sin) * (SCALE * LOG2E)
        if carry:
            q_ref[hs, :] = blk.T.astype(q_ref.dtype)
        else:
            q_ref[:, hs] = blk.astype(q_ref.dtype)

    pkv = _dot(h, wa_ref[:, ATT_W:ATT_W + 3 * SLAB_W])
    for slab, (o_ref, kb_ref) in enumerate(((cmp_ref, None), (slc_ref, ksb_ref), (win_ref, kwb_ref))):
        base = slab * SLAB_W
        for g in range(N_KV):
            gs = slice(g * HEAD_DIM, (g + 1) * HEAD_DIM)
            k = pkv[:, base + g * HEAD_DIM:base + (g + 1) * HEAD_DIM]
            v = pkv[:, base + KV_W + g * HEAD_DIM:base + KV_W + (g + 1) * HEAD_DIM]
            if slab > 0:
                k = _rope(k, cos, sin)
            if kb_ref is not None:
                kb_ref[:, gs] = k.astype(BF16)
                v_t = v.T.astype(BF16)
                if slab == 1:
                    vst_ref[0, gs, :] = v_t
                else:
                    for blk in range(tm // Q_BLOCK):
                        vwt_ref[blk, gs, :] = v_t[:, blk * Q_BLOCK:(blk + 1) * Q_BLOCK]
            o_ref[pl.ds(g, tm, stride=KV_ROWS), :] = k
            o_ref[pl.ds(N_KV + g, tm, stride=KV_ROWS), :] = v

    gate = jax.nn.sigmoid(_dot(h, wg_ref[...]))
    gate_ref[...] = gate.T if carry else gate

    pc = _dot(h, wc_ref[...])
    u = pc[:, 0:CONV_W] * pc[:, 2 * CONV_W:3 * CONV_W]
    c_out = pc[:, CONV_W:2 * CONV_W]
    row = lax.broadcasted_iota(jnp.int32, (tm, CONV_W), 0)
    if carry:
        @pl.when(i == 0)
        def _():
            carry_scr[...] = jnp.zeros_like(carry_scr)
        prev1 = carry_scr[7:8, :]
        prev2 = carry_scr[6:7, :]
        rs = row
    else:
        prev1 = p1_ref[...]
        prev2 = p2_ref[...]
        rs = row & (seq_rows - 1)
    um1 = jnp.where(rs >= 1, pltpu.roll(u, 1, axis=0), prev1)
    um2 = jnp.where(rs >= 2, pltpu.roll(u, 2, axis=0), jnp.where(rs == 1, prev1, prev2))
    y = um2 * cw_ref[0:1, :] + um1 * cw_ref[1:2, :] + u * cw_ref[2:3, :] + cb_ref[...]
    ocn_ref[...] = (_rms(c_out * y) * nco_ref[...]).astype(ocn_ref.dtype)
    if carry:
        carry_scr[...] = u[tm - 8:tm, :]
        u_ref[...] = u[tm - 8:tm, :]
    else:
        u_ref[...] = u


def _proj(x, shift, scale, norm_g, w_groups, cos, sin, prev1, prev2, conv_w, conv_b, norm_co, *, carry, seq_rows, tm=256):
    t = x.shape[0]
    mod_spec, rep = _mod_spec(shift.shape[0], t, tm, 1)
    rows = lambda wdt: pl.BlockSpec((tm, wdt), lambda i: (i, 0))
    cols = lambda h: pl.BlockSpec((h, tm), lambda i: (0, i))
    const = lambda r, wdt: pl.BlockSpec((r, wdt), lambda i: (0, 0))
    vmem = pl.BlockSpec(memory_space=pltpu.VMEM)
    sds = jax.ShapeDtypeStruct
    lin = (sds((t * KV_ROWS, HEAD_DIM), F32), pl.BlockSpec((tm * KV_ROWS, HEAD_DIM), lambda i: (i, 0)))
    if carry:
        per_ck = P_CK // tm
        kvb = (sds((t, KV_W), BF16), rows(KV_W))
        outs = [(sds((ATT_W, t), BF16), cols(ATT_W)), lin, lin, lin, kvb,
                (sds((t // P_CK, KV_W, P_CK), BF16), pl.BlockSpec((1, KV_W, tm), lambda i: (i // per_ck, 0, i % per_ck))),
                kvb,
                (sds((t // Q_BLOCK, KV_W, Q_BLOCK), BF16), pl.BlockSpec((tm // Q_BLOCK, KV_W, Q_BLOCK), lambda i: (i, 0, 0))),
                (sds((GATE_PAD, t), F32), cols(GATE_PAD)), (sds((t, CONV_W), BF16), rows(CONV_W)),
                (sds((8, CONV_W), F32), const(8, CONV_W))]
    else:
        outs = [(sds((t, ATT_W), F32), rows(ATT_W)), lin, lin, lin, (sds((t, GATE_PAD), F32), rows(GATE_PAD)),
                (sds((t, CONV_W), F32), rows(CONV_W)), (sds((t, CONV_W), F32), rows(CONV_W))]
    tab_spec = rows(HEAD_DIM) if carry else const(tm, HEAD_DIM)
    prev_spec = const(8, CONV_W) if carry else rows(CONV_W)
    return pl.pallas_call(
        functools.partial(_proj_kernel, tm=tm, seq_rows=seq_rows, carry=carry, rep=rep),
        grid=(t // tm,),
        in_specs=[rows(D_MODEL), mod_spec, mod_spec, const(1, D_MODEL), vmem, vmem, vmem,
                  tab_spec, tab_spec, prev_spec, prev_spec, const(CONV_K, CONV_W), const(1, CONV_W),
                  const(1, CONV_W)],
        out_specs=[spec for _, spec in outs],
        out_shape=[shape for shape, _ in outs],
        scratch_shapes=[pltpu.VMEM((8, CONV_W), F32)],
        compiler_params=_cparams("arbitrary"),
        name="proj",
    )(x, shift, scale, norm_g, *w_groups, cos, sin, prev1, prev2, conv_w, conv_b, norm_co)


def _cmp_kernel(x_ref, w_ref, o_ref, pad_scr, *, n_j, tm):
    pitch = n_j * KV_ROWS
    ppitch = pitch + SUBLANES

    def repitch(c, _):
        src = pl.multiple_of(c * pitch, SUBLANES)
        dst = pl.multiple_of(c * ppitch, SUBLANES)
        pad_scr[pl.ds(dst, pitch), :] = x_ref[pl.ds(src, pitch), :]
        return 0

    lax.fori_loop(0, tm, repitch, 0, unroll=8)
    flat = pad_scr
    for kv in range(2):
        for g in range(N_KV):
            xs = jnp.concatenate([flat[pl.ds(j * KV_ROWS + kv * N_KV + g, tm, stride=ppitch), :] for j in range(n_j)],
                                 axis=1)
            col = (kv * N_KV + g) * 2 * HEAD_DIM
            o_ref[:, col:col + 2 * HEAD_DIM] = _dot(xs.astype(BF16), w_ref[kv])


def _compress_products(x, w, n_j, tm):
    pitch = n_j * KV_ROWS
    m = x.shape[0] // pitch
    n_out = 2 * N_KV * 2 * HEAD_DIM
    tm = min(tm, m)
    return pl.pallas_call(
        functools.partial(_cmp_kernel, n_j=n_j, tm=tm),
        grid=(m // tm,),
        in_specs=[pl.BlockSpec((tm * pitch, HEAD_DIM), lambda i: (i, 0)),
                  pl.BlockSpec((2, n_j * HEAD_DIM, 2 * HEAD_DIM), lambda i: (0, 0, 0))],
        out_specs=pl.BlockSpec((tm, n_out), lambda i: (i, 0)),
        out_shape=jax.ShapeDtypeStruct((m, n_out), F32),
        scratch_shapes=[pltpu.VMEM((tm * (pitch + SUBLANES), HEAD_DIM), F32)],
        compiler_params=_cparams("arbitrary"),
        name="compress",
    )(x, w)


def _r_cols(kv, g):
    return (kv * N_KV + g) * 2 * HEAD_DIM


P_CK = 512
P_WKEYS = WINDOW + Q_BLOCK
P_TRIP = 4
N_FORCED = 3


def _pattn_t_kernel(qt_ref, gatet_ref, rp_ref, cosc_ref, sinc_ref, ks_ref, vst_ref, kw_ref, vwt_ref, et_ref, band_ref,
                    nao_ref, o_ref, kc_scr, vct_scr, pt_scr, ot_scr, s_scr, *, n_cmp_pad, n_blocks):
    i = pl.program_id(0)
    s0 = i * Q_BLOCK
    nq = Q_BLOCK
    rows = HEADS_PER_KV * nq
    lanes4 = lambda a: jnp.concatenate([a] * HEADS_PER_KV, axis=1)

    @pl.when(i == 0)
    def _():
        for g in range(N_KV):
            gs = slice(g * HEAD_DIM, (g + 1) * HEAD_DIM)
            ck, cv = _r_cols(0, g), _r_cols(1, g)
            kc = rp_ref[0:n_cmp_pad, ck:ck + HEAD_DIM] + rp_ref[pl.ds(1, n_cmp_pad), ck + HEAD_DIM:ck + 2 * HEAD_DIM]
            kc_scr[:, gs] = _rope(kc, cosc_ref[...], sinc_ref[...]).astype(BF16)
            vc = rp_ref[0:n_cmp_pad, cv:cv + HEAD_DIM] + rp_ref[pl.ds(1, n_cmp_pad), cv + HEAD_DIM:cv + 2 * HEAD_DIM]
            vct_scr[gs, :] = vc.T.astype(BF16)
        pt_scr[...] = jnp.zeros_like(pt_scr)

    t_q = s0 + lax.broadcasted_iota(jnp.int32, (n_cmp_pad, nq), 1)
    c_end = lax.broadcasted_iota(jnp.int32, (n_cmp_pad, nq), 0) * CMP_STRIDE + (CMP_BLOCK - 1)
    bias_c = lanes4(jnp.where(c_end <= t_q, 0.0, NEG))
    j_io = lax.broadcasted_iota(jnp.int32, (P_WKEYS, nq), 0)
    bias_w = lanes4(jnp.where(j_io >= WINDOW - s0, band_ref[...], NEG))

    blocks = [jnp.maximum(i + k - WINDOW // nq, 0) for k in range(P_WKEYS // nq)]
    scores, rhs, s_cmp, s_win = [], [], [], []
    for g in range(N_KV):
        gs = slice(g * HEAD_DIM, (g + 1) * HEAD_DIM)
        q_t = jnp.concatenate([qt_ref[(g * HEADS_PER_KV + r) * HEAD_DIM:(g * HEADS_PER_KV + r + 1) * HEAD_DIM, :]
                               for r in range(HEADS_PER_KV)], axis=1)
        rhs.append(q_t)
        s_cmp.append(_dot(kc_scr[:, gs], q_t))
        k_w = jnp.concatenate([kw_ref[pl.ds(pl.multiple_of(b * nq, nq), nq), gs] for b in blocks], axis=0)
        s_win.append(_dot(k_w, q_t))

    for g in range(N_KV):
        gs = slice(g * HEAD_DIM, (g + 1) * HEAD_DIM)
        s = s_cmp[g] + bias_c
        m = jnp.max(s, axis=0, keepdims=True)
        e = jnp.exp2(s - m)
        inv = jnp.where(m > 0.5 * NEG, 1.0 / jnp.maximum(jnp.sum(e, axis=0, keepdims=True), TINY), 0.0)
        o_c = _dot(vct_scr[gs, :], e.astype(BF16)) * inv

        p = e * inv
        pt_scr[SUBLANES:SUBLANES + n_cmp_pad, :] = sum(p[:, r * nq:(r + 1) * nq] for r in range(HEADS_PER_KV))
        st = lambda k: pt_scr[pl.ds(SUBLANES - 1 + k, n_blocks, stride=CHUNKS_PER_SEL), :]
        score = 0.5 * st(0) + st(1) + st(2) + st(3) + 0.5 * st(4)
        b_io = lax.broadcasted_iota(jnp.int32, (n_blocks, nq), 0)
        t_lane = s0 + lax.broadcasted_iota(jnp.int32, (n_blocks, nq), 1)
        cur = t_lane >> SEL_SHIFT
        forced = (b_io == 0) | (b_io == cur) | (b_io == cur - 1)
        valid = b_io * SEL_BLOCK <= t_lane
        scores.append((jnp.where(valid & jnp.logical_not(forced), score, -jnp.inf), jnp.where(forced, 1.0, 0.0)))

        v_wt = jnp.concatenate([vwt_ref[b, gs, :] for b in blocks], axis=1)
        s = s_win[g] + bias_w
        e = jnp.exp2(s - jnp.max(s, axis=0, keepdims=True))
        o_w = _dot(v_wt, e.astype(BF16)) / jnp.maximum(jnp.sum(e, axis=0, keepdims=True), TINY)

        for r in range(HEADS_PER_KV):
            hd = g * HEADS_PER_KV + r
            ls = slice(r * nq, (r + 1) * nq)
            gc = gatet_ref[hd * N_BRANCH + 0:hd * N_BRANCH + 1, :]
            gw = gatet_ref[hd * N_BRANCH + 2:hd * N_BRANCH + 3, :]
            ot_scr[hd * HEAD_DIM:(hd + 1) * HEAD_DIM, :] = gc * o_c[:, ls] + gw * o_w[:, ls]

    b_f = lax.broadcasted_iota(jnp.int32, (n_blocks, nq), 0).astype(F32)

    def pick(_, c):
        out = []
        for work, sel in c:
            m = jnp.max(work, axis=0, keepdims=True)
            idx = jnp.min(jnp.where(work == m, b_f, float(n_blocks)), axis=0, keepdims=True)
            hit = b_f == idx
            out.append((jnp.where(hit, -jnp.inf, work), jnp.where(hit, 1.0, sel)))
        return tuple(out)

    picked = lax.fori_loop(0, min(N_SEL, n_blocks) - N_FORCED, pick, tuple(scores))

    for g in range(N_KV):
        sel_bias = jnp.where(picked[g][1] > 0.5, 0.0, NEG).astype(BF16)
        rhs[g] = jnp.concatenate([rhs[g], lanes4(sel_bias)], axis=0)

    def qk_scores(c, slot):
        k0 = pl.multiple_of(c * P_CK, P_CK)
        blk_hot = et_ref[pl.ds(k0, P_CK), :]
        for g in range(N_KV):
            gs = slice(g * HEAD_DIM, (g + 1) * HEAD_DIM)
            s_scr[slot, g] = _dot(jnp.concatenate([ks_ref[pl.ds(k0, P_CK), gs], blk_hot], axis=1), rhs[g])

    ones_rows = jnp.ones((2 * SUBLANES, P_CK), BF16)

    def softmax_pv(c, slot, carry, causal):
        if causal:
            key = c * P_CK + lax.broadcasted_iota(jnp.int32, (P_CK, rows), 0)
            t_k = s0 + (lax.broadcasted_iota(jnp.int32, (P_CK, rows), 1) & (nq - 1))
            cb = jnp.where(key <= t_k, 0.0, NEG)
        out = []
        for g in range(N_KV):
            gs = slice(g * HEAD_DIM, (g + 1) * HEAD_DIM)
            m_i, l_i, acc = carry[g]
            s = s_scr[slot, g]
            if causal:
                s = s + cb
            m_n = jnp.maximum(m_i, jnp.max(s, axis=0, keepdims=True))
            p = jnp.exp2(s - m_n)
            alpha = jnp.exp2(m_i - m_n)
            pv = _dot(jnp.concatenate([vst_ref[c, gs, :], ones_rows], axis=0), p.astype(BF16))
            out.append((m_n, alpha * l_i + pv[HEAD_DIM:HEAD_DIM + 1], alpha * acc + pv[0:HEAD_DIM]))
        return tuple(out)

    def ahead(c, slot, carry):
        qk_scores(c + 1, 1 - slot)
        return softmax_pv(c, slot, carry, False)

    def trip(tr, carry):
        for k in range(P_TRIP):
            carry = ahead(P_TRIP * tr + k, k & 1, carry)
        return carry

    last = (s0 + nq + P_CK - 1) // P_CK - 1
    n_trips = last // P_TRIP
    init = tuple((jnp.full((1, rows), NEG, F32), jnp.zeros((1, rows), F32), jnp.zeros((HEAD_DIM, rows), F32))
                 for _ in range(N_KV))
    qk_scores(0, 0)
    carry = lax.fori_loop(0, n_trips, trip, init)
    done = P_TRIP * n_trips
    for k in range(P_TRIP - 1):
        carry = lax.cond(last - done > k, lambda cr, k=k: ahead(done + k, k & 1, cr), lambda cr: cr, carry)
    carry = lax.cond(((last - done) & 1) == 1, lambda cr: softmax_pv(last, 1, cr, True),
                     lambda cr: softmax_pv(last, 0, cr, True), carry)
    for g in range(N_KV):
        _, l_s, acc = carry[g]
        o_st = acc / jnp.maximum(l_s, TINY)
        for r in range(HEADS_PER_KV):
            hd = g * HEADS_PER_KV + r
            gsl = gatet_ref[hd * N_BRANCH + 1:hd * N_BRANCH + 2, :]
            ot_scr[hd * HEAD_DIM:(hd + 1) * HEAD_DIM, :] += gsl * o_st[:, r * nq:(r + 1) * nq]

    o_t = ot_scr[...]
    o_t = o_t * lax.rsqrt(jnp.mean(o_t * o_t, axis=0, keepdims=True) + EPS)
    o_ref[...] = (o_t.T * nao_ref[...]).astype(o_ref.dtype)


def _prompt_attention_t(q_t, gates_t, rp, cosc, sinc, ks, vs_t, kw, vw_t, e_t, band, norm_ao):
    s_len = q_t.shape[1]
    n_cmp_pad = s_len // CMP_STRIDE
    n_blocks = s_len // SEL_BLOCK
    vmem = pl.BlockSpec(memory_space=pltpu.VMEM)
    cols = lambda h: pl.BlockSpec((h, Q_BLOCK), lambda i: (0, i))
    return pl.pallas_call(
        functools.partial(_pattn_t_kernel, n_cmp_pad=n_cmp_pad, n_blocks=n_blocks),
        grid=(s_len // Q_BLOCK,),
        in_specs=[cols(ATT_W), cols(GATE_PAD), vmem, vmem, vmem, vmem, vmem, vmem, vmem, vmem, vmem,
                  pl.BlockSpec((1, ATT_W), lambda i: (0, 0))],
        out_specs=pl.BlockSpec((Q_BLOCK, ATT_W), lambda i: (i, 0)),
        out_shape=jax.ShapeDtypeStruct((s_len, ATT_W), BF16),
        scratch_shapes=[pltpu.VMEM((n_cmp_pad, KV_W), BF16), pltpu.VMEM((KV_W, n_cmp_pad), BF16),
                        pltpu.VMEM((n_cmp_pad + 2 * SUBLANES, Q_BLOCK), F32), pltpu.VMEM((ATT_W, Q_BLOCK), F32),
                        pltpu.VMEM((2, N_KV, P_CK, HEADS_PER_KV * Q_BLOCK), F32)],
        compiler_params=_cparams("arbitrary"),
        name="prompt_attn",
    )(q_t, gates_t, rp, cosc, sinc, ks, vs_t, kw, vw_t, e_t, band, norm_ao)


def _sattn_kernel(pt_ref, *refs, n_seqs, n_pages, page, n_new, past, wbuf):
    del pt_ref
    n_pg = n_seqs * n_pages
    r_pages, s_pages = refs[0:n_pg], refs[n_pg:2 * n_pg]
    (rnew_ref, q_ref, gate_ref, snew_ref, wst_ref, wnew_ref, cosc_ref, sinc_ref, e_ref, nao_ref,
     o_ref, wout_ref, r_scr, k_scr, v_scr, kw_scr, vw_scr, o_scr) = refs[2 * n_pg:]
    chunks_pp = page // CMP_STRIDE
    n_cmp = n_pages * chunks_pp
    rows = HEADS_PER_KV * n_new
    n_keys = k_scr.shape[1]
    n_wkeys = kw_scr.shape[1]
    pad = n_keys - past
    wpad = n_wkeys - wbuf
    batch = [(sq, g) for sq in range(n_seqs) for g in range(N_KV)]
    cat = lambda parts: jnp.concatenate(parts, axis=0)

    def with_zero_rows(new_rows, n_zero):
        return cat([new_rows, jnp.zeros((n_zero, HEAD_DIM), F32)]).astype(BF16)

    for sq in range(n_seqs):
        n0 = sq * n_new * KV_ROWS
        w0 = sq * wbuf * KV_ROWS
        for p in range(n_pages):
            r_scr[sq, p * chunks_pp:(p + 1) * chunks_pp, :] = r_pages[sq * n_pages + p][...]
            pg = s_pages[sq * n_pages + p]
            for g in range(N_KV):
                b = sq * N_KV + g
                k_scr[b, p * page:(p + 1) * page, :] = pg[pl.ds(g, page, stride=KV_ROWS), :].astype(BF16)
                v_scr[b, p * page:(p + 1) * page, :] = pg[pl.ds(N_KV + g, page, stride=KV_ROWS), :].astype(BF16)
        r_scr[sq, n_cmp:n_cmp + 8, :] = cat([rnew_ref[sq], jnp.zeros((7, r_scr.shape[2]), F32)])
        for g in range(N_KV):
            b = sq * N_KV + g
            k_scr[b, past:n_keys, :] = with_zero_rows(snew_ref[pl.ds(n0 + g, n_new, stride=KV_ROWS), :], pad - n_new)
            v_scr[b, past:n_keys, :] = with_zero_rows(snew_ref[pl.ds(n0 + N_KV + g, n_new, stride=KV_ROWS), :],
                                                      pad - n_new)
            kw_scr[b, 0:wbuf, :] = wst_ref[pl.ds(w0 + g, wbuf, stride=KV_ROWS), :].astype(BF16)
            vw_scr[b, 0:wbuf, :] = wst_ref[pl.ds(w0 + N_KV + g, wbuf, stride=KV_ROWS), :].astype(BF16)
            kw_scr[b, wbuf:n_wkeys, :] = with_zero_rows(wnew_ref[pl.ds(n0 + g, n_new, stride=KV_ROWS), :],
                                                        wpad - n_new)
            vw_scr[b, wbuf:n_wkeys, :] = with_zero_rows(wnew_ref[pl.ds(n0 + N_KV + g, n_new, stride=KV_ROWS), :],
                                                        wpad - n_new)
        keep = (wbuf - n_new) * KV_ROWS
        wout_ref[w0:w0 + keep, :] = wst_ref[w0 + n_new * KV_ROWS:w0 + wbuf * KV_ROWS, :]
        wout_ref[w0 + keep:w0 + wbuf * KV_ROWS, :] = wnew_ref[n0:n0 + n_new * KV_ROWS, :]

    n_b = len(batch)
    all_rows = n_b * rows
    tok = lax.broadcasted_iota(jnp.int32, (all_rows, 1), 0) & (n_new - 1)
    t_rows = past + tok

    qg, kc, vc = [], [], []
    for sq, g in batch:
        qs = slice(sq * n_new, (sq + 1) * n_new)
        qg.append(cat([q_ref[qs, (g * HEADS_PER_KV + r) * HEAD_DIM:(g * HEADS_PER_KV + r + 1) * HEAD_DIM]
                       for r in range(HEADS_PER_KV)]).astype(BF16))
        ck, cv = _r_cols(0, g), _r_cols(1, g)
        kc.append(r_scr[sq, 0:n_cmp, ck:ck + HEAD_DIM] + r_scr[sq, pl.ds(1, n_cmp), ck + HEAD_DIM:ck + 2 * HEAD_DIM])
        vc.append((r_scr[sq, 0:n_cmp, cv:cv + HEAD_DIM]
                   + r_scr[sq, pl.ds(1, n_cmp), cv + HEAD_DIM:cv + 2 * HEAD_DIM]).astype(BF16))
    cos_all, sin_all = cat([cosc_ref[...]] * n_b), cat([sinc_ref[...]] * n_b)
    kc_all = _rope(cat(kc), cos_all, sin_all).astype(BF16)
    rb = lambda b: slice(b * rows, (b + 1) * rows)

    s_c = cat([_dot_nt(qg[b], kc_all[b * n_cmp:(b + 1) * n_cmp]) for b in range(n_b)])
    s_w = cat([_dot_nt(qg[b], kw_scr[b]) for b in range(n_b)])
    s_s = cat([_dot_nt(qg[b], k_scr[b]) for b in range(n_b)])

    c_end = lax.broadcasted_iota(jnp.int32, (all_rows, n_cmp), 1) * CMP_STRIDE + (CMP_BLOCK - 1)
    p_c = _masked_softmax(s_c, c_end <= t_rows)
    p_cb = p_c.astype(BF16)
    o_c = [_dot(p_cb[rb(b)], vc[b]) for b in range(n_b)]

    sel_rows = n_b * n_new
    p_grp = cat([sum(p_c[b * rows + r * n_new:b * rows + (r + 1) * n_new] for r in range(HEADS_PER_KV))
                 for b in range(n_b)])
    lane = lax.broadcasted_iota(jnp.int32, (sel_rows, HEAD_DIM), 1)
    pch = 0.5 * (p_grp + jnp.where(lane >= 1, pltpu.roll(p_grp, 1, axis=1), 0.0))
    score = pch
    for k in range(1, CHUNKS_PER_SEL):
        score = score + pltpu.roll(pch, HEAD_DIM - k, axis=1)
    blk = lane >> (CHUNKS_PER_SEL.bit_length() - 1)
    t_tok = past + (lax.broadcasted_iota(jnp.int32, (sel_rows, HEAD_DIM), 0) & (n_new - 1))
    cur = t_tok >> SEL_SHIFT
    forced = (blk == 0) | (blk == cur) | (blk == cur - 1)
    score = jnp.where(forced, jnp.inf, score)
    n_pb = past // SEL_BLOCK
    ahead = jnp.zeros((sel_rows, HEAD_DIM), F32)
    for k in range(1, n_pb):
        other = pltpu.roll(score, CHUNKS_PER_SEL * k, axis=1)
        wins = (other > score) | ((other == score) & (blk >= k))
        ahead = ahead + jnp.where(wins, 1.0, 0.0)
    sel = jnp.where(((lane & (CHUNKS_PER_SEL - 1)) == 0) & (ahead < N_SEL - 1), 1.0, 0.0).astype(BF16)
    sel_keys = _dot(sel, e_ref[...])
    new_ok = (lax.broadcasted_iota(jnp.int32, (sel_rows, pad), 1)
              <= (lax.broadcasted_iota(jnp.int32, (sel_rows, pad), 0) & (n_new - 1)))
    bias = jnp.concatenate([jnp.where(sel_keys > 0.5, 0.0, NEG), jnp.where(new_ok, 0.0, NEG)], axis=1)
    bias = cat([bias[b * n_new:(b + 1) * n_new] for b in range(n_b) for _ in range(HEADS_PER_KV)])

    s_s = s_s + bias
    e_s = jnp.exp2(s_s - jnp.max(s_s, axis=-1, keepdims=True))
    p_s = (e_s / jnp.maximum(jnp.sum(e_s, axis=-1, keepdims=True), TINY)).astype(BF16)
    o_s = [_dot(p_s[rb(b)], v_scr[b]) for b in range(n_b)]

    j_io = lax.broadcasted_iota(jnp.int32, (all_rows, n_wkeys), 1)
    m_w = (((j_io < wbuf) & (j_io > tok + (wbuf - WINDOW)) & (j_io >= wbuf - past))
           | ((j_io >= wbuf) & (j_io - wbuf <= tok)))
    p_w = _masked_softmax(s_w, m_w).astype(BF16)
    o_w = [_dot(p_w[rb(b)], vw_scr[b]) for b in range(n_b)]

    for b, (sq, g) in enumerate(batch):
        qs = slice(sq * n_new, (sq + 1) * n_new)
        for r in range(HEADS_PER_KV):
            hd = g * HEADS_PER_KV + r
            rs = slice(r * n_new, (r + 1) * n_new)
            gc = gate_ref[qs, hd * N_BRANCH + 0:hd * N_BRANCH + 1]
            gsl = gate_ref[qs, hd * N_BRANCH + 1:hd * N_BRANCH + 2]
            gw = gate_ref[qs, hd * N_BRANCH + 2:hd * N_BRANCH + 3]
            o_scr[qs, hd * HEAD_DIM:(hd + 1) * HEAD_DIM] = gc * o_c[b][rs] + gsl * o_s[b][rs] + gw * o_w[b][rs]

    o_ref[...] = _rms(o_scr[...]) * nao_ref[...]


S_SEQS = 2


def _sample_attention(page_table, r_all, r_new, q, gates, slc_cache, slc_new, win_state, win_new, cosc, sinc, e_mat,
                      norm_ao, *, n_new, page, wbuf):
    n_seq, n_pages = page_table.shape
    past = n_pages * page
    chunks_pp = page // CMP_STRIDE
    n_cmp = n_pages * chunks_pp
    assert n_cmp == HEAD_DIM and n_new == 8 and past % SEL_BLOCK == 0 and n_new <= SEL_BLOCK
    assert (past + n_new - 1) // SEL_BLOCK == past // SEL_BLOCK and wbuf == WINDOW and past >= WINDOW
    n_keys = past + HEAD_DIM
    n_wkeys = wbuf + HEAD_DIM
    r_w = r_all.shape[1]

    ns = S_SEQS
    assert n_seq % ns == 0
    page_map = lambda sq, p: (lambda b, pt: (pt[b * ns + sq, p], 0))
    in_specs = [pl.BlockSpec((chunks_pp, r_w), page_map(sq, p)) for sq in range(ns) for p in range(n_pages)]
    in_specs += [pl.BlockSpec((page * KV_ROWS, HEAD_DIM), page_map(sq, p)) for sq in range(ns) for p in range(n_pages)]
    seq_rows = lambda wdt: pl.BlockSpec((ns * n_new, wdt), lambda b, pt: (b, 0))
    kv_rows = lambda n_tok: pl.BlockSpec((ns * n_tok * KV_ROWS, HEAD_DIM), lambda b, pt: (b, 0))
    const = lambda shape: pl.BlockSpec(shape, lambda b, pt: (0,) * len(shape))
    in_specs += [pl.BlockSpec((ns, 1, r_w), lambda b, pt: (b, 0, 0)), seq_rows(ATT_W), seq_rows(GATE_PAD),
                 kv_rows(n_new), kv_rows(wbuf), kv_rows(n_new),
                 const((n_cmp, HEAD_DIM)), const((n_cmp, HEAD_DIM)), const((HEAD_DIM, past)), const((1, ATT_W))]
    grid_spec = pltpu.PrefetchScalarGridSpec(
        num_scalar_prefetch=1,
        grid=(n_seq // ns,),
        in_specs=in_specs,
        out_specs=[seq_rows(ATT_W), kv_rows(wbuf)],
        scratch_shapes=[pltpu.VMEM((ns, n_cmp + 8, r_w), F32), pltpu.VMEM((ns * N_KV, n_keys, HEAD_DIM), BF16),
                        pltpu.VMEM((ns * N_KV, n_keys, HEAD_DIM), BF16), pltpu.VMEM((ns * N_KV, n_wkeys, HEAD_DIM), BF16),
                        pltpu.VMEM((ns * N_KV, n_wkeys, HEAD_DIM), BF16), pltpu.VMEM((ns * n_new, ATT_W), F32)],
    )
    return pl.pallas_call(
        functools.partial(_sattn_kernel, n_seqs=ns, n_pages=n_pages, page=page, n_new=n_new, past=past, wbuf=wbuf),
        grid_spec=grid_spec,
        out_shape=[jax.ShapeDtypeStruct((n_seq * n_new, ATT_W), F32),
                   jax.ShapeDtypeStruct((n_seq * wbuf * KV_ROWS, HEAD_DIM), F32)],
        compiler_params=_cparams("arbitrary"),
        name="sample_attn",
    )(page_table, *([r_all] * (ns * n_pages)), *([slc_cache] * (ns * n_pages)), r_new, q, gates, slc_new, win_state,
      win_new, cosc, sinc, e_mat, norm_ao)


def _outproj_kernel(x_ref, oa_ref, oc_ref, gt_ref, w_ref, o_ref, *, rep):
    y = _dot(oa_ref[...].astype(BF16), w_ref[0:ATT_W, :]) + _dot(oc_ref[...].astype(BF16), w_ref[ATT_W:D_MODEL, :])
    o_ref[...] = x_ref[...] + (1.0 + _mod_rows(gt_ref, rep)) * y


def _outproj(x, oa, oc, gate, w, tm=512):
    t = x.shape[0]
    mod_spec, rep = _mod_spec(gate.shape[0], t, tm, 1)
    rows = lambda wdt: pl.BlockSpec((tm, wdt), lambda i: (i, 0))
    return pl.pallas_call(
        functools.partial(_outproj_kernel, rep=rep),
        grid=(t // tm,),
        in_specs=[rows(D_MODEL), rows(ATT_W), rows(CONV_W), mod_spec, pl.BlockSpec(memory_space=pltpu.VMEM)],
        out_specs=rows(D_MODEL),
        out_shape=jax.ShapeDtypeStruct((t, D_MODEL), F32),
        compiler_params=_cparams("arbitrary"),
        name="outproj",
    )(x, oa, oc, gate, w)


def _rope_tables(pos):
    half = HEAD_DIM // 2
    inv = np.float32(ROPE_THETA) ** (-np.arange(half, dtype=np.float32) * np.float32(2.0) / np.float32(HEAD_DIM))
    ang = np.asarray(pos, np.float32)[:, None] * inv[None, :]
    cos, sin = np.cos(ang), np.sin(ang)
    return (jnp.asarray(np.concatenate([cos, cos], axis=1), F32),
            jnp.asarray(np.concatenate([-sin, sin], axis=1), F32))


def _pack_kernel(w_ref, a_ref, c_ref, g_ref):
    off_g = ATT_W + 3 * SLAB_W
    n_gate = N_HEADS * N_BRANCH
    a_ref[...] = w_ref[:, 0:off_g].astype(BF16)
    c_ref[...] = w_ref[:, off_g + n_gate:off_g + n_gate + 3 * CONV_W].astype(BF16)
    g = w_ref[:, off_g:off_g + GATE_PAD]
    lane = lax.broadcasted_iota(jnp.int32, g.shape, 1)
    g_ref[...] = jnp.where(lane < n_gate, g, 0.0).astype(BF16)


def _pack_w_in(w_in, tm=256):
    d, n = w_in.shape
    widths = (ATT_W + 3 * SLAB_W, 3 * CONV_W, GATE_PAD)
    return pl.pallas_call(
        _pack_kernel,
        grid=(d // tm,),
        in_specs=[pl.BlockSpec((tm, n), lambda i: (i, 0))],
        out_specs=[pl.BlockSpec((tm, w), lambda i: (i, 0)) for w in widths],
        out_shape=[jax.ShapeDtypeStruct((d, w), BF16) for w in widths],
        compiler_params=_cparams("arbitrary"),
        name="pack_w_in",
    )(w_in)


def _pack_w_cmp(w_ck, w_cv, n_j):
    def one(w):
        lo = w[0:n_j].reshape(n_j * HEAD_DIM, HEAD_DIM)
        hi = w[CMP_STRIDE:CMP_STRIDE + n_j].reshape(n_j * HEAD_DIM, HEAD_DIM)
        return jnp.concatenate([lo, hi], axis=1)
    return jnp.stack([one(w_ck), one(w_cv)]).astype(BF16)


def kernel(x_prompt, x_sample, c_prompt, c_sample, cache_cmp_kv, cache_slc_kv, state_win_kv, state_conv, page_table,
           w_ada, b_ada, norm_ffn1, ffn1_gate, ffn1_up, ffn1_down, norm_mix, w_in, w_cmp_k, w_cmp_v, conv_w, conv_b,
           norm_att_out, norm_conv_out, w_out, norm_ffn2, ffn2_gate, ffn2_up, ffn2_down, norm_final):
    n_p, s_len, _ = x_prompt.shape
    n_seq, n_new, _ = x_sample.shape
    depth = w_ada.shape[0]
    assert n_p == 1 and depth == 1
    n_pages = page_table.shape[1]
    page = cache_slc_kv.shape[2]
    n_phys = cache_slc_kv.shape[1]
    past = n_pages * page
    wbuf = state_win_kv.shape[2]
    keep_p = min(WINDOW, s_len)
    t_s = n_seq * n_new
    l = 0

    c_all = jnp.concatenate([c_sample, c_prompt, jnp.zeros((8 - n_p, D_MODEL), F32)], axis=0)
    mod = _ada(c_all, w_ada[l], b_ada[l])
    mod_p = [mod[n_seq:n_seq + 1, k * D_MODEL:(k + 1) * D_MODEL] for k in range(N_MOD)]
    mod_s = [mod[0:n_seq, k * D_MODEL:(k + 1) * D_MODEL] for k in range(N_MOD)]

    row = lambda v: v.reshape(1, -1)
    w_proj = _pack_w_in(w_in[l])
    w_o = w_out[l].astype(BF16)
    nfin = row(norm_final)

    xp = x_prompt.reshape(s_len, D_MODEL)
    xs = x_sample.reshape(t_s, D_MODEL)

    xs, *f1 = _ffn_stream(xs, mod_s[0], mod_s[1], mod_s[2], row(norm_ffn1[l]), nfin, ffn1_gate[l], ffn1_up[l],
                          ffn1_down[l], final_norm=False)
    xp = _ffn(xp, mod_p[0], mod_p[1], mod_p[2], row(norm_ffn1[l]), nfin, *f1, final_norm=False)

    cos_p, sin_p = _rope_tables(np.arange(s_len))
    tm_s = 256
    cos_s, sin_s = _rope_tables(np.tile(past + np.arange(n_new), tm_s // n_new))
    zero8 = jnp.zeros((8, CONV_W), F32)
    conv_args = (conv_w[l], row(conv_b[l]), row(norm_conv_out[l]))
    (qt_p, cmp_p, slc_p, win_p, ksb_p, vst_p, kwb_p, vwt_p, gatet_p, ocn_p, utail_p) = _proj(
        xp, mod_p[3], mod_p[4], row(norm_mix[l]), w_proj, cos_p, sin_p, zero8, zero8, *conv_args,
        carry=True, seq_rows=s_len)
    prev1 = jnp.repeat(state_conv[l][:, CONV_K - 2], n_new, axis=0)
    prev2 = jnp.repeat(state_conv[l][:, CONV_K - 3], n_new, axis=0)
    (q_s, cmp_s, slc_s, win_s, gate_s, ocn_s, u_s) = _proj(
        xs, mod_s[3], mod_s[4], row(norm_mix[l]), w_proj, cos_s, sin_s, prev1, prev2, *conv_args,
        carry=False, seq_rows=n_new, tm=tm_s)

    n_j = CMP_STRIDE
    w_c = _pack_w_cmp(w_cmp_k[l], w_cmp_v[l], n_j)
    lin = lambda a: a.reshape(-1, HEAD_DIM)
    r_p = _compress_products(cmp_p, w_c, n_j, tm=256)
    r_cache = _compress_products(lin(cache_cmp_kv), w_c, n_j, tm=256)
    r_new = _compress_products(cmp_s, _pack_w_cmp(w_cmp_k[l], w_cmp_v[l], n_new), n_new, tm=n_seq)

    n_cmp_pad = s_len // CMP_STRIDE
    cosc, sinc = _rope_tables(np.arange(n_cmp_pad) * CMP_STRIDE + (CMP_BLOCK - 1))
    r_p = jnp.pad(r_p, ((0, 8), (0, 0)))
    n_blocks = s_len // SEL_BLOCK
    e_t = ((jnp.arange(s_len) // SEL_BLOCK)[:, None] == jnp.arange(n_blocks)[None, :]).astype(BF16)
    j_w, ti_w = jnp.arange(P_WKEYS)[:, None], jnp.arange(Q_BLOCK)[None, :]
    band = jnp.where((j_w > ti_w) & (j_w <= ti_w + WINDOW), 0.0, NEG).astype(F32)
    oa_p = _prompt_attention_t(qt_p, gatet_p, r_p, cosc, sinc, ksb_p, vst_p, kwb_p, vwt_p, e_t, band,
                               row(norm_att_out[l]))

    n_cmp_s = past // CMP_STRIDE
    cosc_s, sinc_s = _rope_tables(np.arange(n_cmp_s) * CMP_STRIDE + (CMP_BLOCK - 1))
    e_s =(jnp.arange(HEAD_DIM)[:, None] == (jnp.arange(past) // SEL_BLOCK * CHUNKS_PER_SEL)[None, :]).astype(BF16)
    oa_s, win_new_state = _sample_attention(
        page_table, r_cache, r_new.reshape(n_seq, 1, -1), q_s, gate_s,
        lin(cache_slc_kv), slc_s, lin(state_win_kv), win_s,
        cosc_s, sinc_s, e_s, row(norm_att_out[l]), n_new=n_new, page=page, wbuf=wbuf)

    xp = _outproj(xp, oa_p, ocn_p, mod_p[5], w_o)
    xs = _outproj(xs, oa_s, ocn_s, mod_s[5], w_o)
    ys, *f2 = _ffn_stream(xs, mod_s[6], mod_s[7], mod_s[8], row(norm_ffn2[l]), nfin, ffn2_gate[l], ffn2_up[l],
                          ffn2_down[l], final_norm=True)
    yp = _ffn(xp, mod_p[6], mod_p[7], mod_p[8], row(norm_ffn2[l]), nfin, *f2, final_norm=True)

    kv6 = lambda a, n, s: a.reshape(1, n, s, 2, N_KV, HEAD_DIM)
    return (yp.reshape(n_p, s_len, D_MODEL), ys.reshape(n_seq, n_new, D_MODEL),
            kv6(cmp_p, n_p, s_len), kv6(slc_p, n_p, s_len), kv6(win_p[(s_len - keep_p) * KV_ROWS:], n_p, keep_p),
            utail_p[8 - (CONV_K - 1):].reshape(1, n_p, CONV_K - 1, CONV_W),
            kv6(cmp_s, n_seq, n_new), kv6(slc_s, n_seq, n_new), kv6(win_new_state, n_seq, wbuf),
            u_s.reshape(n_seq, n_new, CONV_W)[:, n_new - (CONV_K - 1):].reshape(1, n_seq, CONV_K - 1, CONV_W))
```

```python
import functools

import jax
import jax.numpy as jnp
import numpy as np
from jax import lax
from jax.experimental import pallas as pl
from jax.experimental.pallas import tpu as pltpu

F32 = jnp.float32
BF16 = jnp.bfloat16

D_MODEL = 2048
HEAD_DIM = 128
N_HEADS = 8
N_KV = 2
HEADS_PER_KV = N_HEADS // N_KV
ATT_W = N_HEADS * HEAD_DIM
KV_W = N_KV * HEAD_DIM
CONV_W = D_MODEL - ATT_W
CONV_K = 3
CMP_BLOCK = 32
CMP_STRIDE = 16
SEL_BLOCK = 64
N_SEL = 16
WINDOW = 512
Q_BLOCK = 128
N_BRANCH = 3
N_MOD = 9
ROPE_THETA = 10000.0
EPS = 1e-6
NEG = -1e30
TINY = 1e-30
SCALE = HEAD_DIM ** -0.5
LOG2E = 1.4426950408889634
SLAB_W = 2 * KV_W
KV_ROWS = 2 * N_KV
GATE_PAD = 128
SUBLANES = 8
CHUNKS_PER_SEL = SEL_BLOCK // CMP_STRIDE
SEL_SHIFT = SEL_BLOCK.bit_length() - 1

VMEM_LIMIT = 56 * 1024 * 1024


def _cparams(*sem):
    return pltpu.CompilerParams(dimension_semantics=sem, vmem_limit_bytes=VMEM_LIMIT)


def _dot(a, b):
    return jnp.dot(a, b, preferred_element_type=F32)


def _dot_nt(a, b):
    return lax.dot_general(a, b, (((1,), (1,)), ((), ())), preferred_element_type=F32)


def _rms(x):
    return x * lax.rsqrt(jnp.mean(x * x, axis=-1, keepdims=True) + EPS)


def _silu(x):
    return x * jax.nn.sigmoid(x)


def _rope(x, cos, sin_signed):
    return x * cos + pltpu.roll(x, HEAD_DIM // 2, axis=1) * sin_signed


def _masked_softmax(s, mask):
    s = jnp.where(mask, s, NEG)
    m = jnp.max(s, axis=-1, keepdims=True)
    e = jnp.where(mask, jnp.exp2(s - m), 0.0)
    return e / jnp.maximum(jnp.sum(e, axis=-1, keepdims=True), TINY)


def _mod_rows(ref, rep):
    m = ref[...]
    return m if rep == 1 else jnp.repeat(m, rep, axis=0)


def _mod_spec(mrows, t, tm, n_grid_axes):
    rep = 1 if mrows == 1 else t // mrows
    shape = (1, D_MODEL) if mrows == 1 else (tm // rep, D_MODEL)
    first = (lambda i: 0) if mrows == 1 else (lambda i: i)
    index = (lambda i: (first(i), 0)) if n_grid_axes == 1 else (lambda i, j: (first(i), 0))
    return pl.BlockSpec(shape, index), rep


def _ada_kernel(c_ref, w_ref, b_ref, o_ref):
    a = _silu(c_ref[...]).astype(BF16)
    o_ref[...] = _dot(a, w_ref[...].astype(BF16)) + b_ref[...]


def _ada(c, w, b, tn=1024):
    m, n = c.shape[0], w.shape[1]
    return pl.pallas_call(
        _ada_kernel,
        grid=(n // tn,),
        in_specs=[pl.BlockSpec((m, D_MODEL), lambda j: (0, 0)),
                  pl.BlockSpec((D_MODEL, tn), lambda j: (0, j)),
                  pl.BlockSpec((1, tn), lambda j: (0, j))],
        out_specs=pl.BlockSpec((m, tn), lambda j: (0, j)),
        out_shape=jax.ShapeDtypeStruct((m, n), F32),
        compiler_params=_cparams("arbitrary"),
        name="ada",
    )(c, w, b.reshape(1, n))


def _ffn_kernel(x_ref, sh_ref, sc_ref, gt_ref, ng_ref, nf_ref, wg_ref, wu_ref, wd_ref, o_ref, h_scr, *, n_f, final_norm,
                rep):
    j = pl.program_id(1)
    mod = lambda ref: _mod_rows(ref, rep)

    @pl.when(j == 0)
    def _():
        h = _rms(x_ref[...]) * (ng_ref[...] * (1.0 + mod(sc_ref))) + mod(sh_ref)
        h_scr[...] = h.astype(BF16)
        o_ref[...] = jnp.zeros_like(o_ref)

    h = h_scr[...]
    a = (_silu(_dot(h, wg_ref[...])) * _dot(h, wu_ref[...])).astype(BF16)
    o_ref[...] += _dot(a, wd_ref[...])

    @pl.when(j == n_f - 1)
    def _():
        out = x_ref[...] + (0.5 * (1.0 + mod(gt_ref))) * o_ref[...]
        if final_norm:
            out = _rms(out) * nf_ref[...]
        o_ref[...] = out


def _ffn(x, shift, scale, gate, norm_g, norm_final, wg, wu, wd, *, final_norm, tm=512, tf=512):
    t = x.shape[0]
    d_ff = wg.shape[1]
    n_f = d_ff // tf
    mod_spec, rep = _mod_spec(shift.shape[0], t, tm, 2)
    row_spec = pl.BlockSpec((tm, D_MODEL), lambda i, j: (i, 0))
    vec_spec = pl.BlockSpec((1, D_MODEL), lambda i, j: (0, 0))
    return pl.pallas_call(
        functools.partial(_ffn_kernel, n_f=n_f, final_norm=final_norm, rep=rep),
        grid=(t // tm, n_f),
        in_specs=[row_spec, mod_spec, mod_spec, mod_spec, vec_spec, vec_spec,
                  pl.BlockSpec((D_MODEL, tf), lambda i, j: (0, j)),
                  pl.BlockSpec((D_MODEL, tf), lambda i, j: (0, j)),
                  pl.BlockSpec((tf, D_MODEL), lambda i, j: (j, 0))],
        out_specs=row_spec,
        out_shape=jax.ShapeDtypeStruct((t, D_MODEL), F32),
        scratch_shapes=[pltpu.VMEM((tm, D_MODEL), BF16)],
        compiler_params=_cparams("arbitrary", "arbitrary"),
        name="ffn",
    )(x, shift, scale, gate, norm_g, norm_final, wg, wu, wd)


FFN_ROW_CHUNK = 128


def _ffn_stream_kernel(x_ref, sh_ref, sc_ref, gt_ref, ng_ref, nf_ref, wg_ref, wu_ref, wd_ref,
                       o_ref, wgb_ref, wub_ref, wdb_ref, h_scr, *, n_f, final_norm, rep):
    j = pl.program_id(0)
    t = x_ref.shape[0]
    rc = FFN_ROW_CHUNK

    def rows_of(ref, c, width):
        return jnp.repeat(ref[pl.ds(pl.multiple_of(c * (width // rep), width // rep), width // rep), :], rep, axis=0)

    @pl.when(j == 0)
    def _():
        def body(c, _):
            r0 = pl.multiple_of(c * rc, rc)
            h = _rms(x_ref[pl.ds(r0, rc), :]) * ng_ref[...] * (1.0 + rows_of(sc_ref, c, rc)) + rows_of(sh_ref, c, rc)
            h_scr[pl.ds(r0, rc), :] = h.astype(BF16)
            return 0
        lax.fori_loop(0, t // rc, body, 0)
        o_ref[...] = jnp.zeros_like(o_ref)

    wg, wu, wd = wg_ref[...].astype(BF16), wu_ref[...].astype(BF16), wd_ref[...].astype(BF16)
    wgb_ref[...] = wg
    wub_ref[...] = wu
    wdb_ref[...] = wd
    h = h_scr[...]
    a = (_silu(_dot(h, wg)) * _dot(h, wu)).astype(BF16)
    o_ref[...] += _dot(a, wd)

    @pl.when(j == n_f - 1)
    def _():
        def body(c, _):
            r0 = pl.multiple_of(c * rc, rc)
            out = x_ref[pl.ds(r0, rc), :] + 0.5 * (1.0 + rows_of(gt_ref, c, rc)) * o_ref[pl.ds(r0, rc), :]
            if final_norm:
                out = _rms(out) * nf_ref[...]
            o_ref[pl.ds(r0, rc), :] = out
            return 0
        lax.fori_loop(0, t // rc, body, 0)


def _ffn_stream(x, shift, scale, gate, norm_g, norm_final, wg, wu, wd, *, final_norm, tf=256):
    t = x.shape[0]
    d_ff = wg.shape[1]
    n_f = d_ff // tf
    rep = t // shift.shape[0]
    assert t % FFN_ROW_CHUNK == 0 and FFN_ROW_CHUNK % rep == 0
    vmem = pl.BlockSpec(memory_space=pltpu.VMEM)
    col_tile = pl.BlockSpec((D_MODEL, tf), lambda j: (0, j))
    row_tile = pl.BlockSpec((tf, D_MODEL), lambda j: (j, 0))
    return pl.pallas_call(
        functools.partial(_ffn_stream_kernel, n_f=n_f, final_norm=final_norm, rep=rep),
        grid=(n_f,),
        in_specs=[vmem, vmem, vmem, vmem, vmem, vmem, col_tile, col_tile, row_tile],
        out_specs=[pl.BlockSpec((t, D_MODEL), lambda j: (0, 0)), col_tile, col_tile, row_tile],
        out_shape=[jax.ShapeDtypeStruct((t, D_MODEL), F32), jax.ShapeDtypeStruct(wg.shape, BF16),
                   jax.ShapeDtypeStruct(wu.shape, BF16), jax.ShapeDtypeStruct(wd.shape, BF16)],
        scratch_shapes=[pltpu.VMEM((t, D_MODEL), BF16)],
        compiler_params=_cparams("arbitrary"),
        name="ffn_stream",
    )(x, shift, scale, gate, norm_g, norm_final, wg, wu, wd)


def _proj_kernel(x_ref, sh_ref, sc_ref, ng_ref, wa_ref, wc_ref, wg_ref, cos_ref, sin_ref, p1_ref, p2_ref, cw_ref,
                 cb_ref, nco_ref, *rest, tm, seq_rows, carry, rep):
    if carry:
        q_ref, cmp_ref, slc_ref, win_ref, ksb_ref, vst_ref, kwb_ref, vwt_ref, gate_ref, ocn_ref, u_ref, carry_scr = rest
    else:
        q_ref, cmp_ref, slc_ref, win_ref, gate_ref, ocn_ref, u_ref, carry_scr = rest
        ksb_ref = kwb_ref = None
    i = pl.program_id(0)
    h = (_rms(x_ref[...]) * ng_ref[...] * (1.0 + _mod_rows(sc_ref, rep)) + _mod_rows(sh_ref, rep)).astype(BF16)
    cos, sin = cos_ref[...], sin_ref[...]

    pq = _dot(h, wa_ref[:, 0:ATT_W])
    for hd in range(N_HEADS):
        hs = slice(hd * HEAD_DIM, (hd + 1) * HEAD_DIM)
        blk = _rope(pq[:, hs], cos, sin) * (SCALE * LOG2E)
        if carry:
            q_ref[hs, :] = blk.T.astype(q_ref.dtype)
        else:
            q_ref[:, hs] = blk.astype(q_ref.dtype)

    pkv = _dot(h, wa_ref[:, ATT_W:ATT_W + 3 * SLAB_W])
    for slab, (o_ref, kb_ref) in enumerate(((cmp_ref, None), (slc_ref, ksb_ref), (win_ref, kwb_ref))):
        base = slab * SLAB_W
        for g in range(N_KV):
            gs = slice(g * HEAD_DIM, (g + 1) * HEAD_DIM)
            k = pkv[:, base + g * HEAD_DIM:base + (g + 1) * HEAD_DIM]
            v = pkv[:, base + KV_W + g * HEAD_DIM:base + KV_W + (g + 1) * HEAD_DIM]
            if slab > 0:
                k = _rope(k, cos, sin)
            if kb_ref is not None:
                kb_ref[:, gs] = k.astype(BF16)
                v_t = v.T.astype(BF16)
                if slab == 1:
                    vst_ref[0, gs, :] = v_t
                else:
                    for blk in range(tm // Q_BLOCK):
                        vwt_ref[blk, gs, :] = v_t[:, blk * Q_BLOCK:(blk + 1) * Q_BLOCK]
            o_ref[pl.ds(g, tm, stride=KV_ROWS), :] = k
            o_ref[pl.ds(N_KV + g, tm, stride=KV_ROWS), :] = v

    gate = jax.nn.sigmoid(_dot(h, wg_ref[...]))
    gate_ref[...] = gate.T if carry else gate

    pc = _dot(h, wc_ref[...])
    u = pc[:, 0:CONV_W] * pc[:, 2 * CONV_W:3 * CONV_W]
    c_out = pc[:, CONV_W:2 * CONV_W]
    row = lax.broadcasted_iota(jnp.int32, (tm, CONV_W), 0)
    if carry:
        @pl.when(i == 0)
        def _():
            carry_scr[...] = jnp.zeros_like(carry_scr)
        prev1 = carry_scr[7:8, :]
        prev2 = carry_scr[6:7, :]
        rs = row
    else:
        prev1 = p1_ref[...]
        prev2 = p2_ref[...]
        rs = row & (seq_rows - 1)
    um1 = jnp.where(rs >= 1, pltpu.roll(u, 1, axis=0), prev1)
    um2 = jnp.where(rs >= 2, pltpu.roll(u, 2, axis=0), jnp.where(rs == 1, prev1, prev2))
    y = um2 * cw_ref[0:1, :] + um1 * cw_ref[1:2, :] + u * cw_ref[2:3, :] + cb_ref[...]
    ocn_ref[...] = (_rms(c_out * y) * nco_ref[...]).astype(ocn_ref.dtype)
    if carry:
        carry_scr[...] = u[tm - 8:tm, :]
        u_ref[...] = u[tm - 8:tm, :]
    else:
        u_ref[...] = u


def _proj(x, shift, scale, norm_g, w_groups, cos, sin, prev1, prev2, conv_w, conv_b, norm_co, *, carry, seq_rows, tm=256):
    t = x.shape[0]
    mod_spec, rep = _mod_spec(shift.shape[0], t, tm, 1)
    rows = lambda wdt: pl.BlockSpec((tm, wdt), lambda i: (i, 0))
    cols = lambda h: pl.BlockSpec((h, tm), lambda i: (0, i))
    const = lambda r, wdt: pl.BlockSpec((r, wdt), lambda i: (0, 0))
    vmem = pl.BlockSpec(memory_space=pltpu.VMEM)
    sds = jax.ShapeDtypeStruct
    lin = (sds((t * KV_ROWS, HEAD_DIM), F32), pl.BlockSpec((tm * KV_ROWS, HEAD_DIM), lambda i: (i, 0)))
    if carry:
        per_ck = P_CK // tm
        kvb = (sds((t, KV_W), BF16), rows(KV_W))
        outs = [(sds((ATT_W, t), BF16), cols(ATT_W)), lin, lin, lin, kvb,
                (sds((t // P_CK, KV_W, P_CK), BF16), pl.BlockSpec((1, KV_W, tm), lambda i: (i // per_ck, 0, i % per_ck))),
                kvb,
                (sds((t // Q_BLOCK, KV_W, Q_BLOCK), BF16), pl.BlockSpec((tm // Q_BLOCK, KV_W, Q_BLOCK), lambda i: (i, 0, 0))),
                (sds((GATE_PAD, t), F32), cols(GATE_PAD)), (sds((t, CONV_W), BF16), rows(CONV_W)),
                (sds((8, CONV_W), F32), const(8, CONV_W))]
    else:
        outs = [(sds((t, ATT_W), F32), rows(ATT_W)), lin, lin, lin, (sds((t, GATE_PAD), F32), rows(GATE_PAD)),
                (sds((t, CONV_W), F32), rows(CONV_W)), (sds((t, CONV_W), F32), rows(CONV_W))]
    tab_spec = rows(HEAD_DIM) if carry else const(tm, HEAD_DIM)
    prev_spec = const(8, CONV_W) if carry else rows(CONV_W)
    return pl.pallas_call(
        functools.partial(_proj_kernel, tm=tm, seq_rows=seq_rows, carry=carry, rep=rep),
        grid=(t // tm,),
        in_specs=[rows(D_MODEL), mod_spec, mod_spec, const(1, D_MODEL), vmem, vmem, vmem,
                  tab_spec, tab_spec, prev_spec, prev_spec, const(CONV_K, CONV_W), const(1, CONV_W),
                  const(1, CONV_W)],
        out_specs=[spec for _, spec in outs],
        out_shape=[shape for shape, _ in outs],
        scratch_shapes=[pltpu.VMEM((8, CONV_W), F32)],
        compiler_params=_cparams("arbitrary"),
        name="proj",
    )(x, shift, scale, norm_g, *w_groups, cos, sin, prev1, prev2, conv_w, conv_b, norm_co)


def _cmp_kernel(x_ref, w_ref, o_ref, pad_scr, *, n_j, tm):
    pitch = n_j * KV_ROWS
    ppitch = pitch + SUBLANES

    def repitch(c, _):
        src = pl.multiple_of(c * pitch, SUBLANES)
        dst = pl.multiple_of(c * ppitch, SUBLANES)
        pad_scr[pl.ds(dst, pitch), :] = x_ref[pl.ds(src, pitch), :]
        return 0

    lax.fori_loop(0, tm, repitch, 0, unroll=8)
    _cmp_products(pad_scr, w_ref, o_ref, n_j, tm)


def _cmp_products(pad_scr, w_ref, o_ref, n_j, tm):
    ppitch = n_j * KV_ROWS + SUBLANES
    for kv in range(2):
        for g in range(N_KV):
            xs = jnp.concatenate([pad_scr[pl.ds(j * KV_ROWS + kv * N_KV + g, tm, stride=ppitch), :]
                                  for j in range(n_j)], axis=1)
            col = (kv * N_KV + g) * 2 * HEAD_DIM
            o_ref[:, col:col + 2 * HEAD_DIM] = _dot(xs.astype(BF16), w_ref[kv])


def _cmp_pages_kernel(pt_ref, *refs, n_pg, chunks_pp, n_j):
    del pt_ref
    pages = refs[:n_pg]
    w_ref, o_ref, pad_scr = refs[n_pg:]
    pitch = n_j * KV_ROWS
    ppitch = pitch + SUBLANES
    for p in range(n_pg):
        for c in range(chunks_pp):
            dst = (p * chunks_pp + c) * ppitch
            pad_scr[dst:dst + pitch, :] = pages[p][c * pitch:(c + 1) * pitch, :]
    _cmp_products(pad_scr, w_ref, o_ref, n_j, n_pg * chunks_pp)


def _compress_pages(page_table, x, w, n_j, page, ns):
    n_seq, n_pages = page_table.shape
    chunks_pp = page // n_j
    pitch = n_j * KV_ROWS
    n_out = 2 * N_KV * 2 * HEAD_DIM
    tm = ns * n_pages * chunks_pp
    page_map = lambda sq, p: (lambda b, pt: (pt[b * ns + sq, p], 0))
    in_specs = [pl.BlockSpec((page * KV_ROWS, HEAD_DIM), page_map(sq, p)) for sq in range(ns) for p in range(n_pages)]
    in_specs += [pl.BlockSpec(w.shape, lambda b, pt: (0, 0, 0))]
    grid_spec = pltpu.PrefetchScalarGridSpec(
        num_scalar_prefetch=1,
        grid=(n_seq // ns,),
        in_specs=in_specs,
        out_specs=pl.BlockSpec((tm, n_out), lambda b, pt: (b, 0)),
        scratch_shapes=[pltpu.VMEM((tm * (pitch + SUBLANES), HEAD_DIM), F32)],
    )
    return pl.pallas_call(
        functools.partial(_cmp_pages_kernel, n_pg=ns * n_pages, chunks_pp=chunks_pp, n_j=n_j),
        grid_spec=grid_spec,
        out_shape=jax.ShapeDtypeStruct((n_seq * n_pages * chunks_pp, n_out), F32),
        compiler_params=_cparams("arbitrary"),
        name="compress_pages",
    )(page_table, *([x] * (ns * n_pages)), w)


def _compress_products(x, w, n_j, tm):
    pitch = n_j * KV_ROWS
    m = x.shape[0] // pitch
    n_out = 2 * N_KV * 2 * HEAD_DIM
    tm = min(tm, m)
    return pl.pallas_call(
        functools.partial(_cmp_kernel, n_j=n_j, tm=tm),
        grid=(m // tm,),
        in_specs=[pl.BlockSpec((tm * pitch, HEAD_DIM), lambda i: (i, 0)),
                  pl.BlockSpec((2, n_j * HEAD_DIM, 2 * HEAD_DIM), lambda i: (0, 0, 0))],
        out_specs=pl.BlockSpec((tm, n_out), lambda i: (i, 0)),
        out_shape=jax.ShapeDtypeStruct((m, n_out), F32),
        scratch_shapes=[pltpu.VMEM((tm * (pitch + SUBLANES), HEAD_DIM), F32)],
        compiler_params=_cparams("arbitrary"),
        name="compress",
    )(x, w)


def _r_cols(kv, g):
    return (kv * N_KV + g) * 2 * HEAD_DIM


P_CK = 512
P_WKEYS = WINDOW + Q_BLOCK
P_TRIP = 4
N_FORCED = 3


def _pattn_t_kernel(qt_ref, gatet_ref, rp_ref, cosc_ref, sinc_ref, ks_ref, vst_ref, kw_ref, vwt_ref, et_ref, band_ref,
                    nao_ref, o_ref, kc_scr, vct_scr, pt_scr, ot_scr, s_scr, *, n_cmp_pad, n_blocks):
    i = pl.program_id(0)
    s0 = i * Q_BLOCK
    nq = Q_BLOCK
    rows = HEADS_PER_KV * nq
    lanes4 = lambda a: jnp.concatenate([a] * HEADS_PER_KV, axis=1)

    @pl.when(i == 0)
    def _():
        for g in range(N_KV):
            gs = slice(g * HEAD_DIM, (g + 1) * HEAD_DIM)
            ck, cv = _r_cols(0, g), _r_cols(1, g)
            kc = rp_ref[0:n_cmp_pad, ck:ck + HEAD_DIM] + rp_ref[pl.ds(1, n_cmp_pad), ck + HEAD_DIM:ck + 2 * HEAD_DIM]
            kc_scr[:, gs] = _rope(kc, cosc_ref[...], sinc_ref[...]).astype(BF16)
            vc = rp_ref[0:n_cmp_pad, cv:cv + HEAD_DIM] + rp_ref[pl.ds(1, n_cmp_pad), cv + HEAD_DIM:cv + 2 * HEAD_DIM]
            vct_scr[gs, :] = vc.T.astype(BF16)
        pt_scr[...] = jnp.zeros_like(pt_scr)

    t_q = s0 + lax.broadcasted_iota(jnp.int32, (n_cmp_pad, nq), 1)
    c_end = lax.broadcasted_iota(jnp.int32, (n_cmp_pad, nq), 0) * CMP_STRIDE + (CMP_BLOCK - 1)
    bias_c = lanes4(jnp.where(c_end <= t_q, 0.0, NEG))
    j_io = lax.broadcasted_iota(jnp.int32, (P_WKEYS, nq), 0)
    bias_w = lanes4(jnp.where(j_io >= WINDOW - s0, band_ref[...], NEG))

    blocks = [jnp.maximum(i + k - WINDOW // nq, 0) for k in range(P_WKEYS // nq)]
    scores, rhs, s_cmp, s_win = [], [], [], []
    for g in range(N_KV):
        gs = slice(g * HEAD_DIM, (g + 1) * HEAD_DIM)
        q_t = jnp.concatenate([qt_ref[(g * HEADS_PER_KV + r) * HEAD_DIM:(g * HEADS_PER_KV + r + 1) * HEAD_DIM, :]
                               for r in range(HEADS_PER_KV)], axis=1)
        rhs.append(q_t)
        s_cmp.append(_dot(kc_scr[:, gs], q_t))
        k_w = jnp.concatenate([kw_ref[pl.ds(pl.multiple_of(b * nq, nq), nq), gs] for b in blocks], axis=0)
        s_win.append(_dot(k_w, q_t))

    for g in range(N_KV):
        gs = slice(g * HEAD_DIM, (g + 1) * HEAD_DIM)
        s = s_cmp[g] + bias_c
        m = jnp.max(s, axis=0, keepdims=True)
        e = jnp.exp2(s - m)
        inv = jnp.where(m > 0.5 * NEG, 1.0 / jnp.maximum(jnp.sum(e, axis=0, keepdims=True), TINY), 0.0)
        o_c = _dot(vct_scr[gs, :], e.astype(BF16)) * inv

        p = e * inv
        pt_scr[SUBLANES:SUBLANES + n_cmp_pad, :] = sum(p[:, r * nq:(r + 1) * nq] for r in range(HEADS_PER_KV))
        st = lambda k: pt_scr[pl.ds(SUBLANES - 1 + k, n_blocks, stride=CHUNKS_PER_SEL), :]
        score = 0.5 * st(0) + st(1) + st(2) + st(3) + 0.5 * st(4)
        b_io = lax.broadcasted_iota(jnp.int32, (n_blocks, nq), 0)
        t_lane = s0 + lax.broadcasted_iota(jnp.int32, (n_blocks, nq), 1)
        cur = t_lane >> SEL_SHIFT
        forced = (b_io == 0) | (b_io == cur) | (b_io == cur - 1)
        valid = b_io * SEL_BLOCK <= t_lane
        scores.append((jnp.where(valid & jnp.logical_not(forced), score, -jnp.inf), jnp.where(forced, 1.0, 0.0)))

        v_wt = jnp.concatenate([vwt_ref[b, gs, :] for b in blocks], axis=1)
        s = s_win[g] + bias_w
        e = jnp.exp2(s - jnp.max(s, axis=0, keepdims=True))
        o_w = _dot(v_wt, e.astype(BF16)) / jnp.maximum(jnp.sum(e, axis=0, keepdims=True), TINY)

        for r in range(HEADS_PER_KV):
            hd = g * HEADS_PER_KV + r
            ls = slice(r * nq, (r + 1) * nq)
            gc = gatet_ref[hd * N_BRANCH + 0:hd * N_BRANCH + 1, :]
            gw = gatet_ref[hd * N_BRANCH + 2:hd * N_BRANCH + 3, :]
            ot_scr[hd * HEAD_DIM:(hd + 1) * HEAD_DIM, :] = gc * o_c[:, ls] + gw * o_w[:, ls]

    b_f = lax.broadcasted_iota(jnp.int32, (n_blocks, nq), 0).astype(F32)

    def pick(_, c):
        out = []
        for work, sel in c:
            m = jnp.max(work, axis=0, keepdims=True)
            idx = jnp.min(jnp.where(work == m, b_f, float(n_blocks)), axis=0, keepdims=True)
            hit = b_f == idx
            out.append((jnp.where(hit, -jnp.inf, work), jnp.where(hit, 1.0, sel)))
        return tuple(out)

    picked = lax.fori_loop(0, min(N_SEL, n_blocks) - N_FORCED, pick, tuple(scores))

    for g in range(N_KV):
        sel_bias = jnp.where(picked[g][1] > 0.5, 0.0, NEG).astype(BF16)
        rhs[g] = jnp.concatenate([rhs[g], lanes4(sel_bias)], axis=0)

    def qk_scores(c, slot):
        k0 = pl.multiple_of(c * P_CK, P_CK)
        blk_hot = et_ref[pl.ds(k0, P_CK), :]
        for g in range(N_KV):
            gs = slice(g * HEAD_DIM, (g + 1) * HEAD_DIM)
            s_scr[slot, g] = _dot(jnp.concatenate([ks_ref[pl.ds(k0, P_CK), gs], blk_hot], axis=1), rhs[g])

    ones_rows = jnp.ones((2 * SUBLANES, P_CK), BF16)

    def softmax_pv(c, slot, carry, causal):
        if causal:
            key = c * P_CK + lax.broadcasted_iota(jnp.int32, (P_CK, rows), 0)
            t_k = s0 + (lax.broadcasted_iota(jnp.int32, (P_CK, rows), 1) & (nq - 1))
            cb = jnp.where(key <= t_k, 0.0, NEG)
        out = []
        for g in range(N_KV):
            gs = slice(g * HEAD_DIM, (g + 1) * HEAD_DIM)
            m_i, l_i, acc = carry[g]
            s = s_scr[slot, g]
            if causal:
                s = s + cb
            m_n = jnp.maximum(m_i, jnp.max(s, axis=0, keepdims=True))
            p = jnp.exp2(s - m_n)
            alpha = jnp.exp2(m_i - m_n)
            pv = _dot(jnp.concatenate([vst_ref[c, gs, :], ones_rows], axis=0), p.astype(BF16))
            out.append((m_n, alpha * l_i + pv[HEAD_DIM:HEAD_DIM + 1], alpha * acc + pv[0:HEAD_DIM]))
        return tuple(out)

    def ahead(c, slot, carry):
        qk_scores(c + 1, 1 - slot)
        return softmax_pv(c, slot, carry, False)

    def trip(tr, carry):
        for k in range(P_TRIP):
            carry = ahead(P_TRIP * tr + k, k & 1, carry)
        return carry

    last = (s0 + nq + P_CK - 1) // P_CK - 1
    n_trips = last // P_TRIP
    init = tuple((jnp.full((1, rows), NEG, F32), jnp.zeros((1, rows), F32), jnp.zeros((HEAD_DIM, rows), F32))
                 for _ in range(N_KV))
    qk_scores(0, 0)
    carry = lax.fori_loop(0, n_trips, trip, init)
    done = P_TRIP * n_trips
    for k in range(P_TRIP - 1):
        carry = lax.cond(last - done > k, lambda cr, k=k: ahead(done + k, k & 1, cr), lambda cr: cr, carry)
    carry = lax.cond(((last - done) & 1) == 1, lambda cr: softmax_pv(last, 1, cr, True),
                     lambda cr: softmax_pv(last, 0, cr, True), carry)
    for g in range(N_KV):
        _, l_s, acc = carry[g]
        o_st = acc / jnp.maximum(l_s, TINY)
        for r in range(HEADS_PER_KV):
            hd = g * HEADS_PER_KV + r
            gsl = gatet_ref[hd * N_BRANCH + 1:hd * N_BRANCH + 2, :]
            ot_scr[hd * HEAD_DIM:(hd + 1) * HEAD_DIM, :] += gsl * o_st[:, r * nq:(r + 1) * nq]

    o_t = ot_scr[...]
    o_t = o_t * lax.rsqrt(jnp.mean(o_t * o_t, axis=0, keepdims=True) + EPS)
    o_ref[...] = (o_t.T * nao_ref[...]).astype(o_ref.dtype)


def _prompt_attention_t(q_t, gates_t, rp, cosc, sinc, ks, vs_t, kw, vw_t, e_t, band, norm_ao):
    s_len = q_t.shape[1]
    n_cmp_pad = s_len // CMP_STRIDE
    n_blocks = s_len // SEL_BLOCK
    vmem = pl.BlockSpec(memory_space=pltpu.VMEM)
    cols = lambda h: pl.BlockSpec((h, Q_BLOCK), lambda i: (0, i))
    return pl.pallas_call(
        functools.partial(_pattn_t_kernel, n_cmp_pad=n_cmp_pad, n_blocks=n_blocks),
        grid=(s_len // Q_BLOCK,),
        in_specs=[cols(ATT_W), cols(GATE_PAD), vmem, vmem, vmem, vmem, vmem, vmem, vmem, vmem, vmem,
                  pl.BlockSpec((1, ATT_W), lambda i: (0, 0))],
        out_specs=pl.BlockSpec((Q_BLOCK, ATT_W), lambda i: (i, 0)),
        out_shape=jax.ShapeDtypeStruct((s_len, ATT_W), BF16),
        scratch_shapes=[pltpu.VMEM((n_cmp_pad, KV_W), BF16), pltpu.VMEM((KV_W, n_cmp_pad), BF16),
                        pltpu.VMEM((n_cmp_pad + 2 * SUBLANES, Q_BLOCK), F32), pltpu.VMEM((ATT_W, Q_BLOCK), F32),
                        pltpu.VMEM((2, N_KV, P_CK, HEADS_PER_KV * Q_BLOCK), F32)],
        compiler_params=_cparams("arbitrary"),
        name="prompt_attn",
    )(q_t, gates_t, rp, cosc, sinc, ks, vs_t, kw, vw_t, e_t, band, norm_ao)


def _sattn_kernel(pt_ref, *refs, n_seqs, n_pages, page, n_new, past, wbuf):
    del pt_ref
    n_pg = n_seqs * n_pages
    r_ref, s_pages = refs[0], refs[1:1 + n_pg]
    (rnew_ref, q_ref, gate_ref, snew_ref, wst_ref, wnew_ref, cosc_ref, sinc_ref, e_ref, nao_ref,
     o_ref, wout_ref, r_scr, k_scr, v_scr, kw_scr, vw_scr, o_scr) = refs[1 + n_pg:]
    chunks_pp = page // CMP_STRIDE
    n_cmp = n_pages * chunks_pp
    rows = HEADS_PER_KV * n_new
    n_keys = k_scr.shape[1]
    n_wkeys = kw_scr.shape[1]
    pad = n_keys - past
    wpad = n_wkeys - wbuf
    batch = [(sq, g) for sq in range(n_seqs) for g in range(N_KV)]
    cat = lambda parts: jnp.concatenate(parts, axis=0)

    def with_zero_rows(new_rows, n_zero):
        return cat([new_rows, jnp.zeros((n_zero, HEAD_DIM), F32)]).astype(BF16)

    for sq in range(n_seqs):
        n0 = sq * n_new * KV_ROWS
        w0 = sq * wbuf * KV_ROWS
        r_scr[sq, 0:n_cmp, :] = r_ref[sq * n_cmp:(sq + 1) * n_cmp, :]
        for p in range(n_pages):
            pg = s_pages[sq * n_pages + p]
            for g in range(N_KV):
                b = sq * N_KV + g
                k_scr[b, p * page:(p + 1) * page, :] = pg[pl.ds(g, page, stride=KV_ROWS), :].astype(BF16)
                v_scr[b, p * page:(p + 1) * page, :] = pg[pl.ds(N_KV + g, page, stride=KV_ROWS), :].astype(BF16)
        r_scr[sq, n_cmp:n_cmp + 8, :] = cat([rnew_ref[sq], jnp.zeros((7, r_scr.shape[2]), F32)])
        for g in range(N_KV):
            b = sq * N_KV + g
            k_scr[b, past:n_keys, :] = with_zero_rows(snew_ref[pl.ds(n0 + g, n_new, stride=KV_ROWS), :], pad - n_new)
            v_scr[b, past:n_keys, :] = with_zero_rows(snew_ref[pl.ds(n0 + N_KV + g, n_new, stride=KV_ROWS), :],
                                                      pad - n_new)
            kw_scr[b, 0:wbuf, :] = wst_ref[pl.ds(w0 + g, wbuf, stride=KV_ROWS), :].astype(BF16)
            vw_scr[b, 0:wbuf, :] = wst_ref[pl.ds(w0 + N_KV + g, wbuf, stride=KV_ROWS), :].astype(BF16)
            kw_scr[b, wbuf:n_wkeys, :] = with_zero_rows(wnew_ref[pl.ds(n0 + g, n_new, stride=KV_ROWS), :],
                                                        wpad - n_new)
            vw_scr[b, wbuf:n_wkeys, :] = with_zero_rows(wnew_ref[pl.ds(n0 + N_KV + g, n_new, stride=KV_ROWS), :],
                                                        wpad - n_new)
        keep = (wbuf - n_new) * KV_ROWS
        wout_ref[w0:w0 + keep, :] = wst_ref[w0 + n_new * KV_ROWS:w0 + wbuf * KV_ROWS, :]
        wout_ref[w0 + keep:w0 + wbuf * KV_ROWS, :] = wnew_ref[n0:n0 + n_new * KV_ROWS, :]

    n_b = len(batch)
    all_rows = n_b * rows
    tok = lax.broadcasted_iota(jnp.int32, (all_rows, 1), 0) & (n_new - 1)
    t_rows = past + tok

    qg, kc, vc = [], [], []
    for sq, g in batch:
        qs = slice(sq * n_new, (sq + 1) * n_new)
        qg.append(cat([q_ref[qs, (g * HEADS_PER_KV + r) * HEAD_DIM:(g * HEADS_PER_KV + r + 1) * HEAD_DIM]
                       for r in range(HEADS_PER_KV)]).astype(BF16))
        ck, cv = _r_cols(0, g), _r_cols(1, g)
        kc.append(r_scr[sq, 0:n_cmp, ck:ck + HEAD_DIM] + r_scr[sq, pl.ds(1, n_cmp), ck + HEAD_DIM:ck + 2 * HEAD_DIM])
        vc.append((r_scr[sq, 0:n_cmp, cv:cv + HEAD_DIM]
                   + r_scr[sq, pl.ds(1, n_cmp), cv + HEAD_DIM:cv + 2 * HEAD_DIM]).astype(BF16))
    cos_all, sin_all = cat([cosc_ref[...]] * n_b), cat([sinc_ref[...]] * n_b)
    kc_all = _rope(cat(kc), cos_all, sin_all).astype(BF16)
    rb = lambda b: slice(b * rows, (b + 1) * rows)

    s_c = cat([_dot_nt(qg[b], kc_all[b * n_cmp:(b + 1) * n_cmp]) for b in range(n_b)])
    s_w = cat([_dot_nt(qg[b], kw_scr[b]) for b in range(n_b)])
    s_s = cat([_dot_nt(qg[b], k_scr[b]) for b in range(n_b)])

    c_end = lax.broadcasted_iota(jnp.int32, (all_rows, n_cmp), 1) * CMP_STRIDE + (CMP_BLOCK - 1)
    p_c = _masked_softmax(s_c, c_end <= t_rows)
    p_cb = p_c.astype(BF16)
    o_c = [_dot(p_cb[rb(b)], vc[b]) for b in range(n_b)]

    sel_rows = n_b * n_new
    p_grp = cat([sum(p_c[b * rows + r * n_new:b * rows + (r + 1) * n_new] for r in range(HEADS_PER_KV))
                 for b in range(n_b)])
    lane = lax.broadcasted_iota(jnp.int32, (sel_rows, HEAD_DIM), 1)
    pch = 0.5 * (p_grp + jnp.where(lane >= 1, pltpu.roll(p_grp, 1, axis=1), 0.0))
    score = pch
    for k in range(1, CHUNKS_PER_SEL):
        score = score + pltpu.roll(pch, HEAD_DIM - k, axis=1)
    blk = lane >> (CHUNKS_PER_SEL.bit_length() - 1)
    t_tok = past + (lax.broadcasted_iota(jnp.int32, (sel_rows, HEAD_DIM), 0) & (n_new - 1))
    cur = t_tok >> SEL_SHIFT
    forced = (blk == 0) | (blk == cur) | (blk == cur - 1)
    score = jnp.where(forced, jnp.inf, score)
    n_pb = past // SEL_BLOCK
    ahead = jnp.zeros((sel_rows, HEAD_DIM), F32)
    for k in range(1, n_pb):
        other = pltpu.roll(score, CHUNKS_PER_SEL * k, axis=1)
        wins = (other > score) | ((other == score) & (blk >= k))
        ahead = ahead + jnp.where(wins, 1.0, 0.0)
    sel = jnp.where(((lane & (CHUNKS_PER_SEL - 1)) == 0) & (ahead < N_SEL - 1), 1.0, 0.0).astype(BF16)
    sel_keys = _dot(sel, e_ref[...])
    new_ok = (lax.broadcasted_iota(jnp.int32, (sel_rows, pad), 1)
              <= (lax.broadcasted_iota(jnp.int32, (sel_rows, pad), 0) & (n_new - 1)))
    bias = jnp.concatenate([jnp.where(sel_keys > 0.5, 0.0, NEG), jnp.where(new_ok, 0.0, NEG)], axis=1)
    bias = cat([bias[b * n_new:(b + 1) * n_new] for b in range(n_b) for _ in range(HEADS_PER_KV)])

    s_s = s_s + bias
    e_s = jnp.exp2(s_s - jnp.max(s_s, axis=-1, keepdims=True))
    p_s = (e_s / jnp.maximum(jnp.sum(e_s, axis=-1, keepdims=True), TINY)).astype(BF16)
    o_s = [_dot(p_s[rb(b)], v_scr[b]) for b in range(n_b)]

    j_io = lax.broadcasted_iota(jnp.int32, (all_rows, n_wkeys), 1)
    m_w = (((j_io < wbuf) & (j_io > tok + (wbuf - WINDOW)) & (j_io >= wbuf - past))
           | ((j_io >= wbuf) & (j_io - wbuf <= tok)))
    p_w = _masked_softmax(s_w, m_w).astype(BF16)
    o_w = [_dot(p_w[rb(b)], vw_scr[b]) for b in range(n_b)]

    for b, (sq, g) in enumerate(batch):
        qs = slice(sq * n_new, (sq + 1) * n_new)
        for r in range(HEADS_PER_KV):
            hd = g * HEADS_PER_KV + r
            rs = slice(r * n_new, (r + 1) * n_new)
            gc = gate_ref[qs, hd * N_BRANCH + 0:hd * N_BRANCH + 1]
            gsl = gate_ref[qs, hd * N_BRANCH + 1:hd * N_BRANCH + 2]
            gw = gate_ref[qs, hd * N_BRANCH + 2:hd * N_BRANCH + 3]
            o_scr[qs, hd * HEAD_DIM:(hd + 1) * HEAD_DIM] = gc * o_c[b][rs] + gsl * o_s[b][rs] + gw * o_w[b][rs]

    o_ref[...] = _rms(o_scr[...]) * nao_ref[...]


S_SEQS = 2


def _sample_attention(page_table, r_all, r_new, q, gates, slc_cache, slc_new, win_state, win_new, cosc, sinc, e_mat,
                      norm_ao, *, n_new, page, wbuf):
    n_seq, n_pages = page_table.shape
    past = n_pages * page
    chunks_pp = page // CMP_STRIDE
    n_cmp = n_pages * chunks_pp
    assert n_cmp == HEAD_DIM and n_new == 8 and past % SEL_BLOCK == 0 and n_new <= SEL_BLOCK
    assert (past + n_new - 1) // SEL_BLOCK == past // SEL_BLOCK and wbuf == WINDOW and past >= WINDOW
    n_keys = past + HEAD_DIM
    n_wkeys = wbuf + HEAD_DIM
    r_w = r_all.shape[1]

    ns = S_SEQS
    assert n_seq % ns == 0
    page_map = lambda sq, p: (lambda b, pt: (pt[b * ns + sq, p], 0))
    in_specs = [pl.BlockSpec((ns * n_cmp, r_w), lambda b, pt: (b, 0))]
    in_specs += [pl.BlockSpec((page * KV_ROWS, HEAD_DIM), page_map(sq, p)) for sq in range(ns) for p in range(n_pages)]
    seq_rows = lambda wdt: pl.BlockSpec((ns * n_new, wdt), lambda b, pt: (b, 0))
    kv_rows = lambda n_tok: pl.BlockSpec((ns * n_tok * KV_ROWS, HEAD_DIM), lambda b, pt: (b, 0))
    const = lambda shape: pl.BlockSpec(shape, lambda b, pt: (0,) * len(shape))
    in_specs += [pl.BlockSpec((ns, 1, r_w), lambda b, pt: (b, 0, 0)), seq_rows(ATT_W), seq_rows(GATE_PAD),
                 kv_rows(n_new), kv_rows(wbuf), kv_rows(n_new),
                 const((n_cmp, HEAD_DIM)), const((n_cmp, HEAD_DIM)), const((HEAD_DIM, past)), const((1, ATT_W))]
    grid_spec = pltpu.PrefetchScalarGridSpec(
        num_scalar_prefetch=1,
        grid=(n_seq // ns,),
        in_specs=in_specs,
        out_specs=[seq_rows(ATT_W), kv_rows(wbuf)],
        scratch_shapes=[pltpu.VMEM((ns, n_cmp + 8, r_w), F32), pltpu.VMEM((ns * N_KV, n_keys, HEAD_DIM), BF16),
                        pltpu.VMEM((ns * N_KV, n_keys, HEAD_DIM), BF16), pltpu.VMEM((ns * N_KV, n_wkeys, HEAD_DIM), BF16),
                        pltpu.VMEM((ns * N_KV, n_wkeys, HEAD_DIM), BF16), pltpu.VMEM((ns * n_new, ATT_W), F32)],
    )
    return pl.pallas_call(
        functools.partial(_sattn_kernel, n_seqs=ns, n_pages=n_pages, page=page, n_new=n_new, past=past, wbuf=wbuf),
        grid_spec=grid_spec,
        out_shape=[jax.ShapeDtypeStruct((n_seq * n_new, ATT_W), F32),
                   jax.ShapeDtypeStruct((n_seq * wbuf * KV_ROWS, HEAD_DIM), F32)],
        compiler_params=_cparams("arbitrary"),
        name="sample_attn",
    )(page_table, r_all, *([slc_cache] * (ns * n_pages)), r_new, q, gates, slc_new, win_state,
      win_new, cosc, sinc, e_mat, norm_ao)


def _outproj_kernel(x_ref, oa_ref, oc_ref, gt_ref, w_ref, o_ref, *, rep):
    y = _dot(oa_ref[...].astype(BF16), w_ref[0:ATT_W, :]) + _dot(oc_ref[...].astype(BF16), w_ref[ATT_W:D_MODEL, :])
    o_ref[...] = x_ref[...] + (1.0 + _mod_rows(gt_ref, rep)) * y


def _outproj(x, oa, oc, gate, w, tm=512):
    t = x.shape[0]
    mod_spec, rep = _mod_spec(gate.shape[0], t, tm, 1)
    rows = lambda wdt: pl.BlockSpec((tm, wdt), lambda i: (i, 0))
    return pl.pallas_call(
        functools.partial(_outproj_kernel, rep=rep),
        grid=(t // tm,),
        in_specs=[rows(D_MODEL), rows(ATT_W), rows(CONV_W), mod_spec, pl.BlockSpec(memory_space=pltpu.VMEM)],
        out_specs=rows(D_MODEL),
        out_shape=jax.ShapeDtypeStruct((t, D_MODEL), F32),
        compiler_params=_cparams("arbitrary"),
        name="outproj",
    )(x, oa, oc, gate, w)


def _rope_tables(pos):
    half = HEAD_DIM // 2
    inv = np.float32(ROPE_THETA) ** (-np.arange(half, dtype=np.float32) * np.float32(2.0) / np.float32(HEAD_DIM))
    ang = np.asarray(pos, np.float32)[:, None] * inv[None, :]
    cos, sin = np.cos(ang), np.sin(ang)
    return (jnp.asarray(np.concatenate([cos, cos], axis=1), F32),
            jnp.asarray(np.concatenate([-sin, sin], axis=1), F32))


def _pack_w_in(w_in):
    off_g = ATT_W + 3 * SLAB_W
    off_c = off_g + N_HEADS * N_BRANCH
    gate_cols = jnp.pad(w_in[:, off_g:off_c], ((0, 0), (0, GATE_PAD - N_HEADS * N_BRANCH)))
    return (w_in[:, 0:off_g].astype(BF16), w_in[:, off_c:off_c + 3 * CONV_W].astype(BF16), gate_cols.astype(BF16))


def _pack_w_cmp(w_ck, w_cv, n_j):
    def one(w):
        lo = w[0:n_j].reshape(n_j * HEAD_DIM, HEAD_DIM)
        hi = w[CMP_STRIDE:CMP_STRIDE + n_j].reshape(n_j * HEAD_DIM, HEAD_DIM)
        return jnp.concatenate([lo, hi], axis=1)
    return jnp.stack([one(w_ck), one(w_cv)]).astype(BF16)


def kernel(x_prompt, x_sample, c_prompt, c_sample, cache_cmp_kv, cache_slc_kv, state_win_kv, state_conv, page_table,
           w_ada, b_ada, norm_ffn1, ffn1_gate, ffn1_up, ffn1_down, norm_mix, w_in, w_cmp_k, w_cmp_v, conv_w, conv_b,
           norm_att_out, norm_conv_out, w_out, norm_ffn2, ffn2_gate, ffn2_up, ffn2_down, norm_final):
    n_p, s_len, _ = x_prompt.shape
    n_seq, n_new, _ = x_sample.shape
    depth = w_ada.shape[0]
    assert n_p == 1 and depth == 1
    n_pages = page_table.shape[1]
    page = cache_slc_kv.shape[2]
    n_phys = cache_slc_kv.shape[1]
    past = n_pages * page
    wbuf = state_win_kv.shape[2]
    keep_p = min(WINDOW, s_len)
    t_s = n_seq * n_new
    l = 0

    c_all = jnp.concatenate([c_sample, c_prompt, jnp.zeros((8 - n_p, D_MODEL), F32)], axis=0)
    mod = _ada(c_all, w_ada[l], b_ada[l])
    mod_p = [mod[n_seq:n_seq + 1, k * D_MODEL:(k + 1) * D_MODEL] for k in range(N_MOD)]
    mod_s = [mod[0:n_seq, k * D_MODEL:(k + 1) * D_MODEL] for k in range(N_MOD)]

    row = lambda v: v.reshape(1, -1)
    w_proj = _pack_w_in(w_in[l])
    w_o = w_out[l].astype(BF16)
    nfin = row(norm_final)

    xp = x_prompt.reshape(s_len, D_MODEL)
    xs = x_sample.reshape(t_s, D_MODEL)

    xs, *f1 = _ffn_stream(xs, mod_s[0], mod_s[1], mod_s[2], row(norm_ffn1[l]), nfin, ffn1_gate[l], ffn1_up[l],
                          ffn1_down[l], final_norm=False)
    xp = _ffn(xp, mod_p[0], mod_p[1], mod_p[2], row(norm_ffn1[l]), nfin, *f1, final_norm=False)

    cos_p, sin_p = _rope_tables(np.arange(s_len))
    tm_s = 256
    cos_s, sin_s = _rope_tables(np.tile(past + np.arange(n_new), tm_s // n_new))
    zero8 = jnp.zeros((8, CONV_W), F32)
    conv_args = (conv_w[l], row(conv_b[l]), row(norm_conv_out[l]))
    (qt_p, cmp_p, slc_p, win_p, ksb_p, vst_p, kwb_p, vwt_p, gatet_p, ocn_p, utail_p) = _proj(
        xp, mod_p[3], mod_p[4], row(norm_mix[l]), w_proj, cos_p, sin_p, zero8, zero8, *conv_args,
        carry=True, seq_rows=s_len)
    prev1 = jnp.repeat(state_conv[l][:, CONV_K - 2], n_new, axis=0)
    prev2 = jnp.repeat(state_conv[l][:, CONV_K - 3], n_new, axis=0)
    (q_s, cmp_s, slc_s, win_s, gate_s, ocn_s, u_s) = _proj(
        xs, mod_s[3], mod_s[4], row(norm_mix[l]), w_proj, cos_s, sin_s, prev1, prev2, *conv_args,
        carry=False, seq_rows=n_new, tm=tm_s)

    n_j = CMP_STRIDE
    w_c = _pack_w_cmp(w_cmp_k[l], w_cmp_v[l], n_j)
    lin = lambda a: a.reshape(-1, HEAD_DIM)
    r_p = _compress_products(cmp_p, w_c, n_j, tm=256)
    r_cache = _compress_pages(page_table, lin(cache_cmp_kv), w_c, n_j, page, ns=2)
    r_new = _compress_products(cmp_s, _pack_w_cmp(w_cmp_k[l], w_cmp_v[l], n_new), n_new, tm=n_seq)

    n_cmp_pad = s_len // CMP_STRIDE
    cosc, sinc = _rope_tables(np.arange(n_cmp_pad) * CMP_STRIDE + (CMP_BLOCK - 1))
    r_p = jnp.pad(r_p, ((0, 8), (0, 0)))
    n_blocks = s_len // SEL_BLOCK
    e_t = ((jnp.arange(s_len) // SEL_BLOCK)[:, None] == jnp.arange(n_blocks)[None, :]).astype(BF16)
    j_w, ti_w = jnp.arange(P_WKEYS)[:, None], jnp.arange(Q_BLOCK)[None, :]
    band = jnp.where((j_w > ti_w) & (j_w <= ti_w + WINDOW), 0.0, NEG).astype(F32)
    oa_p = _prompt_attention_t(qt_p, gatet_p, r_p, cosc, sinc, ksb_p, vst_p, kwb_p, vwt_p, e_t, band,
                               row(norm_att_out[l]))

    n_cmp_s = past // CMP_STRIDE
    cosc_s, sinc_s = _rope_tables(np.arange(n_cmp_s) * CMP_STRIDE + (CMP_BLOCK - 1))
    e_s =(jnp.arange(HEAD_DIM)[:, None] == (jnp.arange(past) // SEL_BLOCK * CHUNKS_PER_SEL)[None, :]).astype(BF16)
    oa_s, win_new_state = _sample_attention(
        page_table, r_cache, r_new.reshape(n_seq, 1, -1), q_s, gate_s,
        lin(cache_slc_kv), slc_s, lin(state_win_kv), win_s,
        cosc_s, sinc_s, e_s, row(norm_att_out[l]), n_new=n_new, page=page, wbuf=wbuf)

    xp = _outproj(xp, oa_p, ocn_p, mod_p[5], w_o)
    xs = _outproj(xs, oa_s, ocn_s, mod_s[5], w_o)
    ys, *f2 = _ffn_stream(xs, mod_s[6], mod_s[7], mod_s[8], row(norm_ffn2[l]), nfin, ffn2_gate[l], ffn2_up[l],
                          ffn2_down[l], final_norm=True)
    yp = _ffn(xp, mod_p[6], mod_p[7], mod_p[8], row(norm_ffn2[l]), nfin, *f2, final_norm=True)

    kv6 = lambda a, n, s: a.reshape(1, n, s, 2, N_KV, HEAD_DIM)
    return (yp.reshape(n_p, s_len, D_MODEL), ys.reshape(n_seq, n_new, D_MODEL),
            kv6(cmp_p, n_p, s_len), kv6(slc_p, n_p, s_len), kv6(win_p[(s_len - keep_p) * KV_ROWS:], n_p, keep_p),
            utail_p[8 - (CONV_K - 1):].reshape(1, n_p, CONV_K - 1, CONV_W),
            kv6(cmp_s, n_seq, n_new), kv6(slc_s, n_seq, n_new), kv6(win_new_state, n_seq, wbuf),
            u_s.reshape(n_seq, n_new, CONV_W)[:, n_new - (CONV_K - 1):].reshape(1, n_seq, CONV_K - 1, CONV_W))
```
